```python
import jax
import jax.numpy as jnp
from jax import lax
import numpy as np

D_MODEL = 2048
BATCH = 2
SEQ = 8192
DEPTH = 2

GRID_W = 64
CTX_LEN = 256
HEAD_DIM = 64
ATTN_SCALE = HEAD_DIM ** -0.5
EPS = 1e-6
NEG_INF = -1e30

D_MIX = D_MODEL
GROUP_W = D_MIX // 4

NA_HEADS = GROUP_W // HEAD_DIM
NA_KH = 8
NA_KW = 16

N_POOL = 4
POOL_WINDOWS = (2, 4, 8, 16)
POOL_CH = GROUP_W // N_POOL

CONV_W = 3

SWA_HEADS = GROUP_W // HEAD_DIM
SWA_KV_HEADS = 2
SWA_GROUP = SWA_HEADS // SWA_KV_HEADS
SWA_WINDOW = 128
SWA_BLOCK = 128
ROPE_BASE = 10000.0

D_FF = 7 * D_MODEL // 2
N_EXPERTS = 8
TOP_K = 2
N_DENSE = (DEPTH + 1) // 2
N_MOE = DEPTH // 2

KV_W = SWA_KV_HEADS * HEAD_DIM
IN_WIDTHS = (GROUP_W, GROUP_W, GROUP_W, GROUP_W, GROUP_W, GROUP_W, GROUP_W, GROUP_W, KV_W, KV_W)
D_IN = sum(IN_WIDTHS)
SPLIT_POINTS = tuple(sum(IN_WIDTHS[:i + 1]) for i in range(len(IN_WIDTHS) - 1))

kernel_name = "hybrid_parallel_heads_dit_block"


def rms_norm(x, g):
    xf = x.astype(jnp.float32)
    y = xf * lax.rsqrt(jnp.mean(xf * xf, axis=-1, keepdims=True) + EPS)
    return (y * g.astype(jnp.float32)).astype(x.dtype)


def modulate(h, shift, scale):
    return h * (1 + scale) + shift


def heads(t, n):
    b, m, _ = t.shape
    return t.reshape(b, m, n, HEAD_DIM).transpose(0, 2, 1, 3)


def merge_heads(o):
    b, h, m, d = o.shape
    return o.transpose(0, 2, 1, 3).reshape(b, m, h * d)


def gqa_heads(t):
    b, m, _ = t.shape
    return t.reshape(b, m, SWA_KV_HEADS, SWA_GROUP, HEAD_DIM).transpose(0, 2, 3, 1, 4)


def merge_gqa(o):
    b, k, g, m, d = o.shape
    return o.transpose(0, 3, 1, 2, 4).reshape(b, m, k * g * d)


def axial_rope(n):
    t = jnp.arange(n, dtype=jnp.int32)
    rows = (t // GRID_W).astype(jnp.float32)
    cols = (t % GRID_W).astype(jnp.float32)
    nf = HEAD_DIM // 4
    inv = ROPE_BASE ** (-jnp.arange(nf, dtype=jnp.float32) / nf)
    ang = jnp.concatenate([rows[:, None] * inv, cols[:, None] * inv], axis=-1)
    return jnp.cos(ang), jnp.sin(ang)


def apply_rope(x, cos, sin):
    half = HEAD_DIM // 2
    x1 = x[..., :half].astype(jnp.float32)
    x2 = x[..., half:].astype(jnp.float32)
    return jnp.concatenate([x1 * cos - x2 * sin, x1 * sin + x2 * cos], axis=-1).astype(x.dtype)


def ctx_attention(q, k, v, sink):
    s = jnp.einsum("bkgqd,bkld->bkgql", q, k).astype(jnp.float32) * ATTN_SCALE
    if sink is None:
        p = jax.nn.softmax(s, axis=-1)
    else:
        s_sink = jnp.broadcast_to(sink.astype(jnp.float32)[None, :, :, None, None], s.shape[:-1] + (1,))
        p = jax.nn.softmax(jnp.concatenate([s, s_sink], axis=-1), axis=-1)[..., :-1]
    return jnp.einsum("bkgql,bkld->bkgqd", p.astype(v.dtype), v)


def neighbourhood_attention(q, k, v, kc, vc, rpb):
    b, h, s, d = q.shape
    rows = s // GRID_W
    kh = min(NA_KH, rows)
    r = jnp.arange(rows)
    r0 = jnp.clip(r - kh // 2, 0, rows - kh)
    key_rows = r0[:, None] + jnp.arange(kh)[None, :]
    kg = k.reshape(b, h, rows, GRID_W, d)[:, :, key_rows]
    vg = v.reshape(b, h, rows, GRID_W, d)[:, :, key_rows]
    qg = q.reshape(b, h, rows, GRID_W, d)
    s_loc = jnp.einsum("bhrqd,bhrkwd->bhrqkw", qg, kg).astype(jnp.float32) * ATTN_SCALE
    col = jnp.arange(GRID_W)
    c0 = jnp.clip(col - NA_KW // 2, 0, GRID_W - NA_KW)
    col_mask = (col[None, :] >= c0[:, None]) & (col[None, :] < c0[:, None] + NA_KW)
    dr = key_rows - r[:, None] + (NA_KH - 1)
    dc = jnp.clip(col[None, :] - col[:, None], -(NA_KW - 1), NA_KW - 1) + (NA_KW - 1)
    bias = rpb.astype(jnp.float32)[:, dr[:, None, :, None], dc[None, :, None, :]]
    s_loc = jnp.where(col_mask[None, None, None, :, None, :], s_loc + bias[None], NEG_INF)
    s_loc = s_loc.reshape(b, h, rows, GRID_W, kh * GRID_W)
    s_ctx = jnp.einsum("bhrqd,bhld->bhrql", qg, kc).astype(jnp.float32) * ATTN_SCALE
    p = jax.nn.softmax(jnp.concatenate([s_loc, s_ctx], axis=-1), axis=-1).astype(v.dtype)
    n_loc = kh * GRID_W
    out = jnp.einsum("bhrqn,bhrnd->bhrqd", p[..., :n_loc], vg.reshape(b, h, rows, n_loc, d))
    out = out + jnp.einsum("bhrql,bhld->bhrqd", p[..., n_loc:], vc)
    return out.reshape(b, h, s, d)


def window_attention(q, k, v, kc, vc, sink):
    b, hk, g, s, d = q.shape
    nb = s // SWA_BLOCK
    pad = ((0, 0), (0, 0), (SWA_BLOCK, SWA_BLOCK), (0, 0))
    kp = jnp.pad(k, pad).reshape(b, hk, nb + 2, SWA_BLOCK, d)
    vp = jnp.pad(v, pad).reshape(b, hk, nb + 2, SWA_BLOCK, d)
    kb = jnp.concatenate([kp[:, :, :-2], kp[:, :, 1:-1], kp[:, :, 2:]], axis=3)
    vb = jnp.concatenate([vp[:, :, :-2], vp[:, :, 1:-1], vp[:, :, 2:]], axis=3)
    qb = q.reshape(b, hk, g, nb, SWA_BLOCK, d)
    s_loc = jnp.einsum("bkgnqd,bknjd->bkgnqj", qb, kb).astype(jnp.float32) * ATTN_SCALE
    blk = jnp.arange(nb)[:, None] * SWA_BLOCK
    qpos = blk + jnp.arange(SWA_BLOCK)[None, :]
    kpos = blk - SWA_BLOCK + jnp.arange(3 * SWA_BLOCK)[None, :]
    valid = (jnp.abs(qpos[:, :, None] - kpos[:, None, :]) <= SWA_WINDOW) & (kpos[:, None, :] >= 0) & (kpos[:, None, :] < s)
    s_loc = jnp.where(valid[None, None, None], s_loc, NEG_INF)
    s_ctx = jnp.einsum("bkgnqd,bkld->bkgnql", qb, kc).astype(jnp.float32) * ATTN_SCALE
    s_sink = jnp.broadcast_to(sink.astype(jnp.float32)[None, :, :, None, None, None], s_loc.shape[:-1] + (1,))
    p = jax.nn.softmax(jnp.concatenate([s_loc, s_ctx, s_sink], axis=-1), axis=-1).astype(v.dtype)
    n_loc = 3 * SWA_BLOCK
    n_ctx = kc.shape[2]
    out = jnp.einsum("bkgnqj,bknjd->bkgnqd", p[..., :n_loc], vb)
    out = out + jnp.einsum("bkgnql,bkld->bkgnqd", p[..., n_loc:n_loc + n_ctx], vc)
    return out.reshape(b, hk, g, s, d)


def multiscale_pool(u, pool_w, pool_scale):
    b, n, _ = u.shape
    uf = u.astype(jnp.float32)
    cs = jnp.pad(jnp.cumsum(uf, axis=1), ((0, 0), (1, 0), (0, 0)))
    t = jnp.arange(n)
    outs = []
    for gi, w in enumerate(POOL_WINDOWS):
        sl = slice(gi * POOL_CH, (gi + 1) * POOL_CH)
        lo = jnp.clip(t - w // 2, 0, n)
        hi = jnp.clip(t + w - w // 2, 0, n)
        cnt = (hi - lo).astype(jnp.float32)[None, :, None]
        mean = (cs[:, hi, sl] - cs[:, lo, sl]) / cnt
        outs.append(mean - uf[..., sl])
    pooled = jnp.concatenate(outs, axis=-1).astype(u.dtype).reshape(b, n, N_POOL, POOL_CH)
    y = jnp.einsum("bngc,gcd->bngd", pooled, pool_w).reshape(b, n, GROUP_W)
    return y * pool_scale


def short_conv_mixer(gate_b, gate_c, xv, conv_w):
    u = gate_c * xv
    n = u.shape[1]
    pad = CONV_W // 2
    up = jnp.pad(u, ((0, 0), (pad, pad), (0, 0)))
    y = sum(up[:, j:j + n] * conv_w[j] for j in range(CONV_W))
    return gate_b * y


def mixing_sublayer(h, hc, w_in, w_out, rpb, pool_w, pool_scale, conv_w, sink, cos, sin, ctx_out):
    qa, ka, va, ub, gb, gc, xv, qd, kd, vd = jnp.split(h @ w_in, SPLIT_POINTS, axis=-1)
    qac, kac, vac, ubc, gbc, gcc, xvc, qdc, kdc, vdc = jnp.split(hc @ w_in, SPLIT_POINTS, axis=-1)
    kac_h, vac_h = heads(kac, NA_HEADS), heads(vac, NA_HEADS)
    kdc_h, vdc_h = heads(kdc, SWA_KV_HEADS), heads(vdc, SWA_KV_HEADS)
    o_a = neighbourhood_attention(heads(qa, NA_HEADS), heads(ka, NA_HEADS), heads(va, NA_HEADS), kac_h, vac_h, rpb)
    o_b = multiscale_pool(ub, pool_w, pool_scale)
    o_c = short_conv_mixer(gb, gc, xv, conv_w)
    o_d = window_attention(apply_rope(gqa_heads(qd), cos, sin), apply_rope(heads(kd, SWA_KV_HEADS), cos, sin),
                           heads(vd, SWA_KV_HEADS), kdc_h, vdc_h, sink)
    y = jnp.concatenate([merge_heads(o_a), o_b, o_c, merge_gqa(o_d)], axis=-1) @ w_out
    if not ctx_out:
        return y, None
    oc_a = ctx_attention(heads(qac, NA_HEADS)[:, :, None], kac_h, vac_h, None)[:, :, 0]
    oc_b = multiscale_pool(ubc, pool_w, pool_scale)
    oc_c = short_conv_mixer(gbc, gcc, xvc, conv_w)
    oc_d = ctx_attention(gqa_heads(qdc), kdc_h, vdc_h, sink)
    yc = jnp.concatenate([merge_heads(oc_a), oc_b, oc_c, merge_gqa(oc_d)], axis=-1) @ w_out
    return y, yc


def swiglu(h, w_gate, w_up, w_down):
    return (jax.nn.silu(h @ w_gate) * (h @ w_up)) @ w_down


def moe_swiglu(h, router, w_gate, w_up, w_down):
    logits = (h @ router).astype(jnp.float32)
    top_val, top_idx = lax.top_k(logits, TOP_K)
    top_p = jax.nn.softmax(top_val, axis=-1)
    gate = jnp.einsum("bnk,bnke->bne", top_p, jax.nn.one_hot(top_idx, N_EXPERTS, dtype=jnp.float32)).astype(h.dtype)
    out = jnp.zeros_like(h)
    for e in range(N_EXPERTS):
        out = out + gate[..., e:e + 1] * swiglu(h, w_gate[e], w_up[e], w_down[e])
    return out


def setup_inputs(seed: int = 0) -> dict:
    key = jax.random.key(seed)
    ks = jax.random.split(key, 24)

    def nrm(k, shape, s):
        return jax.random.normal(k, shape, jnp.float32) * s

    return {
        "x": nrm(ks[0], (BATCH, SEQ, D_MODEL), 1.0),
        "c": nrm(ks[1], (BATCH, D_MODEL), 1.0),
        "ctx": nrm(ks[2], (BATCH, CTX_LEN, D_MODEL), 1.0),
        "c_ctx": nrm(ks[3], (D_MODEL,), 1.0),
        "w_ada": nrm(ks[4], (DEPTH, D_MODEL, 6 * D_MODEL), 0.5 * D_MODEL ** -0.5),
        "b_ada": nrm(ks[5], (DEPTH, 6 * D_MODEL), 0.02),
        "norm_mix": 1.0 + nrm(ks[6], (DEPTH, D_MODEL), 0.1),
        "norm_ffn": 1.0 + nrm(ks[7], (DEPTH, D_MODEL), 0.1),
        "norm_final": 1.0 + nrm(ks[8], (D_MODEL,), 0.1),
        "w_in": nrm(ks[9], (DEPTH, D_MODEL, D_IN), D_MODEL ** -0.5),
        "w_out": nrm(ks[10], (DEPTH, D_MIX, D_MODEL), D_MIX ** -0.5),
        "na_rpb": nrm(ks[11], (DEPTH, NA_HEADS, 2 * NA_KH - 1, 2 * NA_KW - 1), 0.5),
        "pool_w": nrm(ks[12], (DEPTH, N_POOL, POOL_CH, POOL_CH), POOL_CH ** -0.5),
        "pool_scale": 1.0 + nrm(ks[13], (DEPTH, GROUP_W), 0.1),
        "conv_w": nrm(ks[14], (DEPTH, CONV_W, GROUP_W), CONV_W ** -0.5),
        "swa_sink": nrm(ks[15], (DEPTH, SWA_KV_HEADS, SWA_GROUP), 1.0),
        "ffn_w_gate": nrm(ks[16], (N_DENSE, D_MODEL, D_FF), D_MODEL ** -0.5),
        "ffn_w_up": nrm(ks[17], (N_DENSE, D_MODEL, D_FF), D_MODEL ** -0.5),
        "ffn_w_down": nrm(ks[18], (N_DENSE, D_FF, D_MODEL), D_FF ** -0.5),
        "moe_router": nrm(ks[19], (N_MOE, D_MODEL, N_EXPERTS), D_MODEL ** -0.5),
        "moe_w_gate": nrm(ks[20], (N_MOE, N_EXPERTS, D_MODEL, D_FF), D_MODEL ** -0.5),
        "moe_w_up": nrm(ks[21], (N_MOE, N_EXPERTS, D_MODEL, D_FF), D_MODEL ** -0.5),
        "moe_w_down": nrm(ks[22], (N_MOE, N_EXPERTS, D_FF, D_MODEL), D_FF ** -0.5),
    }


def reference(x, c, ctx, c_ctx, w_ada, b_ada, norm_mix, norm_ffn, norm_final, w_in, w_out, na_rpb,
              pool_w, pool_scale, conv_w, swa_sink, ffn_w_gate, ffn_w_up, ffn_w_down,
              moe_router, moe_w_gate, moe_w_up, moe_w_down):
    n_ctx = ctx.shape[1]
    cos, sin = axial_rope(x.shape[1])
    cx = ctx
    for i in range(DEPTH):
        last = i == DEPTH - 1
        mod_x = (jax.nn.silu(c) @ w_ada[i] + b_ada[i])[:, None, :]
        mod_c = (jax.nn.silu(c_ctx) @ w_ada[i] + b_ada[i])[None, None, :]
        sh1, sc1, g1, sh2, sc2, g2 = jnp.split(mod_x, 6, axis=-1)
        csh1, csc1, cg1, csh2, csc2, cg2 = jnp.split(mod_c, 6, axis=-1)
        h = modulate(rms_norm(x, norm_mix[i]), sh1, sc1)
        hc = modulate(rms_norm(cx, norm_mix[i]), csh1, csc1)
        y, yc = mixing_sublayer(h, hc, w_in[i], w_out[i], na_rpb[i], pool_w[i], pool_scale[i], conv_w[i],
                                swa_sink[i], cos, sin, not last)
        x = x + g1 * y
        h = modulate(rms_norm(x, norm_ffn[i]), sh2, sc2)
        if not last:
            cx = cx + cg1 * yc
            hc = modulate(rms_norm(cx, norm_ffn[i]), csh2, csc2)
            h = jnp.concatenate([hc, h], axis=1)
        if i % 2 == 0:
            f = swiglu(h, ffn_w_gate[i // 2], ffn_w_up[i // 2], ffn_w_down[i // 2])
        else:
            f = moe_swiglu(h, moe_router[i // 2], moe_w_gate[i // 2], moe_w_up[i // 2], moe_w_down[i // 2])
        if not last:
            cx = cx + cg2 * f[:, :n_ctx]
            f = f[:, n_ctx:]
        x = x + g2 * f
    return rms_norm(x, norm_final)
```

```python
import functools

import jax
import jax.numpy as jnp
from jax import lax
from jax.experimental import pallas as pl
from jax.experimental.pallas import tpu as pltpu

F32 = jnp.float32
BF16 = jnp.bfloat16

D_MODEL = 2048
GRID_W = 64
HEAD_DIM = 64
EPS = 1e-6
NEG_INF = -1e30
GROUP_W = 512
NA_KH = 8
NA_KW = 16
POOL_WINDOWS = (2, 4, 8, 16)
POOL_CH = 128
SWA_WINDOW = 128
SWA_BLOCK = 128
ROPE_BASE = 10000.0
N_EXPERTS = 8
LANES = 128
HALO = 16

C_UBC = 0
C_QA = 2048
C_KA = 2560
C_VA = 3072
C_QD = 3584
C_KD = 4096
C_VD = 4352
D_PROJ = 4608
TN_PROJ = 512

VMEM_LIMIT = 56 * 1024 * 1024


def _cparams(sem):
    return pltpu.CompilerParams(dimension_semantics=sem, vmem_limit_bytes=VMEM_LIMIT)


def _dot(a, b):
    return jnp.dot(a, b, preferred_element_type=F32)


def _dot_t(a, b):
    return lax.dot_general(a, b, (((1,), (1,)), ((), ())), preferred_element_type=F32)


def _rms_mod(xv, nw, sh, sc):
    ms = jnp.mean(xv * xv, axis=-1, keepdims=True)
    y = xv * lax.rsqrt(ms + EPS) * nw
    return y * (1.0 + sc) + sh


def _ada_kernel(c_ref, w_ref, b_ref, o_ref):
    tn = w_ref.shape[2]
    o_ref[...] = jnp.zeros(o_ref.shape, F32)
    for r in range(3):
        cv = c_ref[r]
        m = cv * (1.0 / (1.0 + jnp.exp(-cv)))
        cols = []
        for j in range(tn // LANES):
            wj = w_ref[0, :, j * LANES:(j + 1) * LANES]
            cols.append(jnp.sum(wj * m, axis=0, keepdims=True))
        o_ref[0, r:r + 1, :] = jnp.concatenate(cols, axis=1) + b_ref[0]


def _ada(cond, w_ada, b_ada):
    depth, d, n6 = w_ada.shape
    tn = 1024
    cb = jnp.broadcast_to(cond[:, :, None], (3, d, LANES))
    return pl.pallas_call(
        _ada_kernel,
        grid=(depth, n6 // tn),
        in_specs=[
            pl.BlockSpec((3, d, LANES), lambda i, j: (0, 0, 0)),
            pl.BlockSpec((1, d, tn), lambda i, j: (i, 0, j)),
            pl.BlockSpec((1, 1, tn), lambda i, j: (i, 0, j)),
        ],
        out_specs=pl.BlockSpec((1, 8, tn), lambda i, j: (i, 0, j)),
        out_shape=jax.ShapeDtypeStruct((depth, 8, n6), F32),
        compiler_params=_cparams(("arbitrary", "arbitrary")),
        name="ada",
    )(cb, w_ada, b_ada.reshape(depth, 1, n6))


def _rope(a, cosv, sinv, lo32):
    sw = jnp.where(lo32, pltpu.roll(a, 96, 1), pltpu.roll(a, 32, 1))
    return a * cosv + sw * sinv


def _inproj_kernel(*refs, has_res):
    if has_res:
        (x_ref, f_ref, g_ref, nw_ref, sh_ref, sc_ref, cos_ref, sin_ref, w_ref,
         o_ref, x2_ref, h_scr) = refs
    else:
        x_ref, nw_ref, sh_ref, sc_ref, cos_ref, sin_ref, w_ref, o_ref, h_scr = refs
    n = pl.program_id(2)

    @pl.when(n == 0)
    def _():
        xv = x_ref[0]
        if has_res:
            xv = xv + g_ref[0, 0] * f_ref[0].astype(F32)
            x2_ref[0] = xv
        h_scr[...] = _rms_mod(xv, nw_ref[...], sh_ref[0, 0], sc_ref[0, 0]).astype(BF16)

    acc = _dot(h_scr[...], w_ref[...])
    rope_q = C_QD // TN_PROJ
    rope_kv = C_KD // TN_PROJ

    @pl.when(n < rope_q)
    def _():
        o_ref[0] = acc.astype(BF16)

    def rope_groups(n_rope):
        lane = lax.broadcasted_iota(jnp.int32, (acc.shape[0], LANES), 1)
        lo32 = (lane % HEAD_DIM) < (HEAD_DIM // 2)
        cosv = cos_ref[...]
        sinv = sin_ref[...]
        for g in range(TN_PROJ // LANES):
            a = acc[:, g * LANES:(g + 1) * LANES]
            if g < n_rope:
                a = _rope(a, cosv, sinv, lo32)
            o_ref[0, :, g * LANES:(g + 1) * LANES] = a.astype(BF16)

    @pl.when(n == rope_q)
    def _():
        rope_groups(4)

    @pl.when(n == rope_kv)
    def _():
        rope_groups(2)


def _inproj(x, w, nw, mods, mod_row, cos_t, sin_t, res=None):
    b, s, d = x.shape
    tm = min(s, 512)
    has_res = res is not None
    row = mod_row
    xspec = pl.BlockSpec((1, tm, d), lambda bi, m, n: (bi, m, 0))

    def mspec(k):
        return pl.BlockSpec((1, 1, 1, d), lambda bi, m, n: (row(bi), k, 0, 0))

    in_specs = [xspec]
    args = [x]
    if has_res:
        f, gk = res
        in_specs += [xspec, mspec(gk)]
        args += [f, mods]
    in_specs += [
        pl.BlockSpec((1, d), lambda bi, m, n: (0, 0)),
        mspec(0), mspec(1),
        pl.BlockSpec((tm, LANES), lambda bi, m, n: (m, 0)),
        pl.BlockSpec((tm, LANES), lambda bi, m, n: (m, 0)),
        pl.BlockSpec((d, TN_PROJ), lambda bi, m, n: (0, n)),
    ]
    args += [nw.reshape(1, d), mods, mods, cos_t, sin_t, w]
    out_shape = [jax.ShapeDtypeStruct((b, s, D_PROJ), BF16)]
    out_specs = [pl.BlockSpec((1, tm, TN_PROJ), lambda bi, m, n: (bi, m, n))]
    if has_res:
        out_shape.append(jax.ShapeDtypeStruct((b, s, d), F32))
        out_specs.append(xspec)
    outs = pl.pallas_call(
        functools.partial(_inproj_kernel, has_res=has_res),
        grid=(b, s // tm, D_PROJ // TN_PROJ),
        in_specs=in_specs,
        out_specs=out_specs,
        out_shape=out_shape,
        scratch_shapes=[pltpu.VMEM((tm, d), BF16)],
        compiler_params=_cparams(("arbitrary", "arbitrary", "arbitrary")),
        name="inproj",
    )(*args)
    return outs if has_res else (outs[0], None)


def _na_bias_kernel(rpb_ref, o_ref):
    h = pl.program_id(0)
    nd = 2 * NA_KH - 1
    nj = 2 * NA_KW - 1
    q = lax.broadcasted_iota(jnp.int32, (GRID_W, LANES), 0)
    lane = lax.broadcasted_iota(jnp.int32, (GRID_W, LANES), 1)
    kw = lane % GRID_W
    dc = jnp.clip(kw - q, -(NA_KW - 1), NA_KW - 1) + (NA_KW - 1)
    c0 = jnp.clip(q - NA_KW // 2, 0, GRID_W - NA_KW)
    valid = (kw >= c0) & (kw < c0 + NA_KW)
    tabs = []
    for d in range(nd):
        t = jnp.zeros((GRID_W, LANES), F32)
        for j in range(nj):
            t = jnp.where(dc == j, rpb_ref[h * (nd * nj) + d * nj + j], t)
        tabs.append(jnp.where(valid, t, NEG_INF))
    for c in range(NA_KH):
        for g in range(NA_KH // 2):
            d_lo = 2 * g - c + (NA_KH - 1)
            o_ref[c, 0, :, g * LANES:(g + 1) * LANES] = jnp.where(lane < GRID_W, tabs[d_lo], tabs[d_lo + 1])


def _na_bias(rpb):
    nh = rpb.shape[0]
    return pl.pallas_call(
        _na_bias_kernel,
        grid=(nh,),
        in_specs=[pl.BlockSpec(memory_space=pltpu.SMEM)],
        out_specs=pl.BlockSpec((NA_KH, 1, GRID_W, NA_KH * GRID_W), lambda h: (0, h, 0, 0)),
        out_shape=jax.ShapeDtypeStruct((NA_KH, nh, GRID_W, NA_KH * GRID_W), F32),
        compiler_params=_cparams(("arbitrary",)),
        name="na_bias",
    )(rpb.reshape(-1))


def _softmax_pv(s_parts, v_parts, extra=None):
    m = jnp.max(s_parts[0], axis=-1, keepdims=True)
    for s in s_parts[1:]:
        m = jnp.maximum(m, jnp.max(s, axis=-1, keepdims=True))
    if extra is not None:
        m = jnp.maximum(m, extra)
    den = None
    o = None
    for s, v in zip(s_parts, v_parts):
        p = jnp.exp(s - m)
        ps = jnp.sum(p, axis=-1, keepdims=True)
        den = ps if den is None else den + ps
        pv = _dot(p.astype(BF16), v)
        o = pv if o is None else o + pv
    if extra is not None:
        den = den + jnp.exp(extra - m)
    return o / den


ROWS_PER_STEP = 8


def _na_kernel(q_ref, k_ref, v_ref, kc_ref, vc_ref, bias_ref, o_ref, *, rows):
    rb = pl.program_id(2)
    lane = lax.broadcasted_iota(jnp.int32, (GRID_W, LANES), 1)
    lo = lane < HEAD_DIM
    kc = kc_ref[0]
    vc = vc_ref[0]
    kh = min(NA_KH, rows)
    win = kh * GRID_W

    def row_body(i, carry):
        r = rb * ROWS_PER_STEP + i
        r0 = jnp.clip(r - kh // 2, 0, rows - kh)
        c = r - r0
        qoff = pl.multiple_of(i * GRID_W, GRID_W)
        koff = pl.multiple_of(r0 * GRID_W, GRID_W)
        q = q_ref[0, pl.ds(qoff, GRID_W), :]
        kw = k_ref[0, pl.ds(koff, win), :]
        vw = v_ref[0, pl.ds(koff, win), :]
        outs = []
        for hh in range(2):
            qm = jnp.where(lo if hh == 0 else jnp.logical_not(lo), q, jnp.zeros_like(q))
            s_loc = _dot_t(qm, kw) + bias_ref[c, hh]
            s_ctx = _dot_t(qm, kc)
            outs.append(_softmax_pv([s_loc, s_ctx], [vw, vc]))
        o_ref[0, pl.ds(qoff, GRID_W), :] = jnp.where(lo, outs[0], outs[1]).astype(BF16)
        return carry

    lax.fori_loop(0, ROWS_PER_STEP, row_body, 0)


def _na_attn(proj, projc, bias):
    b, s, _ = proj.shape
    lc = projc.shape[1]
    rows = s // GRID_W
    assert rows >= NA_KH and rows % ROWS_PER_STEP == 0
    tq = ROWS_PER_STEP * GRID_W
    npair = GROUP_W // LANES
    return pl.pallas_call(
        functools.partial(_na_kernel, rows=rows),
        grid=(b, npair, rows // ROWS_PER_STEP),
        in_specs=[
            pl.BlockSpec((1, tq, LANES), lambda bi, p, r: (bi, r, C_QA // LANES + p)),
            pl.BlockSpec((1, s, LANES), lambda bi, p, r: (bi, 0, C_KA // LANES + p)),
            pl.BlockSpec((1, s, LANES), lambda bi, p, r: (bi, 0, C_VA // LANES + p)),
            pl.BlockSpec((1, lc, LANES), lambda bi, p, r: (bi, 0, C_KA // LANES + p)),
            pl.BlockSpec((1, lc, LANES), lambda bi, p, r: (bi, 0, C_VA // LANES + p)),
            pl.BlockSpec((NA_KH, 2, GRID_W, NA_KH * GRID_W), lambda bi, p, r: (0, p, 0, 0)),
        ],
        out_specs=pl.BlockSpec((1, tq, LANES), lambda bi, p, r: (bi, r, p)),
        out_shape=jax.ShapeDtypeStruct((b, s, GROUP_W), BF16),
        compiler_params=_cparams(("arbitrary", "arbitrary", "arbitrary")),
        name="na_attn",
    )(proj, proj, proj, projc, projc, bias)


def _gqa_group(q, kv, sink_ref, k_parts, v_parts, mask):
    m = q.shape[0]
    lane = lax.broadcasted_iota(jnp.int32, (m, LANES), 1)
    lo = lane < HEAD_DIM
    qs = []
    for j in range(2):
        qg = q[:, (kv * 2 + j) * LANES:(kv * 2 + j + 1) * LANES]
        qs.append(jnp.where(lo, qg, jnp.zeros_like(qg)))
        qs.append(jnp.where(lo, jnp.zeros_like(qg), qg))
    qq = jnp.concatenate(qs, axis=0)
    s_all = [_dot_t(qq, k) for k in k_parts]
    heads = []
    for i in range(4):
        parts = [s[i * m:(i + 1) * m] for s in s_all]
        if mask is not None:
            parts[0] = jnp.where(mask, parts[0], NEG_INF)
        extra = None
        if sink_ref is not None:
            extra = jnp.full((m, 1), sink_ref[kv * 4 + i], F32)
        heads.append(_softmax_pv(parts, v_parts, extra))
    return [jnp.where(lo, heads[0], heads[1]), jnp.where(lo, heads[2], heads[3])]


def _swa_kernel(sink_ref, q_ref, k_ref, v_ref, kc_ref, vc_ref, o_ref):
    n = pl.program_id(1)
    nb = pl.num_programs(1)
    start = jnp.clip(n - 1, 0, nb - 3)
    koff = pl.multiple_of(start * SWA_BLOCK, SWA_BLOCK)
    kw = k_ref[0, pl.ds(koff, 3 * SWA_BLOCK), :]
    vw = v_ref[0, pl.ds(koff, 3 * SWA_BLOCK), :]
    kc = kc_ref[0]
    vc = vc_ref[0]
    rel = (n - start) * SWA_BLOCK + lax.broadcasted_iota(jnp.int32, (SWA_BLOCK, 3 * SWA_BLOCK), 0)
    kj = lax.broadcasted_iota(jnp.int32, (SWA_BLOCK, 3 * SWA_BLOCK), 1)
    valid = jnp.abs(rel - kj) <= SWA_WINDOW
    q = q_ref[0]
    for kv in range(2):
        sl = slice(kv * LANES, (kv + 1) * LANES)
        outs = _gqa_group(q, kv, sink_ref, [kw[:, sl], kc[:, sl]], [vw[:, sl], vc[:, sl]], valid)
        for j in range(2):
            g = kv * 2 + j
            o_ref[0, :, g * LANES:(g + 1) * LANES] = outs[j].astype(BF16)


def _swa_attn(proj, projc, sink):
    b, s, _ = proj.shape
    lc = projc.shape[1]
    nb = s // SWA_BLOCK
    assert nb >= 3
    w2 = 2 * LANES
    return pl.pallas_call(
        _swa_kernel,
        grid=(b, nb),
        in_specs=[
            pl.BlockSpec(memory_space=pltpu.SMEM),
            pl.BlockSpec((1, SWA_BLOCK, GROUP_W), lambda bi, n: (bi, n, C_QD // GROUP_W)),
            pl.BlockSpec((1, s, w2), lambda bi, n: (bi, 0, C_KD // w2)),
            pl.BlockSpec((1, s, w2), lambda bi, n: (bi, 0, C_VD // w2)),
            pl.BlockSpec((1, lc, w2), lambda bi, n: (bi, 0, C_KD // w2)),
            pl.BlockSpec((1, lc, w2), lambda bi, n: (bi, 0, C_VD // w2)),
        ],
        out_specs=pl.BlockSpec((1, SWA_BLOCK, GROUP_W), lambda bi, n: (bi, n, 0)),
        out_shape=jax.ShapeDtypeStruct((b, s, GROUP_W), BF16),
        compiler_params=_cparams(("arbitrary", "arbitrary")),
        name="swa_attn",
    )(sink.reshape(-1), proj, proj, proj, projc, projc)


def _ctx_attn_kernel(sink_ref, qa_ref, ka_ref, va_ref, qd_ref, kd_ref, vd_ref, oa_ref, od_ref):
    m = qa_ref.shape[1]
    lane = lax.broadcasted_iota(jnp.int32, (m, LANES), 1)
    lo = lane < HEAD_DIM
    for p in range(GROUP_W // LANES):
        sl = slice(p * LANES, (p + 1) * LANES)
        q = qa_ref[0, :, sl]
        k = ka_ref[0, :, sl]
        v = va_ref[0, :, sl]
        outs = []
        for hh in range(2):
            qm = jnp.where(lo if hh == 0 else jnp.logical_not(lo), q, jnp.zeros_like(q))
            outs.append(_softmax_pv([_dot_t(qm, k)], [v]))
        oa_ref[0, :, sl] = jnp.where(lo, outs[0], outs[1]).astype(BF16)
    q = qd_ref[0]
    for kv in range(2):
        sl = slice(kv * LANES, (kv + 1) * LANES)
        outs = _gqa_group(q, kv, sink_ref, [kd_ref[0, :, sl]], [vd_ref[0, :, sl]], None)
        for j in range(2):
            g = kv * 2 + j
            od_ref[0, :, g * LANES:(g + 1) * LANES] = outs[j].astype(BF16)


def _ctx_attn(projc, sink):
    b, lc, _ = projc.shape
    w2 = 2 * LANES

    def spec(width, col):
        return pl.BlockSpec((1, lc, width), lambda bi: (bi, 0, col // width))

    ospec = pl.BlockSpec((1, lc, GROUP_W), lambda bi: (bi, 0, 0))
    return pl.pallas_call(
        _ctx_attn_kernel,
        grid=(b,),
        in_specs=[pl.BlockSpec(memory_space=pltpu.SMEM),
                  spec(GROUP_W, C_QA), spec(GROUP_W, C_KA), spec(GROUP_W, C_VA),
                  spec(GROUP_W, C_QD), spec(w2, C_KD), spec(w2, C_VD)],
        out_specs=[ospec, ospec],
        out_shape=[jax.ShapeDtypeStruct((b, lc, GROUP_W), BF16)] * 2,
        compiler_params=_cparams(("arbitrary",)),
        name="ctx_attn",
    )(sink.reshape(-1), projc, projc, projc, projc, projc, projc)


def _outproj_kernel(*refs, tm, moe):
    (oa_ref, od_ref, u_ref, up_ref, un_ref, pw_ref, ps_ref, cw_ref, wo_ref, x_ref, g1_ref,
     nw_ref, sh_ref, sc_ref) = refs[:14]
    rest = refs[14:]
    if moe:
        rt_ref, xo_ref, h_ref, gate_ref, ext_ub, ext_u = rest
    else:
        xo_ref, h_ref, ext_ub, ext_u = rest
    m = pl.program_id(1)
    nm = pl.num_programs(1)
    g4 = GROUP_W

    def cols(ref, k):
        return ref[0, :, k * g4:(k + 1) * g4].astype(F32)

    ub = cols(u_ref, 0)
    gb = cols(u_ref, 1)
    has_prev = m > 0
    has_next = m < nm - 1
    ext_ub[0:HALO] = jnp.where(has_prev, cols(up_ref, 0), 0.0)
    ext_ub[HALO:HALO + tm] = ub
    ext_ub[HALO + tm:] = jnp.where(has_next, cols(un_ref, 0), 0.0)
    ext_u[0:HALO] = jnp.where(has_prev, cols(up_ref, 2) * cols(up_ref, 3), 0.0)
    ext_u[HALO:HALO + tm] = cols(u_ref, 2) * cols(u_ref, 3)
    ext_u[HALO + tm:] = jnp.where(has_next, cols(un_ref, 2) * cols(un_ref, 3), 0.0)

    n_tok = nm * tm
    t = m * tm + lax.broadcasted_iota(jnp.int32, (tm, LANES), 0)
    obs = []
    for g, w in enumerate(POOL_WINDOWS):
        sl = slice(g * LANES, (g + 1) * LANES)
        acc = None
        for d in range(-(w // 2), w - w // 2):
            term = ext_ub[HALO + d:HALO + d + tm, sl]
            acc = term if acc is None else acc + term
        cnt = (jnp.clip(t + (w - w // 2), 0, n_tok) - jnp.clip(t - w // 2, 0, n_tok)).astype(F32)
        pooled = acc / cnt - ub[:, sl]
        obs.append(_dot(pooled.astype(BF16), pw_ref[g]) * ps_ref[:, sl])
    o_b = jnp.concatenate(obs, axis=1).astype(BF16)

    y = (ext_u[HALO - 1:HALO - 1 + tm] * cw_ref[0:1, :] + ext_u[HALO:HALO + tm] * cw_ref[1:2, :]
         + ext_u[HALO + 1:HALO + 1 + tm] * cw_ref[2:3, :])
    o_c = (gb * y).astype(BF16)

    yy = (_dot(oa_ref[0], wo_ref[0:g4]) + _dot(o_b, wo_ref[g4:2 * g4])
          + _dot(o_c, wo_ref[2 * g4:3 * g4]) + _dot(od_ref[0], wo_ref[3 * g4:4 * g4]))
    xn = x_ref[0] + g1_ref[0, 0] * yy
    xo_ref[0] = xn
    h2 = _rms_mod(xn, nw_ref[...], sh_ref[0, 0], sc_ref[0, 0])
    h_ref[0] = h2.astype(BF16)

    if moe:
        logits = jnp.dot(h2, rt_ref[...], preferred_element_type=F32, precision=lax.Precision.HIGHEST)
        lane = lax.broadcasted_iota(jnp.int32, logits.shape, 1)
        logits = jnp.where(lane < N_EXPERTS, logits, -jnp.inf)
        v1 = jnp.max(logits, axis=-1, keepdims=True)
        i1 = jnp.min(jnp.where(logits == v1, lane, LANES), axis=-1, keepdims=True)
        l2 = jnp.where(lane == i1, -jnp.inf, logits)
        v2 = jnp.max(l2, axis=-1, keepdims=True)
        i2 = jnp.min(jnp.where(l2 == v2, lane, LANES), axis=-1, keepdims=True)
        e2 = jnp.exp(v2 - v1)
        p1 = 1.0 / (1.0 + e2)
        p2 = e2 / (1.0 + e2)
        gate_ref[0] = jnp.where(lane == i1, p1, 0.0) + jnp.where(lane == i2, p2, 0.0)


def _outproj(o_a, o_d, proj, x, wo, pool_w, pool_scale, conv_w, nw, mods, mod_row, router=None):
    b, s, d = x.shape
    tm = min(s, 256)
    moe = router is not None
    row = mod_row
    nh = s // HALO
    th = tm // HALO

    def mspec(k):
        return pl.BlockSpec((1, 1, 1, d), lambda bi, m: (row(bi), k, 0, 0))

    gspec = pl.BlockSpec((1, tm, GROUP_W), lambda bi, m: (bi, m, 0))
    xspec = pl.BlockSpec((1, tm, d), lambda bi, m: (bi, m, 0))
    in_specs = [
        gspec, gspec,
        pl.BlockSpec((1, tm, 4 * GROUP_W), lambda bi, m: (bi, m, 0)),
        pl.BlockSpec((1, HALO, 4 * GROUP_W), lambda bi, m: (bi, jnp.maximum(m * th - 1, 0), 0)),
        pl.BlockSpec((1, HALO, 4 * GROUP_W), lambda bi, m: (bi, jnp.minimum((m + 1) * th, nh - 1), 0)),
        pl.BlockSpec((4, POOL_CH, POOL_CH), lambda bi, m: (0, 0, 0)),
        pl.BlockSpec((1, GROUP_W), lambda bi, m: (0, 0)),
        pl.BlockSpec((3, GROUP_W), lambda bi, m: (0, 0)),
        pl.BlockSpec((d, d), lambda bi, m: (0, 0)),
        xspec, mspec(2),
        pl.BlockSpec((1, d), lambda bi, m: (0, 0)),
        mspec(3), mspec(4),
    ]
    args = [o_a, o_d, proj, proj, proj, pool_w.astype(BF16), pool_scale.reshape(1, GROUP_W), conv_w, wo,
            x, mods, nw.reshape(1, d), mods, mods]
    out_shape = [jax.ShapeDtypeStruct((b, s, d), F32), jax.ShapeDtypeStruct((b, s, d), BF16)]
    out_specs = [xspec, xspec]
    if moe:
        in_specs.append(pl.BlockSpec((d, LANES), lambda bi, m: (0, 0)))
        args.append(jnp.pad(router, ((0, 0), (0, LANES - router.shape[1]))))
        out_shape.append(jax.ShapeDtypeStruct((b, s, LANES), F32))
        out_specs.append(pl.BlockSpec((1, tm, LANES), lambda bi, m: (bi, m, 0)))
    return pl.pallas_call(
        functools.partial(_outproj_kernel, tm=tm, moe=moe),
        grid=(b, s // tm),
        in_specs=in_specs,
        out_specs=out_specs,
        out_shape=out_shape,
        scratch_shapes=[pltpu.VMEM((tm + 2 * HALO, GROUP_W), F32), pltpu.VMEM((tm + 2 * HALO, GROUP_W), F32)],
        compiler_params=_cparams(("arbitrary", "arbitrary")),
        name="outproj",
    )(*args)


def _swiglu_step(h, wg_ref, wu_ref, wd_ref, scale):
    a = _dot(h, wg_ref[0].astype(BF16))
    u = _dot(h, wu_ref[0].astype(BF16))
    act = a * (1.0 / (1.0 + jnp.exp(-a))) * u
    if scale is not None:
        act = act * scale
    return _dot(act.astype(BF16), wd_ref[0].astype(BF16))


def _ffn_kernel(h_ref, wg_ref, wu_ref, wd_ref, o_ref, acc_ref):
    f = pl.program_id(1)

    @pl.when(f == 0)
    def _():
        acc_ref[...] = jnp.zeros(acc_ref.shape, F32)

    acc_ref[...] += _swiglu_step(h_ref[...], wg_ref, wu_ref, wd_ref, None)

    @pl.when(f == pl.num_programs(1) - 1)
    def _():
        o_ref[...] = acc_ref[...].astype(BF16)


def _ffn(h, wg, wu, wd, layer):
    mt, d = h.shape
    ff = wg.shape[2]
    tm = min(mt, 1024)
    tf = 256
    return pl.pallas_call(
        _ffn_kernel,
        grid=(mt // tm, ff // tf),
        in_specs=[
            pl.BlockSpec((tm, d), lambda m, f: (m, 0)),
            pl.BlockSpec((1, d, tf), lambda m, f: (layer, 0, f)),
            pl.BlockSpec((1, d, tf), lambda m, f: (layer, 0, f)),
            pl.BlockSpec((1, tf, d), lambda m, f: (layer, f, 0)),
        ],
        out_specs=pl.BlockSpec((tm, d), lambda m, f: (m, 0)),
        out_shape=jax.ShapeDtypeStruct((mt, d), BF16),
        scratch_shapes=[pltpu.VMEM((tm, d), F32)],
        compiler_params=_cparams(("arbitrary", "arbitrary")),
        name="ffn",
    )(h, wg, wu, wd)


def _moe_dense_kernel(h_ref, gate_ref, wg_ref, wu_ref, wd_ref, o_ref, acc_ref):
    e = pl.program_id(1)
    f = pl.program_id(2)

    @pl.when((e == 0) & (f == 0))
    def _():
        acc_ref[...] = jnp.zeros(acc_ref.shape, F32)

    gate = gate_ref[...]
    lane = lax.broadcasted_iota(jnp.int32, gate.shape, 1)
    ge = jnp.sum(jnp.where(lane == e, gate, 0.0), axis=-1, keepdims=True)
    acc_ref[...] += _swiglu_step(h_ref[...], wg_ref, wu_ref, wd_ref, ge)

    @pl.when((e == pl.num_programs(1) - 1) & (f == pl.num_programs(2) - 1))
    def _():
        o_ref[...] = acc_ref[...].astype(BF16)


def _moe_dense(h, gate, wg, wu, wd, layer):
    mt, d = h.shape
    _, ne, _, ff = wg.shape
    wg = wg.reshape(-1, d, ff)
    wu = wu.reshape(-1, d, ff)
    wd = wd.reshape(-1, ff, d)
    e0 = layer * ne
    tm = min(mt, 1024)
    tf = 256
    return pl.pallas_call(
        _moe_dense_kernel,
        grid=(mt // tm, ne, ff // tf),
        in_specs=[
            pl.BlockSpec((tm, d), lambda m, e, f: (m, 0)),
            pl.BlockSpec((tm, LANES), lambda m, e, f: (m, 0)),
            pl.BlockSpec((1, d, tf), lambda m, e, f: (e0 + e, 0, f)),
            pl.BlockSpec((1, d, tf), lambda m, e, f: (e0 + e, 0, f)),
            pl.BlockSpec((1, tf, d), lambda m, e, f: (e0 + e, f, 0)),
        ],
        out_specs=pl.BlockSpec((tm, d), lambda m, e, f: (m, 0)),
        out_shape=jax.ShapeDtypeStruct((mt, d), BF16),
        scratch_shapes=[pltpu.VMEM((tm, d), F32)],
        compiler_params=_cparams(("arbitrary", "arbitrary", "arbitrary")),
        name="moe",
    )(h, gate, wg, wu, wd)


def _final_kernel(x_ref, f_ref, g_ref, nw_ref, o_ref):
    xv = x_ref[0] + g_ref[0, 0] * f_ref[0].astype(F32)
    ms = jnp.mean(xv * xv, axis=-1, keepdims=True)
    o_ref[0] = xv * lax.rsqrt(ms + EPS) * nw_ref[...]


def _final(x, f, mods, nw):
    b, s, d = x.shape
    tm = min(s, 512)
    xspec = pl.BlockSpec((1, tm, d), lambda bi, m: (bi, m, 0))
    return pl.pallas_call(
        _final_kernel,
        grid=(b, s // tm),
        in_specs=[xspec, xspec,
                  pl.BlockSpec((1, 1, 1, d), lambda bi, m: (bi, 5, 0, 0)),
                  pl.BlockSpec((1, d), lambda bi, m: (0, 0))],
        out_specs=xspec,
        out_shape=jax.ShapeDtypeStruct((b, s, d), F32),
        compiler_params=_cparams(("arbitrary", "arbitrary")),
        name="final",
    )(x, f, mods, nw.reshape(1, d))


def _prep_w_in(w):
    scale = HEAD_DIM ** -0.5

    def dup(t):
        return jnp.concatenate([t[:, :HEAD_DIM], t[:, :HEAD_DIM], t[:, HEAD_DIM:], t[:, HEAD_DIM:]], axis=1)

    g = GROUP_W
    kd = w[:, 8 * g:8 * g + 2 * HEAD_DIM]
    vd = w[:, 8 * g + 2 * HEAD_DIM:]
    return jnp.concatenate(
        [w[:, 3 * g:7 * g], w[:, 0:g] * scale, w[:, g:3 * g], w[:, 7 * g:8 * g] * scale, dup(kd), dup(vd)],
        axis=1).astype(BF16)


def _rope_tables(n):
    t = jnp.arange(n, dtype=jnp.int32)
    rows = (t // GRID_W).astype(F32)
    cols = (t % GRID_W).astype(F32)
    nf = HEAD_DIM // 4
    inv = ROPE_BASE ** (-jnp.arange(nf, dtype=F32) / nf)
    ang = jnp.concatenate([rows[:, None] * inv, cols[:, None] * inv], axis=-1)
    cos = jnp.cos(ang)
    sin = jnp.sin(ang)
    cos_t = jnp.concatenate([cos, cos, cos, cos], axis=-1)
    sin_t = jnp.concatenate([-sin, sin, -sin, sin], axis=-1)
    return cos_t, sin_t


def kernel(x, c, ctx, c_ctx, w_ada, b_ada, norm_mix, norm_ffn, norm_final, w_in, w_out, na_rpb, pool_w, pool_scale, conv_w, swa_sink, ffn_w_gate, ffn_w_up, ffn_w_down, moe_router, moe_w_gate, moe_w_up, moe_w_down):
    b, s, d = x.shape
    lc = ctx.shape[1]
    depth = w_ada.shape[0]
    cond = jnp.concatenate([c, c_ctx[None, :]], axis=0)
    assert b == 2
    mods_all = _ada(cond, w_ada, b_ada)
    cos_t, sin_t = _rope_tables(s)
    cos_c = jnp.ones((lc, LANES), F32)
    sin_c = jnp.zeros((lc, LANES), F32)
    row_x = lambda bi: bi
    row_c = lambda bi: b

    cx = ctx
    fx = fc = None
    for i in range(depth):
        last = i == depth - 1
        mods = mods_all[i].reshape(8, 6, 1, d)
        wi = _prep_w_in(w_in[i])
        wo = w_out[i].astype(BF16)
        if i == 0:
            proj, _ = _inproj(x, wi, norm_mix[i], mods, row_x, cos_t, sin_t)
            projc, _ = _inproj(cx, wi, norm_mix[i], mods, row_c, cos_c, sin_c)
        else:
            mods_prev = mods_all[i - 1].reshape(8, 6, 1, d)
            proj, x = _inproj_res(x, fx, mods_prev, wi, norm_mix[i], mods, row_x, cos_t, sin_t)
            projc, cx = _inproj_res(cx, fc, mods_prev, wi, norm_mix[i], mods, row_c, cos_c, sin_c)
        bias = _na_bias(na_rpb[i])
        o_a = _na_attn(proj, projc, bias)
        o_d = _swa_attn(proj, projc, swa_sink[i])
        moe = i % 2 == 1
        router = moe_router[i // 2] if moe else None
        outs = _outproj(o_a, o_d, proj, x, wo, pool_w[i], pool_scale[i], conv_w[i], norm_ffn[i], mods, row_x,
                        router)
        x, h2 = outs[0], outs[1]
        if not last:
            oc_a, oc_d = _ctx_attn(projc, swa_sink[i])
            cx, h2c = _outproj(oc_a, oc_d, projc, cx, wo, pool_w[i], pool_scale[i], conv_w[i], norm_ffn[i],
                               mods, row_c)
        if moe:
            j = i // 2
            fx = _moe_dense(h2.reshape(b * s, d), outs[2].reshape(b * s, LANES),
                            moe_w_gate, moe_w_up, moe_w_down, j).reshape(b, s, d)
        else:
            j = i // 2
            fx = _ffn(h2.reshape(b * s, d), ffn_w_gate, ffn_w_up, ffn_w_down, j).reshape(b, s, d)
        if not last:
            if moe:
                raise NotImplementedError("context tokens through an expert layer")
            fc = _ffn(h2c.reshape(b * lc, d), ffn_w_gate, ffn_w_up, ffn_w_down, j).reshape(b, lc, d)
    return _final(x, fx, mods_all[depth - 1].reshape(8, 6, 1, d), norm_final)


def _inproj_res(x, f, mods_prev, w, nw, mods, mod_row, cos_t, sin_t):
    both = jnp.concatenate([mods, mods_prev], axis=1)
    return _inproj(x, w, nw, both, mod_row, cos_t, sin_t, res=(f, 6 + 5))
```

```python
import functools

import jax
import jax.numpy as jnp
from jax import lax
from jax.experimental import pallas as pl
from jax.experimental.pallas import tpu as pltpu
from jax.experimental.pallas import tpu_sc as plsc

F32 = jnp.float32
BF16 = jnp.bfloat16

D_MODEL = 2048
GRID_W = 64
HEAD_DIM = 64
EPS = 1e-6
NEG_INF = -1e30
GROUP_W = 512
NA_KH = 8
NA_KW = 16
POOL_WINDOWS = (2, 4, 8, 16)
POOL_CH = 128
SWA_WINDOW = 128
SWA_BLOCK = 128
ROPE_BASE = 10000.0
N_EXPERTS = 8
LANES = 128
HALO = 16

C_UBC = 0
C_QA = 2048
C_KA = 2560
C_VA = 3072
C_QD = 3584
C_KD = 4096
C_VD = 4352
D_PROJ = 4608
TN_PROJ = 512

VMEM_LIMIT = 56 * 1024 * 1024


def _cparams(sem):
    return pltpu.CompilerParams(dimension_semantics=sem, vmem_limit_bytes=VMEM_LIMIT)


def _dot(a, b):
    return jnp.dot(a, b, preferred_element_type=F32)


def _dot_t(a, b):
    return lax.dot_general(a, b, (((1,), (1,)), ((), ())), preferred_element_type=F32)


def _rms_mod(xv, nw, sh, sc):
    ms = jnp.mean(xv * xv, axis=-1, keepdims=True)
    y = xv * lax.rsqrt(ms + EPS) * nw
    return y * (1.0 + sc) + sh


PACK_PLANES = 4
PLANE_W = D_MODEL // 2 // PACK_PLANES
SC_WINDOW = 128


def _pack_bf16_pairs(v):
    half = v.shape[1] // 2
    hi = lax.bitcast_convert_type(v[:, :half].astype(BF16).astype(F32), jnp.uint32)
    lo = lax.bitcast_convert_type(v[:, half:].astype(BF16).astype(F32), jnp.uint32)
    return hi | (lo >> 16)


def _unpack_hi(w):
    return lax.bitcast_convert_type(w & jnp.uint32(0xFFFF0000), F32)


def _unpack_lo(w):
    return lax.bitcast_convert_type(w << 16, F32)


def _ada_kernel(c_ref, w_ref, b_ref, o_ref):
    tn = w_ref.shape[2]
    o_ref[...] = jnp.zeros(o_ref.shape, F32)
    for r in range(3):
        cv = c_ref[r]
        m = cv * (1.0 / (1.0 + jnp.exp(-cv)))
        cols = []
        for j in range(tn // LANES):
            wj = w_ref[0, :, j * LANES:(j + 1) * LANES]
            cols.append(jnp.sum(wj * m, axis=0, keepdims=True))
        o_ref[0, r:r + 1, :] = jnp.concatenate(cols, axis=1) + b_ref[0]


def _ada(cond, w_ada, b_ada):
    depth, d, n6 = w_ada.shape
    tn = 1024
    cb = jnp.broadcast_to(cond[:, :, None], (3, d, LANES))
    return pl.pallas_call(
        _ada_kernel,
        grid=(depth, n6 // tn),
        in_specs=[
            pl.BlockSpec((3, d, LANES), lambda i, j: (0, 0, 0)),
            pl.BlockSpec((1, d, tn), lambda i, j: (i, 0, j)),
            pl.BlockSpec((1, 1, tn), lambda i, j: (i, 0, j)),
        ],
        out_specs=pl.BlockSpec((1, 8, tn), lambda i, j: (i, 0, j)),
        out_shape=jax.ShapeDtypeStruct((depth, 8, n6), F32),
        compiler_params=_cparams(("arbitrary", "arbitrary")),
        name="ada",
    )(cb, w_ada, b_ada.reshape(depth, 1, n6))


def _rope(a, cosv, sinv, lo32):
    sw = jnp.where(lo32, pltpu.roll(a, 96, 1), pltpu.roll(a, 32, 1))
    return a * cosv + sw * sinv


def _inproj_kernel(*refs, has_res):
    if has_res:
        (x_ref, f_ref, g_ref, nw_ref, sh_ref, sc_ref, cos_ref, sin_ref, w_ref,
         o_ref, x2_ref, h_scr) = refs
    else:
        x_ref, nw_ref, sh_ref, sc_ref, cos_ref, sin_ref, w_ref, o_ref, h_scr = refs
    n = pl.program_id(2)

    @pl.when(n == 0)
    def _():
        xv = x_ref[0]
        if has_res:
            xv = xv + g_ref[0, 0] * f_ref[0].astype(F32)
            x2_ref[0] = xv
        h_scr[...] = _rms_mod(xv, nw_ref[...], sh_ref[0, 0], sc_ref[0, 0]).astype(BF16)

    acc = _dot(h_scr[...], w_ref[...])
    rope_q = C_QD // TN_PROJ
    rope_kv = C_KD // TN_PROJ

    @pl.when(n < rope_q)
    def _():
        o_ref[0] = acc.astype(BF16)

    def rope_groups(n_rope):
        lane = lax.broadcasted_iota(jnp.int32, (acc.shape[0], LANES), 1)
        lo32 = (lane % HEAD_DIM) < (HEAD_DIM // 2)
        cosv = cos_ref[...]
        sinv = sin_ref[...]
        for g in range(TN_PROJ // LANES):
            a = acc[:, g * LANES:(g + 1) * LANES]
            if g < n_rope:
                a = _rope(a, cosv, sinv, lo32)
            o_ref[0, :, g * LANES:(g + 1) * LANES] = a.astype(BF16)

    @pl.when(n == rope_q)
    def _():
        rope_groups(4)

    @pl.when(n == rope_kv)
    def _():
        rope_groups(2)


def _inproj(x, w, nw, mods, mod_row, cos_t, sin_t, res=None):
    b, s, d = x.shape
    tm = min(s, 512)
    has_res = res is not None
    row = mod_row
    xspec = pl.BlockSpec((1, tm, d), lambda bi, m, n: (bi, m, 0))

    def mspec(k):
        return pl.BlockSpec((1, 1, 1, d), lambda bi, m, n: (row(bi), k, 0, 0))

    in_specs = [xspec]
    args = [x]
    if has_res:
        f, gk = res
        in_specs += [xspec, mspec(gk)]
        args += [f, mods]
    in_specs += [
        pl.BlockSpec((1, d), lambda bi, m, n: (0, 0)),
        mspec(0), mspec(1),
        pl.BlockSpec((tm, LANES), lambda bi, m, n: (m, 0)),
        pl.BlockSpec((tm, LANES), lambda bi, m, n: (m, 0)),
        pl.BlockSpec((d, TN_PROJ), lambda bi, m, n: (0, n)),
    ]
    args += [nw.reshape(1, d), mods, mods, cos_t, sin_t, w]
    out_shape = [jax.ShapeDtypeStruct((b, s, D_PROJ), BF16)]
    out_specs = [pl.BlockSpec((1, tm, TN_PROJ), lambda bi, m, n: (bi, m, n))]
    if has_res:
        out_shape.append(jax.ShapeDtypeStruct((b, s, d), F32))
        out_specs.append(xspec)
    outs = pl.pallas_call(
        functools.partial(_inproj_kernel, has_res=has_res),
        grid=(b, s // tm, D_PROJ // TN_PROJ),
        in_specs=in_specs,
        out_specs=out_specs,
        out_shape=out_shape,
        scratch_shapes=[pltpu.VMEM((tm, d), BF16)],
        compiler_params=_cparams(("arbitrary", "arbitrary", "arbitrary")),
        name="inproj",
    )(*args)
    return outs if has_res else (outs[0], None)


def _na_bias_kernel(rpb_ref, o_ref):
    h = pl.program_id(0)
    nd = 2 * NA_KH - 1
    nj = 2 * NA_KW - 1
    q = lax.broadcasted_iota(jnp.int32, (GRID_W, LANES), 0)
    lane = lax.broadcasted_iota(jnp.int32, (GRID_W, LANES), 1)
    kw = lane % GRID_W
    dc = jnp.clip(kw - q, -(NA_KW - 1), NA_KW - 1) + (NA_KW - 1)
    c0 = jnp.clip(q - NA_KW // 2, 0, GRID_W - NA_KW)
    valid = (kw >= c0) & (kw < c0 + NA_KW)
    tabs = []
    for d in range(nd):
        t = jnp.zeros((GRID_W, LANES), F32)
        for j in range(nj):
            t = jnp.where(dc == j, rpb_ref[h * (nd * nj) + d * nj + j], t)
        tabs.append(jnp.where(valid, t, NEG_INF))
    for c in range(NA_KH):
        for g in range(NA_KH // 2):
            d_lo = 2 * g - c + (NA_KH - 1)
            o_ref[c, 0, :, g * LANES:(g + 1) * LANES] = jnp.where(lane < GRID_W, tabs[d_lo], tabs[d_lo + 1])


def _na_bias(rpb):
    nh = rpb.shape[0]
    return pl.pallas_call(
        _na_bias_kernel,
        grid=(nh,),
        in_specs=[pl.BlockSpec(memory_space=pltpu.SMEM)],
        out_specs=pl.BlockSpec((NA_KH, 1, GRID_W, NA_KH * GRID_W), lambda h: (0, h, 0, 0)),
        out_shape=jax.ShapeDtypeStruct((NA_KH, nh, GRID_W, NA_KH * GRID_W), F32),
        compiler_params=_cparams(("arbitrary",)),
        name="na_bias",
    )(rpb.reshape(-1))


def _softmax_pv(s_parts, v_parts, extra=None):
    m = jnp.max(s_parts[0], axis=-1, keepdims=True)
    for s in s_parts[1:]:
        m = jnp.maximum(m, jnp.max(s, axis=-1, keepdims=True))
    if extra is not None:
        m = jnp.maximum(m, extra)
    den = None
    o = None
    for s, v in zip(s_parts, v_parts):
        p = jnp.exp(s - m)
        ps = jnp.sum(p, axis=-1, keepdims=True)
        den = ps if den is None else den + ps
        pv = _dot(p.astype(BF16), v)
        o = pv if o is None else o + pv
    if extra is not None:
        den = den + jnp.exp(extra - m)
    return o / den


ROWS_PER_STEP = 8


def _na_kernel(q_ref, k_ref, v_ref, kc_ref, vc_ref, bias_ref, o_ref, *, rows):
    rb = pl.program_id(2)
    lane = lax.broadcasted_iota(jnp.int32, (GRID_W, LANES), 1)
    lo = lane < HEAD_DIM
    kc = kc_ref[0]
    vc = vc_ref[0]
    kh = min(NA_KH, rows)
    win = kh * GRID_W

    def row_body(i, carry):
        r = rb * ROWS_PER_STEP + i
        r0 = jnp.clip(r - kh // 2, 0, rows - kh)
        c = r - r0
        qoff = pl.multiple_of(i * GRID_W, GRID_W)
        koff = pl.multiple_of(r0 * GRID_W, GRID_W)
        q = q_ref[0, pl.ds(qoff, GRID_W), :]
        kw = k_ref[0, pl.ds(koff, win), :]
        vw = v_ref[0, pl.ds(koff, win), :]
        outs = []
        for hh in range(2):
            qm = jnp.where(lo if hh == 0 else jnp.logical_not(lo), q, jnp.zeros_like(q))
            s_loc = _dot_t(qm, kw) + bias_ref[c, hh]
            s_ctx = _dot_t(qm, kc)
            outs.append(_softmax_pv([s_loc, s_ctx], [vw, vc]))
        o_ref[0, pl.ds(qoff, GRID_W), :] = jnp.where(lo, outs[0], outs[1]).astype(BF16)
        return carry

    lax.fori_loop(0, ROWS_PER_STEP, row_body, 0)


def _na_attn(proj, projc, bias):
    b, s, _ = proj.shape
    lc = projc.shape[1]
    rows = s // GRID_W
    assert rows >= NA_KH and rows % ROWS_PER_STEP == 0
    tq = ROWS_PER_STEP * GRID_W
    npair = GROUP_W // LANES
    return pl.pallas_call(
        functools.partial(_na_kernel, rows=rows),
        grid=(b, npair, rows // ROWS_PER_STEP),
        in_specs=[
            pl.BlockSpec((1, tq, LANES), lambda bi, p, r: (bi, r, C_QA // LANES + p)),
            pl.BlockSpec((1, s, LANES), lambda bi, p, r: (bi, 0, C_KA // LANES + p)),
            pl.BlockSpec((1, s, LANES), lambda bi, p, r: (bi, 0, C_VA // LANES + p)),
            pl.BlockSpec((1, lc, LANES), lambda bi, p, r: (bi, 0, C_KA // LANES + p)),
            pl.BlockSpec((1, lc, LANES), lambda bi, p, r: (bi, 0, C_VA // LANES + p)),
            pl.BlockSpec((NA_KH, 2, GRID_W, NA_KH * GRID_W), lambda bi, p, r: (0, p, 0, 0)),
        ],
        out_specs=pl.BlockSpec((1, tq, LANES), lambda bi, p, r: (bi, r, p)),
        out_shape=jax.ShapeDtypeStruct((b, s, GROUP_W), BF16),
        compiler_params=_cparams(("arbitrary", "arbitrary", "arbitrary")),
        name="na_attn",
    )(proj, proj, proj, projc, projc, bias)


def _gqa_group(q, kv, sink_ref, k_parts, v_parts, mask):
    m = q.shape[0]
    lane = lax.broadcasted_iota(jnp.int32, (m, LANES), 1)
    lo = lane < HEAD_DIM
    qs = []
    for j in range(2):
        qg = q[:, (kv * 2 + j) * LANES:(kv * 2 + j + 1) * LANES]
        qs.append(jnp.where(lo, qg, jnp.zeros_like(qg)))
        qs.append(jnp.where(lo, jnp.zeros_like(qg), qg))
    qq = jnp.concatenate(qs, axis=0)
    s_all = [_dot_t(qq, k) for k in k_parts]
    heads = []
    for i in range(4):
        parts = [s[i * m:(i + 1) * m] for s in s_all]
        if mask is not None:
            parts[0] = jnp.where(mask, parts[0], NEG_INF)
        extra = None
        if sink_ref is not None:
            extra = jnp.full((m, 1), sink_ref[kv * 4 + i], F32)
        heads.append(_softmax_pv(parts, v_parts, extra))
    return [jnp.where(lo, heads[0], heads[1]), jnp.where(lo, heads[2], heads[3])]


def _swa_kernel(sink_ref, q_ref, k_ref, v_ref, kc_ref, vc_ref, o_ref):
    n = pl.program_id(1)
    nb = pl.num_programs(1)
    start = jnp.clip(n - 1, 0, nb - 3)
    koff = pl.multiple_of(start * SWA_BLOCK, SWA_BLOCK)
    kw = k_ref[0, pl.ds(koff, 3 * SWA_BLOCK), :]
    vw = v_ref[0, pl.ds(koff, 3 * SWA_BLOCK), :]
    kc = kc_ref[0]
    vc = vc_ref[0]
    rel = (n - start) * SWA_BLOCK + lax.broadcasted_iota(jnp.int32, (SWA_BLOCK, 3 * SWA_BLOCK), 0)
    kj = lax.broadcasted_iota(jnp.int32, (SWA_BLOCK, 3 * SWA_BLOCK), 1)
    valid = jnp.abs(rel - kj) <= SWA_WINDOW
    q = q_ref[0]
    for kv in range(2):
        sl = slice(kv * LANES, (kv + 1) * LANES)
        outs = _gqa_group(q, kv, sink_ref, [kw[:, sl], kc[:, sl]], [vw[:, sl], vc[:, sl]], valid)
        for j in range(2):
            g = kv * 2 + j
            o_ref[0, :, g * LANES:(g + 1) * LANES] = outs[j].astype(BF16)


def _swa_attn(proj, projc, sink):
    b, s, _ = proj.shape
    lc = projc.shape[1]
    nb = s // SWA_BLOCK
    assert nb >= 3
    w2 = 2 * LANES
    return pl.pallas_call(
        _swa_kernel,
        grid=(b, nb),
        in_specs=[
            pl.BlockSpec(memory_space=pltpu.SMEM),
            pl.BlockSpec((1, SWA_BLOCK, GROUP_W), lambda bi, n: (bi, n, C_QD // GROUP_W)),
            pl.BlockSpec((1, s, w2), lambda bi, n: (bi, 0, C_KD // w2)),
            pl.BlockSpec((1, s, w2), lambda bi, n: (bi, 0, C_VD // w2)),
            pl.BlockSpec((1, lc, w2), lambda bi, n: (bi, 0, C_KD // w2)),
            pl.BlockSpec((1, lc, w2), lambda bi, n: (bi, 0, C_VD // w2)),
        ],
        out_specs=pl.BlockSpec((1, SWA_BLOCK, GROUP_W), lambda bi, n: (bi, n, 0)),
        out_shape=jax.ShapeDtypeStruct((b, s, GROUP_W), BF16),
        compiler_params=_cparams(("arbitrary", "arbitrary")),
        name="swa_attn",
    )(sink.reshape(-1), proj, proj, proj, projc, projc)


def _ctx_attn_kernel(sink_ref, qa_ref, ka_ref, va_ref, qd_ref, kd_ref, vd_ref, oa_ref, od_ref):
    m = qa_ref.shape[1]
    lane = lax.broadcasted_iota(jnp.int32, (m, LANES), 1)
    lo = lane < HEAD_DIM
    for p in range(GROUP_W // LANES):
        sl = slice(p * LANES, (p + 1) * LANES)
        q = qa_ref[0, :, sl]
        k = ka_ref[0, :, sl]
        v = va_ref[0, :, sl]
        outs = []
        for hh in range(2):
            qm = jnp.where(lo if hh == 0 else jnp.logical_not(lo), q, jnp.zeros_like(q))
            outs.append(_softmax_pv([_dot_t(qm, k)], [v]))
        oa_ref[0, :, sl] = jnp.where(lo, outs[0], outs[1]).astype(BF16)
    q = qd_ref[0]
    for kv in range(2):
        sl = slice(kv * LANES, (kv + 1) * LANES)
        outs = _gqa_group(q, kv, sink_ref, [kd_ref[0, :, sl]], [vd_ref[0, :, sl]], None)
        for j in range(2):
            g = kv * 2 + j
            od_ref[0, :, g * LANES:(g + 1) * LANES] = outs[j].astype(BF16)


def _ctx_attn(projc, sink):
    b, lc, _ = projc.shape
    w2 = 2 * LANES

    def spec(width, col):
        return pl.BlockSpec((1, lc, width), lambda bi: (bi, 0, col // width))

    ospec = pl.BlockSpec((1, lc, GROUP_W), lambda bi: (bi, 0, 0))
    return pl.pallas_call(
        _ctx_attn_kernel,
        grid=(b,),
        in_specs=[pl.BlockSpec(memory_space=pltpu.SMEM),
                  spec(GROUP_W, C_QA), spec(GROUP_W, C_KA), spec(GROUP_W, C_VA),
                  spec(GROUP_W, C_QD), spec(w2, C_KD), spec(w2, C_VD)],
        out_specs=[ospec, ospec],
        out_shape=[jax.ShapeDtypeStruct((b, lc, GROUP_W), BF16)] * 2,
        compiler_params=_cparams(("arbitrary",)),
        name="ctx_attn",
    )(sink.reshape(-1), projc, projc, projc, projc, projc, projc)


def _outproj_kernel(*refs, tm, moe):
    (oa_ref, od_ref, u_ref, up_ref, un_ref, pw_ref, ps_ref, cw_ref, wo_ref, x_ref, g1_ref,
     nw_ref, sh_ref, sc_ref) = refs[:14]
    rest = refs[14:]
    if moe:
        rt_ref, xo_ref, h_ref, gate_ref, ext_ub, ext_u = rest
    else:
        xo_ref, h_ref, ext_ub, ext_u = rest
    m = pl.program_id(1)
    nm = pl.num_programs(1)
    g4 = GROUP_W

    def cols(ref, k):
        return ref[0, :, k * g4:(k + 1) * g4].astype(F32)

    ub = cols(u_ref, 0)
    gb = cols(u_ref, 1)
    has_prev = m > 0
    has_next = m < nm - 1
    ext_ub[0:HALO] = jnp.where(has_prev, cols(up_ref, 0), 0.0)
    ext_ub[HALO:HALO + tm] = ub
    ext_ub[HALO + tm:] = jnp.where(has_next, cols(un_ref, 0), 0.0)
    ext_u[0:HALO] = jnp.where(has_prev, cols(up_ref, 2) * cols(up_ref, 3), 0.0)
    ext_u[HALO:HALO + tm] = cols(u_ref, 2) * cols(u_ref, 3)
    ext_u[HALO + tm:] = jnp.where(has_next, cols(un_ref, 2) * cols(un_ref, 3), 0.0)

    n_tok = nm * tm
    t = m * tm + lax.broadcasted_iota(jnp.int32, (tm, LANES), 0)
    obs = []
    for g, w in enumerate(POOL_WINDOWS):
        sl = slice(g * LANES, (g + 1) * LANES)
        acc = None
        for d in range(-(w // 2), w - w // 2):
            term = ext_ub[HALO + d:HALO + d + tm, sl]
            acc = term if acc is None else acc + term
        cnt = (jnp.clip(t + (w - w // 2), 0, n_tok) - jnp.clip(t - w // 2, 0, n_tok)).astype(F32)
        pooled = acc / cnt - ub[:, sl]
        obs.append(_dot(pooled.astype(BF16), pw_ref[g]) * ps_ref[:, sl])
    o_b = jnp.concatenate(obs, axis=1).astype(BF16)

    y = (ext_u[HALO - 1:HALO - 1 + tm] * cw_ref[0:1, :] + ext_u[HALO:HALO + tm] * cw_ref[1:2, :]
         + ext_u[HALO + 1:HALO + 1 + tm] * cw_ref[2:3, :])
    o_c = (gb * y).astype(BF16)

    yy = (_dot(oa_ref[0], wo_ref[0:g4]) + _dot(o_b, wo_ref[g4:2 * g4])
          + _dot(o_c, wo_ref[2 * g4:3 * g4]) + _dot(od_ref[0], wo_ref[3 * g4:4 * g4]))
    xn = x_ref[0] + g1_ref[0, 0] * yy
    xo_ref[0] = xn
    h2 = _rms_mod(xn, nw_ref[...], sh_ref[0, 0], sc_ref[0, 0])

    if not moe:
        h_ref[0] = h2.astype(BF16)
    else:
        packed = _pack_bf16_pairs(h2)
        for c in range(PACK_PLANES):
            h_ref[c] = packed[:, c * PLANE_W:(c + 1) * PLANE_W]
        lane = lax.broadcasted_iota(jnp.int32, (tm, LANES), 1)
        logits = jnp.full((tm, LANES), -jnp.inf, F32)
        for e in range(N_EXPERTS):
            le = jnp.sum(h2 * rt_ref[e:e + 1, :], axis=-1, keepdims=True)
            logits = jnp.where(lane == e, le, logits)
        v1 = jnp.max(logits, axis=-1, keepdims=True)
        i1 = jnp.min(jnp.where(logits == v1, lane, LANES), axis=-1, keepdims=True)
        l2 = jnp.where(lane == i1, -jnp.inf, logits)
        v2 = jnp.max(l2, axis=-1, keepdims=True)
        i2 = jnp.min(jnp.where(l2 == v2, lane, LANES), axis=-1, keepdims=True)
        e2 = jnp.exp(v2 - v1)
        p1 = 1.0 / (1.0 + e2)
        p2 = e2 / (1.0 + e2)
        gate_ref[0, :, 0:LANES] = jnp.where(lane == i1, 1.0, 0.0)
        gate_ref[0, :, LANES:2 * LANES] = jnp.where(lane == i2, 1.0, 0.0)
        gate_ref[0, :, 2 * LANES:3 * LANES] = jnp.broadcast_to(p1, (tm, LANES))
        gate_ref[0, :, 3 * LANES:4 * LANES] = jnp.broadcast_to(p2, (tm, LANES))


def _outproj(o_a, o_d, proj, x, wo, pool_w, pool_scale, conv_w, nw, mods, mod_row, router=None):
    b, s, d = x.shape
    tm = min(s, 256)
    moe = router is not None
    row = mod_row
    nh = s // HALO
    th = tm // HALO

    def mspec(k):
        return pl.BlockSpec((1, 1, 1, d), lambda bi, m: (row(bi), k, 0, 0))

    gspec = pl.BlockSpec((1, tm, GROUP_W), lambda bi, m: (bi, m, 0))
    xspec = pl.BlockSpec((1, tm, d), lambda bi, m: (bi, m, 0))
    in_specs = [
        gspec, gspec,
        pl.BlockSpec((1, tm, 4 * GROUP_W), lambda bi, m: (bi, m, 0)),
        pl.BlockSpec((1, HALO, 4 * GROUP_W), lambda bi, m: (bi, jnp.maximum(m * th - 1, 0), 0)),
        pl.BlockSpec((1, HALO, 4 * GROUP_W), lambda bi, m: (bi, jnp.minimum((m + 1) * th, nh - 1), 0)),
        pl.BlockSpec((4, POOL_CH, POOL_CH), lambda bi, m: (0, 0, 0)),
        pl.BlockSpec((1, GROUP_W), lambda bi, m: (0, 0)),
        pl.BlockSpec((3, GROUP_W), lambda bi, m: (0, 0)),
        pl.BlockSpec((d, d), lambda bi, m: (0, 0)),
        xspec, mspec(2),
        pl.BlockSpec((1, d), lambda bi, m: (0, 0)),
        mspec(3), mspec(4),
    ]
    args = [o_a, o_d, proj, proj, proj, pool_w.astype(BF16), pool_scale.reshape(1, GROUP_W), conv_w, wo,
            x, mods, nw.reshape(1, d), mods, mods]
    if moe:
        nm = s // tm
        in_specs.append(pl.BlockSpec((N_EXPERTS, d), lambda bi, m: (0, 0)))
        args.append(router.T)
        out_shape = [jax.ShapeDtypeStruct((b, s, d), F32),
                     jax.ShapeDtypeStruct((PACK_PLANES, b * s, PLANE_W), jnp.uint32),
                     jax.ShapeDtypeStruct((b, s, 4 * LANES), F32)]
        out_specs = [xspec,
                     pl.BlockSpec((PACK_PLANES, tm, PLANE_W), lambda bi, m: (0, bi * nm + m, 0)),
                     pl.BlockSpec((1, tm, 4 * LANES), lambda bi, m: (bi, m, 0))]
    else:
        out_shape = [jax.ShapeDtypeStruct((b, s, d), F32), jax.ShapeDtypeStruct((b, s, d), BF16)]
        out_specs = [xspec, xspec]
    return pl.pallas_call(
        functools.partial(_outproj_kernel, tm=tm, moe=moe),
        grid=(b, s // tm),
        in_specs=in_specs,
        out_specs=out_specs,
        out_shape=out_shape,
        scratch_shapes=[pltpu.VMEM((tm + 2 * HALO, GROUP_W), F32), pltpu.VMEM((tm + 2 * HALO, GROUP_W), F32)],
        compiler_params=_cparams(("arbitrary", "arbitrary")),
        name="outproj",
    )(*args)


def _swiglu_step(h, wg_ref, wu_ref, wd_ref, scale):
    a = _dot(h, wg_ref[0].astype(BF16))
    u = _dot(h, wu_ref[0].astype(BF16))
    act = a * (1.0 / (1.0 + jnp.exp(-a))) * u
    if scale is not None:
        act = act * scale
    return _dot(act.astype(BF16), wd_ref[0].astype(BF16))


def _ffn_kernel(h_ref, wg_ref, wu_ref, wd_ref, o_ref, acc_ref):
    f = pl.program_id(1)

    @pl.when(f == 0)
    def _():
        acc_ref[...] = jnp.zeros(acc_ref.shape, F32)

    acc_ref[...] += _swiglu_step(h_ref[...], wg_ref, wu_ref, wd_ref, None)

    @pl.when(f == pl.num_programs(1) - 1)
    def _():
        o_ref[...] = acc_ref[...].astype(BF16)


def _ffn(h, wg, wu, wd, layer):
    mt, d = h.shape
    ff = wg.shape[2]
    tm = min(mt, 1024)
    tf = 256
    return pl.pallas_call(
        _ffn_kernel,
        grid=(mt // tm, ff // tf),
        in_specs=[
            pl.BlockSpec((tm, d), lambda m, f: (m, 0)),
            pl.BlockSpec((1, d, tf), lambda m, f: (layer, 0, f)),
            pl.BlockSpec((1, d, tf), lambda m, f: (layer, 0, f)),
            pl.BlockSpec((1, tf, d), lambda m, f: (layer, f, 0)),
        ],
        out_specs=pl.BlockSpec((tm, d), lambda m, f: (m, 0)),
        out_shape=jax.ShapeDtypeStruct((mt, d), BF16),
        scratch_shapes=[pltpu.VMEM((tm, d), F32)],
        compiler_params=_cparams(("arbitrary", "arbitrary")),
        name="ffn",
    )(h, wg, wu, wd)


def _rank_kernel(r_ref, o_ref, cnt_ref, carry):
    i = pl.program_id(0)
    tm = r_ref.shape[0]

    @pl.when(i == 0)
    def _():
        carry[...] = jnp.zeros(carry.shape, F32)

    oh1 = r_ref[:, 0:LANES]
    oh2 = r_ref[:, LANES:2 * LANES]
    sel = oh1 + oh2
    row = lax.broadcasted_iota(jnp.int32, (tm, tm), 0)
    col = lax.broadcasted_iota(jnp.int32, (tm, tm), 1)
    tri = jnp.where(col < row, 1.0, 0.0).astype(BF16)
    excl = _dot(tri, sel.astype(BF16)) + carry[0:1, :]
    lane = lax.broadcasted_iota(jnp.int32, (tm, LANES), 1)
    lane_f = lane.astype(F32)
    e1 = jnp.sum(oh1 * lane_f, axis=-1, keepdims=True)
    r1 = jnp.sum(oh1 * excl, axis=-1, keepdims=True)
    e2 = jnp.sum(oh2 * lane_f, axis=-1, keepdims=True)
    r2 = jnp.sum(oh2 * excl, axis=-1, keepdims=True)
    o_ref[...] = jnp.where(lane == 0, e1, jnp.where(lane == 1, r1, jnp.where(lane == 2, e2,
                           jnp.where(lane == 3, r2, 0.0))))
    carry[...] = carry[...] + jnp.sum(sel, axis=0, keepdims=True)
    cnt_ref[...] = carry[...]


def _rank(route):
    t = route.shape[0]
    tm = min(t, 512)
    return pl.pallas_call(
        _rank_kernel,
        grid=(t // tm,),
        in_specs=[pl.BlockSpec((tm, 2 * LANES), lambda i: (i, 0))],
        out_specs=[pl.BlockSpec((tm, LANES), lambda i: (i, 0)), pl.BlockSpec((8, LANES), lambda i: (0, 0))],
        out_shape=[jax.ShapeDtypeStruct((t, LANES), F32), jax.ShapeDtypeStruct((8, LANES), F32)],
        scratch_shapes=[pltpu.VMEM((8, LANES), F32)],
        compiler_params=_cparams(("arbitrary",)),
        name="rank",
    )(route)


def _sc_mesh():
    return plsc.VectorSubcoreMesh(core_axis_name="core", subcore_axis_name="subcore")


def _sc_scatter2(x, idx1, idx2, n_out):
    n, w = x.shape

    @functools.partial(pl.kernel, out_type=jax.ShapeDtypeStruct((n_out, w), x.dtype), mesh=_sc_mesh(),
                       scratch_types=[], name="sc_dispatch")
    def k(x_hbm, i1_hbm, i2_hbm, o_hbm):
        def body(x_vmem, i1_vmem, i2_vmem):
            pltpu.sync_copy(x_vmem, o_hbm.at[i1_vmem.at[0]])
            pltpu.sync_copy(x_vmem, o_hbm.at[i2_vmem.at[0]])

        pltpu.emit_pipeline(
            body, grid=(n // SC_WINDOW,),
            in_specs=[pl.BlockSpec((SC_WINDOW, w), lambda i: (i, 0)),
                      pl.BlockSpec((1, SC_WINDOW), lambda i: (0, i)),
                      pl.BlockSpec((1, SC_WINDOW), lambda i: (0, i))],
            out_specs=[], core_axis_name=("core", "subcore"),
            dimension_semantics=(pltpu.PARALLEL,))(x_hbm, i1_hbm, i2_hbm)

    return k(x, idx1.reshape(1, n), idx2.reshape(1, n))


def _sc_gather(y, idx):
    (n,) = idx.shape
    w = y.shape[1]

    @functools.partial(pl.kernel, out_type=jax.ShapeDtypeStruct((n, w), y.dtype), mesh=_sc_mesh(),
                       scratch_types=[], name="sc_combine")
    def k(y_hbm, i_hbm, o_hbm):
        def body(i_vmem, o_vmem):
            pltpu.sync_copy(y_hbm.at[i_vmem.at[0]], o_vmem)

        pltpu.emit_pipeline(
            body, grid=(n // SC_WINDOW,),
            in_specs=[pl.BlockSpec((1, SC_WINDOW), lambda i: (0, i))],
            out_specs=[pl.BlockSpec((SC_WINDOW, w), lambda i: (i, 0))],
            core_axis_name=("core", "subcore"),
            dimension_semantics=(pltpu.PARALLEL,))(i_hbm, o_hbm)

    return k(y, idx.reshape(1, n))


def _moe_kernel(te_ref, nu_ref, h_ref, wg_ref, wu_ref, wd_ref, o_ref, acc_ref, hb_ref):
    m = pl.program_id(0)
    f = pl.program_id(1)
    half = hb_ref.shape[1] // 2

    @pl.when(m < nu_ref[0])
    def _():
        @pl.when(f == 0)
        def _():
            acc_ref[...] = jnp.zeros(acc_ref.shape, F32)
            for c in range(PACK_PLANES):
                w = h_ref[c]
                hb_ref[:, c * PLANE_W:(c + 1) * PLANE_W] = _unpack_hi(w).astype(BF16)
                hb_ref[:, half + c * PLANE_W:half + (c + 1) * PLANE_W] = _unpack_lo(w).astype(BF16)

        acc_ref[...] += _swiglu_step(hb_ref[...], wg_ref, wu_ref, wd_ref, None)

        @pl.when(f == pl.num_programs(1) - 1)
        def _():
            packed = _pack_bf16_pairs(acc_ref[...])
            for c in range(PACK_PLANES):
                o_ref[c] = packed[:, c * PLANE_W:(c + 1) * PLANE_W]


MOE_TM = 1024


def _moe_routed(hp, tile_expert, n_used, wg, wu, wd, layer, tm):
    _, r, _ = hp.shape
    _, ne, d, ff = wg.shape
    wg = wg.reshape(-1, d, ff)
    wu = wu.reshape(-1, d, ff)
    wd = wd.reshape(-1, ff, d)
    e0 = layer * ne
    tf = 256
    nf = ff // tf

    def row_map(m, f, te, nu):
        return (0, jnp.minimum(m, nu[0] - 1), 0)

    def fidx(m, f, nu):
        return jnp.where(m < nu[0], f, nf - 1)

    hspec = pl.BlockSpec((PACK_PLANES, tm, PLANE_W), row_map)
    return pl.pallas_call(
        _moe_kernel,
        grid_spec=pltpu.PrefetchScalarGridSpec(
            num_scalar_prefetch=2,
            grid=(r // tm, nf),
            in_specs=[
                hspec,
                pl.BlockSpec((1, d, tf), lambda m, f, te, nu: (e0 + te[m], 0, fidx(m, f, nu))),
                pl.BlockSpec((1, d, tf), lambda m, f, te, nu: (e0 + te[m], 0, fidx(m, f, nu))),
                pl.BlockSpec((1, tf, d), lambda m, f, te, nu: (e0 + te[m], fidx(m, f, nu), 0)),
            ],
            out_specs=hspec,
            scratch_shapes=[pltpu.VMEM((tm, d), F32), pltpu.VMEM((tm, d), BF16)],
        ),
        out_shape=jax.ShapeDtypeStruct(hp.shape, jnp.uint32),
        compiler_params=_cparams(("arbitrary", "arbitrary")),
        name="moe",
    )(tile_expert, n_used, hp, wg, wu, wd)


def _route_plan(meta, counts, tm, n_tiles):
    e1 = meta[:, 0].astype(jnp.int32)
    r1 = meta[:, 1].astype(jnp.int32)
    e2 = meta[:, 2].astype(jnp.int32)
    r2 = meta[:, 3].astype(jnp.int32)
    cnt = counts[0, :N_EXPERTS].astype(jnp.int32)
    tiles_per = (cnt + tm - 1) // tm
    tile_end = jnp.cumsum(tiles_per)
    start_row = (tile_end - tiles_per) * tm
    pos1 = start_row[e1] + r1
    pos2 = start_row[e2] + r2
    n_used = tile_end[-1]
    tiles = jnp.arange(n_tiles, dtype=jnp.int32)
    tile_expert = jnp.sum((tiles[:, None] >= tile_end[None, :]).astype(jnp.int32), axis=1)
    last_expert = jnp.sum((n_used - 1 >= tile_end).astype(jnp.int32))
    tile_expert = jnp.minimum(tile_expert, last_expert)
    return pos1, pos2, tile_expert, n_used.reshape(1)


def _plane_rows(pos, n_rows):
    return (jnp.arange(PACK_PLANES, dtype=jnp.int32)[:, None] * n_rows + pos[None, :]).reshape(-1)


def _moe(hp, route, wg, wu, wd, layer):
    _, t, _ = hp.shape
    tm = min(MOE_TM, t)
    n_tiles = 2 * t // tm + N_EXPERTS
    n_rows = n_tiles * tm
    meta, counts = _rank(route)
    pos1, pos2, tile_expert, n_used = _route_plan(meta, counts, tm, n_tiles)
    i1 = _plane_rows(pos1, n_rows)
    i2 = _plane_rows(pos2, n_rows)
    hs = _sc_scatter2(hp.reshape(PACK_PLANES * t, PLANE_W), i1, i2, PACK_PLANES * n_rows)
    ys = _moe_routed(hs.reshape(PACK_PLANES, n_rows, PLANE_W), tile_expert, n_used, wg, wu, wd, layer, tm)
    ys = ys.reshape(PACK_PLANES * n_rows, PLANE_W)
    y1 = _sc_gather(ys, i1).reshape(PACK_PLANES, t, PLANE_W)
    y2 = _sc_gather(ys, i2).reshape(PACK_PLANES, t, PLANE_W)
    return y1, y2


def _final_kernel(x_ref, f_ref, g_ref, nw_ref, o_ref):
    xv = x_ref[0] + g_ref[0, 0] * f_ref[0].astype(F32)
    ms = jnp.mean(xv * xv, axis=-1, keepdims=True)
    o_ref[0] = xv * lax.rsqrt(ms + EPS) * nw_ref[...]


def _final(x, f, mods, nw):
    b, s, d = x.shape
    tm = min(s, 512)
    xspec = pl.BlockSpec((1, tm, d), lambda bi, m: (bi, m, 0))
    return pl.pallas_call(
        _final_kernel,
        grid=(b, s // tm),
        in_specs=[xspec, xspec,
                  pl.BlockSpec((1, 1, 1, d), lambda bi, m: (bi, 5, 0, 0)),
                  pl.BlockSpec((1, d), lambda bi, m: (0, 0))],
        out_specs=xspec,
        out_shape=jax.ShapeDtypeStruct((b, s, d), F32),
        compiler_params=_cparams(("arbitrary", "arbitrary")),
        name="final",
    )(x, f, mods, nw.reshape(1, d))


def _final_moe_kernel(x_ref, y1_ref, y2_ref, p_ref, g_ref, nw_ref, o_ref):
    tm, d = x_ref.shape[1], x_ref.shape[2]
    half = d // 2
    p = p_ref[0]
    p1 = jnp.concatenate([p[:, 0:LANES]] * (PLANE_W // LANES), axis=1)
    p2 = jnp.concatenate([p[:, LANES:2 * LANES]] * (PLANE_W // LANES), axis=1)
    ssq = jnp.zeros((tm, 1), F32)
    for c in range(PACK_PLANES):
        w1 = y1_ref[c]
        w2 = y2_ref[c]
        for unpack, off in ((_unpack_hi, 0), (_unpack_lo, half)):
            sl = slice(off + c * PLANE_W, off + (c + 1) * PLANE_W)
            f = p1 * unpack(w1) + p2 * unpack(w2)
            xv = x_ref[0, :, sl] + g_ref[0, 0, :, sl] * f
            o_ref[0, :, sl] = xv
            ssq = ssq + jnp.sum(xv * xv, axis=-1, keepdims=True)
    o_ref[0] = o_ref[0] * lax.rsqrt(ssq / d + EPS) * nw_ref[...]


def _final_moe(x, y1, y2, route, mods, nw):
    b, s, d = x.shape
    tm = min(s, 512)
    nm = s // tm
    xspec = pl.BlockSpec((1, tm, d), lambda bi, m: (bi, m, 0))
    yspec = pl.BlockSpec((PACK_PLANES, tm, PLANE_W), lambda bi, m: (0, bi * nm + m, 0))
    return pl.pallas_call(
        _final_moe_kernel,
        grid=(b, nm),
        in_specs=[xspec, yspec, yspec,
                  pl.BlockSpec((1, tm, 2 * LANES), lambda bi, m: (bi, m, 1)),
                  pl.BlockSpec((1, 1, 1, d), lambda bi, m: (bi, 5, 0, 0)),
                  pl.BlockSpec((1, d), lambda bi, m: (0, 0))],
        out_specs=xspec,
        out_shape=jax.ShapeDtypeStruct((b, s, d), F32),
        compiler_params=_cparams(("arbitrary", "arbitrary")),
        name="final",
    )(x, y1, y2, route, mods, nw.reshape(1, d))


def _prep_w_in(w):
    scale = HEAD_DIM ** -0.5

    def dup(t):
        return jnp.concatenate([t[:, :HEAD_DIM], t[:, :HEAD_DIM], t[:, HEAD_DIM:], t[:, HEAD_DIM:]], axis=1)

    g = GROUP_W
    kd = w[:, 8 * g:8 * g + 2 * HEAD_DIM]
    vd = w[:, 8 * g + 2 * HEAD_DIM:]
    return jnp.concatenate(
        [w[:, 3 * g:7 * g], w[:, 0:g] * scale, w[:, g:3 * g], w[:, 7 * g:8 * g] * scale, dup(kd), dup(vd)],
        axis=1).astype(BF16)


def _rope_tables(n):
    t = jnp.arange(n, dtype=jnp.int32)
    rows = (t // GRID_W).astype(F32)
    cols = (t % GRID_W).astype(F32)
    nf = HEAD_DIM // 4
    inv = ROPE_BASE ** (-jnp.arange(nf, dtype=F32) / nf)
    ang = jnp.concatenate([rows[:, None] * inv, cols[:, None] * inv], axis=-1)
    cos = jnp.cos(ang)
    sin = jnp.sin(ang)
    cos_t = jnp.concatenate([cos, cos, cos, cos], axis=-1)
    sin_t = jnp.concatenate([-sin, sin, -sin, sin], axis=-1)
    return cos_t, sin_t


def kernel(x, c, ctx, c_ctx, w_ada, b_ada, norm_mix, norm_ffn, norm_final, w_in, w_out, na_rpb, pool_w, pool_scale, conv_w, swa_sink, ffn_w_gate, ffn_w_up, ffn_w_down, moe_router, moe_w_gate, moe_w_up, moe_w_down):
    b, s, d = x.shape
    lc = ctx.shape[1]
    depth = w_ada.shape[0]
    cond = jnp.concatenate([c, c_ctx[None, :]], axis=0)
    assert b == 2
    mods_all = _ada(cond, w_ada, b_ada)
    cos_t, sin_t = _rope_tables(s)
    cos_c = jnp.ones((lc, LANES), F32)
    sin_c = jnp.zeros((lc, LANES), F32)
    row_x = lambda bi: bi
    row_c = lambda bi: b

    cx = ctx
    fx = fc = None
    for i in range(depth):
        last = i == depth - 1
        mods = mods_all[i].reshape(8, 6, 1, d)
        wi = _prep_w_in(w_in[i])
        wo = w_out[i].astype(BF16)
        if i == 0:
            proj, _ = _inproj(x, wi, norm_mix[i], mods, row_x, cos_t, sin_t)
            projc, _ = _inproj(cx, wi, norm_mix[i], mods, row_c, cos_c, sin_c)
        else:
            mods_prev = mods_all[i - 1].reshape(8, 6, 1, d)
            proj, x = _inproj_res(x, fx, mods_prev, wi, norm_mix[i], mods, row_x, cos_t, sin_t)
            projc, cx = _inproj_res(cx, fc, mods_prev, wi, norm_mix[i], mods, row_c, cos_c, sin_c)
        bias = _na_bias(na_rpb[i])
        o_a = _na_attn(proj, projc, bias)
        o_d = _swa_attn(proj, projc, swa_sink[i])
        moe = i % 2 == 1
        router = moe_router[i // 2] if moe else None
        outs = _outproj(o_a, o_d, proj, x, wo, pool_w[i], pool_scale[i], conv_w[i], norm_ffn[i], mods, row_x,
                        router)
        x, h2 = outs[0], outs[1]
        if not last:
            oc_a, oc_d = _ctx_attn(projc, swa_sink[i])
            cx, h2c = _outproj(oc_a, oc_d, projc, cx, wo, pool_w[i], pool_scale[i], conv_w[i], norm_ffn[i],
                               mods, row_c)
        j = i // 2
        if moe:
            if not last:
                raise NotImplementedError("an expert layer that is not the last layer")
            route = outs[2]
            y1, y2 = _moe(h2, route.reshape(b * s, 4 * LANES), moe_w_gate, moe_w_up, moe_w_down, j)
            return _final_moe(x, y1, y2, route, mods, norm_final)
        fx = _ffn(h2.reshape(b * s, d), ffn_w_gate, ffn_w_up, ffn_w_down, j).reshape(b, s, d)
        if not last:
            fc = _ffn(h2c.reshape(b * lc, d), ffn_w_gate, ffn_w_up, ffn_w_down, j).reshape(b, lc, d)
    return _final(x, fx, mods_all[depth - 1].reshape(8, 6, 1, d), norm_final)


def _inproj_res(x, f, mods_prev, w, nw, mods, mod_row, cos_t, sin_t):
    both = jnp.concatenate([mods, mods_prev], axis=1)
    return _inproj(x, w, nw, both, mod_row, cos_t, sin_t, res=(f, 6 + 5))
```

```python
import functools

import jax
import jax.numpy as jnp
from jax import lax
from jax.experimental import pallas as pl
from jax.experimental.pallas import tpu as pltpu
from jax.experimental.pallas import tpu_sc as plsc

F32 = jnp.float32
BF16 = jnp.bfloat16

D_MODEL = 2048
GRID_W = 64
HEAD_DIM = 64
EPS = 1e-6
NEG_INF = -1e30
GROUP_W = 512
NA_KH = 8
NA_KW = 16
POOL_WINDOWS = (2, 4, 8, 16)
POOL_CH = 128
SWA_WINDOW = 128
SWA_BLOCK = 128
ROPE_BASE = 10000.0
N_EXPERTS = 8
LANES = 128
HALO = 16

C_UBC = 0
C_QA = 2048
C_KA = 2560
C_VA = 3072
C_QD = 3584
C_KD = 4096
C_VD = 4352
D_PROJ = 4608
TN_PROJ = 512

VMEM_LIMIT = 56 * 1024 * 1024


def _cparams(sem):
    return pltpu.CompilerParams(dimension_semantics=sem, vmem_limit_bytes=VMEM_LIMIT)


def _dot(a, b):
    return jnp.dot(a, b, preferred_element_type=F32)


def _dot_t(a, b):
    return lax.dot_general(a, b, (((1,), (1,)), ((), ())), preferred_element_type=F32)


def _rms_mod(xv, nw, sh, sc):
    ms = jnp.mean(xv * xv, axis=-1, keepdims=True)
    y = xv * lax.rsqrt(ms + EPS) * nw
    return y * (1.0 + sc) + sh


PACK_PLANES = 4
PLANE_W = D_MODEL // 2 // PACK_PLANES
SC_WINDOW = 128


def _pack_bf16_pairs(v):
    half = v.shape[1] // 2
    hi = lax.bitcast_convert_type(v[:, :half].astype(BF16).astype(F32), jnp.uint32)
    lo = lax.bitcast_convert_type(v[:, half:].astype(BF16).astype(F32), jnp.uint32)
    return hi | (lo >> 16)


def _unpack_hi(w):
    return lax.bitcast_convert_type(w & jnp.uint32(0xFFFF0000), F32)


def _unpack_lo(w):
    return lax.bitcast_convert_type(w << 16, F32)


def _ada_kernel(c_ref, w_ref, b_ref, o_ref):
    tn = w_ref.shape[2]
    o_ref[...] = jnp.zeros(o_ref.shape, F32)
    for r in range(3):
        cv = c_ref[r]
        m = cv * (1.0 / (1.0 + jnp.exp(-cv)))
        cols = []
        for j in range(tn // LANES):
            wj = w_ref[0, :, j * LANES:(j + 1) * LANES]
            cols.append(jnp.sum(wj * m, axis=0, keepdims=True))
        o_ref[0, r:r + 1, :] = jnp.concatenate(cols, axis=1) + b_ref[0]


def _ada(cond, w_ada, b_ada):
    depth, d, n6 = w_ada.shape
    tn = 1024
    cb = jnp.broadcast_to(cond[:, :, None], (3, d, LANES))
    return pl.pallas_call(
        _ada_kernel,
        grid=(depth, n6 // tn),
        in_specs=[
            pl.BlockSpec((3, d, LANES), lambda i, j: (0, 0, 0)),
            pl.BlockSpec((1, d, tn), lambda i, j: (i, 0, j)),
            pl.BlockSpec((1, 1, tn), lambda i, j: (i, 0, j)),
        ],
        out_specs=pl.BlockSpec((1, 8, tn), lambda i, j: (i, 0, j)),
        out_shape=jax.ShapeDtypeStruct((depth, 8, n6), F32),
        compiler_params=_cparams(("arbitrary", "arbitrary")),
        name="ada",
    )(cb, w_ada, b_ada.reshape(depth, 1, n6))


def _rope(a, cosv, sinv, lo32):
    sw = jnp.where(lo32, pltpu.roll(a, 96, 1), pltpu.roll(a, 32, 1))
    return a * cosv + sw * sinv


def _inproj_kernel(*refs, has_res):
    if has_res:
        x_ref, f_ref, g_ref, nw_ref, sh_ref, sc_ref, cos_ref, sin_ref, w_ref, o_ref, x2_ref = refs
    else:
        x_ref, nw_ref, sh_ref, sc_ref, cos_ref, sin_ref, w_ref, o_ref = refs
    xv = x_ref[0]
    if has_res:
        xv = xv + g_ref[0, 0] * f_ref[0].astype(F32)
        x2_ref[0] = xv
    h = _rms_mod(xv, nw_ref[...], sh_ref[0, 0], sc_ref[0, 0]).astype(BF16)
    tm = h.shape[0]
    lane = lax.broadcasted_iota(jnp.int32, (tm, LANES), 1)
    lo32 = (lane % HEAD_DIM) < (HEAD_DIM // 2)
    rope_end = C_VD
    for n in range(D_PROJ // TN_PROJ):
        c0 = n * TN_PROJ
        acc = _dot(h, w_ref[:, c0:c0 + TN_PROJ])
        if c0 + TN_PROJ <= C_QD:
            o_ref[0, :, c0:c0 + TN_PROJ] = acc.astype(BF16)
            continue
        for g in range(TN_PROJ // LANES):
            a = acc[:, g * LANES:(g + 1) * LANES]
            if c0 + g * LANES < rope_end:
                a = _rope(a, cos_ref[...], sin_ref[...], lo32)
            o_ref[0, :, c0 + g * LANES:c0 + (g + 1) * LANES] = a.astype(BF16)


def _inproj(x, w, nw, mods, mod_row, cos_t, sin_t, res=None):
    b, s, d = x.shape
    tm = min(s, 512)
    has_res = res is not None
    row = mod_row
    xspec = pl.BlockSpec((1, tm, d), lambda bi, m: (bi, m, 0))

    def mspec(k):
        return pl.BlockSpec((1, 1, 1, d), lambda bi, m: (row(bi), k, 0, 0))

    in_specs = [xspec]
    args = [x]
    if has_res:
        f, gk = res
        in_specs += [xspec, mspec(gk)]
        args += [f, mods]
    in_specs += [
        pl.BlockSpec((1, d), lambda bi, m: (0, 0)),
        mspec(0), mspec(1),
        pl.BlockSpec((tm, LANES), lambda bi, m: (m, 0)),
        pl.BlockSpec((tm, LANES), lambda bi, m: (m, 0)),
        pl.BlockSpec((d, D_PROJ), lambda bi, m: (0, 0), pipeline_mode=pl.Buffered(1)),
    ]
    args += [nw.reshape(1, d), mods, mods, cos_t, sin_t, w]
    out_shape = [jax.ShapeDtypeStruct((b, s, D_PROJ), BF16)]
    out_specs = [pl.BlockSpec((1, tm, D_PROJ), lambda bi, m: (bi, m, 0))]
    if has_res:
        out_shape.append(jax.ShapeDtypeStruct((b, s, d), F32))
        out_specs.append(xspec)
    outs = pl.pallas_call(
        functools.partial(_inproj_kernel, has_res=has_res),
        grid=(b, s // tm),
        in_specs=in_specs,
        out_specs=out_specs,
        out_shape=out_shape,
        compiler_params=_cparams(("arbitrary", "arbitrary")),
        name="inproj",
    )(*args)
    return outs if has_res else (outs[0], None)


def _na_bias_kernel(rpb_ref, o_ref):
    h = pl.program_id(0)
    nd = 2 * NA_KH - 1
    nj = 2 * NA_KW - 1
    q = lax.broadcasted_iota(jnp.int32, (GRID_W, LANES), 0)
    lane = lax.broadcasted_iota(jnp.int32, (GRID_W, LANES), 1)
    kw = lane % GRID_W
    dc = jnp.clip(kw - q, -(NA_KW - 1), NA_KW - 1) + (NA_KW - 1)
    c0 = jnp.clip(q - NA_KW // 2, 0, GRID_W - NA_KW)
    valid = (kw >= c0) & (kw < c0 + NA_KW)
    tabs = []
    for d in range(nd):
        t = jnp.zeros((GRID_W, LANES), F32)
        for j in range(nj):
            t = jnp.where(dc == j, rpb_ref[h * (nd * nj) + d * nj + j], t)
        tabs.append(jnp.where(valid, t, NEG_INF))
    for c in range(NA_KH):
        for g in range(NA_KH // 2):
            d_lo = 2 * g - c + (NA_KH - 1)
            o_ref[c, 0, :, g * LANES:(g + 1) * LANES] = jnp.where(lane < GRID_W, tabs[d_lo], tabs[d_lo + 1])


def _na_bias(rpb):
    nh = rpb.shape[0]
    return pl.pallas_call(
        _na_bias_kernel,
        grid=(nh,),
        in_specs=[pl.BlockSpec(memory_space=pltpu.SMEM)],
        out_specs=pl.BlockSpec((NA_KH, 1, GRID_W, NA_KH * GRID_W), lambda h: (0, h, 0, 0)),
        out_shape=jax.ShapeDtypeStruct((NA_KH, nh, GRID_W, NA_KH * GRID_W), F32),
        compiler_params=_cparams(("arbitrary",)),
        name="na_bias",
    )(rpb.reshape(-1))


def _softmax_pv_staged(chains):
    add = lambda a, b: a + b
    ms = []
    for sp, _, extra in chains:
        m = functools.reduce(jnp.maximum, [jnp.max(s, axis=-1, keepdims=True) for s in sp])
        ms.append(m if extra is None else jnp.maximum(m, extra))
    ps = [[jnp.exp(s - m) for s in sp] for (sp, _, _), m in zip(chains, ms)]
    dens = []
    for pp, (_, _, extra), m in zip(ps, chains, ms):
        den = functools.reduce(add, [jnp.sum(p, axis=-1, keepdims=True) for p in pp])
        dens.append(den if extra is None else den + jnp.exp(extra - m))
    outs = [functools.reduce(add, [_dot(p.astype(BF16), v) for p, v in zip(pp, vp)])
            for pp, (_, vp, _) in zip(ps, chains)]
    return [o / d for o, d in zip(outs, dens)]


ROWS_PER_STEP = 8
NA_ROW_GROUP = 4


def _na_kernel(q_ref, k_ref, v_ref, kc_ref, vc_ref, bias_ref, o_ref, *, rows):
    rb = pl.program_id(2)
    lane = lax.broadcasted_iota(jnp.int32, (GRID_W, LANES), 1)
    lo = lane < HEAD_DIM
    kc = kc_ref[0]
    vc = vc_ref[0]
    kh = min(NA_KH, rows)
    win = kh * GRID_W

    for i0 in range(0, ROWS_PER_STEP, NA_ROW_GROUP):
        chains = []
        for i in range(i0, i0 + NA_ROW_GROUP):
            r = rb * ROWS_PER_STEP + i
            r0 = jnp.clip(r - kh // 2, 0, rows - kh)
            c = r - r0
            koff = pl.multiple_of(r0 * GRID_W, GRID_W)
            q = q_ref[0, i * GRID_W:(i + 1) * GRID_W, :]
            kw = k_ref[0, pl.ds(koff, win), :]
            vw = v_ref[0, pl.ds(koff, win), :]
            for hh in range(2):
                qm = jnp.where(lo if hh == 0 else jnp.logical_not(lo), q, jnp.zeros_like(q))
                chains.append(([_dot_t(qm, kw) + bias_ref[c, hh], _dot_t(qm, kc)], [vw, vc], None))
        outs = _softmax_pv_staged(chains)
        for j, i in enumerate(range(i0, i0 + NA_ROW_GROUP)):
            o_ref[0, i * GRID_W:(i + 1) * GRID_W, :] = jnp.where(lo, outs[2 * j], outs[2 * j + 1]).astype(BF16)


def _na_attn(proj, projc, bias):
    b, s, _ = proj.shape
    lc = projc.shape[1]
    rows = s // GRID_W
    assert rows >= NA_KH and rows % ROWS_PER_STEP == 0
    tq = ROWS_PER_STEP * GRID_W
    npair = GROUP_W // LANES
    return pl.pallas_call(
        functools.partial(_na_kernel, rows=rows),
        grid=(b, npair, rows // ROWS_PER_STEP),
        in_specs=[
            pl.BlockSpec((1, tq, LANES), lambda bi, p, r: (bi, r, C_QA // LANES + p)),
            pl.BlockSpec((1, s, LANES), lambda bi, p, r: (bi, 0, C_KA // LANES + p)),
            pl.BlockSpec((1, s, LANES), lambda bi, p, r: (bi, 0, C_VA // LANES + p)),
            pl.BlockSpec((1, lc, LANES), lambda bi, p, r: (bi, 0, C_KA // LANES + p)),
            pl.BlockSpec((1, lc, LANES), lambda bi, p, r: (bi, 0, C_VA // LANES + p)),
            pl.BlockSpec((NA_KH, 2, GRID_W, NA_KH * GRID_W), lambda bi, p, r: (0, p, 0, 0)),
        ],
        out_specs=pl.BlockSpec((1, tq, LANES), lambda bi, p, r: (bi, r, p)),
        out_shape=jax.ShapeDtypeStruct((b, s, GROUP_W), BF16),
        compiler_params=_cparams(("arbitrary", "arbitrary", "arbitrary")),
        name="na_attn",
    )(proj, proj, proj, projc, projc, bias)


def _gqa_chains(q, kv, sink_ref, k_parts, v_parts, mask):
    m = q.shape[0]
    lane = lax.broadcasted_iota(jnp.int32, (m, LANES), 1)
    lo = lane < HEAD_DIM
    qs = []
    for j in range(2):
        qg = q[:, (kv * 2 + j) * LANES:(kv * 2 + j + 1) * LANES]
        qs.append(jnp.where(lo, qg, jnp.zeros_like(qg)))
        qs.append(jnp.where(lo, jnp.zeros_like(qg), qg))
    qq = jnp.concatenate(qs, axis=0)
    s_all = [_dot_t(qq, k) for k in k_parts]
    chains = []
    for i in range(4):
        parts = [s[i * m:(i + 1) * m] for s in s_all]
        if mask is not None:
            parts[0] = jnp.where(mask, parts[0], NEG_INF)
        chains.append((parts, v_parts, jnp.full((m, 1), sink_ref[kv * 4 + i], F32)))
    return chains


def _gqa_store(o_ref, heads):
    lane = lax.broadcasted_iota(jnp.int32, heads[0].shape, 1)
    lo = lane < HEAD_DIM
    for g in range(len(heads) // 2):
        o_ref[0, :, g * LANES:(g + 1) * LANES] = jnp.where(lo, heads[2 * g], heads[2 * g + 1]).astype(BF16)


def _swa_kernel(sink_ref, q_ref, k_ref, v_ref, kc_ref, vc_ref, o_ref):
    n = pl.program_id(1)
    nb = pl.num_programs(1)
    start = jnp.clip(n - 1, 0, nb - 3)
    koff = pl.multiple_of(start * SWA_BLOCK, SWA_BLOCK)
    kw = k_ref[0, pl.ds(koff, 3 * SWA_BLOCK), :]
    vw = v_ref[0, pl.ds(koff, 3 * SWA_BLOCK), :]
    kc = kc_ref[0]
    vc = vc_ref[0]
    rel = (n - start) * SWA_BLOCK + lax.broadcasted_iota(jnp.int32, (SWA_BLOCK, 3 * SWA_BLOCK), 0)
    kj = lax.broadcasted_iota(jnp.int32, (SWA_BLOCK, 3 * SWA_BLOCK), 1)
    valid = jnp.abs(rel - kj) <= SWA_WINDOW
    q = q_ref[0]
    chains = []
    for kv in range(2):
        sl = slice(kv * LANES, (kv + 1) * LANES)
        chains += _gqa_chains(q, kv, sink_ref, [kw[:, sl], kc[:, sl]], [vw[:, sl], vc[:, sl]], valid)
    _gqa_store(o_ref, _softmax_pv_staged(chains))


def _swa_attn(proj, projc, sink):
    b, s, _ = proj.shape
    lc = projc.shape[1]
    nb = s // SWA_BLOCK
    assert nb >= 3
    w2 = 2 * LANES
    return pl.pallas_call(
        _swa_kernel,
        grid=(b, nb),
        in_specs=[
            pl.BlockSpec(memory_space=pltpu.SMEM),
            pl.BlockSpec((1, SWA_BLOCK, GROUP_W), lambda bi, n: (bi, n, C_QD // GROUP_W)),
            pl.BlockSpec((1, s, w2), lambda bi, n: (bi, 0, C_KD // w2)),
            pl.BlockSpec((1, s, w2), lambda bi, n: (bi, 0, C_VD // w2)),
            pl.BlockSpec((1, lc, w2), lambda bi, n: (bi, 0, C_KD // w2)),
            pl.BlockSpec((1, lc, w2), lambda bi, n: (bi, 0, C_VD // w2)),
        ],
        out_specs=pl.BlockSpec((1, SWA_BLOCK, GROUP_W), lambda bi, n: (bi, n, 0)),
        out_shape=jax.ShapeDtypeStruct((b, s, GROUP_W), BF16),
        compiler_params=_cparams(("arbitrary", "arbitrary")),
        name="swa_attn",
    )(sink.reshape(-1), proj, proj, proj, projc, projc)


def _ctx_attn_kernel(sink_ref, qa_ref, ka_ref, va_ref, qd_ref, kd_ref, vd_ref, oa_ref, od_ref):
    m = qa_ref.shape[1]
    lane = lax.broadcasted_iota(jnp.int32, (m, LANES), 1)
    lo = lane < HEAD_DIM
    chains = []
    for p in range(GROUP_W // LANES):
        sl = slice(p * LANES, (p + 1) * LANES)
        q = qa_ref[0, :, sl]
        k = ka_ref[0, :, sl]
        v = va_ref[0, :, sl]
        for hh in range(2):
            qm = jnp.where(lo if hh == 0 else jnp.logical_not(lo), q, jnp.zeros_like(q))
            chains.append(([_dot_t(qm, k)], [v], None))
    _gqa_store(oa_ref, _softmax_pv_staged(chains))
    q = qd_ref[0]
    chains = []
    for kv in range(2):
        sl = slice(kv * LANES, (kv + 1) * LANES)
        chains += _gqa_chains(q, kv, sink_ref, [kd_ref[0, :, sl]], [vd_ref[0, :, sl]], None)
    _gqa_store(od_ref, _softmax_pv_staged(chains))


def _ctx_attn(projc, sink):
    b, lc, _ = projc.shape
    w2 = 2 * LANES

    def spec(width, col):
        return pl.BlockSpec((1, lc, width), lambda bi: (bi, 0, col // width))

    ospec = pl.BlockSpec((1, lc, GROUP_W), lambda bi: (bi, 0, 0))
    return pl.pallas_call(
        _ctx_attn_kernel,
        grid=(b,),
        in_specs=[pl.BlockSpec(memory_space=pltpu.SMEM),
                  spec(GROUP_W, C_QA), spec(GROUP_W, C_KA), spec(GROUP_W, C_VA),
                  spec(GROUP_W, C_QD), spec(w2, C_KD), spec(w2, C_VD)],
        out_specs=[ospec, ospec],
        out_shape=[jax.ShapeDtypeStruct((b, lc, GROUP_W), BF16)] * 2,
        compiler_params=_cparams(("arbitrary",)),
        name="ctx_attn",
    )(sink.reshape(-1), projc, projc, projc, projc, projc, projc)


def _outproj_kernel(*refs, tm, moe):
    (oa_ref, od_ref, u_ref, up_ref, un_ref, pw_ref, ps_ref, cw_ref, wo_ref, x_ref, g1_ref,
     nw_ref, sh_ref, sc_ref) = refs[:14]
    rest = refs[14:]
    if moe:
        rt_ref, xo_ref, h_ref, gate_ref, ext_ub, ext_u = rest
    else:
        xo_ref, h_ref, ext_ub, ext_u = rest
    m = pl.program_id(1)
    nm = pl.num_programs(1)
    g4 = GROUP_W

    def cols(ref, k):
        return ref[0, :, k * g4:(k + 1) * g4].astype(F32)

    ub = cols(u_ref, 0)
    gb = cols(u_ref, 1)
    has_prev = m > 0
    has_next = m < nm - 1
    ext_ub[0:HALO] = jnp.where(has_prev, cols(up_ref, 0), 0.0)
    ext_ub[HALO:HALO + tm] = ub
    ext_ub[HALO + tm:] = jnp.where(has_next, cols(un_ref, 0), 0.0)
    ext_u[0:HALO] = jnp.where(has_prev, cols(up_ref, 2) * cols(up_ref, 3), 0.0)
    ext_u[HALO:HALO + tm] = cols(u_ref, 2) * cols(u_ref, 3)
    ext_u[HALO + tm:] = jnp.where(has_next, cols(un_ref, 2) * cols(un_ref, 3), 0.0)

    n_tok = nm * tm
    t = m * tm + lax.broadcasted_iota(jnp.int32, (tm, LANES), 0)
    obs = []
    for g, w in enumerate(POOL_WINDOWS):
        sl = slice(g * LANES, (g + 1) * LANES)
        acc = None
        for d in range(-(w // 2), w - w // 2):
            term = ext_ub[HALO + d:HALO + d + tm, sl]
            acc = term if acc is None else acc + term
        cnt = (jnp.clip(t + (w - w // 2), 0, n_tok) - jnp.clip(t - w // 2, 0, n_tok)).astype(F32)
        pooled = acc / cnt - ub[:, sl]
        obs.append(_dot(pooled.astype(BF16), pw_ref[g]) * ps_ref[:, sl])
    o_b = jnp.concatenate(obs, axis=1).astype(BF16)

    y = (ext_u[HALO - 1:HALO - 1 + tm] * cw_ref[0:1, :] + ext_u[HALO:HALO + tm] * cw_ref[1:2, :]
         + ext_u[HALO + 1:HALO + 1 + tm] * cw_ref[2:3, :])
    o_c = (gb * y).astype(BF16)

    yy = (_dot(oa_ref[0], wo_ref[0:g4]) + _dot(o_b, wo_ref[g4:2 * g4])
          + _dot(o_c, wo_ref[2 * g4:3 * g4]) + _dot(od_ref[0], wo_ref[3 * g4:4 * g4]))
    xn = x_ref[0] + g1_ref[0, 0] * yy
    xo_ref[0] = xn
    h2 = _rms_mod(xn, nw_ref[...], sh_ref[0, 0], sc_ref[0, 0])

    if not moe:
        h_ref[0] = h2.astype(BF16)
    else:
        packed = _pack_bf16_pairs(h2)
        for c in range(PACK_PLANES):
            h_ref[c] = packed[:, c * PLANE_W:(c + 1) * PLANE_W]
        lane = lax.broadcasted_iota(jnp.int32, (tm, LANES), 1)
        logits = jnp.full((tm, LANES), -jnp.inf, F32)
        for e in range(N_EXPERTS):
            le = jnp.sum(h2 * rt_ref[e:e + 1, :], axis=-1, keepdims=True)
            logits = jnp.where(lane == e, le, logits)
        v1 = jnp.max(logits, axis=-1, keepdims=True)
        i1 = jnp.min(jnp.where(logits == v1, lane, LANES), axis=-1, keepdims=True)
        l2 = jnp.where(lane == i1, -jnp.inf, logits)
        v2 = jnp.max(l2, axis=-1, keepdims=True)
        i2 = jnp.min(jnp.where(l2 == v2, lane, LANES), axis=-1, keepdims=True)
        e2 = jnp.exp(v2 - v1)
        p1 = 1.0 / (1.0 + e2)
        p2 = e2 / (1.0 + e2)
        gate_ref[0, :, 0:LANES] = jnp.where(lane == i1, 1.0, 0.0)
        gate_ref[0, :, LANES:2 * LANES] = jnp.where(lane == i2, 1.0, 0.0)
        gate_ref[0, :, 2 * LANES:3 * LANES] = jnp.broadcast_to(p1, (tm, LANES))
        gate_ref[0, :, 3 * LANES:4 * LANES] = jnp.broadcast_to(p2, (tm, LANES))


def _outproj(o_a, o_d, proj, x, wo, pool_w, pool_scale, conv_w, nw, mods, mod_row, router=None):
    b, s, d = x.shape
    tm = min(s, 512)
    moe = router is not None
    row = mod_row
    nh = s // HALO
    th = tm // HALO

    def mspec(k):
        return pl.BlockSpec((1, 1, 1, d), lambda bi, m: (row(bi), k, 0, 0))

    gspec = pl.BlockSpec((1, tm, GROUP_W), lambda bi, m: (bi, m, 0))
    xspec = pl.BlockSpec((1, tm, d), lambda bi, m: (bi, m, 0))
    in_specs = [
        gspec, gspec,
        pl.BlockSpec((1, tm, 4 * GROUP_W), lambda bi, m: (bi, m, 0)),
        pl.BlockSpec((1, HALO, 4 * GROUP_W), lambda bi, m: (bi, jnp.maximum(m * th - 1, 0), 0)),
        pl.BlockSpec((1, HALO, 4 * GROUP_W), lambda bi, m: (bi, jnp.minimum((m + 1) * th, nh - 1), 0)),
        pl.BlockSpec((4, POOL_CH, POOL_CH), lambda bi, m: (0, 0, 0)),
        pl.BlockSpec((1, GROUP_W), lambda bi, m: (0, 0)),
        pl.BlockSpec((3, GROUP_W), lambda bi, m: (0, 0)),
        pl.BlockSpec((d, d), lambda bi, m: (0, 0), pipeline_mode=pl.Buffered(1)),
        xspec, mspec(2),
        pl.BlockSpec((1, d), lambda bi, m: (0, 0)),
        mspec(3), mspec(4),
    ]
    args = [o_a, o_d, proj, proj, proj, pool_w.astype(BF16), pool_scale.reshape(1, GROUP_W), conv_w, wo,
            x, mods, nw.reshape(1, d), mods, mods]
    if moe:
        nm = s // tm
        in_specs.append(pl.BlockSpec((N_EXPERTS, d), lambda bi, m: (0, 0)))
        args.append(router.T)
        out_shape = [jax.ShapeDtypeStruct((b, s, d), F32),
                     jax.ShapeDtypeStruct((PACK_PLANES, b * s, PLANE_W), jnp.uint32),
                     jax.ShapeDtypeStruct((b, s, 4 * LANES), F32)]
        out_specs = [xspec,
                     pl.BlockSpec((PACK_PLANES, tm, PLANE_W), lambda bi, m: (0, bi * nm + m, 0)),
                     pl.BlockSpec((1, tm, 4 * LANES), lambda bi, m: (bi, m, 0))]
    else:
        out_shape = [jax.ShapeDtypeStruct((b, s, d), F32), jax.ShapeDtypeStruct((b, s, d), BF16)]
        out_specs = [xspec, xspec]
    return pl.pallas_call(
        functools.partial(_outproj_kernel, tm=tm, moe=moe),
        grid=(b, s // tm),
        in_specs=in_specs,
        out_specs=out_specs,
        out_shape=out_shape,
        scratch_shapes=[pltpu.VMEM((tm + 2 * HALO, GROUP_W), F32), pltpu.VMEM((tm + 2 * HALO, GROUP_W), F32)],
        compiler_params=_cparams(("arbitrary", "arbitrary")),
        name="outproj",
    )(*args)


def _swiglu_step(h, wg_ref, wu_ref, wd_ref, scale):
    a = _dot(h, wg_ref[0].astype(BF16))
    u = _dot(h, wu_ref[0].astype(BF16))
    act = a * (1.0 / (1.0 + jnp.exp(-a))) * u
    if scale is not None:
        act = act * scale
    return _dot(act.astype(BF16), wd_ref[0].astype(BF16))


def _ffn_kernel(h_ref, wg_ref, wu_ref, wd_ref, o_ref, acc_ref):
    f = pl.program_id(1)

    @pl.when(f == 0)
    def _():
        acc_ref[...] = jnp.zeros(acc_ref.shape, F32)

    acc_ref[...] += _swiglu_step(h_ref[...], wg_ref, wu_ref, wd_ref, None)

    @pl.when(f == pl.num_programs(1) - 1)
    def _():
        o_ref[...] = acc_ref[...].astype(BF16)


def _ffn(h, wg, wu, wd, layer):
    mt, d = h.shape
    ff = wg.shape[2]
    tm = min(mt, 1024)
    tf = 256
    return pl.pallas_call(
        _ffn_kernel,
        grid=(mt // tm, ff // tf),
        in_specs=[
            pl.BlockSpec((tm, d), lambda m, f: (m, 0)),
            pl.BlockSpec((1, d, tf), lambda m, f: (layer, 0, f)),
            pl.BlockSpec((1, d, tf), lambda m, f: (layer, 0, f)),
            pl.BlockSpec((1, tf, d), lambda m, f: (layer, f, 0)),
        ],
        out_specs=pl.BlockSpec((tm, d), lambda m, f: (m, 0)),
        out_shape=jax.ShapeDtypeStruct((mt, d), BF16),
        scratch_shapes=[pltpu.VMEM((tm, d), F32)],
        compiler_params=_cparams(("arbitrary", "arbitrary")),
        name="ffn",
    )(h, wg, wu, wd)


def _rank_kernel(r_ref, o_ref, cnt_ref, carry):
    i = pl.program_id(0)
    tm = r_ref.shape[0]

    @pl.when(i == 0)
    def _():
        carry[...] = jnp.zeros(carry.shape, F32)

    oh1 = r_ref[:, 0:LANES]
    oh2 = r_ref[:, LANES:2 * LANES]
    sel = oh1 + oh2
    row = lax.broadcasted_iota(jnp.int32, (tm, tm), 0)
    col = lax.broadcasted_iota(jnp.int32, (tm, tm), 1)
    tri = jnp.where(col < row, 1.0, 0.0).astype(BF16)
    excl = _dot(tri, sel.astype(BF16)) + carry[0:1, :]
    lane = lax.broadcasted_iota(jnp.int32, (tm, LANES), 1)
    lane_f = lane.astype(F32)
    e1 = jnp.sum(oh1 * lane_f, axis=-1, keepdims=True)
    r1 = jnp.sum(oh1 * excl, axis=-1, keepdims=True)
    e2 = jnp.sum(oh2 * lane_f, axis=-1, keepdims=True)
    r2 = jnp.sum(oh2 * excl, axis=-1, keepdims=True)
    o_ref[...] = jnp.where(lane == 0, e1, jnp.where(lane == 1, r1, jnp.where(lane == 2, e2,
                           jnp.where(lane == 3, r2, 0.0))))
    carry[...] = carry[...] + jnp.sum(sel, axis=0, keepdims=True)
    cnt_ref[...] = carry[...]


def _rank(route):
    t = route.shape[0]
    tm = min(t, 512)
    return pl.pallas_call(
        _rank_kernel,
        grid=(t // tm,),
        in_specs=[pl.BlockSpec((tm, 2 * LANES), lambda i: (i, 0))],
        out_specs=[pl.BlockSpec((tm, LANES), lambda i: (i, 0)), pl.BlockSpec((8, LANES), lambda i: (0, 0))],
        out_shape=[jax.ShapeDtypeStruct((t, LANES), F32), jax.ShapeDtypeStruct((8, LANES), F32)],
        scratch_shapes=[pltpu.VMEM((8, LANES), F32)],
        compiler_params=_cparams(("arbitrary",)),
        name="rank",
    )(route)


def _sc_mesh():
    return plsc.VectorSubcoreMesh(core_axis_name="core", subcore_axis_name="subcore")


def _sc_scatter2(x, idx1, idx2, n_out):
    n, w = x.shape

    @functools.partial(pl.kernel, out_type=jax.ShapeDtypeStruct((n_out, w), x.dtype), mesh=_sc_mesh(),
                       scratch_types=[], name="sc_dispatch")
    def k(x_hbm, i1_hbm, i2_hbm, o_hbm):
        def body(x_vmem, i1_vmem, i2_vmem):
            pltpu.sync_copy(x_vmem, o_hbm.at[i1_vmem.at[0]])
            pltpu.sync_copy(x_vmem, o_hbm.at[i2_vmem.at[0]])

        pltpu.emit_pipeline(
            body, grid=(n // SC_WINDOW,),
            in_specs=[pl.BlockSpec((SC_WINDOW, w), lambda i: (i, 0)),
                      pl.BlockSpec((1, SC_WINDOW), lambda i: (0, i)),
                      pl.BlockSpec((1, SC_WINDOW), lambda i: (0, i))],
            out_specs=[], core_axis_name=("core", "subcore"),
            dimension_semantics=(pltpu.PARALLEL,))(x_hbm, i1_hbm, i2_hbm)

    return k(x, idx1.reshape(1, n), idx2.reshape(1, n))


def _sc_gather(y, idx):
    (n,) = idx.shape
    w = y.shape[1]

    @functools.partial(pl.kernel, out_type=jax.ShapeDtypeStruct((n, w), y.dtype), mesh=_sc_mesh(),
                       scratch_types=[], name="sc_combine")
    def k(y_hbm, i_hbm, o_hbm):
        def body(i_vmem, o_vmem):
            pltpu.sync_copy(y_hbm.at[i_vmem.at[0]], o_vmem)

        pltpu.emit_pipeline(
            body, grid=(n // SC_WINDOW,),
            in_specs=[pl.BlockSpec((1, SC_WINDOW), lambda i: (0, i))],
            out_specs=[pl.BlockSpec((SC_WINDOW, w), lambda i: (i, 0))],
            core_axis_name=("core", "subcore"),
            dimension_semantics=(pltpu.PARALLEL,))(i_hbm, o_hbm)

    return k(y, idx.reshape(1, n))


MOE_SUB = 256


def _moe_kernel(te_ref, nu_ref, tv_ref, h_ref, wg_ref, wu_ref, wd_ref, o_ref, acc_ref, hb_ref):
    m = pl.program_id(0)
    f = pl.program_id(1)
    tm = hb_ref.shape[0]
    half = hb_ref.shape[1] // 2

    @pl.when(m < nu_ref[0])
    def _():
        @pl.when(f == 0)
        def _():
            acc_ref[...] = jnp.zeros(acc_ref.shape, F32)
            for c in range(PACK_PLANES):
                w = h_ref[c]
                hb_ref[:, c * PLANE_W:(c + 1) * PLANE_W] = _unpack_hi(w).astype(BF16)
                hb_ref[:, half + c * PLANE_W:half + (c + 1) * PLANE_W] = _unpack_lo(w).astype(BF16)

        valid = tv_ref[m]
        wg = wg_ref[0].astype(BF16)
        wu = wu_ref[0].astype(BF16)
        wd = wd_ref[0].astype(BF16)
        for sb in range(tm // MOE_SUB):
            @pl.when(sb * MOE_SUB < valid)
            def _():
                rows = slice(sb * MOE_SUB, (sb + 1) * MOE_SUB)
                h = hb_ref[rows, :]
                a = _dot(h, wg)
                u = _dot(h, wu)
                act = a * (1.0 / (1.0 + jnp.exp(-a))) * u
                acc_ref[rows, :] += _dot(act.astype(BF16), wd)

        @pl.when(f == pl.num_programs(1) - 1)
        def _():
            packed = _pack_bf16_pairs(acc_ref[...])
            for c in range(PACK_PLANES):
                o_ref[c] = packed[:, c * PLANE_W:(c + 1) * PLANE_W]


MOE_TM = 1024


def _moe_routed(hp, tile_expert, n_used, tile_valid, wg, wu, wd, layer, tm):
    _, r, _ = hp.shape
    _, ne, d, ff = wg.shape
    wg = wg.reshape(-1, d, ff)
    wu = wu.reshape(-1, d, ff)
    wd = wd.reshape(-1, ff, d)
    e0 = layer * ne
    tf = 256
    nf = ff // tf

    def row_map(m, f, te, nu, tv):
        return (0, jnp.minimum(m, nu[0] - 1), 0)

    def fidx(m, f, nu):
        return jnp.where(m < nu[0], f, nf - 1)

    hspec = pl.BlockSpec((PACK_PLANES, tm, PLANE_W), row_map)
    return pl.pallas_call(
        _moe_kernel,
        grid_spec=pltpu.PrefetchScalarGridSpec(
            num_scalar_prefetch=3,
            grid=(r // tm, nf),
            in_specs=[
                hspec,
                pl.BlockSpec((1, d, tf), lambda m, f, te, nu, tv: (e0 + te[m], 0, fidx(m, f, nu))),
                pl.BlockSpec((1, d, tf), lambda m, f, te, nu, tv: (e0 + te[m], 0, fidx(m, f, nu))),
                pl.BlockSpec((1, tf, d), lambda m, f, te, nu, tv: (e0 + te[m], fidx(m, f, nu), 0)),
            ],
            out_specs=hspec,
            scratch_shapes=[pltpu.VMEM((tm, d), F32), pltpu.VMEM((tm, d), BF16)],
        ),
        out_shape=jax.ShapeDtypeStruct(hp.shape, jnp.uint32),
        compiler_params=_cparams(("arbitrary", "arbitrary")),
        name="moe",
    )(tile_expert, n_used, tile_valid, hp, wg, wu, wd)


def _route_plan(meta, counts, tm, n_tiles):
    e1 = meta[:, 0].astype(jnp.int32)
    r1 = meta[:, 1].astype(jnp.int32)
    e2 = meta[:, 2].astype(jnp.int32)
    r2 = meta[:, 3].astype(jnp.int32)
    cnt = counts[0, :N_EXPERTS].astype(jnp.int32)
    tiles_per = (cnt + tm - 1) // tm
    tile_end = jnp.cumsum(tiles_per)
    start_row = (tile_end - tiles_per) * tm
    pos1 = start_row[e1] + r1
    pos2 = start_row[e2] + r2
    n_used = tile_end[-1]
    tiles = jnp.arange(n_tiles, dtype=jnp.int32)
    tile_expert = jnp.sum((tiles[:, None] >= tile_end[None, :]).astype(jnp.int32), axis=1)
    last_expert = jnp.sum((n_used - 1 >= tile_end).astype(jnp.int32))
    tile_expert = jnp.minimum(tile_expert, last_expert)
    tile_start = (tile_end - tiles_per)[tile_expert]
    tile_valid = jnp.clip(cnt[tile_expert] - (tiles - tile_start) * tm, 0, tm)
    return pos1, pos2, tile_expert, n_used.reshape(1), tile_valid


def _plane_rows(pos, n_rows):
    return (jnp.arange(PACK_PLANES, dtype=jnp.int32)[:, None] * n_rows + pos[None, :]).reshape(-1)


def _moe(hp, route, wg, wu, wd, layer):
    _, t, _ = hp.shape
    tm = min(MOE_TM, t)
    n_tiles = 2 * t // tm + N_EXPERTS
    n_rows = n_tiles * tm
    meta, counts = _rank(route)
    pos1, pos2, tile_expert, n_used, tile_valid = _route_plan(meta, counts, tm, n_tiles)
    i1 = _plane_rows(pos1, n_rows)
    i2 = _plane_rows(pos2, n_rows)
    hs = _sc_scatter2(hp.reshape(PACK_PLANES * t, PLANE_W), i1, i2, PACK_PLANES * n_rows)
    ys = _moe_routed(hs.reshape(PACK_PLANES, n_rows, PLANE_W), tile_expert, n_used, tile_valid,
                     wg, wu, wd, layer, tm)
    ys = ys.reshape(PACK_PLANES * n_rows, PLANE_W)
    y1 = _sc_gather(ys, i1).reshape(PACK_PLANES, t, PLANE_W)
    y2 = _sc_gather(ys, i2).reshape(PACK_PLANES, t, PLANE_W)
    return y1, y2


def _final_kernel(x_ref, f_ref, g_ref, nw_ref, o_ref):
    xv = x_ref[0] + g_ref[0, 0] * f_ref[0].astype(F32)
    ms = jnp.mean(xv * xv, axis=-1, keepdims=True)
    o_ref[0] = xv * lax.rsqrt(ms + EPS) * nw_ref[...]


def _final(x, f, mods, nw):
    b, s, d = x.shape
    tm = min(s, 512)
    xspec = pl.BlockSpec((1, tm, d), lambda bi, m: (bi, m, 0))
    return pl.pallas_call(
        _final_kernel,
        grid=(b, s // tm),
        in_specs=[xspec, xspec,
                  pl.BlockSpec((1, 1, 1, d), lambda bi, m: (bi, 5, 0, 0)),
                  pl.BlockSpec((1, d), lambda bi, m: (0, 0))],
        out_specs=xspec,
        out_shape=jax.ShapeDtypeStruct((b, s, d), F32),
        compiler_params=_cparams(("arbitrary", "arbitrary")),
        name="final",
    )(x, f, mods, nw.reshape(1, d))


def _final_moe_kernel(x_ref, y1_ref, y2_ref, p_ref, g_ref, nw_ref, o_ref):
    tm, d = x_ref.shape[1], x_ref.shape[2]
    half = d // 2
    p = p_ref[0]
    p1 = jnp.concatenate([p[:, 0:LANES]] * (PLANE_W // LANES), axis=1)
    p2 = jnp.concatenate([p[:, LANES:2 * LANES]] * (PLANE_W // LANES), axis=1)
    ssq = jnp.zeros((tm, 1), F32)
    for c in range(PACK_PLANES):
        w1 = y1_ref[c]
        w2 = y2_ref[c]
        for unpack, off in ((_unpack_hi, 0), (_unpack_lo, half)):
            sl = slice(off + c * PLANE_W, off + (c + 1) * PLANE_W)
            f = p1 * unpack(w1) + p2 * unpack(w2)
            xv = x_ref[0, :, sl] + g_ref[0, 0, :, sl] * f
            o_ref[0, :, sl] = xv
            ssq = ssq + jnp.sum(xv * xv, axis=-1, keepdims=True)
    o_ref[0] = o_ref[0] * lax.rsqrt(ssq / d + EPS) * nw_ref[...]


def _final_moe(x, y1, y2, route, mods, nw):
    b, s, d = x.shape
    tm = min(s, 512)
    nm = s // tm
    xspec = pl.BlockSpec((1, tm, d), lambda bi, m: (bi, m, 0))
    yspec = pl.BlockSpec((PACK_PLANES, tm, PLANE_W), lambda bi, m: (0, bi * nm + m, 0))
    return pl.pallas_call(
        _final_moe_kernel,
        grid=(b, nm),
        in_specs=[xspec, yspec, yspec,
                  pl.BlockSpec((1, tm, 2 * LANES), lambda bi, m: (bi, m, 1)),
                  pl.BlockSpec((1, 1, 1, d), lambda bi, m: (bi, 5, 0, 0)),
                  pl.BlockSpec((1, d), lambda bi, m: (0, 0))],
        out_specs=xspec,
        out_shape=jax.ShapeDtypeStruct((b, s, d), F32),
        compiler_params=_cparams(("arbitrary", "arbitrary")),
        name="final",
    )(x, y1, y2, route, mods, nw.reshape(1, d))


def _prep_w_in(w):
    scale = HEAD_DIM ** -0.5

    def dup(t):
        return jnp.concatenate([t[:, :HEAD_DIM], t[:, :HEAD_DIM], t[:, HEAD_DIM:], t[:, HEAD_DIM:]], axis=1)

    g = GROUP_W
    kd = w[:, 8 * g:8 * g + 2 * HEAD_DIM]
    vd = w[:, 8 * g + 2 * HEAD_DIM:]
    return jnp.concatenate(
        [w[:, 3 * g:7 * g], w[:, 0:g] * scale, w[:, g:3 * g], w[:, 7 * g:8 * g] * scale, dup(kd), dup(vd)],
        axis=1).astype(BF16)


def _rope_tables(n):
    t = jnp.arange(n, dtype=jnp.int32)
    rows = (t // GRID_W).astype(F32)
    cols = (t % GRID_W).astype(F32)
    nf = HEAD_DIM // 4
    inv = ROPE_BASE ** (-jnp.arange(nf, dtype=F32) / nf)
    ang = jnp.concatenate([rows[:, None] * inv, cols[:, None] * inv], axis=-1)
    cos = jnp.cos(ang)
    sin = jnp.sin(ang)
    cos_t = jnp.concatenate([cos, cos, cos, cos], axis=-1)
    sin_t = jnp.concatenate([-sin, sin, -sin, sin], axis=-1)
    return cos_t, sin_t


def kernel(x, c, ctx, c_ctx, w_ada, b_ada, norm_mix, norm_ffn, norm_final, w_in, w_out, na_rpb, pool_w, pool_scale, conv_w, swa_sink, ffn_w_gate, ffn_w_up, ffn_w_down, moe_router, moe_w_gate, moe_w_up, moe_w_down):
    b, s, d = x.shape
    lc = ctx.shape[1]
    depth = w_ada.shape[0]
    cond = jnp.concatenate([c, c_ctx[None, :]], axis=0)
    assert b == 2
    mods_all = _ada(cond, w_ada, b_ada)
    cos_t, sin_t = _rope_tables(s)
    cos_c = jnp.ones((lc, LANES), F32)
    sin_c = jnp.zeros((lc, LANES), F32)
    row_x = lambda bi: bi
    row_c = lambda bi: b

    cx = ctx
    fx = fc = None
    for i in range(depth):
        last = i == depth - 1
        mods = mods_all[i].reshape(8, 6, 1, d)
        wi = _prep_w_in(w_in[i])
        wo = w_out[i].astype(BF16)
        if i == 0:
            proj, _ = _inproj(x, wi, norm_mix[i], mods, row_x, cos_t, sin_t)
            projc, _ = _inproj(cx, wi, norm_mix[i], mods, row_c, cos_c, sin_c)
        else:
            mods_prev = mods_all[i - 1].reshape(8, 6, 1, d)
            proj, x = _inproj_res(x, fx, mods_prev, wi, norm_mix[i], mods, row_x, cos_t, sin_t)
            projc, cx = _inproj_res(cx, fc, mods_prev, wi, norm_mix[i], mods, row_c, cos_c, sin_c)
        bias = _na_bias(na_rpb[i])
        o_a = _na_attn(proj, projc, bias)
        o_d = _swa_attn(proj, projc, swa_sink[i])
        moe = i % 2 == 1
        router = moe_router[i // 2] if moe else None
        outs = _outproj(o_a, o_d, proj, x, wo, pool_w[i], pool_scale[i], conv_w[i], norm_ffn[i], mods, row_x,
                        router)
        x, h2 = outs[0], outs[1]
        if not last:
            oc_a, oc_d = _ctx_attn(projc, swa_sink[i])
            cx, h2c = _outproj(oc_a, oc_d, projc, cx, wo, pool_w[i], pool_scale[i], conv_w[i], norm_ffn[i],
                               mods, row_c)
        j = i // 2
        if moe:
            if not last:
                raise NotImplementedError("an expert layer that is not the last layer")
            route = outs[2]
            y1, y2 = _moe(h2, route.reshape(b * s, 4 * LANES), moe_w_gate, moe_w_up, moe_w_down, j)
            return _final_moe(x, y1, y2, route, mods, norm_final)
        fx = _ffn(h2.reshape(b * s, d), ffn_w_gate, ffn_w_up, ffn_w_down, j).reshape(b, s, d)
        if not last:
            fc = _ffn(h2c.reshape(b * lc, d), ffn_w_gate, ffn_w_up, ffn_w_down, j).reshape(b, lc, d)
    return _final(x, fx, mods_all[depth - 1].reshape(8, 6, 1, d), norm_final)


def _inproj_res(x, f, mods_prev, w, nw, mods, mod_row, cos_t, sin_t):
    both = jnp.concatenate([mods, mods_prev], axis=1)
    return _inproj(x, w, nw, both, mod_row, cos_t, sin_t, res=(f, 6 + 5))
```

```python
import functools

import jax
import jax.numpy as jnp
from jax import lax
from jax.experimental import pallas as pl
from jax.experimental.pallas import tpu as pltpu
from jax.experimental.pallas import tpu_sc as plsc

F32 = jnp.float32
BF16 = jnp.bfloat16

D_MODEL = 2048
GRID_W = 64
HEAD_DIM = 64
EPS = 1e-6
NEG_INF = -1e30
GROUP_W = 512
NA_KH = 8
NA_KW = 16
POOL_WINDOWS = (2, 4, 8, 16)
POOL_CH = 128
SWA_WINDOW = 128
SWA_BLOCK = 128
ROPE_BASE = 10000.0
N_EXPERTS = 8
LANES = 128
HALO = 16

C_UBC = 0
C_QA = 2048
C_KA = 2560
C_VA = 3072
C_QD = 3584
C_KD = 4096
C_VD = 4352
D_PROJ = 4608
TN_PROJ = 512

VMEM_LIMIT = 56 * 1024 * 1024


def _cparams(sem):
    return pltpu.CompilerParams(dimension_semantics=sem, vmem_limit_bytes=VMEM_LIMIT)


def _dot(a, b):
    return jnp.dot(a, b, preferred_element_type=F32)


def _dot_t(a, b):
    return lax.dot_general(a, b, (((1,), (1,)), ((), ())), preferred_element_type=F32)


def _rms_mod(xv, nw, sh, sc):
    ms = jnp.mean(xv * xv, axis=-1, keepdims=True)
    y = xv * lax.rsqrt(ms + EPS) * nw
    return y * (1.0 + sc) + sh


PACK_PLANES = 4
PLANE_W = D_MODEL // 2 // PACK_PLANES
SC_WINDOW = 128


def _pack_bf16_pairs(v):
    half = v.shape[1] // 2
    hi = lax.bitcast_convert_type(v[:, :half].astype(BF16).astype(F32), jnp.uint32)
    lo = lax.bitcast_convert_type(v[:, half:].astype(BF16).astype(F32), jnp.uint32)
    return hi | (lo >> 16)


def _unpack_hi(w):
    return lax.bitcast_convert_type(w & jnp.uint32(0xFFFF0000), F32)


def _unpack_lo(w):
    return lax.bitcast_convert_type(w << 16, F32)


def _ada_kernel(c_ref, w_ref, b_ref, o_ref):
    tn = w_ref.shape[2]
    o_ref[...] = jnp.zeros(o_ref.shape, F32)
    for r in range(3):
        cv = c_ref[r]
        m = cv * (1.0 / (1.0 + jnp.exp(-cv)))
        cols = []
        for j in range(tn // LANES):
            wj = w_ref[0, :, j * LANES:(j + 1) * LANES]
            cols.append(jnp.sum(wj * m, axis=0, keepdims=True))
        o_ref[0, r:r + 1, :] = jnp.concatenate(cols, axis=1) + b_ref[0]


def _ada(cond, w_ada, b_ada):
    depth, d, n6 = w_ada.shape
    tn = 1024
    cb = jnp.broadcast_to(cond[:, :, None], (3, d, LANES))
    return pl.pallas_call(
        _ada_kernel,
        grid=(depth, n6 // tn),
        in_specs=[
            pl.BlockSpec((3, d, LANES), lambda i, j: (0, 0, 0)),
            pl.BlockSpec((1, d, tn), lambda i, j: (i, 0, j)),
            pl.BlockSpec((1, 1, tn), lambda i, j: (i, 0, j)),
        ],
        out_specs=pl.BlockSpec((1, 8, tn), lambda i, j: (i, 0, j)),
        out_shape=jax.ShapeDtypeStruct((depth, 8, n6), F32),
        compiler_params=_cparams(("arbitrary", "arbitrary")),
        name="ada",
    )(cb, w_ada, b_ada.reshape(depth, 1, n6))


def _rope(a, cosv, sinv, lo32):
    sw = jnp.where(lo32, pltpu.roll(a, 96, 1), pltpu.roll(a, 32, 1))
    return a * cosv + sw * sinv


def _inproj_kernel(*refs, has_res):
    if has_res:
        x_ref, f_ref, g_ref, nw_ref, sh_ref, sc_ref, cos_ref, sin_ref, w_ref, o_ref, x2_ref = refs
    else:
        x_ref, nw_ref, sh_ref, sc_ref, cos_ref, sin_ref, w_ref, o_ref = refs
    xv = x_ref[0]
    if has_res:
        xv = xv + g_ref[0, 0] * f_ref[0].astype(F32)
        x2_ref[0] = xv
    h = _rms_mod(xv, nw_ref[...], sh_ref[0, 0], sc_ref[0, 0]).astype(BF16)
    tm = h.shape[0]
    lane = lax.broadcasted_iota(jnp.int32, (tm, LANES), 1)
    lo32 = (lane % HEAD_DIM) < (HEAD_DIM // 2)
    rope_end = C_VD
    for n in range(D_PROJ // TN_PROJ):
        c0 = n * TN_PROJ
        acc = _dot(h, w_ref[0, :, c0:c0 + TN_PROJ])
        if c0 + TN_PROJ <= C_QD:
            o_ref[0, :, c0:c0 + TN_PROJ] = acc.astype(BF16)
            continue
        for g in range(TN_PROJ // LANES):
            a = acc[:, g * LANES:(g + 1) * LANES]
            if c0 + g * LANES < rope_end:
                a = _rope(a, cos_ref[...], sin_ref[...], lo32)
            o_ref[0, :, c0 + g * LANES:c0 + (g + 1) * LANES] = a.astype(BF16)


def _inproj(x, w, nw, mods, mod_row, cos_t, sin_t, res=None):
    b, s, d = x.shape
    w_all, layer = w
    tm = min(s, 512)
    has_res = res is not None
    row = mod_row
    xspec = pl.BlockSpec((1, tm, d), lambda bi, m: (bi, m, 0))

    def mspec(k):
        return pl.BlockSpec((1, 1, 1, d), lambda bi, m: (row(bi), k, 0, 0))

    in_specs = [xspec]
    args = [x]
    if has_res:
        f, gk = res
        in_specs += [xspec, mspec(gk)]
        args += [f, mods]
    in_specs += [
        pl.BlockSpec((1, d), lambda bi, m: (0, 0)),
        mspec(0), mspec(1),
        pl.BlockSpec((tm, LANES), lambda bi, m: (m, 0)),
        pl.BlockSpec((tm, LANES), lambda bi, m: (m, 0)),
        pl.BlockSpec((1, d, D_PROJ), lambda bi, m: (layer, 0, 0), pipeline_mode=pl.Buffered(1)),
    ]
    args += [nw.reshape(1, d), mods, mods, cos_t, sin_t, w_all]
    out_shape = [jax.ShapeDtypeStruct((b, s, D_PROJ), BF16)]
    out_specs = [pl.BlockSpec((1, tm, D_PROJ), lambda bi, m: (bi, m, 0))]
    if has_res:
        out_shape.append(jax.ShapeDtypeStruct((b, s, d), F32))
        out_specs.append(xspec)
    outs = pl.pallas_call(
        functools.partial(_inproj_kernel, has_res=has_res),
        grid=(b, s // tm),
        in_specs=in_specs,
        out_specs=out_specs,
        out_shape=out_shape,
        compiler_params=_cparams(("arbitrary", "arbitrary")),
        name="inproj",
    )(*args)
    return outs if has_res else (outs[0], None)


def _na_bias_kernel(rpb_ref, o_ref):
    h = pl.program_id(0)
    nd = 2 * NA_KH - 1
    nj = 2 * NA_KW - 1
    q = lax.broadcasted_iota(jnp.int32, (GRID_W, LANES), 0)
    lane = lax.broadcasted_iota(jnp.int32, (GRID_W, LANES), 1)
    kw = lane % GRID_W
    dc = jnp.clip(kw - q, -(NA_KW - 1), NA_KW - 1) + (NA_KW - 1)
    c0 = jnp.clip(q - NA_KW // 2, 0, GRID_W - NA_KW)
    valid = (kw >= c0) & (kw < c0 + NA_KW)
    tabs = []
    for d in range(nd):
        t = jnp.zeros((GRID_W, LANES), F32)
        for j in range(nj):
            t = jnp.where(dc == j, rpb_ref[h * (nd * nj) + d * nj + j], t)
        tabs.append(jnp.where(valid, t, NEG_INF))
    for c in range(NA_KH):
        for g in range(NA_KH // 2):
            d_lo = 2 * g - c + (NA_KH - 1)
            o_ref[c, 0, :, g * LANES:(g + 1) * LANES] = jnp.where(lane < GRID_W, tabs[d_lo], tabs[d_lo + 1])


def _na_bias(rpb):
    nh = rpb.shape[0]
    return pl.pallas_call(
        _na_bias_kernel,
        grid=(nh,),
        in_specs=[pl.BlockSpec(memory_space=pltpu.SMEM)],
        out_specs=pl.BlockSpec((NA_KH, 1, GRID_W, NA_KH * GRID_W), lambda h: (0, h, 0, 0)),
        out_shape=jax.ShapeDtypeStruct((NA_KH, nh, GRID_W, NA_KH * GRID_W), F32),
        compiler_params=_cparams(("arbitrary",)),
        name="na_bias",
    )(rpb.reshape(-1))


def _softmax_pv_staged(chains):
    add = lambda a, b: a + b
    ms = []
    for sp, _, extra in chains:
        m = functools.reduce(jnp.maximum, [jnp.max(s, axis=-1, keepdims=True) for s in sp])
        ms.append(m if extra is None else jnp.maximum(m, extra))
    ps = [[jnp.exp(s - m) for s in sp] for (sp, _, _), m in zip(chains, ms)]
    dens = []
    for pp, (_, _, extra), m in zip(ps, chains, ms):
        den = functools.reduce(add, [jnp.sum(p, axis=-1, keepdims=True) for p in pp])
        dens.append(den if extra is None else den + jnp.exp(extra - m))
    outs = [functools.reduce(add, [_dot(p.astype(BF16), v) for p, v in zip(pp, vp)])
            for pp, (_, vp, _) in zip(ps, chains)]
    return [o / d for o, d in zip(outs, dens)]


ROWS_PER_STEP = 8
NA_ROW_GROUP = 4


def _na_kernel(q_ref, k_ref, v_ref, kc_ref, vc_ref, bias_ref, o_ref, *, rows):
    rb = pl.program_id(2)
    lane = lax.broadcasted_iota(jnp.int32, (GRID_W, LANES), 1)
    lo = lane < HEAD_DIM
    kc = kc_ref[0]
    vc = vc_ref[0]
    kh = min(NA_KH, rows)
    win = kh * GRID_W

    for i0 in range(0, ROWS_PER_STEP, NA_ROW_GROUP):
        chains = []
        for i in range(i0, i0 + NA_ROW_GROUP):
            r = rb * ROWS_PER_STEP + i
            r0 = jnp.clip(r - kh // 2, 0, rows - kh)
            c = r - r0
            koff = pl.multiple_of(r0 * GRID_W, GRID_W)
            q = q_ref[0, i * GRID_W:(i + 1) * GRID_W, :]
            kw = k_ref[0, pl.ds(koff, win), :]
            vw = v_ref[0, pl.ds(koff, win), :]
            for hh in range(2):
                qm = jnp.where(lo if hh == 0 else jnp.logical_not(lo), q, jnp.zeros_like(q))
                chains.append(([_dot_t(qm, kw) + bias_ref[c, hh], _dot_t(qm, kc)], [vw, vc], None))
        outs = _softmax_pv_staged(chains)
        for j, i in enumerate(range(i0, i0 + NA_ROW_GROUP)):
            o_ref[0, i * GRID_W:(i + 1) * GRID_W, :] = jnp.where(lo, outs[2 * j], outs[2 * j + 1]).astype(BF16)


def _na_attn(proj, projc, bias):
    b, s, _ = proj.shape
    lc = projc.shape[1]
    rows = s // GRID_W
    assert rows >= NA_KH and rows % ROWS_PER_STEP == 0
    tq = ROWS_PER_STEP * GRID_W
    npair = GROUP_W // LANES
    return pl.pallas_call(
        functools.partial(_na_kernel, rows=rows),
        grid=(b, npair, rows // ROWS_PER_STEP),
        in_specs=[
            pl.BlockSpec((1, tq, LANES), lambda bi, p, r: (bi, r, C_QA // LANES + p)),
            pl.BlockSpec((1, s, LANES), lambda bi, p, r: (bi, 0, C_KA // LANES + p)),
            pl.BlockSpec((1, s, LANES), lambda bi, p, r: (bi, 0, C_VA // LANES + p)),
            pl.BlockSpec((1, lc, LANES), lambda bi, p, r: (bi, 0, C_KA // LANES + p)),
            pl.BlockSpec((1, lc, LANES), lambda bi, p, r: (bi, 0, C_VA // LANES + p)),
            pl.BlockSpec((NA_KH, 2, GRID_W, NA_KH * GRID_W), lambda bi, p, r: (0, p, 0, 0)),
        ],
        out_specs=pl.BlockSpec((1, tq, LANES), lambda bi, p, r: (bi, r, p)),
        out_shape=jax.ShapeDtypeStruct((b, s, GROUP_W), BF16),
        compiler_params=_cparams(("arbitrary", "arbitrary", "arbitrary")),
        name="na_attn",
    )(proj, proj, proj, projc, projc, bias)


def _gqa_chains(q, kv, sink_ref, k_parts, v_parts, mask):
    m = q.shape[0]
    lane = lax.broadcasted_iota(jnp.int32, (m, LANES), 1)
    lo = lane < HEAD_DIM
    qs = []
    for j in range(2):
        qg = q[:, (kv * 2 + j) * LANES:(kv * 2 + j + 1) * LANES]
        qs.append(jnp.where(lo, qg, jnp.zeros_like(qg)))
        qs.append(jnp.where(lo, jnp.zeros_like(qg), qg))
    qq = jnp.concatenate(qs, axis=0)
    s_all = [_dot_t(qq, k) for k in k_parts]
    chains = []
    for i in range(4):
        parts = [s[i * m:(i + 1) * m] for s in s_all]
        if mask is not None:
            parts[0] = jnp.where(mask, parts[0], NEG_INF)
        chains.append((parts, v_parts, jnp.full((m, 1), sink_ref[kv * 4 + i], F32)))
    return chains


def _gqa_store(o_ref, heads):
    lane = lax.broadcasted_iota(jnp.int32, heads[0].shape, 1)
    lo = lane < HEAD_DIM
    for g in range(len(heads) // 2):
        o_ref[0, :, g * LANES:(g + 1) * LANES] = jnp.where(lo, heads[2 * g], heads[2 * g + 1]).astype(BF16)


def _swa_kernel(sink_ref, q_ref, k_ref, v_ref, kc_ref, vc_ref, o_ref):
    n = pl.program_id(1)
    nb = pl.num_programs(1)
    start = jnp.clip(n - 1, 0, nb - 3)
    koff = pl.multiple_of(start * SWA_BLOCK, SWA_BLOCK)
    kw = k_ref[0, pl.ds(koff, 3 * SWA_BLOCK), :]
    vw = v_ref[0, pl.ds(koff, 3 * SWA_BLOCK), :]
    kc = kc_ref[0]
    vc = vc_ref[0]
    rel = (n - start) * SWA_BLOCK + lax.broadcasted_iota(jnp.int32, (SWA_BLOCK, 3 * SWA_BLOCK), 0)
    kj = lax.broadcasted_iota(jnp.int32, (SWA_BLOCK, 3 * SWA_BLOCK), 1)
    valid = jnp.abs(rel - kj) <= SWA_WINDOW
    q = q_ref[0]
    chains = []
    for kv in range(2):
        sl = slice(kv * LANES, (kv + 1) * LANES)
        chains += _gqa_chains(q, kv, sink_ref, [kw[:, sl], kc[:, sl]], [vw[:, sl], vc[:, sl]], valid)
    _gqa_store(o_ref, _softmax_pv_staged(chains))


def _swa_attn(proj, projc, sink):
    b, s, _ = proj.shape
    lc = projc.shape[1]
    nb = s // SWA_BLOCK
    assert nb >= 3
    w2 = 2 * LANES
    return pl.pallas_call(
        _swa_kernel,
        grid=(b, nb),
        in_specs=[
            pl.BlockSpec(memory_space=pltpu.SMEM),
            pl.BlockSpec((1, SWA_BLOCK, GROUP_W), lambda bi, n: (bi, n, C_QD // GROUP_W)),
            pl.BlockSpec((1, s, w2), lambda bi, n: (bi, 0, C_KD // w2)),
            pl.BlockSpec((1, s, w2), lambda bi, n: (bi, 0, C_VD // w2)),
            pl.BlockSpec((1, lc, w2), lambda bi, n: (bi, 0, C_KD // w2)),
            pl.BlockSpec((1, lc, w2), lambda bi, n: (bi, 0, C_VD // w2)),
        ],
        out_specs=pl.BlockSpec((1, SWA_BLOCK, GROUP_W), lambda bi, n: (bi, n, 0)),
        out_shape=jax.ShapeDtypeStruct((b, s, GROUP_W), BF16),
        compiler_params=_cparams(("arbitrary", "arbitrary")),
        name="swa_attn",
    )(sink.reshape(-1), proj, proj, proj, projc, projc)


def _ctx_attn_kernel(sink_ref, qa_ref, ka_ref, va_ref, qd_ref, kd_ref, vd_ref, oa_ref, od_ref):
    m = qa_ref.shape[1]
    lane = lax.broadcasted_iota(jnp.int32, (m, LANES), 1)
    lo = lane < HEAD_DIM
    chains = []
    for p in range(GROUP_W // LANES):
        sl = slice(p * LANES, (p + 1) * LANES)
        q = qa_ref[0, :, sl]
        k = ka_ref[0, :, sl]
        v = va_ref[0, :, sl]
        for hh in range(2):
            qm = jnp.where(lo if hh == 0 else jnp.logical_not(lo), q, jnp.zeros_like(q))
            chains.append(([_dot_t(qm, k)], [v], None))
    _gqa_store(oa_ref, _softmax_pv_staged(chains))
    q = qd_ref[0]
    chains = []
    for kv in range(2):
        sl = slice(kv * LANES, (kv + 1) * LANES)
        chains += _gqa_chains(q, kv, sink_ref, [kd_ref[0, :, sl]], [vd_ref[0, :, sl]], None)
    _gqa_store(od_ref, _softmax_pv_staged(chains))


def _ctx_attn(projc, sink):
    b, lc, _ = projc.shape
    w2 = 2 * LANES

    def spec(width, col):
        return pl.BlockSpec((1, lc, width), lambda bi: (bi, 0, col // width))

    ospec = pl.BlockSpec((1, lc, GROUP_W), lambda bi: (bi, 0, 0))
    return pl.pallas_call(
        _ctx_attn_kernel,
        grid=(b,),
        in_specs=[pl.BlockSpec(memory_space=pltpu.SMEM),
                  spec(GROUP_W, C_QA), spec(GROUP_W, C_KA), spec(GROUP_W, C_VA),
                  spec(GROUP_W, C_QD), spec(w2, C_KD), spec(w2, C_VD)],
        out_specs=[ospec, ospec],
        out_shape=[jax.ShapeDtypeStruct((b, lc, GROUP_W), BF16)] * 2,
        compiler_params=_cparams(("arbitrary",)),
        name="ctx_attn",
    )(sink.reshape(-1), projc, projc, projc, projc, projc, projc)


def _outproj_kernel(*refs, tm, moe):
    (oa_ref, od_ref, u_ref, up_ref, un_ref, pw_ref, ps_ref, cw_ref, wo_ref, x_ref, g1_ref,
     nw_ref, sh_ref, sc_ref) = refs[:14]
    rest = refs[14:]
    if moe:
        rt_ref, xo_ref, h_ref, gate_ref, ext_ub, ext_u = rest
    else:
        xo_ref, h_ref, ext_ub, ext_u = rest
    m = pl.program_id(1)
    nm = pl.num_programs(1)
    g4 = GROUP_W

    def cols(ref, k):
        return ref[0, :, k * g4:(k + 1) * g4].astype(F32)

    ub = cols(u_ref, 0)
    gb = cols(u_ref, 1)
    has_prev = m > 0
    has_next = m < nm - 1
    ext_ub[0:HALO] = jnp.where(has_prev, cols(up_ref, 0), 0.0)
    ext_ub[HALO:HALO + tm] = ub
    ext_ub[HALO + tm:] = jnp.where(has_next, cols(un_ref, 0), 0.0)
    ext_u[0:HALO] = jnp.where(has_prev, cols(up_ref, 2) * cols(up_ref, 3), 0.0)
    ext_u[HALO:HALO + tm] = cols(u_ref, 2) * cols(u_ref, 3)
    ext_u[HALO + tm:] = jnp.where(has_next, cols(un_ref, 2) * cols(un_ref, 3), 0.0)

    n_tok = nm * tm
    t = m * tm + lax.broadcasted_iota(jnp.int32, (tm, LANES), 0)
    obs = []
    for g, w in enumerate(POOL_WINDOWS):
        sl = slice(g * LANES, (g + 1) * LANES)
        acc = None
        for d in range(-(w // 2), w - w // 2):
            term = ext_ub[HALO + d:HALO + d + tm, sl]
            acc = term if acc is None else acc + term
        cnt = (jnp.clip(t + (w - w // 2), 0, n_tok) - jnp.clip(t - w // 2, 0, n_tok)).astype(F32)
        pooled = acc / cnt - ub[:, sl]
        obs.append(_dot(pooled.astype(BF16), pw_ref[g]) * ps_ref[:, sl])
    o_b = jnp.concatenate(obs, axis=1).astype(BF16)

    y = (ext_u[HALO - 1:HALO - 1 + tm] * cw_ref[0:1, :] + ext_u[HALO:HALO + tm] * cw_ref[1:2, :]
         + ext_u[HALO + 1:HALO + 1 + tm] * cw_ref[2:3, :])
    o_c = (gb * y).astype(BF16)

    yy = _dot(jnp.concatenate([oa_ref[0], o_b, o_c, od_ref[0]], axis=1), wo_ref[...])
    xn = x_ref[0] + g1_ref[0, 0] * yy
    xo_ref[0] = xn
    h2 = _rms_mod(xn, nw_ref[...], sh_ref[0, 0], sc_ref[0, 0])

    if not moe:
        h_ref[0] = h2.astype(BF16)
    else:
        packed = _pack_bf16_pairs(h2)
        for c in range(PACK_PLANES):
            h_ref[c] = packed[:, c * PLANE_W:(c + 1) * PLANE_W]
        lane = lax.broadcasted_iota(jnp.int32, (tm, LANES), 1)
        logits = jnp.full((tm, LANES), -jnp.inf, F32)
        for e in range(N_EXPERTS):
            le = jnp.sum(h2 * rt_ref[e:e + 1, :], axis=-1, keepdims=True)
            logits = jnp.where(lane == e, le, logits)
        v1 = jnp.max(logits, axis=-1, keepdims=True)
        i1 = jnp.min(jnp.where(logits == v1, lane, LANES), axis=-1, keepdims=True)
        l2 = jnp.where(lane == i1, -jnp.inf, logits)
        v2 = jnp.max(l2, axis=-1, keepdims=True)
        i2 = jnp.min(jnp.where(l2 == v2, lane, LANES), axis=-1, keepdims=True)
        e2 = jnp.exp(v2 - v1)
        p1 = 1.0 / (1.0 + e2)
        p2 = e2 / (1.0 + e2)
        gate_ref[0, :, 0:LANES] = jnp.where(lane == i1, 1.0, 0.0)
        gate_ref[0, :, LANES:2 * LANES] = jnp.where(lane == i2, 1.0, 0.0)
        gate_ref[0, :, 2 * LANES:3 * LANES] = jnp.broadcast_to(p1, (tm, LANES))
        gate_ref[0, :, 3 * LANES:4 * LANES] = jnp.broadcast_to(p2, (tm, LANES))


def _outproj(o_a, o_d, proj, x, wo, pool_w, pool_scale, conv_w, nw, mods, mod_row, router=None):
    b, s, d = x.shape
    tm = min(s, 512)
    moe = router is not None
    row = mod_row
    nh = s // HALO
    th = tm // HALO

    def mspec(k):
        return pl.BlockSpec((1, 1, 1, d), lambda bi, m: (row(bi), k, 0, 0))

    gspec = pl.BlockSpec((1, tm, GROUP_W), lambda bi, m: (bi, m, 0))
    xspec = pl.BlockSpec((1, tm, d), lambda bi, m: (bi, m, 0))
    in_specs = [
        gspec, gspec,
        pl.BlockSpec((1, tm, 4 * GROUP_W), lambda bi, m: (bi, m, 0)),
        pl.BlockSpec((1, HALO, 4 * GROUP_W), lambda bi, m: (bi, jnp.maximum(m * th - 1, 0), 0)),
        pl.BlockSpec((1, HALO, 4 * GROUP_W), lambda bi, m: (bi, jnp.minimum((m + 1) * th, nh - 1), 0)),
        pl.BlockSpec((4, POOL_CH, POOL_CH), lambda bi, m: (0, 0, 0)),
        pl.BlockSpec((1, GROUP_W), lambda bi, m: (0, 0)),
        pl.BlockSpec((3, GROUP_W), lambda bi, m: (0, 0)),
        pl.BlockSpec((d, d), lambda bi, m: (0, 0), pipeline_mode=pl.Buffered(1)),
        xspec, mspec(2),
        pl.BlockSpec((1, d), lambda bi, m: (0, 0)),
        mspec(3), mspec(4),
    ]
    args = [o_a, o_d, proj, proj, proj, pool_w.astype(BF16), pool_scale.reshape(1, GROUP_W), conv_w, wo,
            x, mods, nw.reshape(1, d), mods, mods]
    if moe:
        nm = s // tm
        in_specs.append(pl.BlockSpec((N_EXPERTS, d), lambda bi, m: (0, 0)))
        args.append(router.T)
        out_shape = [jax.ShapeDtypeStruct((b, s, d), F32),
                     jax.ShapeDtypeStruct((PACK_PLANES, b * s, PLANE_W), jnp.uint32),
                     jax.ShapeDtypeStruct((b, s, 4 * LANES), F32)]
        out_specs = [xspec,
                     pl.BlockSpec((PACK_PLANES, tm, PLANE_W), lambda bi, m: (0, bi * nm + m, 0)),
                     pl.BlockSpec((1, tm, 4 * LANES), lambda bi, m: (bi, m, 0))]
    else:
        out_shape = [jax.ShapeDtypeStruct((b, s, d), F32), jax.ShapeDtypeStruct((b, s, d), BF16)]
        out_specs = [xspec, xspec]
    return pl.pallas_call(
        functools.partial(_outproj_kernel, tm=tm, moe=moe),
        grid=(b, s // tm),
        in_specs=in_specs,
        out_specs=out_specs,
        out_shape=out_shape,
        scratch_shapes=[pltpu.VMEM((tm + 2 * HALO, GROUP_W), F32), pltpu.VMEM((tm + 2 * HALO, GROUP_W), F32)],
        compiler_params=_cparams(("arbitrary", "arbitrary")),
        name="outproj",
    )(*args)


def _swiglu_accumulate(h_ref, wg_ref, wu_ref, wd_ref, acc_ref):
    h = h_ref[...]
    a = _dot(h, wg_ref[0].astype(BF16))
    u = _dot(h, wu_ref[0].astype(BF16))
    act = a * (1.0 / (1.0 + jnp.exp(-a))) * u
    acc_ref[...] += _dot(act.astype(BF16), wd_ref[0].astype(BF16))


def _ffn_kernel(h_ref, wg_ref, wu_ref, wd_ref, o_ref, acc_ref):
    f = pl.program_id(1)

    @pl.when(f == 0)
    def _():
        acc_ref[...] = jnp.zeros(acc_ref.shape, F32)

    _swiglu_accumulate(h_ref, wg_ref, wu_ref, wd_ref, acc_ref)

    @pl.when(f == pl.num_programs(1) - 1)
    def _():
        o_ref[...] = acc_ref[...].astype(BF16)


def _ffn(h, wg, wu, wd, layer):
    mt, d = h.shape
    ff = wg.shape[2]
    tm = min(mt, 1024)
    tf = 256
    return pl.pallas_call(
        _ffn_kernel,
        grid=(mt // tm, ff // tf),
        in_specs=[
            pl.BlockSpec((tm, d), lambda m, f: (m, 0)),
            pl.BlockSpec((1, d, tf), lambda m, f: (layer, 0, f)),
            pl.BlockSpec((1, d, tf), lambda m, f: (layer, 0, f)),
            pl.BlockSpec((1, tf, d), lambda m, f: (layer, f, 0)),
        ],
        out_specs=pl.BlockSpec((tm, d), lambda m, f: (m, 0)),
        out_shape=jax.ShapeDtypeStruct((mt, d), BF16),
        scratch_shapes=[pltpu.VMEM((tm, d), F32)],
        compiler_params=_cparams(("arbitrary", "arbitrary")),
        name="ffn",
    )(h, wg, wu, wd)


def _rank_kernel(r_ref, o_ref, cnt_ref, carry):
    i = pl.program_id(0)
    tm = r_ref.shape[0]

    @pl.when(i == 0)
    def _():
        carry[...] = jnp.zeros(carry.shape, F32)

    oh1 = r_ref[:, 0:LANES]
    oh2 = r_ref[:, LANES:2 * LANES]
    sel = oh1 + oh2
    row = lax.broadcasted_iota(jnp.int32, (tm, tm), 0)
    col = lax.broadcasted_iota(jnp.int32, (tm, tm), 1)
    tri = jnp.where(col < row, 1.0, 0.0).astype(BF16)
    excl = _dot(tri, sel.astype(BF16)) + carry[0:1, :]
    lane = lax.broadcasted_iota(jnp.int32, (tm, LANES), 1)
    lane_f = lane.astype(F32)
    e1 = jnp.sum(oh1 * lane_f, axis=-1, keepdims=True)
    r1 = jnp.sum(oh1 * excl, axis=-1, keepdims=True)
    e2 = jnp.sum(oh2 * lane_f, axis=-1, keepdims=True)
    r2 = jnp.sum(oh2 * excl, axis=-1, keepdims=True)
    o_ref[...] = jnp.where(lane == 0, e1, jnp.where(lane == 1, r1, jnp.where(lane == 2, e2,
                           jnp.where(lane == 3, r2, 0.0))))
    carry[...] = carry[...] + jnp.sum(sel, axis=0, keepdims=True)
    cnt_ref[...] = carry[...]


def _rank(route):
    t = route.shape[0]
    tm = min(t, 512)
    return pl.pallas_call(
        _rank_kernel,
        grid=(t // tm,),
        in_specs=[pl.BlockSpec((tm, 2 * LANES), lambda i: (i, 0))],
        out_specs=[pl.BlockSpec((tm, LANES), lambda i: (i, 0)), pl.BlockSpec((8, LANES), lambda i: (0, 0))],
        out_shape=[jax.ShapeDtypeStruct((t, LANES), F32), jax.ShapeDtypeStruct((8, LANES), F32)],
        scratch_shapes=[pltpu.VMEM((8, LANES), F32)],
        compiler_params=_cparams(("arbitrary",)),
        name="rank",
    )(route)


def _sc_mesh():
    return plsc.VectorSubcoreMesh(core_axis_name="core", subcore_axis_name="subcore")


def _sc_scatter2(x, idx1, idx2, n_out):
    n, w = x.shape

    @functools.partial(pl.kernel, out_type=jax.ShapeDtypeStruct((n_out, w), x.dtype), mesh=_sc_mesh(),
                       scratch_types=[], name="sc_dispatch")
    def k(x_hbm, i1_hbm, i2_hbm, o_hbm):
        def body(x_vmem, i1_vmem, i2_vmem):
            pltpu.sync_copy(x_vmem, o_hbm.at[i1_vmem.at[0]])
            pltpu.sync_copy(x_vmem, o_hbm.at[i2_vmem.at[0]])

        pltpu.emit_pipeline(
            body, grid=(n // SC_WINDOW,),
            in_specs=[pl.BlockSpec((SC_WINDOW, w), lambda i: (i, 0)),
                      pl.BlockSpec((1, SC_WINDOW), lambda i: (0, i)),
                      pl.BlockSpec((1, SC_WINDOW), lambda i: (0, i))],
            out_specs=[], core_axis_name=("core", "subcore"),
            dimension_semantics=(pltpu.PARALLEL,))(x_hbm, i1_hbm, i2_hbm)

    return k(x, idx1.reshape(1, n), idx2.reshape(1, n))


def _sc_gather(y, idx):
    (n,) = idx.shape
    w = y.shape[1]

    @functools.partial(pl.kernel, out_type=jax.ShapeDtypeStruct((n, w), y.dtype), mesh=_sc_mesh(),
                       scratch_types=[], name="sc_combine")
    def k(y_hbm, i_hbm, o_hbm):
        def body(i_vmem, o_vmem):
            pltpu.sync_copy(y_hbm.at[i_vmem.at[0]], o_vmem)

        pltpu.emit_pipeline(
            body, grid=(n // SC_WINDOW,),
            in_specs=[pl.BlockSpec((1, SC_WINDOW), lambda i: (0, i))],
            out_specs=[pl.BlockSpec((SC_WINDOW, w), lambda i: (i, 0))],
            core_axis_name=("core", "subcore"),
            dimension_semantics=(pltpu.PARALLEL,))(i_hbm, o_hbm)

    return k(y, idx.reshape(1, n))


MOE_SUB = 256


def _moe_kernel(te_ref, nu_ref, tv_ref, h_ref, wg_ref, wu_ref, wd_ref, o_ref, acc_ref, hb_ref):
    m = pl.program_id(0)
    f = pl.program_id(1)
    tm = hb_ref.shape[0]
    half = hb_ref.shape[1] // 2

    @pl.when(m < nu_ref[0])
    def _():
        @pl.when(f == 0)
        def _():
            acc_ref[...] = jnp.zeros(acc_ref.shape, F32)
            for c in range(PACK_PLANES):
                w = h_ref[c]
                hb_ref[:, c * PLANE_W:(c + 1) * PLANE_W] = _unpack_hi(w).astype(BF16)
                hb_ref[:, half + c * PLANE_W:half + (c + 1) * PLANE_W] = _unpack_lo(w).astype(BF16)

        valid = tv_ref[m]

        @pl.when(valid > tm - MOE_SUB)
        def _():
            _swiglu_accumulate(hb_ref, wg_ref, wu_ref, wd_ref, acc_ref)

        @pl.when(valid <= tm - MOE_SUB)
        def _():
            wg = wg_ref[0].astype(BF16)
            wu = wu_ref[0].astype(BF16)
            wd = wd_ref[0].astype(BF16)
            for sb in range(tm // MOE_SUB - 1):
                @pl.when(sb * MOE_SUB < valid)
                def _():
                    rows = slice(sb * MOE_SUB, (sb + 1) * MOE_SUB)
                    h = hb_ref[rows, :]
                    a = _dot(h, wg)
                    u = _dot(h, wu)
                    act = a * (1.0 / (1.0 + jnp.exp(-a))) * u
                    acc_ref[rows, :] += _dot(act.astype(BF16), wd)

        @pl.when(f == pl.num_programs(1) - 1)
        def _():
            packed = _pack_bf16_pairs(acc_ref[...])
            for c in range(PACK_PLANES):
                o_ref[c] = packed[:, c * PLANE_W:(c + 1) * PLANE_W]


MOE_TM = 1024


def _moe_routed(hp, tile_expert, n_used, tile_valid, wg, wu, wd, layer, tm):
    _, r, _ = hp.shape
    _, ne, d, ff = wg.shape
    wg = wg.reshape(-1, d, ff)
    wu = wu.reshape(-1, d, ff)
    wd = wd.reshape(-1, ff, d)
    e0 = layer * ne
    tf = 256
    nf = ff // tf

    def row_map(m, f, te, nu, tv):
        return (0, jnp.minimum(m, nu[0] - 1), 0)

    def fidx(m, f, nu):
        return jnp.where(m < nu[0], f, nf - 1)

    hspec = pl.BlockSpec((PACK_PLANES, tm, PLANE_W), row_map)
    return pl.pallas_call(
        _moe_kernel,
        grid_spec=pltpu.PrefetchScalarGridSpec(
            num_scalar_prefetch=3,
            grid=(r // tm, nf),
            in_specs=[
                hspec,
                pl.BlockSpec((1, d, tf), lambda m, f, te, nu, tv: (e0 + te[m], 0, fidx(m, f, nu))),
                pl.BlockSpec((1, d, tf), lambda m, f, te, nu, tv: (e0 + te[m], 0, fidx(m, f, nu))),
                pl.BlockSpec((1, tf, d), lambda m, f, te, nu, tv: (e0 + te[m], fidx(m, f, nu), 0)),
            ],
            out_specs=hspec,
            scratch_shapes=[pltpu.VMEM((tm, d), F32), pltpu.VMEM((tm, d), BF16)],
        ),
        out_shape=jax.ShapeDtypeStruct(hp.shape, jnp.uint32),
        compiler_params=_cparams(("arbitrary", "arbitrary")),
        name="moe",
    )(tile_expert, n_used, tile_valid, hp, wg, wu, wd)


def _route_plan(meta, counts, tm, n_tiles):
    e1 = meta[:, 0].astype(jnp.int32)
    r1 = meta[:, 1].astype(jnp.int32)
    e2 = meta[:, 2].astype(jnp.int32)
    r2 = meta[:, 3].astype(jnp.int32)
    cnt = counts[0, :N_EXPERTS].astype(jnp.int32)
    tiles_per = (cnt + tm - 1) // tm
    tile_end = jnp.cumsum(tiles_per)
    start_row = (tile_end - tiles_per) * tm
    pos1 = start_row[e1] + r1
    pos2 = start_row[e2] + r2
    n_used = tile_end[-1]
    tiles = jnp.arange(n_tiles, dtype=jnp.int32)
    tile_expert = jnp.sum((tiles[:, None] >= tile_end[None, :]).astype(jnp.int32), axis=1)
    last_expert = jnp.sum((n_used - 1 >= tile_end).astype(jnp.int32))
    tile_expert = jnp.minimum(tile_expert, last_expert)
    tile_start = (tile_end - tiles_per)[tile_expert]
    tile_valid = jnp.clip(cnt[tile_expert] - (tiles - tile_start) * tm, 0, tm)
    return pos1, pos2, tile_expert, n_used.reshape(1), tile_valid


def _plane_rows(pos, n_rows):
    return (jnp.arange(PACK_PLANES, dtype=jnp.int32)[:, None] * n_rows + pos[None, :]).reshape(-1)


def _moe(hp, route, wg, wu, wd, layer):
    _, t, _ = hp.shape
    tm = min(MOE_TM, t)
    n_tiles = 2 * t // tm + N_EXPERTS
    n_rows = n_tiles * tm
    meta, counts = _rank(route)
    pos1, pos2, tile_expert, n_used, tile_valid = _route_plan(meta, counts, tm, n_tiles)
    i1 = _plane_rows(pos1, n_rows)
    i2 = _plane_rows(pos2, n_rows)
    hs = _sc_scatter2(hp.reshape(PACK_PLANES * t, PLANE_W), i1, i2, PACK_PLANES * n_rows)
    ys = _moe_routed(hs.reshape(PACK_PLANES, n_rows, PLANE_W), tile_expert, n_used, tile_valid,
                     wg, wu, wd, layer, tm)
    ys = ys.reshape(PACK_PLANES * n_rows, PLANE_W)
    y1 = _sc_gather(ys, i1).reshape(PACK_PLANES, t, PLANE_W)
    y2 = _sc_gather(ys, i2).reshape(PACK_PLANES, t, PLANE_W)
    return y1, y2


def _final_kernel(x_ref, f_ref, g_ref, nw_ref, o_ref):
    xv = x_ref[0] + g_ref[0, 0] * f_ref[0].astype(F32)
    ms = jnp.mean(xv * xv, axis=-1, keepdims=True)
    o_ref[0] = xv * lax.rsqrt(ms + EPS) * nw_ref[...]


def _final(x, f, mods, nw):
    b, s, d = x.shape
    tm = min(s, 512)
    xspec = pl.BlockSpec((1, tm, d), lambda bi, m: (bi, m, 0))
    return pl.pallas_call(
        _final_kernel,
        grid=(b, s // tm),
        in_specs=[xspec, xspec,
                  pl.BlockSpec((1, 1, 1, d), lambda bi, m: (bi, 5, 0, 0)),
                  pl.BlockSpec((1, d), lambda bi, m: (0, 0))],
        out_specs=xspec,
        out_shape=jax.ShapeDtypeStruct((b, s, d), F32),
        compiler_params=_cparams(("arbitrary", "arbitrary")),
        name="final",
    )(x, f, mods, nw.reshape(1, d))


def _final_moe_kernel(x_ref, y1_ref, y2_ref, p_ref, g_ref, nw_ref, o_ref):
    tm, d = x_ref.shape[1], x_ref.shape[2]
    half = d // 2
    p = p_ref[0]
    p1 = jnp.concatenate([p[:, 0:LANES]] * (PLANE_W // LANES), axis=1)
    p2 = jnp.concatenate([p[:, LANES:2 * LANES]] * (PLANE_W // LANES), axis=1)
    ssq = jnp.zeros((tm, 1), F32)
    for c in range(PACK_PLANES):
        w1 = y1_ref[c]
        w2 = y2_ref[c]
        for unpack, off in ((_unpack_hi, 0), (_unpack_lo, half)):
            sl = slice(off + c * PLANE_W, off + (c + 1) * PLANE_W)
            f = p1 * unpack(w1) + p2 * unpack(w2)
            xv = x_ref[0, :, sl] + g_ref[0, 0, :, sl] * f
            o_ref[0, :, sl] = xv
            ssq = ssq + jnp.sum(xv * xv, axis=-1, keepdims=True)
    o_ref[0] = o_ref[0] * lax.rsqrt(ssq / d + EPS) * nw_ref[...]


def _final_moe(x, y1, y2, route, mods, nw):
    b, s, d = x.shape
    tm = min(s, 512)
    nm = s // tm
    xspec = pl.BlockSpec((1, tm, d), lambda bi, m: (bi, m, 0))
    yspec = pl.BlockSpec((PACK_PLANES, tm, PLANE_W), lambda bi, m: (0, bi * nm + m, 0))
    return pl.pallas_call(
        _final_moe_kernel,
        grid=(b, nm),
        in_specs=[xspec, yspec, yspec,
                  pl.BlockSpec((1, tm, 2 * LANES), lambda bi, m: (bi, m, 1)),
                  pl.BlockSpec((1, 1, 1, d), lambda bi, m: (bi, 5, 0, 0)),
                  pl.BlockSpec((1, d), lambda bi, m: (0, 0))],
        out_specs=xspec,
        out_shape=jax.ShapeDtypeStruct((b, s, d), F32),
        compiler_params=_cparams(("arbitrary", "arbitrary")),
        name="final",
    )(x, y1, y2, route, mods, nw.reshape(1, d))


def _w_in_plan():
    nb = GROUP_W // LANES
    plan = [(3 * nb + j, 0, 0) for j in range(4 * nb)]
    plan += [(j, 1, 0) for j in range(nb)]
    plan += [(nb + j, 0, 0) for j in range(2 * nb)]
    plan += [(7 * nb + j, 1, 0) for j in range(nb)]
    plan += [(8 * nb, 0, 1), (8 * nb, 0, 2)]
    plan += [(8 * nb + 1, 0, 1), (8 * nb + 1, 0, 2)]
    assert len(plan) * LANES == D_PROJ
    return jnp.asarray(plan, jnp.int32).T.reshape(-1)


def _prep_w_kernel(plan_ref, w_ref, o_ref):
    j = pl.program_id(1)
    nblk = pl.num_programs(1)
    x = w_ref[0]
    lane = lax.broadcasted_iota(jnp.int32, x.shape, 1)
    lo = lane < HEAD_DIM
    other = pltpu.roll(x, HEAD_DIM, 1)
    mode = plan_ref[2 * nblk + j]
    x = jnp.where(mode == 1, jnp.where(lo, x, other), jnp.where(mode == 2, jnp.where(lo, other, x), x))
    scale = jnp.where(plan_ref[nblk + j] == 1, HEAD_DIM ** -0.5, 1.0)
    o_ref[0] = (x * scale).astype(BF16)


def _prep_w_in(w_in):
    depth, d, _ = w_in.shape
    nblk = D_PROJ // LANES
    return pl.pallas_call(
        _prep_w_kernel,
        grid_spec=pltpu.PrefetchScalarGridSpec(
            num_scalar_prefetch=1,
            grid=(depth, nblk),
            in_specs=[pl.BlockSpec((1, d, LANES), lambda i, j, plan: (i, 0, plan[j]))],
            out_specs=pl.BlockSpec((1, d, LANES), lambda i, j, plan: (i, 0, j)),
        ),
        out_shape=jax.ShapeDtypeStruct((depth, d, D_PROJ), BF16),
        compiler_params=_cparams(("arbitrary", "arbitrary")),
        name="prep_w_in",
    )(_w_in_plan(), w_in)


def _rope_tables(n):
    t = jnp.arange(n, dtype=jnp.int32)
    rows = (t // GRID_W).astype(F32)
    cols = (t % GRID_W).astype(F32)
    nf = HEAD_DIM // 4
    inv = ROPE_BASE ** (-jnp.arange(nf, dtype=F32) / nf)
    ang = jnp.concatenate([rows[:, None] * inv, cols[:, None] * inv], axis=-1)
    cos = jnp.cos(ang)
    sin = jnp.sin(ang)
    cos_t = jnp.concatenate([cos, cos, cos, cos], axis=-1)
    sin_t = jnp.concatenate([-sin, sin, -sin, sin], axis=-1)
    return cos_t, sin_t


def kernel(x, c, ctx, c_ctx, w_ada, b_ada, norm_mix, norm_ffn, norm_final, w_in, w_out, na_rpb, pool_w, pool_scale, conv_w, swa_sink, ffn_w_gate, ffn_w_up, ffn_w_down, moe_router, moe_w_gate, moe_w_up, moe_w_down):
    b, s, d = x.shape
    lc = ctx.shape[1]
    depth = w_ada.shape[0]
    cond = jnp.concatenate([c, c_ctx[None, :]], axis=0)
    assert b == 2
    mods_all = _ada(cond, w_ada, b_ada)
    cos_t, sin_t = _rope_tables(s)
    cos_c = jnp.ones((lc, LANES), F32)
    sin_c = jnp.zeros((lc, LANES), F32)
    row_x = lambda bi: bi
    row_c = lambda bi: b

    w_proj = _prep_w_in(w_in)
    cx = ctx
    fx = fc = None
    for i in range(depth):
        last = i == depth - 1
        mods = mods_all[i].reshape(8, 6, 1, d)
        wi = (w_proj, i)
        wo = w_out[i].astype(BF16)
        if i == 0:
            proj, _ = _inproj(x, wi, norm_mix[i], mods, row_x, cos_t, sin_t)
            projc, _ = _inproj(cx, wi, norm_mix[i], mods, row_c, cos_c, sin_c)
        else:
            mods_prev = mods_all[i - 1].reshape(8, 6, 1, d)
            proj, x = _inproj_res(x, fx, mods_prev, wi, norm_mix[i], mods, row_x, cos_t, sin_t)
            projc, cx = _inproj_res(cx, fc, mods_prev, wi, norm_mix[i], mods, row_c, cos_c, sin_c)
        bias = _na_bias(na_rpb[i])
        o_a = _na_attn(proj, projc, bias)
        o_d = _swa_attn(proj, projc, swa_sink[i])
        moe = i % 2 == 1
        router = moe_router[i // 2] if moe else None
        outs = _outproj(o_a, o_d, proj, x, wo, pool_w[i], pool_scale[i], conv_w[i], norm_ffn[i], mods, row_x,
                        router)
        x, h2 = outs[0], outs[1]
        if not last:
            oc_a, oc_d = _ctx_attn(projc, swa_sink[i])
            cx, h2c = _outproj(oc_a, oc_d, projc, cx, wo, pool_w[i], pool_scale[i], conv_w[i], norm_ffn[i],
                               mods, row_c)
        j = i // 2
        if moe:
            if not last:
                raise NotImplementedError("an expert layer that is not the last layer")
            route = outs[2]
            y1, y2 = _moe(h2, route.reshape(b * s, 4 * LANES), moe_w_gate, moe_w_up, moe_w_down, j)
            return _final_moe(x, y1, y2, route, mods, norm_final)
        fx = _ffn(h2.reshape(b * s, d), ffn_w_gate, ffn_w_up, ffn_w_down, j).reshape(b, s, d)
        if not last:
            fc = _ffn(h2c.reshape(b * lc, d), ffn_w_gate, ffn_w_up, ffn_w_down, j).reshape(b, lc, d)
    return _final(x, fx, mods_all[depth - 1].reshape(8, 6, 1, d), norm_final)


def _inproj_res(x, f, mods_prev, w, nw, mods, mod_row, cos_t, sin_t):
    both = jnp.concatenate([mods, mods_prev], axis=1)
    return _inproj(x, w, nw, both, mod_row, cos_t, sin_t, res=(f, 6 + 5))
```

```python
import functools

import jax
import jax.numpy as jnp
from jax import lax
from jax.experimental import pallas as pl
from jax.experimental.pallas import tpu as pltpu
from jax.experimental.pallas import tpu_sc as plsc

F32 = jnp.float32
BF16 = jnp.bfloat16

D_MODEL = 2048
GRID_W = 64
HEAD_DIM = 64
EPS = 1e-6
NEG_INF = -1e30
GROUP_W = 512
NA_KH = 8
NA_KW = 16
POOL_WINDOWS = (2, 4, 8, 16)
POOL_CH = 128
SWA_WINDOW = 128
SWA_BLOCK = 128
ROPE_BASE = 10000.0
N_EXPERTS = 8
LANES = 128
HALO = 16

C_UBC = 0
C_QA = 2048
C_KA = 2560
C_VA = 3072
C_QD = 3584
C_KD = 4096
C_VD = 4352
D_PROJ = 4608
TN_PROJ = 512

VMEM_LIMIT = 56 * 1024 * 1024


def _cparams(sem):
    return pltpu.CompilerParams(dimension_semantics=sem, vmem_limit_bytes=VMEM_LIMIT)


def _dot(a, b):
    return jnp.dot(a, b, preferred_element_type=F32)


def _dot_t(a, b):
    return lax.dot_general(a, b, (((1,), (1,)), ((), ())), preferred_element_type=F32)


def _rms_mod(xv, nw, sh, sc):
    ms = jnp.mean(xv * xv, axis=-1, keepdims=True)
    y = xv * lax.rsqrt(ms + EPS) * nw
    return y * (1.0 + sc) + sh


PACK_PLANES = 4
PLANE_W = D_MODEL // 2 // PACK_PLANES
SC_WINDOW = 128


def _pack_bf16_pairs(v):
    half = v.shape[1] // 2
    hi = lax.bitcast_convert_type(v[:, :half].astype(BF16).astype(F32), jnp.uint32)
    lo = lax.bitcast_convert_type(v[:, half:].astype(BF16).astype(F32), jnp.uint32)
    return hi | (lo >> 16)


def _unpack_hi(w):
    return lax.bitcast_convert_type(w & jnp.uint32(0xFFFF0000), F32)


def _unpack_lo(w):
    return lax.bitcast_convert_type(w << 16, F32)


def _ada_kernel(c_ref, w_ref, b_ref, o_ref):
    tn = w_ref.shape[2]
    o_ref[...] = jnp.zeros(o_ref.shape, F32)
    for r in range(3):
        cv = c_ref[r]
        m = cv * (1.0 / (1.0 + jnp.exp(-cv)))
        cols = []
        for j in range(tn // LANES):
            wj = w_ref[0, :, j * LANES:(j + 1) * LANES]
            cols.append(jnp.sum(wj * m, axis=0, keepdims=True))
        o_ref[0, r:r + 1, :] = jnp.concatenate(cols, axis=1) + b_ref[0]


def _ada(cond, w_ada, b_ada):
    depth, d, n6 = w_ada.shape
    tn = 1024
    cb = jnp.broadcast_to(cond[:, :, None], (3, d, LANES))
    return pl.pallas_call(
        _ada_kernel,
        grid=(depth, n6 // tn),
        in_specs=[
            pl.BlockSpec((3, d, LANES), lambda i, j: (0, 0, 0)),
            pl.BlockSpec((1, d, tn), lambda i, j: (i, 0, j)),
            pl.BlockSpec((1, 1, tn), lambda i, j: (i, 0, j)),
        ],
        out_specs=pl.BlockSpec((1, 8, tn), lambda i, j: (i, 0, j)),
        out_shape=jax.ShapeDtypeStruct((depth, 8, n6), F32),
        compiler_params=_cparams(("arbitrary", "arbitrary")),
        name="ada",
    )(cb, w_ada, b_ada.reshape(depth, 1, n6))


def _rope(a, cosv, sinv, lo32):
    sw = jnp.where(lo32, pltpu.roll(a, 96, 1), pltpu.roll(a, 32, 1))
    return a * cosv + sw * sinv


def _inproj_kernel(*refs, has_res):
    if has_res:
        x_ref, f_ref, g_ref, nw_ref, sh_ref, sc_ref, cos_ref, sin_ref, w_ref, o_ref, x2_ref = refs
    else:
        x_ref, nw_ref, sh_ref, sc_ref, cos_ref, sin_ref, w_ref, o_ref = refs
    xv = x_ref[0]
    if has_res:
        xv = xv + g_ref[0, 0] * f_ref[0].astype(F32)
        x2_ref[0] = xv
    h = _rms_mod(xv, nw_ref[...], sh_ref[0, 0], sc_ref[0, 0]).astype(BF16)
    tm = h.shape[0]
    lane = lax.broadcasted_iota(jnp.int32, (tm, LANES), 1)
    lo32 = (lane % HEAD_DIM) < (HEAD_DIM // 2)
    rope_end = C_VD
    for n in range(D_PROJ // TN_PROJ):
        c0 = n * TN_PROJ
        acc = _dot(h, w_ref[0, :, c0:c0 + TN_PROJ])
        if c0 + TN_PROJ <= C_QD:
            o_ref[0, :, c0:c0 + TN_PROJ] = acc.astype(BF16)
            continue
        for g in range(TN_PROJ // LANES):
            a = acc[:, g * LANES:(g + 1) * LANES]
            if c0 + g * LANES < rope_end:
                a = _rope(a, cos_ref[...], sin_ref[...], lo32)
            o_ref[0, :, c0 + g * LANES:c0 + (g + 1) * LANES] = a.astype(BF16)


def _inproj(x, w, nw, mods, mod_row, cos_t, sin_t, res=None):
    b, s, d = x.shape
    w_all, layer = w
    tm = min(s, 512)
    has_res = res is not None
    row = mod_row
    xspec = pl.BlockSpec((1, tm, d), lambda bi, m: (bi, m, 0))

    def mspec(k):
        return pl.BlockSpec((1, 1, 1, d), lambda bi, m: (row(bi), k, 0, 0))

    in_specs = [xspec]
    args = [x]
    if has_res:
        f, gk = res
        in_specs += [xspec, mspec(gk)]
        args += [f, mods]
    in_specs += [
        pl.BlockSpec((1, d), lambda bi, m: (0, 0)),
        mspec(0), mspec(1),
        pl.BlockSpec((tm, LANES), lambda bi, m: (m, 0)),
        pl.BlockSpec((tm, LANES), lambda bi, m: (m, 0)),
        pl.BlockSpec((1, d, D_PROJ), lambda bi, m: (layer, 0, 0), pipeline_mode=pl.Buffered(1)),
    ]
    args += [nw.reshape(1, d), mods, mods, cos_t, sin_t, w_all]
    out_shape = [jax.ShapeDtypeStruct((b, s, D_PROJ), BF16)]
    out_specs = [pl.BlockSpec((1, tm, D_PROJ), lambda bi, m: (bi, m, 0))]
    if has_res:
        out_shape.append(jax.ShapeDtypeStruct((b, s, d), F32))
        out_specs.append(xspec)
    outs = pl.pallas_call(
        functools.partial(_inproj_kernel, has_res=has_res),
        grid=(b, s // tm),
        in_specs=in_specs,
        out_specs=out_specs,
        out_shape=out_shape,
        compiler_params=_cparams(("arbitrary", "arbitrary")),
        name="inproj",
    )(*args)
    return outs if has_res else (outs[0], None)


def _na_bias_kernel(rpb_ref, o_ref):
    h = pl.program_id(0)
    nd = 2 * NA_KH - 1
    nj = 2 * NA_KW - 1
    q = lax.broadcasted_iota(jnp.int32, (GRID_W, LANES), 0)
    lane = lax.broadcasted_iota(jnp.int32, (GRID_W, LANES), 1)
    kw = lane % GRID_W
    dc = jnp.clip(kw - q, -(NA_KW - 1), NA_KW - 1) + (NA_KW - 1)
    c0 = jnp.clip(q - NA_KW // 2, 0, GRID_W - NA_KW)
    valid = (kw >= c0) & (kw < c0 + NA_KW)
    tabs = []
    for d in range(nd):
        t = jnp.zeros((GRID_W, LANES), F32)
        for j in range(nj):
            t = jnp.where(dc == j, rpb_ref[h * (nd * nj) + d * nj + j], t)
        tabs.append(jnp.where(valid, t, NEG_INF))
    for c in range(NA_KH):
        for g in range(NA_KH // 2):
            d_lo = 2 * g - c + (NA_KH - 1)
            o_ref[c, 0, :, g * LANES:(g + 1) * LANES] = jnp.where(lane < GRID_W, tabs[d_lo], tabs[d_lo + 1])


def _na_bias(rpb):
    nh = rpb.shape[0]
    return pl.pallas_call(
        _na_bias_kernel,
        grid=(nh,),
        in_specs=[pl.BlockSpec(memory_space=pltpu.SMEM)],
        out_specs=pl.BlockSpec((NA_KH, 1, GRID_W, NA_KH * GRID_W), lambda h: (0, h, 0, 0)),
        out_shape=jax.ShapeDtypeStruct((NA_KH, nh, GRID_W, NA_KH * GRID_W), F32),
        compiler_params=_cparams(("arbitrary",)),
        name="na_bias",
    )(rpb.reshape(-1))


def _softmax_pv_staged(chains):
    add = lambda a, b: a + b
    ms = []
    for sp, _, extra in chains:
        m = functools.reduce(jnp.maximum, [jnp.max(s, axis=-1, keepdims=True) for s in sp])
        ms.append(m if extra is None else jnp.maximum(m, extra))
    ps = [[jnp.exp(s - m) for s in sp] for (sp, _, _), m in zip(chains, ms)]
    dens = []
    for pp, (_, _, extra), m in zip(ps, chains, ms):
        den = functools.reduce(add, [jnp.sum(p, axis=-1, keepdims=True) for p in pp])
        dens.append(den if extra is None else den + jnp.exp(extra - m))
    outs = [functools.reduce(add, [_dot(p.astype(BF16), v) for p, v in zip(pp, vp)])
            for pp, (_, vp, _) in zip(ps, chains)]
    return [o / d for o, d in zip(outs, dens)]


ROWS_PER_STEP = 8
NA_ROW_GROUP = 4


def _na_kernel(q_ref, k_ref, v_ref, kc_ref, vc_ref, bias_ref, o_ref, *, rows):
    rb = pl.program_id(2)
    lane = lax.broadcasted_iota(jnp.int32, (GRID_W, LANES), 1)
    lo = lane < HEAD_DIM
    kc = kc_ref[0]
    vc = vc_ref[0]
    kh = min(NA_KH, rows)
    win = kh * GRID_W

    for i0 in range(0, ROWS_PER_STEP, NA_ROW_GROUP):
        chains = []
        for i in range(i0, i0 + NA_ROW_GROUP):
            r = rb * ROWS_PER_STEP + i
            r0 = jnp.clip(r - kh // 2, 0, rows - kh)
            c = r - r0
            koff = pl.multiple_of(r0 * GRID_W, GRID_W)
            q = q_ref[0, i * GRID_W:(i + 1) * GRID_W, :]
            kw = k_ref[0, pl.ds(koff, win), :]
            vw = v_ref[0, pl.ds(koff, win), :]
            for hh in range(2):
                qm = jnp.where(lo if hh == 0 else jnp.logical_not(lo), q, jnp.zeros_like(q))
                chains.append(([_dot_t(qm, kw) + bias_ref[c, hh], _dot_t(qm, kc)], [vw, vc], None))
        outs = _softmax_pv_staged(chains)
        for j, i in enumerate(range(i0, i0 + NA_ROW_GROUP)):
            o_ref[0, i * GRID_W:(i + 1) * GRID_W, :] = jnp.where(lo, outs[2 * j], outs[2 * j + 1]).astype(BF16)


def _na_attn(proj, projc, bias):
    b, s, _ = proj.shape
    lc = projc.shape[1]
    rows = s // GRID_W
    assert rows >= NA_KH and rows % ROWS_PER_STEP == 0
    tq = ROWS_PER_STEP * GRID_W
    npair = GROUP_W // LANES
    return pl.pallas_call(
        functools.partial(_na_kernel, rows=rows),
        grid=(b, npair, rows // ROWS_PER_STEP),
        in_specs=[
            pl.BlockSpec((1, tq, LANES), lambda bi, p, r: (bi, r, C_QA // LANES + p)),
            pl.BlockSpec((1, s, LANES), lambda bi, p, r: (bi, 0, C_KA // LANES + p)),
            pl.BlockSpec((1, s, LANES), lambda bi, p, r: (bi, 0, C_VA // LANES + p)),
            pl.BlockSpec((1, lc, LANES), lambda bi, p, r: (bi, 0, C_KA // LANES + p)),
            pl.BlockSpec((1, lc, LANES), lambda bi, p, r: (bi, 0, C_VA // LANES + p)),
            pl.BlockSpec((NA_KH, 2, GRID_W, NA_KH * GRID_W), lambda bi, p, r: (0, p, 0, 0)),
        ],
        out_specs=pl.BlockSpec((1, tq, LANES), lambda bi, p, r: (bi, r, p)),
        out_shape=jax.ShapeDtypeStruct((b, s, GROUP_W), BF16),
        compiler_params=_cparams(("arbitrary", "arbitrary", "arbitrary")),
        name="na_attn",
    )(proj, proj, proj, projc, projc, bias)


def _gqa_chains(q, kv, sink_ref, k_parts, v_parts, mask):
    m = q.shape[0]
    lane = lax.broadcasted_iota(jnp.int32, (m, LANES), 1)
    lo = lane < HEAD_DIM
    qs = []
    for j in range(2):
        qg = q[:, (kv * 2 + j) * LANES:(kv * 2 + j + 1) * LANES]
        qs.append(jnp.where(lo, qg, jnp.zeros_like(qg)))
        qs.append(jnp.where(lo, jnp.zeros_like(qg), qg))
    qq = jnp.concatenate(qs, axis=0)
    s_all = [_dot_t(qq, k) for k in k_parts]
    chains = []
    for i in range(4):
        parts = [s[i * m:(i + 1) * m] for s in s_all]
        if mask is not None:
            parts[0] = jnp.where(mask, parts[0], NEG_INF)
        chains.append((parts, v_parts, jnp.full((m, 1), sink_ref[kv * 4 + i], F32)))
    return chains


def _gqa_store(o_ref, heads):
    lane = lax.broadcasted_iota(jnp.int32, heads[0].shape, 1)
    lo = lane < HEAD_DIM
    for g in range(len(heads) // 2):
        o_ref[0, :, g * LANES:(g + 1) * LANES] = jnp.where(lo, heads[2 * g], heads[2 * g + 1]).astype(BF16)


def _swa_kernel(sink_ref, q_ref, k_ref, v_ref, kc_ref, vc_ref, o_ref):
    n = pl.program_id(1)
    nb = pl.num_programs(1)
    start = jnp.clip(n - 1, 0, nb - 3)
    koff = pl.multiple_of(start * SWA_BLOCK, SWA_BLOCK)
    kw = k_ref[0, pl.ds(koff, 3 * SWA_BLOCK), :]
    vw = v_ref[0, pl.ds(koff, 3 * SWA_BLOCK), :]
    kc = kc_ref[0]
    vc = vc_ref[0]
    rel = (n - start) * SWA_BLOCK + lax.broadcasted_iota(jnp.int32, (SWA_BLOCK, 3 * SWA_BLOCK), 0)
    kj = lax.broadcasted_iota(jnp.int32, (SWA_BLOCK, 3 * SWA_BLOCK), 1)
    valid = jnp.abs(rel - kj) <= SWA_WINDOW
    q = q_ref[0]
    chains = []
    for kv in range(2):
        sl = slice(kv * LANES, (kv + 1) * LANES)
        chains += _gqa_chains(q, kv, sink_ref, [kw[:, sl], kc[:, sl]], [vw[:, sl], vc[:, sl]], valid)
    _gqa_store(o_ref, _softmax_pv_staged(chains))


def _swa_attn(proj, projc, sink):
    b, s, _ = proj.shape
    lc = projc.shape[1]
    nb = s // SWA_BLOCK
    assert nb >= 3
    w2 = 2 * LANES
    return pl.pallas_call(
        _swa_kernel,
        grid=(b, nb),
        in_specs=[
            pl.BlockSpec(memory_space=pltpu.SMEM),
            pl.BlockSpec((1, SWA_BLOCK, GROUP_W), lambda bi, n: (bi, n, C_QD // GROUP_W)),
            pl.BlockSpec((1, s, w2), lambda bi, n: (bi, 0, C_KD // w2)),
            pl.BlockSpec((1, s, w2), lambda bi, n: (bi, 0, C_VD // w2)),
            pl.BlockSpec((1, lc, w2), lambda bi, n: (bi, 0, C_KD // w2)),
            pl.BlockSpec((1, lc, w2), lambda bi, n: (bi, 0, C_VD // w2)),
        ],
        out_specs=pl.BlockSpec((1, SWA_BLOCK, GROUP_W), lambda bi, n: (bi, n, 0)),
        out_shape=jax.ShapeDtypeStruct((b, s, GROUP_W), BF16),
        compiler_params=_cparams(("arbitrary", "arbitrary")),
        name="swa_attn",
    )(sink.reshape(-1), proj, proj, proj, projc, projc)


def _ctx_attn_kernel(sink_ref, qa_ref, ka_ref, va_ref, qd_ref, kd_ref, vd_ref, oa_ref, od_ref):
    m = qa_ref.shape[1]
    lane = lax.broadcasted_iota(jnp.int32, (m, LANES), 1)
    lo = lane < HEAD_DIM
    chains = []
    for p in range(GROUP_W // LANES):
        sl = slice(p * LANES, (p + 1) * LANES)
        q = qa_ref[0, :, sl]
        k = ka_ref[0, :, sl]
        v = va_ref[0, :, sl]
        for hh in range(2):
            qm = jnp.where(lo if hh == 0 else jnp.logical_not(lo), q, jnp.zeros_like(q))
            chains.append(([_dot_t(qm, k)], [v], None))
    _gqa_store(oa_ref, _softmax_pv_staged(chains))
    q = qd_ref[0]
    chains = []
    for kv in range(2):
        sl = slice(kv * LANES, (kv + 1) * LANES)
        chains += _gqa_chains(q, kv, sink_ref, [kd_ref[0, :, sl]], [vd_ref[0, :, sl]], None)
    _gqa_store(od_ref, _softmax_pv_staged(chains))


def _ctx_attn(projc, sink):
    b, lc, _ = projc.shape
    w2 = 2 * LANES

    def spec(width, col):
        return pl.BlockSpec((1, lc, width), lambda bi: (bi, 0, col // width))

    ospec = pl.BlockSpec((1, lc, GROUP_W), lambda bi: (bi, 0, 0))
    return pl.pallas_call(
        _ctx_attn_kernel,
        grid=(b,),
        in_specs=[pl.BlockSpec(memory_space=pltpu.SMEM),
                  spec(GROUP_W, C_QA), spec(GROUP_W, C_KA), spec(GROUP_W, C_VA),
                  spec(GROUP_W, C_QD), spec(w2, C_KD), spec(w2, C_VD)],
        out_specs=[ospec, ospec],
        out_shape=[jax.ShapeDtypeStruct((b, lc, GROUP_W), BF16)] * 2,
        compiler_params=_cparams(("arbitrary",)),
        name="ctx_attn",
    )(sink.reshape(-1), projc, projc, projc, projc, projc, projc)


def _outproj_kernel(*refs, tm, moe):
    (oa_ref, od_ref, u_ref, up_ref, un_ref, pw_ref, ps_ref, cw_ref, wo_ref, x_ref, g1_ref,
     nw_ref, sh_ref, sc_ref) = refs[:14]
    rest = refs[14:]
    if moe:
        rt_ref, xo_ref, h_ref, gate_ref, ext_ub, ext_u = rest
    else:
        xo_ref, h_ref, ext_ub, ext_u = rest
    m = pl.program_id(1)
    nm = pl.num_programs(1)
    g4 = GROUP_W

    def cols(ref, k):
        return ref[0, :, k * g4:(k + 1) * g4].astype(F32)

    ub = cols(u_ref, 0)
    gb = cols(u_ref, 1)
    has_prev = m > 0
    has_next = m < nm - 1
    ext_ub[0:HALO] = jnp.where(has_prev, cols(up_ref, 0), 0.0)
    ext_ub[HALO:HALO + tm] = ub
    ext_ub[HALO + tm:] = jnp.where(has_next, cols(un_ref, 0), 0.0)
    ext_u[0:HALO] = jnp.where(has_prev, cols(up_ref, 2) * cols(up_ref, 3), 0.0)
    ext_u[HALO:HALO + tm] = cols(u_ref, 2) * cols(u_ref, 3)
    ext_u[HALO + tm:] = jnp.where(has_next, cols(un_ref, 2) * cols(un_ref, 3), 0.0)

    n_tok = nm * tm
    t = m * tm + lax.broadcasted_iota(jnp.int32, (tm, LANES), 0)
    obs = []
    for g, w in enumerate(POOL_WINDOWS):
        sl = slice(g * LANES, (g + 1) * LANES)
        acc = None
        for d in range(-(w // 2), w - w // 2):
            term = ext_ub[HALO + d:HALO + d + tm, sl]
            acc = term if acc is None else acc + term
        cnt = (jnp.clip(t + (w - w // 2), 0, n_tok) - jnp.clip(t - w // 2, 0, n_tok)).astype(F32)
        pooled = acc / cnt - ub[:, sl]
        obs.append(_dot(pooled.astype(BF16), pw_ref[g]) * ps_ref[:, sl])
    o_b = jnp.concatenate(obs, axis=1).astype(BF16)

    y = (ext_u[HALO - 1:HALO - 1 + tm] * cw_ref[0:1, :] + ext_u[HALO:HALO + tm] * cw_ref[1:2, :]
         + ext_u[HALO + 1:HALO + 1 + tm] * cw_ref[2:3, :])
    o_c = (gb * y).astype(BF16)

    yy = _dot(jnp.concatenate([oa_ref[0], o_b, o_c, od_ref[0]], axis=1), wo_ref[...])
    xn = x_ref[0] + g1_ref[0, 0] * yy
    xo_ref[0] = xn
    h2 = _rms_mod(xn, nw_ref[...], sh_ref[0, 0], sc_ref[0, 0])

    if not moe:
        h_ref[0] = h2.astype(BF16)
    else:
        packed = _pack_bf16_pairs(h2)
        for c in range(PACK_PLANES):
            h_ref[c] = packed[:, c * PLANE_W:(c + 1) * PLANE_W]
        lane = lax.broadcasted_iota(jnp.int32, (tm, LANES), 1)
        logits = jnp.full((tm, LANES), -jnp.inf, F32)
        for e in range(N_EXPERTS):
            le = jnp.sum(h2 * rt_ref[e:e + 1, :], axis=-1, keepdims=True)
            logits = jnp.where(lane == e, le, logits)
        v1 = jnp.max(logits, axis=-1, keepdims=True)
        i1 = jnp.min(jnp.where(logits == v1, lane, LANES), axis=-1, keepdims=True)
        l2 = jnp.where(lane == i1, -jnp.inf, logits)
        v2 = jnp.max(l2, axis=-1, keepdims=True)
        i2 = jnp.min(jnp.where(l2 == v2, lane, LANES), axis=-1, keepdims=True)
        e2 = jnp.exp(v2 - v1)
        p1 = 1.0 / (1.0 + e2)
        p2 = e2 / (1.0 + e2)
        gate_ref[0, :, 0:LANES] = jnp.where(lane == i1, 1.0, 0.0)
        gate_ref[0, :, LANES:2 * LANES] = jnp.where(lane == i2, 1.0, 0.0)
        gate_ref[0, :, 2 * LANES:3 * LANES] = jnp.broadcast_to(p1, (tm, LANES))
        gate_ref[0, :, 3 * LANES:4 * LANES] = jnp.broadcast_to(p2, (tm, LANES))


def _outproj(o_a, o_d, proj, x, wo, pool_w, pool_scale, conv_w, nw, mods, mod_row, router=None):
    b, s, d = x.shape
    tm = min(s, 512)
    moe = router is not None
    row = mod_row
    nh = s // HALO
    th = tm // HALO

    def mspec(k):
        return pl.BlockSpec((1, 1, 1, d), lambda bi, m: (row(bi), k, 0, 0))

    gspec = pl.BlockSpec((1, tm, GROUP_W), lambda bi, m: (bi, m, 0))
    xspec = pl.BlockSpec((1, tm, d), lambda bi, m: (bi, m, 0))
    in_specs = [
        gspec, gspec,
        pl.BlockSpec((1, tm, 4 * GROUP_W), lambda bi, m: (bi, m, 0)),
        pl.BlockSpec((1, HALO, 4 * GROUP_W), lambda bi, m: (bi, jnp.maximum(m * th - 1, 0), 0)),
        pl.BlockSpec((1, HALO, 4 * GROUP_W), lambda bi, m: (bi, jnp.minimum((m + 1) * th, nh - 1), 0)),
        pl.BlockSpec((4, POOL_CH, POOL_CH), lambda bi, m: (0, 0, 0)),
        pl.BlockSpec((1, GROUP_W), lambda bi, m: (0, 0)),
        pl.BlockSpec((3, GROUP_W), lambda bi, m: (0, 0)),
        pl.BlockSpec((d, d), lambda bi, m: (0, 0), pipeline_mode=pl.Buffered(1)),
        xspec, mspec(2),
        pl.BlockSpec((1, d), lambda bi, m: (0, 0)),
        mspec(3), mspec(4),
    ]
    args = [o_a, o_d, proj, proj, proj, pool_w.astype(BF16), pool_scale.reshape(1, GROUP_W), conv_w, wo,
            x, mods, nw.reshape(1, d), mods, mods]
    if moe:
        nm = s // tm
        in_specs.append(pl.BlockSpec((N_EXPERTS, d), lambda bi, m: (0, 0)))
        args.append(router.T)
        out_shape = [jax.ShapeDtypeStruct((b, s, d), F32),
                     jax.ShapeDtypeStruct((PACK_PLANES, b * s, PLANE_W), jnp.uint32),
                     jax.ShapeDtypeStruct((b, s, 4 * LANES), F32)]
        out_specs = [xspec,
                     pl.BlockSpec((PACK_PLANES, tm, PLANE_W), lambda bi, m: (0, bi * nm + m, 0)),
                     pl.BlockSpec((1, tm, 4 * LANES), lambda bi, m: (bi, m, 0))]
    else:
        out_shape = [jax.ShapeDtypeStruct((b, s, d), F32), jax.ShapeDtypeStruct((b, s, d), BF16)]
        out_specs = [xspec, xspec]
    return pl.pallas_call(
        functools.partial(_outproj_kernel, tm=tm, moe=moe),
        grid=(b, s // tm),
        in_specs=in_specs,
        out_specs=out_specs,
        out_shape=out_shape,
        scratch_shapes=[pltpu.VMEM((tm + 2 * HALO, GROUP_W), F32), pltpu.VMEM((tm + 2 * HALO, GROUP_W), F32)],
        compiler_params=_cparams(("arbitrary", "arbitrary")),
        name="outproj",
    )(*args)


def _swiglu_accumulate(h_ref, wg_ref, wu_ref, wd_ref, acc_ref, rows=None):
    r = slice(0, h_ref.shape[0] if rows is None else rows)
    h = h_ref[r, :]
    a = _dot(h, wg_ref[0].astype(BF16))
    u = _dot(h, wu_ref[0].astype(BF16))
    act = a * (1.0 / (1.0 + jnp.exp(-a))) * u
    acc_ref[r, :] += _dot(act.astype(BF16), wd_ref[0].astype(BF16))


def _ffn_kernel(h_ref, wg_ref, wu_ref, wd_ref, o_ref, acc_ref):
    f = pl.program_id(1)

    @pl.when(f == 0)
    def _():
        acc_ref[...] = jnp.zeros(acc_ref.shape, F32)

    _swiglu_accumulate(h_ref, wg_ref, wu_ref, wd_ref, acc_ref)

    @pl.when(f == pl.num_programs(1) - 1)
    def _():
        o_ref[...] = acc_ref[...].astype(BF16)


def _ffn(h, wg, wu, wd, layer):
    mt, d = h.shape
    ff = wg.shape[2]
    tm = min(mt, 1024)
    tf = 256
    return pl.pallas_call(
        _ffn_kernel,
        grid=(mt // tm, ff // tf),
        in_specs=[
            pl.BlockSpec((tm, d), lambda m, f: (m, 0)),
            pl.BlockSpec((1, d, tf), lambda m, f: (layer, 0, f)),
            pl.BlockSpec((1, d, tf), lambda m, f: (layer, 0, f)),
            pl.BlockSpec((1, tf, d), lambda m, f: (layer, f, 0)),
        ],
        out_specs=pl.BlockSpec((tm, d), lambda m, f: (m, 0)),
        out_shape=jax.ShapeDtypeStruct((mt, d), BF16),
        scratch_shapes=[pltpu.VMEM((tm, d), F32)],
        compiler_params=_cparams(("arbitrary", "arbitrary")),
        name="ffn",
    )(h, wg, wu, wd)


def _rank_kernel(r_ref, o_ref, cnt_ref, carry):
    i = pl.program_id(0)
    tm = r_ref.shape[0]

    @pl.when(i == 0)
    def _():
        carry[...] = jnp.zeros(carry.shape, F32)

    oh1 = r_ref[:, 0:LANES]
    oh2 = r_ref[:, LANES:2 * LANES]
    sel = oh1 + oh2
    row = lax.broadcasted_iota(jnp.int32, (tm, tm), 0)
    col = lax.broadcasted_iota(jnp.int32, (tm, tm), 1)
    tri = jnp.where(col < row, 1.0, 0.0).astype(BF16)
    excl = _dot(tri, sel.astype(BF16)) + carry[0:1, :]
    lane = lax.broadcasted_iota(jnp.int32, (tm, LANES), 1)
    lane_f = lane.astype(F32)
    e1 = jnp.sum(oh1 * lane_f, axis=-1, keepdims=True)
    r1 = jnp.sum(oh1 * excl, axis=-1, keepdims=True)
    e2 = jnp.sum(oh2 * lane_f, axis=-1, keepdims=True)
    r2 = jnp.sum(oh2 * excl, axis=-1, keepdims=True)
    o_ref[...] = jnp.where(lane == 0, e1, jnp.where(lane == 1, r1, jnp.where(lane == 2, e2,
                           jnp.where(lane == 3, r2, 0.0))))
    carry[...] = carry[...] + jnp.sum(sel, axis=0, keepdims=True)
    cnt_ref[...] = carry[...]


def _rank(route):
    t = route.shape[0]
    tm = min(t, 512)
    return pl.pallas_call(
        _rank_kernel,
        grid=(t // tm,),
        in_specs=[pl.BlockSpec((tm, 2 * LANES), lambda i: (i, 0))],
        out_specs=[pl.BlockSpec((tm, LANES), lambda i: (i, 0)), pl.BlockSpec((8, LANES), lambda i: (0, 0))],
        out_shape=[jax.ShapeDtypeStruct((t, LANES), F32), jax.ShapeDtypeStruct((8, LANES), F32)],
        scratch_shapes=[pltpu.VMEM((8, LANES), F32)],
        compiler_params=_cparams(("arbitrary",)),
        name="rank",
    )(route)


def _sc_mesh():
    return plsc.VectorSubcoreMesh(core_axis_name="core", subcore_axis_name="subcore")


def _sc_scatter2(x, idx1, idx2, n_out):
    n, w = x.shape

    @functools.partial(pl.kernel, out_type=jax.ShapeDtypeStruct((n_out, w), x.dtype), mesh=_sc_mesh(),
                       scratch_types=[], name="sc_dispatch")
    def k(x_hbm, i1_hbm, i2_hbm, o_hbm):
        def body(x_vmem, i1_vmem, i2_vmem):
            pltpu.sync_copy(x_vmem, o_hbm.at[i1_vmem.at[0]])
            pltpu.sync_copy(x_vmem, o_hbm.at[i2_vmem.at[0]])

        pltpu.emit_pipeline(
            body, grid=(n // SC_WINDOW,),
            in_specs=[pl.BlockSpec((SC_WINDOW, w), lambda i: (i, 0)),
                      pl.BlockSpec((1, SC_WINDOW), lambda i: (0, i)),
                      pl.BlockSpec((1, SC_WINDOW), lambda i: (0, i))],
            out_specs=[], core_axis_name=("core", "subcore"),
            dimension_semantics=(pltpu.PARALLEL,))(x_hbm, i1_hbm, i2_hbm)

    return k(x, idx1.reshape(1, n), idx2.reshape(1, n))


def _sc_gather(y, idx):
    (n,) = idx.shape
    w = y.shape[1]

    @functools.partial(pl.kernel, out_type=jax.ShapeDtypeStruct((n, w), y.dtype), mesh=_sc_mesh(),
                       scratch_types=[], name="sc_combine")
    def k(y_hbm, i_hbm, o_hbm):
        def body(i_vmem, o_vmem):
            pltpu.sync_copy(y_hbm.at[i_vmem.at[0]], o_vmem)

        pltpu.emit_pipeline(
            body, grid=(n // SC_WINDOW,),
            in_specs=[pl.BlockSpec((1, SC_WINDOW), lambda i: (0, i))],
            out_specs=[pl.BlockSpec((SC_WINDOW, w), lambda i: (i, 0))],
            core_axis_name=("core", "subcore"),
            dimension_semantics=(pltpu.PARALLEL,))(i_hbm, o_hbm)

    return k(y, idx.reshape(1, n))


MOE_SUB = 256


def _moe_kernel(te_ref, nu_ref, tv_ref, h_ref, wg_ref, wu_ref, wd_ref, o_ref, acc_ref, hb_ref):
    m = pl.program_id(0)
    f = pl.program_id(1)
    tm = hb_ref.shape[0]
    half = hb_ref.shape[1] // 2

    @pl.when(m < nu_ref[0])
    def _():
        @pl.when(f == 0)
        def _():
            acc_ref[...] = jnp.zeros(acc_ref.shape, F32)
            for c in range(PACK_PLANES):
                w = h_ref[c]
                hb_ref[:, c * PLANE_W:(c + 1) * PLANE_W] = _unpack_hi(w).astype(BF16)
                hb_ref[:, half + c * PLANE_W:half + (c + 1) * PLANE_W] = _unpack_lo(w).astype(BF16)

        n_sub = (tv_ref[m] + MOE_SUB - 1) // MOE_SUB
        for k in range(1, tm // MOE_SUB + 1):
            @pl.when(n_sub == k)
            def _():
                _swiglu_accumulate(hb_ref, wg_ref, wu_ref, wd_ref, acc_ref, k * MOE_SUB)

        @pl.when(f == pl.num_programs(1) - 1)
        def _():
            packed = _pack_bf16_pairs(acc_ref[...])
            for c in range(PACK_PLANES):
                o_ref[c] = packed[:, c * PLANE_W:(c + 1) * PLANE_W]


MOE_TM = 1024


def _moe_routed(hp, tile_expert, n_used, tile_valid, wg, wu, wd, layer, tm):
    _, r, _ = hp.shape
    _, ne, d, ff = wg.shape
    wg = wg.reshape(-1, d, ff)
    wu = wu.reshape(-1, d, ff)
    wd = wd.reshape(-1, ff, d)
    e0 = layer * ne
    tf = 256
    nf = ff // tf

    def row_map(m, f, te, nu, tv):
        return (0, jnp.minimum(m, nu[0] - 1), 0)

    def fidx(m, f, nu):
        return jnp.where(m < nu[0], f, nf - 1)

    hspec = pl.BlockSpec((PACK_PLANES, tm, PLANE_W), row_map)
    return pl.pallas_call(
        _moe_kernel,
        grid_spec=pltpu.PrefetchScalarGridSpec(
            num_scalar_prefetch=3,
            grid=(r // tm, nf),
            in_specs=[
                hspec,
                pl.BlockSpec((1, d, tf), lambda m, f, te, nu, tv: (e0 + te[m], 0, fidx(m, f, nu))),
                pl.BlockSpec((1, d, tf), lambda m, f, te, nu, tv: (e0 + te[m], 0, fidx(m, f, nu))),
                pl.BlockSpec((1, tf, d), lambda m, f, te, nu, tv: (e0 + te[m], fidx(m, f, nu), 0)),
            ],
            out_specs=hspec,
            scratch_shapes=[pltpu.VMEM((tm, d), F32), pltpu.VMEM((tm, d), BF16)],
        ),
        out_shape=jax.ShapeDtypeStruct(hp.shape, jnp.uint32),
        compiler_params=_cparams(("arbitrary", "arbitrary")),
        name="moe",
    )(tile_expert, n_used, tile_valid, hp, wg, wu, wd)


def _route_plan(meta, counts, tm, n_tiles):
    e1 = meta[:, 0].astype(jnp.int32)
    r1 = meta[:, 1].astype(jnp.int32)
    e2 = meta[:, 2].astype(jnp.int32)
    r2 = meta[:, 3].astype(jnp.int32)
    cnt = counts[0, :N_EXPERTS].astype(jnp.int32)
    tiles_per = (cnt + tm - 1) // tm
    tile_end = jnp.cumsum(tiles_per)
    start_row = (tile_end - tiles_per) * tm
    pos1 = start_row[e1] + r1
    pos2 = start_row[e2] + r2
    n_used = tile_end[-1]
    tiles = jnp.arange(n_tiles, dtype=jnp.int32)
    tile_expert = jnp.sum((tiles[:, None] >= tile_end[None, :]).astype(jnp.int32), axis=1)
    last_expert = jnp.sum((n_used - 1 >= tile_end).astype(jnp.int32))
    tile_expert = jnp.minimum(tile_expert, last_expert)
    tile_start = (tile_end - tiles_per)[tile_expert]
    tile_valid = jnp.clip(cnt[tile_expert] - (tiles - tile_start) * tm, 0, tm)
    return pos1, pos2, tile_expert, n_used.reshape(1), tile_valid


def _plane_rows(pos, n_rows):
    return (jnp.arange(PACK_PLANES, dtype=jnp.int32)[:, None] * n_rows + pos[None, :]).reshape(-1)


def _moe(hp, route, wg, wu, wd, layer):
    _, t, _ = hp.shape
    tm = min(MOE_TM, t)
    n_tiles = 2 * t // tm + N_EXPERTS
    n_rows = n_tiles * tm
    meta, counts = _rank(route)
    pos1, pos2, tile_expert, n_used, tile_valid = _route_plan(meta, counts, tm, n_tiles)
    i1 = _plane_rows(pos1, n_rows)
    i2 = _plane_rows(pos2, n_rows)
    hs = _sc_scatter2(hp.reshape(PACK_PLANES * t, PLANE_W), i1, i2, PACK_PLANES * n_rows)
    ys = _moe_routed(hs.reshape(PACK_PLANES, n_rows, PLANE_W), tile_expert, n_used, tile_valid,
                     wg, wu, wd, layer, tm)
    ys = ys.reshape(PACK_PLANES * n_rows, PLANE_W)
    y1 = _sc_gather(ys, i1).reshape(PACK_PLANES, t, PLANE_W)
    y2 = _sc_gather(ys, i2).reshape(PACK_PLANES, t, PLANE_W)
    return y1, y2


def _final_kernel(x_ref, f_ref, g_ref, nw_ref, o_ref):
    xv = x_ref[0] + g_ref[0, 0] * f_ref[0].astype(F32)
    ms = jnp.mean(xv * xv, axis=-1, keepdims=True)
    o_ref[0] = xv * lax.rsqrt(ms + EPS) * nw_ref[...]


def _final(x, f, mods, nw):
    b, s, d = x.shape
    tm = min(s, 512)
    xspec = pl.BlockSpec((1, tm, d), lambda bi, m: (bi, m, 0))
    return pl.pallas_call(
        _final_kernel,
        grid=(b, s // tm),
        in_specs=[xspec, xspec,
                  pl.BlockSpec((1, 1, 1, d), lambda bi, m: (bi, 5, 0, 0)),
                  pl.BlockSpec((1, d), lambda bi, m: (0, 0))],
        out_specs=xspec,
        out_shape=jax.ShapeDtypeStruct((b, s, d), F32),
        compiler_params=_cparams(("arbitrary", "arbitrary")),
        name="final",
    )(x, f, mods, nw.reshape(1, d))


def _final_moe_kernel(x_ref, y1_ref, y2_ref, p_ref, g_ref, nw_ref, o_ref):
    tm, d = x_ref.shape[1], x_ref.shape[2]
    half = d // 2
    p = p_ref[0]
    p1 = jnp.concatenate([p[:, 0:LANES]] * (PLANE_W // LANES), axis=1)
    p2 = jnp.concatenate([p[:, LANES:2 * LANES]] * (PLANE_W // LANES), axis=1)
    ssq = jnp.zeros((tm, 1), F32)
    for c in range(PACK_PLANES):
        w1 = y1_ref[c]
        w2 = y2_ref[c]
        for unpack, off in ((_unpack_hi, 0), (_unpack_lo, half)):
            sl = slice(off + c * PLANE_W, off + (c + 1) * PLANE_W)
            f = p1 * unpack(w1) + p2 * unpack(w2)
            xv = x_ref[0, :, sl] + g_ref[0, 0, :, sl] * f
            o_ref[0, :, sl] = xv
            ssq = ssq + jnp.sum(xv * xv, axis=-1, keepdims=True)
    o_ref[0] = o_ref[0] * lax.rsqrt(ssq / d + EPS) * nw_ref[...]


def _final_moe(x, y1, y2, route, mods, nw):
    b, s, d = x.shape
    tm = min(s, 512)
    nm = s // tm
    xspec = pl.BlockSpec((1, tm, d), lambda bi, m: (bi, m, 0))
    yspec = pl.BlockSpec((PACK_PLANES, tm, PLANE_W), lambda bi, m: (0, bi * nm + m, 0))
    return pl.pallas_call(
        _final_moe_kernel,
        grid=(b, nm),
        in_specs=[xspec, yspec, yspec,
                  pl.BlockSpec((1, tm, 2 * LANES), lambda bi, m: (bi, m, 1)),
                  pl.BlockSpec((1, 1, 1, d), lambda bi, m: (bi, 5, 0, 0)),
                  pl.BlockSpec((1, d), lambda bi, m: (0, 0))],
        out_specs=xspec,
        out_shape=jax.ShapeDtypeStruct((b, s, d), F32),
        compiler_params=_cparams(("arbitrary", "arbitrary")),
        name="final",
    )(x, y1, y2, route, mods, nw.reshape(1, d))


def _w_in_plan():
    nb = GROUP_W // LANES
    plan = [(3 * nb + j, 0, 0) for j in range(4 * nb)]
    plan += [(j, 1, 0) for j in range(nb)]
    plan += [(nb + j, 0, 0) for j in range(2 * nb)]
    plan += [(7 * nb + j, 1, 0) for j in range(nb)]
    plan += [(8 * nb, 0, 1), (8 * nb, 0, 2)]
    plan += [(8 * nb + 1, 0, 1), (8 * nb + 1, 0, 2)]
    assert len(plan) * LANES == D_PROJ
    return jnp.asarray(plan, jnp.int32).T.reshape(-1)


def _prep_w_kernel(plan_ref, w_ref, o_ref):
    j = pl.program_id(1)
    nblk = pl.num_programs(1)
    x = w_ref[0]
    lane = lax.broadcasted_iota(jnp.int32, x.shape, 1)
    lo = lane < HEAD_DIM
    other = pltpu.roll(x, HEAD_DIM, 1)
    mode = plan_ref[2 * nblk + j]
    x = jnp.where(mode == 1, jnp.where(lo, x, other), jnp.where(mode == 2, jnp.where(lo, other, x), x))
    scale = jnp.where(plan_ref[nblk + j] == 1, HEAD_DIM ** -0.5, 1.0)
    o_ref[0] = (x * scale).astype(BF16)


def _prep_w_in(w_in):
    depth, d, _ = w_in.shape
    nblk = D_PROJ // LANES
    return pl.pallas_call(
        _prep_w_kernel,
        grid_spec=pltpu.PrefetchScalarGridSpec(
            num_scalar_prefetch=1,
            grid=(depth, nblk),
            in_specs=[pl.BlockSpec((1, d, LANES), lambda i, j, plan: (i, 0, plan[j]))],
            out_specs=pl.BlockSpec((1, d, LANES), lambda i, j, plan: (i, 0, j)),
        ),
        out_shape=jax.ShapeDtypeStruct((depth, d, D_PROJ), BF16),
        compiler_params=_cparams(("arbitrary", "arbitrary")),
        name="prep_w_in",
    )(_w_in_plan(), w_in)


def _rope_tables(n):
    t = jnp.arange(n, dtype=jnp.int32)
    rows = (t // GRID_W).astype(F32)
    cols = (t % GRID_W).astype(F32)
    nf = HEAD_DIM // 4
    inv = ROPE_BASE ** (-jnp.arange(nf, dtype=F32) / nf)
    ang = jnp.concatenate([rows[:, None] * inv, cols[:, None] * inv], axis=-1)
    cos = jnp.cos(ang)
    sin = jnp.sin(ang)
    cos_t = jnp.concatenate([cos, cos, cos, cos], axis=-1)
    sin_t = jnp.concatenate([-sin, sin, -sin, sin], axis=-1)
    return cos_t, sin_t


def kernel(x, c, ctx, c_ctx, w_ada, b_ada, norm_mix, norm_ffn, norm_final, w_in, w_out, na_rpb, pool_w, pool_scale, conv_w, swa_sink, ffn_w_gate, ffn_w_up, ffn_w_down, moe_router, moe_w_gate, moe_w_up, moe_w_down):
    b, s, d = x.shape
    lc = ctx.shape[1]
    depth = w_ada.shape[0]
    cond = jnp.concatenate([c, c_ctx[None, :]], axis=0)
    assert b == 2
    mods_all = _ada(cond, w_ada, b_ada)
    cos_t, sin_t = _rope_tables(s)
    cos_c = jnp.ones((lc, LANES), F32)
    sin_c = jnp.zeros((lc, LANES), F32)
    row_x = lambda bi: bi
    row_c = lambda bi: b

    w_proj = _prep_w_in(w_in)
    cx = ctx
    fx = fc = None
    for i in range(depth):
        last = i == depth - 1
        mods = mods_all[i].reshape(8, 6, 1, d)
        wi = (w_proj, i)
        wo = w_out[i].astype(BF16)
        if i == 0:
            proj, _ = _inproj(x, wi, norm_mix[i], mods, row_x, cos_t, sin_t)
            projc, _ = _inproj(cx, wi, norm_mix[i], mods, row_c, cos_c, sin_c)
        else:
            mods_prev = mods_all[i - 1].reshape(8, 6, 1, d)
            proj, x = _inproj_res(x, fx, mods_prev, wi, norm_mix[i], mods, row_x, cos_t, sin_t)
            projc, cx = _inproj_res(cx, fc, mods_prev, wi, norm_mix[i], mods, row_c, cos_c, sin_c)
        bias = _na_bias(na_rpb[i])
        o_a = _na_attn(proj, projc, bias)
        o_d = _swa_attn(proj, projc, swa_sink[i])
        moe = i % 2 == 1
        router = moe_router[i // 2] if moe else None
        outs = _outproj(o_a, o_d, proj, x, wo, pool_w[i], pool_scale[i], conv_w[i], norm_ffn[i], mods, row_x,
                        router)
        x, h2 = outs[0], outs[1]
        if not last:
            oc_a, oc_d = _ctx_attn(projc, swa_sink[i])
            cx, h2c = _outproj(oc_a, oc_d, projc, cx, wo, pool_w[i], pool_scale[i], conv_w[i], norm_ffn[i],
                               mods, row_c)
        j = i // 2
        if moe:
            if not last:
                raise NotImplementedError("an expert layer that is not the last layer")
            route = outs[2]
            y1, y2 = _moe(h2, route.reshape(b * s, 4 * LANES), moe_w_gate, moe_w_up, moe_w_down, j)
            return _final_moe(x, y1, y2, route, mods, norm_final)
        wgu = [w[j:j + 1].astype(BF16) for w in (ffn_w_gate, ffn_w_up, ffn_w_down)]
        fx = _ffn(h2.reshape(b * s, d), *wgu, 0).reshape(b, s, d)
        if not last:
            fc = _ffn(h2c.reshape(b * lc, d), *wgu, 0).reshape(b, lc, d)
    return _final(x, fx, mods_all[depth - 1].reshape(8, 6, 1, d), norm_final)


def _inproj_res(x, f, mods_prev, w, nw, mods, mod_row, cos_t, sin_t):
    both = jnp.concatenate([mods, mods_prev], axis=1)
    return _inproj(x, w, nw, both, mod_row, cos_t, sin_t, res=(f, 6 + 5))
```

```python
import functools

import jax
import jax.numpy as jnp
from jax import lax
from jax.experimental import pallas as pl
from jax.experimental.pallas import tpu as pltpu
from jax.experimental.pallas import tpu_sc as plsc

F32 = jnp.float32
BF16 = jnp.bfloat16

D_MODEL = 2048
GRID_W = 64
HEAD_DIM = 64
EPS = 1e-6
NEG_INF = -1e30
GROUP_W = 512
NA_KH = 8
NA_KW = 16
POOL_WINDOWS = (2, 4, 8, 16)
POOL_CH = 128
SWA_WINDOW = 128
SWA_BLOCK = 128
ROPE_BASE = 10000.0
N_EXPERTS = 8
LANES = 128
HALO = 16

C_UBC = 0
C_QA = 2048
C_KA = 2560
C_VA = 3072
C_QD = 3584
C_KD = 4096
C_VD = 4352
D_PROJ = 4608
TN_PROJ = 512

VMEM_LIMIT = 56 * 1024 * 1024


def _cparams(sem):
    return pltpu.CompilerParams(dimension_semantics=sem, vmem_limit_bytes=VMEM_LIMIT)


def _dot(a, b):
    return jnp.dot(a, b, preferred_element_type=F32)


def _dot_t(a, b):
    return lax.dot_general(a, b, (((1,), (1,)), ((), ())), preferred_element_type=F32)


def _rms_mod(xv, nw, sh, sc):
    ms = jnp.mean(xv * xv, axis=-1, keepdims=True)
    y = xv * lax.rsqrt(ms + EPS) * nw
    return y * (1.0 + sc) + sh


PACK_PLANES = 4
PLANE_W = D_MODEL // 2 // PACK_PLANES
SC_WINDOW = 128


def _pack_bf16_pairs(v):
    half = v.shape[1] // 2
    hi = lax.bitcast_convert_type(v[:, :half].astype(BF16).astype(F32), jnp.uint32)
    lo = lax.bitcast_convert_type(v[:, half:].astype(BF16).astype(F32), jnp.uint32)
    return hi | (lo >> 16)


def _unpack_hi(w):
    return lax.bitcast_convert_type(w & jnp.uint32(0xFFFF0000), F32)


def _unpack_lo(w):
    return lax.bitcast_convert_type(w << 16, F32)


def _ada_kernel(c_ref, w_ref, b_ref, o_ref):
    tn = w_ref.shape[2]
    o_ref[...] = jnp.zeros(o_ref.shape, F32)
    for r in range(3):
        cv = c_ref[r]
        m = cv * (1.0 / (1.0 + jnp.exp(-cv)))
        cols = []
        for j in range(tn // LANES):
            wj = w_ref[0, :, j * LANES:(j + 1) * LANES]
            cols.append(jnp.sum(wj * m, axis=0, keepdims=True))
        o_ref[0, r:r + 1, :] = jnp.concatenate(cols, axis=1) + b_ref[0]


def _ada(cond, w_ada, b_ada):
    depth, d, n6 = w_ada.shape
    tn = 1024
    cb = jnp.broadcast_to(cond[:, :, None], (3, d, LANES))
    return pl.pallas_call(
        _ada_kernel,
        grid=(depth, n6 // tn),
        in_specs=[
            pl.BlockSpec((3, d, LANES), lambda i, j: (0, 0, 0)),
            pl.BlockSpec((1, d, tn), lambda i, j: (i, 0, j)),
            pl.BlockSpec((1, 1, tn), lambda i, j: (i, 0, j)),
        ],
        out_specs=pl.BlockSpec((1, 8, tn), lambda i, j: (i, 0, j)),
        out_shape=jax.ShapeDtypeStruct((depth, 8, n6), F32),
        compiler_params=_cparams(("arbitrary", "arbitrary")),
        name="ada",
    )(cb, w_ada, b_ada.reshape(depth, 1, n6))


def _rope(a, cosv, sinv, lo32):
    sw = jnp.where(lo32, pltpu.roll(a, 96, 1), pltpu.roll(a, 32, 1))
    return a * cosv + sw * sinv


def _inproj_kernel(*refs, has_res):
    if has_res:
        x_ref, f_ref, g_ref, nw_ref, sh_ref, sc_ref, cos_ref, sin_ref, w_ref, o_ref, x2_ref = refs
    else:
        x_ref, nw_ref, sh_ref, sc_ref, cos_ref, sin_ref, w_ref, o_ref = refs
    xv = x_ref[0]
    if has_res:
        xv = xv + g_ref[0, 0] * f_ref[0].astype(F32)
        x2_ref[0] = xv
    h = _rms_mod(xv, nw_ref[...], sh_ref[0, 0], sc_ref[0, 0]).astype(BF16)
    tm = h.shape[0]
    lane = lax.broadcasted_iota(jnp.int32, (tm, LANES), 1)
    lo32 = (lane % HEAD_DIM) < (HEAD_DIM // 2)
    rope_end = C_VD
    for n in range(D_PROJ // TN_PROJ):
        c0 = n * TN_PROJ
        acc = _dot(h, w_ref[0, :, c0:c0 + TN_PROJ])
        if c0 + TN_PROJ <= C_QD:
            o_ref[0, :, c0:c0 + TN_PROJ] = acc.astype(BF16)
            continue
        for g in range(TN_PROJ // LANES):
            a = acc[:, g * LANES:(g + 1) * LANES]
            if c0 + g * LANES < rope_end:
                a = _rope(a, cos_ref[...], sin_ref[...], lo32)
            o_ref[0, :, c0 + g * LANES:c0 + (g + 1) * LANES] = a.astype(BF16)


def _inproj(x, w, nw, mods, mod_row, cos_t, sin_t, res=None):
    b, s, d = x.shape
    w_all, layer = w
    tm = min(s, 512)
    has_res = res is not None
    row = mod_row
    xspec = pl.BlockSpec((1, tm, d), lambda bi, m: (bi, m, 0))

    def mspec(k):
        return pl.BlockSpec((1, 1, 1, d), lambda bi, m: (row(bi), k, 0, 0))

    in_specs = [xspec]
    args = [x]
    if has_res:
        f, gk = res
        in_specs += [xspec, mspec(gk)]
        args += [f, mods]
    in_specs += [
        pl.BlockSpec((1, d), lambda bi, m: (0, 0)),
        mspec(0), mspec(1),
        pl.BlockSpec((tm, LANES), lambda bi, m: (m, 0)),
        pl.BlockSpec((tm, LANES), lambda bi, m: (m, 0)),
        pl.BlockSpec((1, d, D_PROJ), lambda bi, m: (layer, 0, 0), pipeline_mode=pl.Buffered(1)),
    ]
    args += [nw.reshape(1, d), mods, mods, cos_t, sin_t, w_all]
    out_shape = [jax.ShapeDtypeStruct((b, s, D_PROJ), BF16)]
    out_specs = [pl.BlockSpec((1, tm, D_PROJ), lambda bi, m: (bi, m, 0))]
    if has_res:
        out_shape.append(jax.ShapeDtypeStruct((b, s, d), F32))
        out_specs.append(xspec)
    outs = pl.pallas_call(
        functools.partial(_inproj_kernel, has_res=has_res),
        grid=(b, s // tm),
        in_specs=in_specs,
        out_specs=out_specs,
        out_shape=out_shape,
        compiler_params=_cparams(("arbitrary", "arbitrary")),
        name="inproj",
    )(*args)
    return outs if has_res else (outs[0], None)


def _na_bias_kernel(rpb_ref, o_ref):
    h = pl.program_id(0)
    nd = 2 * NA_KH - 1
    nj = 2 * NA_KW - 1
    q = lax.broadcasted_iota(jnp.int32, (GRID_W, LANES), 0)
    lane = lax.broadcasted_iota(jnp.int32, (GRID_W, LANES), 1)
    kw = lane % GRID_W
    dc = jnp.clip(kw - q, -(NA_KW - 1), NA_KW - 1) + (NA_KW - 1)
    c0 = jnp.clip(q - NA_KW // 2, 0, GRID_W - NA_KW)
    valid = (kw >= c0) & (kw < c0 + NA_KW)
    tabs = []
    for d in range(nd):
        t = jnp.zeros((GRID_W, LANES), F32)
        for j in range(nj):
            t = jnp.where(dc == j, rpb_ref[h * (nd * nj) + d * nj + j], t)
        tabs.append(jnp.where(valid, t, NEG_INF))
    for c in range(NA_KH):
        for g in range(NA_KH // 2):
            d_lo = 2 * g - c + (NA_KH - 1)
            o_ref[c, 0, :, g * LANES:(g + 1) * LANES] = jnp.where(lane < GRID_W, tabs[d_lo], tabs[d_lo + 1])


def _na_bias(rpb):
    nh = rpb.shape[0]
    return pl.pallas_call(
        _na_bias_kernel,
        grid=(nh,),
        in_specs=[pl.BlockSpec(memory_space=pltpu.SMEM)],
        out_specs=pl.BlockSpec((NA_KH, 1, GRID_W, NA_KH * GRID_W), lambda h: (0, h, 0, 0)),
        out_shape=jax.ShapeDtypeStruct((NA_KH, nh, GRID_W, NA_KH * GRID_W), F32),
        compiler_params=_cparams(("arbitrary",)),
        name="na_bias",
    )(rpb.reshape(-1))


def _softmax_pv_staged(chains):
    add = lambda a, b: a + b
    ms = []
    for sp, _, extra in chains:
        m = functools.reduce(jnp.maximum, [jnp.max(s, axis=-1, keepdims=True) for s in sp])
        ms.append(m if extra is None else jnp.maximum(m, extra))
    ps = [[jnp.exp(s - m) for s in sp] for (sp, _, _), m in zip(chains, ms)]
    dens = []
    for pp, (_, _, extra), m in zip(ps, chains, ms):
        den = functools.reduce(add, [jnp.sum(p, axis=-1, keepdims=True) for p in pp])
        dens.append(den if extra is None else den + jnp.exp(extra - m))
    outs = [functools.reduce(add, [_dot(p.astype(BF16), v) for p, v in zip(pp, vp)])
            for pp, (_, vp, _) in zip(ps, chains)]
    return [o / d for o, d in zip(outs, dens)]


ROWS_PER_STEP = 16
NA_ROW_GROUP = 4


def _na_kernel(q_ref, k_ref, v_ref, kc_ref, vc_ref, bias_ref, o_ref, *, rows):
    rb = pl.program_id(2)
    lo = lax.broadcasted_iota(jnp.int32, (NA_ROW_GROUP * GRID_W, LANES), 1) < HEAD_DIM
    lo_row = lax.broadcasted_iota(jnp.int32, (GRID_W, LANES), 1) < HEAD_DIM
    kc = kc_ref[0]
    vc = vc_ref[0]
    kh = min(NA_KH, rows)
    win = kh * GRID_W

    g = NA_ROW_GROUP
    gq = g * GRID_W
    zero = jnp.zeros((gq, LANES), BF16)
    for i0 in range(0, ROWS_PER_STEP, g):
        q = q_ref[0, i0 * GRID_W:(i0 + g) * GRID_W, :]
        q2 = jnp.concatenate([jnp.where(lo, q, zero), jnp.where(lo, zero, q)], axis=0)
        s_ctx = _dot_t(q2, kc)
        pieces = [(hh, j, slice(hh * gq + j * GRID_W, hh * gq + (j + 1) * GRID_W))
                  for hh in range(2) for j in range(g)]
        vws, s_loc = [], {}
        for j in range(g):
            r = rb * ROWS_PER_STEP + i0 + j
            r0 = jnp.clip(r - kh // 2, 0, rows - kh)
            c = r - r0
            koff = pl.multiple_of(r0 * GRID_W, GRID_W)
            vws.append(v_ref[0, pl.ds(koff, win), :])
            qp = jnp.concatenate([q2[sl] for hh, jj, sl in pieces if jj == j], axis=0)
            sp = _dot_t(qp, k_ref[0, pl.ds(koff, win), :])
            for hh in range(2):
                s_loc[hh, j] = sp[hh * GRID_W:(hh + 1) * GRID_W] + bias_ref[c, hh]
        ms = {(hh, j): jnp.maximum(jnp.max(s_loc[hh, j], axis=-1, keepdims=True),
                                   jnp.max(s_ctx[sl], axis=-1, keepdims=True)) for hh, j, sl in pieces}
        p_loc = {(hh, j): jnp.exp(s_loc[hh, j] - ms[hh, j]) for hh, j, _ in pieces}
        p_ctx = {(hh, j): jnp.exp(s_ctx[sl] - ms[hh, j]) for hh, j, sl in pieces}
        den = {k: jnp.sum(p_loc[k], axis=-1, keepdims=True) + jnp.sum(p_ctx[k], axis=-1, keepdims=True)
               for k in p_loc}
        pv_ctx = _dot(jnp.concatenate([p_ctx[hh, j].astype(BF16) for hh, j, _ in pieces], axis=0), vc)
        for j in range(g):
            pv = _dot(jnp.concatenate([p_loc[0, j].astype(BF16), p_loc[1, j].astype(BF16)], axis=0), vws[j])
            o = [(pv[hh * GRID_W:(hh + 1) * GRID_W] + pv_ctx[hh * gq + j * GRID_W:hh * gq + (j + 1) * GRID_W])
                 / den[hh, j] for hh in range(2)]
            i = i0 + j
            o_ref[0, i * GRID_W:(i + 1) * GRID_W, :] = jnp.where(lo_row, o[0], o[1]).astype(BF16)


def _na_attn(proj, projc, bias):
    b, s, _ = proj.shape
    lc = projc.shape[1]
    rows = s // GRID_W
    assert rows >= NA_KH and rows % ROWS_PER_STEP == 0
    tq = ROWS_PER_STEP * GRID_W
    npair = GROUP_W // LANES
    return pl.pallas_call(
        functools.partial(_na_kernel, rows=rows),
        grid=(b, npair, rows // ROWS_PER_STEP),
        in_specs=[
            pl.BlockSpec((1, tq, LANES), lambda bi, p, r: (bi, r, C_QA // LANES + p)),
            pl.BlockSpec((1, s, LANES), lambda bi, p, r: (bi, 0, C_KA // LANES + p)),
            pl.BlockSpec((1, s, LANES), lambda bi, p, r: (bi, 0, C_VA // LANES + p)),
            pl.BlockSpec((1, lc, LANES), lambda bi, p, r: (bi, 0, C_KA // LANES + p)),
            pl.BlockSpec((1, lc, LANES), lambda bi, p, r: (bi, 0, C_VA // LANES + p)),
            pl.BlockSpec((NA_KH, 2, GRID_W, NA_KH * GRID_W), lambda bi, p, r: (0, p, 0, 0)),
        ],
        out_specs=pl.BlockSpec((1, tq, LANES), lambda bi, p, r: (bi, r, p)),
        out_shape=jax.ShapeDtypeStruct((b, s, GROUP_W), BF16),
        compiler_params=_cparams(("arbitrary", "arbitrary", "arbitrary")),
        name="na_attn",
    )(proj, proj, proj, projc, projc, bias)


def _gqa_chains(q, kv, sink_ref, k_parts, v_parts, mask):
    m = q.shape[0]
    lane = lax.broadcasted_iota(jnp.int32, (m, LANES), 1)
    lo = lane < HEAD_DIM
    qs = []
    for j in range(2):
        qg = q[:, (kv * 2 + j) * LANES:(kv * 2 + j + 1) * LANES]
        qs.append(jnp.where(lo, qg, jnp.zeros_like(qg)))
        qs.append(jnp.where(lo, jnp.zeros_like(qg), qg))
    qq = jnp.concatenate(qs, axis=0)
    s_all = [_dot_t(qq, k) for k in k_parts]
    chains = []
    for i in range(4):
        parts = [s[i * m:(i + 1) * m] for s in s_all]
        if mask is not None:
            parts[0] = jnp.where(mask, parts[0], NEG_INF)
        chains.append((parts, v_parts, jnp.full((m, 1), sink_ref[kv * 4 + i], F32)))
    return chains


def _gqa_store(o_ref, heads):
    lane = lax.broadcasted_iota(jnp.int32, heads[0].shape, 1)
    lo = lane < HEAD_DIM
    for g in range(len(heads) // 2):
        o_ref[0, :, g * LANES:(g + 1) * LANES] = jnp.where(lo, heads[2 * g], heads[2 * g + 1]).astype(BF16)


def _swa_kernel(sink_ref, q_ref, k_ref, v_ref, kc_ref, vc_ref, o_ref):
    n = pl.program_id(1)
    nb = pl.num_programs(1)
    start = jnp.clip(n - 1, 0, nb - 3)
    koff = pl.multiple_of(start * SWA_BLOCK, SWA_BLOCK)
    kw = k_ref[0, pl.ds(koff, 3 * SWA_BLOCK), :]
    vw = v_ref[0, pl.ds(koff, 3 * SWA_BLOCK), :]
    kc = kc_ref[0]
    vc = vc_ref[0]
    rel = (n - start) * SWA_BLOCK + lax.broadcasted_iota(jnp.int32, (SWA_BLOCK, 3 * SWA_BLOCK), 0)
    kj = lax.broadcasted_iota(jnp.int32, (SWA_BLOCK, 3 * SWA_BLOCK), 1)
    valid = jnp.abs(rel - kj) <= SWA_WINDOW
    q = q_ref[0]
    chains = []
    for kv in range(2):
        sl = slice(kv * LANES, (kv + 1) * LANES)
        chains += _gqa_chains(q, kv, sink_ref, [kw[:, sl], kc[:, sl]], [vw[:, sl], vc[:, sl]], valid)
    _gqa_store(o_ref, _softmax_pv_staged(chains))


def _swa_attn(proj, projc, sink):
    b, s, _ = proj.shape
    lc = projc.shape[1]
    nb = s // SWA_BLOCK
    assert nb >= 3
    w2 = 2 * LANES
    return pl.pallas_call(
        _swa_kernel,
        grid=(b, nb),
        in_specs=[
            pl.BlockSpec(memory_space=pltpu.SMEM),
            pl.BlockSpec((1, SWA_BLOCK, GROUP_W), lambda bi, n: (bi, n, C_QD // GROUP_W)),
            pl.BlockSpec((1, s, w2), lambda bi, n: (bi, 0, C_KD // w2)),
            pl.BlockSpec((1, s, w2), lambda bi, n: (bi, 0, C_VD // w2)),
            pl.BlockSpec((1, lc, w2), lambda bi, n: (bi, 0, C_KD // w2)),
            pl.BlockSpec((1, lc, w2), lambda bi, n: (bi, 0, C_VD // w2)),
        ],
        out_specs=pl.BlockSpec((1, SWA_BLOCK, GROUP_W), lambda bi, n: (bi, n, 0)),
        out_shape=jax.ShapeDtypeStruct((b, s, GROUP_W), BF16),
        compiler_params=_cparams(("arbitrary", "arbitrary")),
        name="swa_attn",
    )(sink.reshape(-1), proj, proj, proj, projc, projc)


def _ctx_attn_kernel(sink_ref, qa_ref, ka_ref, va_ref, qd_ref, kd_ref, vd_ref, oa_ref, od_ref):
    m = qa_ref.shape[1]
    lane = lax.broadcasted_iota(jnp.int32, (m, LANES), 1)
    lo = lane < HEAD_DIM
    chains = []
    for p in range(GROUP_W // LANES):
        sl = slice(p * LANES, (p + 1) * LANES)
        q = qa_ref[0, :, sl]
        k = ka_ref[0, :, sl]
        v = va_ref[0, :, sl]
        for hh in range(2):
            qm = jnp.where(lo if hh == 0 else jnp.logical_not(lo), q, jnp.zeros_like(q))
            chains.append(([_dot_t(qm, k)], [v], None))
    _gqa_store(oa_ref, _softmax_pv_staged(chains))
    q = qd_ref[0]
    chains = []
    for kv in range(2):
        sl = slice(kv * LANES, (kv + 1) * LANES)
        chains += _gqa_chains(q, kv, sink_ref, [kd_ref[0, :, sl]], [vd_ref[0, :, sl]], None)
    _gqa_store(od_ref, _softmax_pv_staged(chains))


def _ctx_attn(projc, sink):
    b, lc, _ = projc.shape
    w2 = 2 * LANES

    def spec(width, col):
        return pl.BlockSpec((1, lc, width), lambda bi: (bi, 0, col // width))

    ospec = pl.BlockSpec((1, lc, GROUP_W), lambda bi: (bi, 0, 0))
    return pl.pallas_call(
        _ctx_attn_kernel,
        grid=(b,),
        in_specs=[pl.BlockSpec(memory_space=pltpu.SMEM),
                  spec(GROUP_W, C_QA), spec(GROUP_W, C_KA), spec(GROUP_W, C_VA),
                  spec(GROUP_W, C_QD), spec(w2, C_KD), spec(w2, C_VD)],
        out_specs=[ospec, ospec],
        out_shape=[jax.ShapeDtypeStruct((b, lc, GROUP_W), BF16)] * 2,
        compiler_params=_cparams(("arbitrary",)),
        name="ctx_attn",
    )(sink.reshape(-1), projc, projc, projc, projc, projc, projc)


def _outproj_kernel(*refs, tm, moe):
    (oa_ref, od_ref, u_ref, up_ref, un_ref, pw_ref, ps_ref, cw_ref, wo_ref, x_ref, g1_ref,
     nw_ref, sh_ref, sc_ref) = refs[:14]
    rest = refs[14:]
    if moe:
        rt_ref, xo_ref, h_ref, gate_ref, ext_ub, ext_u = rest
    else:
        xo_ref, h_ref, ext_ub, ext_u = rest
    m = pl.program_id(1)
    nm = pl.num_programs(1)
    g4 = GROUP_W

    def cols(ref, k):
        return ref[0, :, k * g4:(k + 1) * g4].astype(F32)

    ub = cols(u_ref, 0)
    gb = cols(u_ref, 1)
    has_prev = m > 0
    has_next = m < nm - 1
    ext_ub[0:HALO] = jnp.where(has_prev, cols(up_ref, 0), 0.0)
    ext_ub[HALO:HALO + tm] = ub
    ext_ub[HALO + tm:] = jnp.where(has_next, cols(un_ref, 0), 0.0)
    ext_u[0:HALO] = jnp.where(has_prev, cols(up_ref, 2) * cols(up_ref, 3), 0.0)
    ext_u[HALO:HALO + tm] = cols(u_ref, 2) * cols(u_ref, 3)
    ext_u[HALO + tm:] = jnp.where(has_next, cols(un_ref, 2) * cols(un_ref, 3), 0.0)

    n_tok = nm * tm
    t = m * tm + lax.broadcasted_iota(jnp.int32, (tm, LANES), 0)
    obs = []
    for g, w in enumerate(POOL_WINDOWS):
        sl = slice(g * LANES, (g + 1) * LANES)
        acc = None
        for d in range(-(w // 2), w - w // 2):
            term = ext_ub[HALO + d:HALO + d + tm, sl]
            acc = term if acc is None else acc + term
        cnt = (jnp.clip(t + (w - w // 2), 0, n_tok) - jnp.clip(t - w // 2, 0, n_tok)).astype(F32)
        pooled = acc / cnt - ub[:, sl]
        obs.append(_dot(pooled.astype(BF16), pw_ref[g]) * ps_ref[:, sl])
    o_b = jnp.concatenate(obs, axis=1).astype(BF16)

    y = (ext_u[HALO - 1:HALO - 1 + tm] * cw_ref[0:1, :] + ext_u[HALO:HALO + tm] * cw_ref[1:2, :]
         + ext_u[HALO + 1:HALO + 1 + tm] * cw_ref[2:3, :])
    o_c = (gb * y).astype(BF16)

    yy = _dot(jnp.concatenate([oa_ref[0], o_b, o_c, od_ref[0]], axis=1), wo_ref[...])
    xn = x_ref[0] + g1_ref[0, 0] * yy
    xo_ref[0] = xn
    h2 = _rms_mod(xn, nw_ref[...], sh_ref[0, 0], sc_ref[0, 0])

    if not moe:
        h_ref[0] = h2.astype(BF16)
    else:
        packed = _pack_bf16_pairs(h2)
        for c in range(PACK_PLANES):
            h_ref[c] = packed[:, c * PLANE_W:(c + 1) * PLANE_W]
        lane = lax.broadcasted_iota(jnp.int32, (tm, LANES), 1)
        logits = jnp.full((tm, LANES), -jnp.inf, F32)
        for e in range(N_EXPERTS):
            le = jnp.sum(h2 * rt_ref[e:e + 1, :], axis=-1, keepdims=True)
            logits = jnp.where(lane == e, le, logits)
        v1 = jnp.max(logits, axis=-1, keepdims=True)
        i1 = jnp.min(jnp.where(logits == v1, lane, LANES), axis=-1, keepdims=True)
        l2 = jnp.where(lane == i1, -jnp.inf, logits)
        v2 = jnp.max(l2, axis=-1, keepdims=True)
        i2 = jnp.min(jnp.where(l2 == v2, lane, LANES), axis=-1, keepdims=True)
        e2 = jnp.exp(v2 - v1)
        p1 = 1.0 / (1.0 + e2)
        p2 = e2 / (1.0 + e2)
        gate_ref[0, :, 0:LANES] = jnp.where(lane == i1, 1.0, 0.0)
        gate_ref[0, :, LANES:2 * LANES] = jnp.where(lane == i2, 1.0, 0.0)
        gate_ref[0, :, 2 * LANES:3 * LANES] = jnp.broadcast_to(p1, (tm, LANES))
        gate_ref[0, :, 3 * LANES:4 * LANES] = jnp.broadcast_to(p2, (tm, LANES))


def _outproj(o_a, o_d, proj, x, wo, pool_w, pool_scale, conv_w, nw, mods, mod_row, router=None):
    b, s, d = x.shape
    tm = min(s, 512)
    moe = router is not None
    row = mod_row
    nh = s // HALO
    th = tm // HALO

    def mspec(k):
        return pl.BlockSpec((1, 1, 1, d), lambda bi, m: (row(bi), k, 0, 0))

    gspec = pl.BlockSpec((1, tm, GROUP_W), lambda bi, m: (bi, m, 0))
    xspec = pl.BlockSpec((1, tm, d), lambda bi, m: (bi, m, 0))
    in_specs = [
        gspec, gspec,
        pl.BlockSpec((1, tm, 4 * GROUP_W), lambda bi, m: (bi, m, 0)),
        pl.BlockSpec((1, HALO, 4 * GROUP_W), lambda bi, m: (bi, jnp.maximum(m * th - 1, 0), 0)),
        pl.BlockSpec((1, HALO, 4 * GROUP_W), lambda bi, m: (bi, jnp.minimum((m + 1) * th, nh - 1), 0)),
        pl.BlockSpec((4, POOL_CH, POOL_CH), lambda bi, m: (0, 0, 0)),
        pl.BlockSpec((1, GROUP_W), lambda bi, m: (0, 0)),
        pl.BlockSpec((3, GROUP_W), lambda bi, m: (0, 0)),
        pl.BlockSpec((d, d), lambda bi, m: (0, 0), pipeline_mode=pl.Buffered(1)),
        xspec, mspec(2),
        pl.BlockSpec((1, d), lambda bi, m: (0, 0)),
        mspec(3), mspec(4),
    ]
    args = [o_a, o_d, proj, proj, proj, pool_w.astype(BF16), pool_scale.reshape(1, GROUP_W), conv_w, wo,
            x, mods, nw.reshape(1, d), mods, mods]
    if moe:
        nm = s // tm
        in_specs.append(pl.BlockSpec((N_EXPERTS, d), lambda bi, m: (0, 0)))
        args.append(router.T)
        out_shape = [jax.ShapeDtypeStruct((b, s, d), F32),
                     jax.ShapeDtypeStruct((PACK_PLANES, b * s, PLANE_W), jnp.uint32),
                     jax.ShapeDtypeStruct((b, s, 4 * LANES), F32)]
        out_specs = [xspec,
                     pl.BlockSpec((PACK_PLANES, tm, PLANE_W), lambda bi, m: (0, bi * nm + m, 0)),
                     pl.BlockSpec((1, tm, 4 * LANES), lambda bi, m: (bi, m, 0))]
    else:
        out_shape = [jax.ShapeDtypeStruct((b, s, d), F32), jax.ShapeDtypeStruct((b, s, d), BF16)]
        out_specs = [xspec, xspec]
    return pl.pallas_call(
        functools.partial(_outproj_kernel, tm=tm, moe=moe),
        grid=(b, s // tm),
        in_specs=in_specs,
        out_specs=out_specs,
        out_shape=out_shape,
        scratch_shapes=[pltpu.VMEM((tm + 2 * HALO, GROUP_W), F32), pltpu.VMEM((tm + 2 * HALO, GROUP_W), F32)],
        compiler_params=_cparams(("arbitrary", "arbitrary")),
        name="outproj",
    )(*args)


def _swiglu_accumulate(h_ref, wg_ref, wu_ref, wd_ref, acc_ref, rows=None):
    r = slice(0, h_ref.shape[0] if rows is None else rows)
    h = h_ref[r, :]
    a = _dot(h, wg_ref[0].astype(BF16))
    u = _dot(h, wu_ref[0].astype(BF16))
    act = a * (1.0 / (1.0 + jnp.exp(-a))) * u
    acc_ref[r, :] += _dot(act.astype(BF16), wd_ref[0].astype(BF16))


def _ffn_kernel(h_ref, wg_ref, wu_ref, wd_ref, o_ref, acc_ref):
    f = pl.program_id(1)

    @pl.when(f == 0)
    def _():
        acc_ref[...] = jnp.zeros(acc_ref.shape, F32)

    _swiglu_accumulate(h_ref, wg_ref, wu_ref, wd_ref, acc_ref)

    @pl.when(f == pl.num_programs(1) - 1)
    def _():
        o_ref[...] = acc_ref[...].astype(BF16)


def _ffn(h, wg, wu, wd, layer):
    mt, d = h.shape
    ff = wg.shape[2]
    tm = min(mt, 1024)
    tf = 256
    return pl.pallas_call(
        _ffn_kernel,
        grid=(mt // tm, ff // tf),
        in_specs=[
            pl.BlockSpec((tm, d), lambda m, f: (m, 0)),
            pl.BlockSpec((1, d, tf), lambda m, f: (layer, 0, f)),
            pl.BlockSpec((1, d, tf), lambda m, f: (layer, 0, f)),
            pl.BlockSpec((1, tf, d), lambda m, f: (layer, f, 0)),
        ],
        out_specs=pl.BlockSpec((tm, d), lambda m, f: (m, 0)),
        out_shape=jax.ShapeDtypeStruct((mt, d), BF16),
        scratch_shapes=[pltpu.VMEM((tm, d), F32)],
        compiler_params=_cparams(("arbitrary", "arbitrary")),
        name="ffn",
    )(h, wg, wu, wd)


def _rank_kernel(r_ref, o_ref, cnt_ref, carry):
    i = pl.program_id(0)
    tm = r_ref.shape[0]

    @pl.when(i == 0)
    def _():
        carry[...] = jnp.zeros(carry.shape, F32)

    oh1 = r_ref[:, 0:LANES]
    oh2 = r_ref[:, LANES:2 * LANES]
    sel = oh1 + oh2
    row = lax.broadcasted_iota(jnp.int32, (tm, tm), 0)
    col = lax.broadcasted_iota(jnp.int32, (tm, tm), 1)
    tri = jnp.where(col < row, 1.0, 0.0).astype(BF16)
    excl = _dot(tri, sel.astype(BF16)) + carry[0:1, :]
    lane = lax.broadcasted_iota(jnp.int32, (tm, LANES), 1)
    lane_f = lane.astype(F32)
    e1 = jnp.sum(oh1 * lane_f, axis=-1, keepdims=True)
    r1 = jnp.sum(oh1 * excl, axis=-1, keepdims=True)
    e2 = jnp.sum(oh2 * lane_f, axis=-1, keepdims=True)
    r2 = jnp.sum(oh2 * excl, axis=-1, keepdims=True)
    o_ref[...] = jnp.where(lane == 0, e1, jnp.where(lane == 1, r1, jnp.where(lane == 2, e2,
                           jnp.where(lane == 3, r2, 0.0))))
    carry[...] = carry[...] + jnp.sum(sel, axis=0, keepdims=True)
    cnt_ref[...] = carry[...]


def _rank(route):
    t = route.shape[0]
    tm = min(t, 512)
    return pl.pallas_call(
        _rank_kernel,
        grid=(t // tm,),
        in_specs=[pl.BlockSpec((tm, 2 * LANES), lambda i: (i, 0))],
        out_specs=[pl.BlockSpec((tm, LANES), lambda i: (i, 0)), pl.BlockSpec((8, LANES), lambda i: (0, 0))],
        out_shape=[jax.ShapeDtypeStruct((t, LANES), F32), jax.ShapeDtypeStruct((8, LANES), F32)],
        scratch_shapes=[pltpu.VMEM((8, LANES), F32)],
        compiler_params=_cparams(("arbitrary",)),
        name="rank",
    )(route)


def _sc_mesh():
    return plsc.VectorSubcoreMesh(core_axis_name="core", subcore_axis_name="subcore")


def _sc_scatter2(x, idx1, idx2, n_out):
    n, w = x.shape

    @functools.partial(pl.kernel, out_type=jax.ShapeDtypeStruct((n_out, w), x.dtype), mesh=_sc_mesh(),
                       scratch_types=[], name="sc_dispatch")
    def k(x_hbm, i1_hbm, i2_hbm, o_hbm):
        def body(x_vmem, i1_vmem, i2_vmem):
            pltpu.sync_copy(x_vmem, o_hbm.at[i1_vmem.at[0]])
            pltpu.sync_copy(x_vmem, o_hbm.at[i2_vmem.at[0]])

        pltpu.emit_pipeline(
            body, grid=(n // SC_WINDOW,),
            in_specs=[pl.BlockSpec((SC_WINDOW, w), lambda i: (i, 0)),
                      pl.BlockSpec((1, SC_WINDOW), lambda i: (0, i)),
                      pl.BlockSpec((1, SC_WINDOW), lambda i: (0, i))],
            out_specs=[], core_axis_name=("core", "subcore"),
            dimension_semantics=(pltpu.PARALLEL,))(x_hbm, i1_hbm, i2_hbm)

    return k(x, idx1.reshape(1, n), idx2.reshape(1, n))


def _sc_gather(y, idx):
    (n,) = idx.shape
    w = y.shape[1]

    @functools.partial(pl.kernel, out_type=jax.ShapeDtypeStruct((n, w), y.dtype), mesh=_sc_mesh(),
                       scratch_types=[], name="sc_combine")
    def k(y_hbm, i_hbm, o_hbm):
        def body(i_vmem, o_vmem):
            pltpu.sync_copy(y_hbm.at[i_vmem.at[0]], o_vmem)

        pltpu.emit_pipeline(
            body, grid=(n // SC_WINDOW,),
            in_specs=[pl.BlockSpec((1, SC_WINDOW), lambda i: (0, i))],
            out_specs=[pl.BlockSpec((SC_WINDOW, w), lambda i: (i, 0))],
            core_axis_name=("core", "subcore"),
            dimension_semantics=(pltpu.PARALLEL,))(i_hbm, o_hbm)

    return k(y, idx.reshape(1, n))


MOE_SUB = 256


def _moe_kernel(te_ref, nu_ref, tv_ref, h_ref, wg_ref, wu_ref, wd_ref, o_ref, acc_ref, hb_ref):
    m = pl.program_id(0)
    f = pl.program_id(1)
    tm = hb_ref.shape[0]
    half = hb_ref.shape[1] // 2

    @pl.when(m < nu_ref[0])
    def _():
        @pl.when(f == 0)
        def _():
            acc_ref[...] = jnp.zeros(acc_ref.shape, F32)
            for c in range(PACK_PLANES):
                w = h_ref[c]
                hb_ref[:, c * PLANE_W:(c + 1) * PLANE_W] = _unpack_hi(w).astype(BF16)
                hb_ref[:, half + c * PLANE_W:half + (c + 1) * PLANE_W] = _unpack_lo(w).astype(BF16)

        n_sub = (tv_ref[m] + MOE_SUB - 1) // MOE_SUB
        for k in range(1, tm // MOE_SUB + 1):
            @pl.when(n_sub == k)
            def _():
                _swiglu_accumulate(hb_ref, wg_ref, wu_ref, wd_ref, acc_ref, k * MOE_SUB)

        @pl.when(f == pl.num_programs(1) - 1)
        def _():
            packed = _pack_bf16_pairs(acc_ref[...])
            for c in range(PACK_PLANES):
                o_ref[c] = packed[:, c * PLANE_W:(c + 1) * PLANE_W]


MOE_TM = 1024


def _moe_routed(hp, tile_expert, n_used, tile_valid, wg, wu, wd, layer, tm):
    _, r, _ = hp.shape
    _, ne, d, ff = wg.shape
    wg = wg.reshape(-1, d, ff)
    wu = wu.reshape(-1, d, ff)
    wd = wd.reshape(-1, ff, d)
    e0 = layer * ne
    tf = 256
    nf = ff // tf

    def row_map(m, f, te, nu, tv):
        return (0, jnp.minimum(m, nu[0] - 1), 0)

    def fidx(m, f, nu):
        return jnp.where(m < nu[0], f, nf - 1)

    hspec = pl.BlockSpec((PACK_PLANES, tm, PLANE_W), row_map)
    return pl.pallas_call(
        _moe_kernel,
        grid_spec=pltpu.PrefetchScalarGridSpec(
            num_scalar_prefetch=3,
            grid=(r // tm, nf),
            in_specs=[
                hspec,
                pl.BlockSpec((1, d, tf), lambda m, f, te, nu, tv: (e0 + te[m], 0, fidx(m, f, nu))),
                pl.BlockSpec((1, d, tf), lambda m, f, te, nu, tv: (e0 + te[m], 0, fidx(m, f, nu))),
                pl.BlockSpec((1, tf, d), lambda m, f, te, nu, tv: (e0 + te[m], fidx(m, f, nu), 0)),
            ],
            out_specs=hspec,
            scratch_shapes=[pltpu.VMEM((tm, d), F32), pltpu.VMEM((tm, d), BF16)],
        ),
        out_shape=jax.ShapeDtypeStruct(hp.shape, jnp.uint32),
        compiler_params=_cparams(("arbitrary", "arbitrary")),
        name="moe",
    )(tile_expert, n_used, tile_valid, hp, wg, wu, wd)


def _route_plan(meta, counts, tm, n_tiles):
    e1 = meta[:, 0].astype(jnp.int32)
    r1 = meta[:, 1].astype(jnp.int32)
    e2 = meta[:, 2].astype(jnp.int32)
    r2 = meta[:, 3].astype(jnp.int32)
    cnt = counts[0, :N_EXPERTS].astype(jnp.int32)
    tiles_per = (cnt + tm - 1) // tm
    tile_end = jnp.cumsum(tiles_per)
    start_row = (tile_end - tiles_per) * tm
    pos1 = start_row[e1] + r1
    pos2 = start_row[e2] + r2
    n_used = tile_end[-1]
    tiles = jnp.arange(n_tiles, dtype=jnp.int32)
    tile_expert = jnp.sum((tiles[:, None] >= tile_end[None, :]).astype(jnp.int32), axis=1)
    last_expert = jnp.sum((n_used - 1 >= tile_end).astype(jnp.int32))
    tile_expert = jnp.minimum(tile_expert, last_expert)
    tile_start = (tile_end - tiles_per)[tile_expert]
    tile_valid = jnp.clip(cnt[tile_expert] - (tiles - tile_start) * tm, 0, tm)
    return pos1, pos2, tile_expert, n_used.reshape(1), tile_valid


def _plane_rows(pos, n_rows):
    return (jnp.arange(PACK_PLANES, dtype=jnp.int32)[:, None] * n_rows + pos[None, :]).reshape(-1)


def _moe(hp, route, wg, wu, wd, layer):
    _, t, _ = hp.shape
    tm = min(MOE_TM, t)
    n_tiles = 2 * t // tm + N_EXPERTS
    n_rows = n_tiles * tm
    meta, counts = _rank(route)
    pos1, pos2, tile_expert, n_used, tile_valid = _route_plan(meta, counts, tm, n_tiles)
    i1 = _plane_rows(pos1, n_rows)
    i2 = _plane_rows(pos2, n_rows)
    hs = _sc_scatter2(hp.reshape(PACK_PLANES * t, PLANE_W), i1, i2, PACK_PLANES * n_rows)
    ys = _moe_routed(hs.reshape(PACK_PLANES, n_rows, PLANE_W), tile_expert, n_used, tile_valid,
                     wg, wu, wd, layer, tm)
    ys = ys.reshape(PACK_PLANES * n_rows, PLANE_W)
    y1 = _sc_gather(ys, i1).reshape(PACK_PLANES, t, PLANE_W)
    y2 = _sc_gather(ys, i2).reshape(PACK_PLANES, t, PLANE_W)
    return y1, y2


def _final_kernel(x_ref, f_ref, g_ref, nw_ref, o_ref):
    xv = x_ref[0] + g_ref[0, 0] * f_ref[0].astype(F32)
    ms = jnp.mean(xv * xv, axis=-1, keepdims=True)
    o_ref[0] = xv * lax.rsqrt(ms + EPS) * nw_ref[...]


def _final(x, f, mods, nw):
    b, s, d = x.shape
    tm = min(s, 512)
    xspec = pl.BlockSpec((1, tm, d), lambda bi, m: (bi, m, 0))
    return pl.pallas_call(
        _final_kernel,
        grid=(b, s // tm),
        in_specs=[xspec, xspec,
                  pl.BlockSpec((1, 1, 1, d), lambda bi, m: (bi, 5, 0, 0)),
                  pl.BlockSpec((1, d), lambda bi, m: (0, 0))],
        out_specs=xspec,
        out_shape=jax.ShapeDtypeStruct((b, s, d), F32),
        compiler_params=_cparams(("arbitrary", "arbitrary")),
        name="final",
    )(x, f, mods, nw.reshape(1, d))


def _final_moe_kernel(x_ref, y1_ref, y2_ref, p_ref, g_ref, nw_ref, o_ref):
    tm, d = x_ref.shape[1], x_ref.shape[2]
    half = d // 2
    p = p_ref[0]
    p1 = jnp.concatenate([p[:, 0:LANES]] * (PLANE_W // LANES), axis=1)
    p2 = jnp.concatenate([p[:, LANES:2 * LANES]] * (PLANE_W // LANES), axis=1)
    ssq = jnp.zeros((tm, 1), F32)
    for c in range(PACK_PLANES):
        w1 = y1_ref[c]
        w2 = y2_ref[c]
        for unpack, off in ((_unpack_hi, 0), (_unpack_lo, half)):
            sl = slice(off + c * PLANE_W, off + (c + 1) * PLANE_W)
            f = p1 * unpack(w1) + p2 * unpack(w2)
            xv = x_ref[0, :, sl] + g_ref[0, 0, :, sl] * f
            o_ref[0, :, sl] = xv
            ssq = ssq + jnp.sum(xv * xv, axis=-1, keepdims=True)
    o_ref[0] = o_ref[0] * lax.rsqrt(ssq / d + EPS) * nw_ref[...]


def _final_moe(x, y1, y2, route, mods, nw):
    b, s, d = x.shape
    tm = min(s, 512)
    nm = s // tm
    xspec = pl.BlockSpec((1, tm, d), lambda bi, m: (bi, m, 0))
    yspec = pl.BlockSpec((PACK_PLANES, tm, PLANE_W), lambda bi, m: (0, bi * nm + m, 0))
    return pl.pallas_call(
        _final_moe_kernel,
        grid=(b, nm),
        in_specs=[xspec, yspec, yspec,
                  pl.BlockSpec((1, tm, 2 * LANES), lambda bi, m: (bi, m, 1)),
                  pl.BlockSpec((1, 1, 1, d), lambda bi, m: (bi, 5, 0, 0)),
                  pl.BlockSpec((1, d), lambda bi, m: (0, 0))],
        out_specs=xspec,
        out_shape=jax.ShapeDtypeStruct((b, s, d), F32),
        compiler_params=_cparams(("arbitrary", "arbitrary")),
        name="final",
    )(x, y1, y2, route, mods, nw.reshape(1, d))


def _w_in_plan():
    nb = GROUP_W // LANES
    plan = [(3 * nb + j, 0, 0) for j in range(4 * nb)]
    plan += [(j, 1, 0) for j in range(nb)]
    plan += [(nb + j, 0, 0) for j in range(2 * nb)]
    plan += [(7 * nb + j, 1, 0) for j in range(nb)]
    plan += [(8 * nb, 0, 1), (8 * nb, 0, 2)]
    plan += [(8 * nb + 1, 0, 1), (8 * nb + 1, 0, 2)]
    assert len(plan) * LANES == D_PROJ
    return jnp.asarray(plan, jnp.int32).T.reshape(-1)


def _prep_w_kernel(plan_ref, w_ref, o_ref):
    j = pl.program_id(1)
    nblk = pl.num_programs(1)
    x = w_ref[0]
    lane = lax.broadcasted_iota(jnp.int32, x.shape, 1)
    lo = lane < HEAD_DIM
    other = pltpu.roll(x, HEAD_DIM, 1)
    mode = plan_ref[2 * nblk + j]
    x = jnp.where(mode == 1, jnp.where(lo, x, other), jnp.where(mode == 2, jnp.where(lo, other, x), x))
    scale = jnp.where(plan_ref[nblk + j] == 1, HEAD_DIM ** -0.5, 1.0)
    o_ref[0] = (x * scale).astype(BF16)


def _prep_w_in(w_in):
    depth, d, _ = w_in.shape
    nblk = D_PROJ // LANES
    return pl.pallas_call(
        _prep_w_kernel,
        grid_spec=pltpu.PrefetchScalarGridSpec(
            num_scalar_prefetch=1,
            grid=(depth, nblk),
            in_specs=[pl.BlockSpec((1, d, LANES), lambda i, j, plan: (i, 0, plan[j]))],
            out_specs=pl.BlockSpec((1, d, LANES), lambda i, j, plan: (i, 0, j)),
        ),
        out_shape=jax.ShapeDtypeStruct((depth, d, D_PROJ), BF16),
        compiler_params=_cparams(("arbitrary", "arbitrary")),
        name="prep_w_in",
    )(_w_in_plan(), w_in)


def _rope_tables(n):
    t = jnp.arange(n, dtype=jnp.int32)
    rows = (t // GRID_W).astype(F32)
    cols = (t % GRID_W).astype(F32)
    nf = HEAD_DIM // 4
    inv = ROPE_BASE ** (-jnp.arange(nf, dtype=F32) / nf)
    ang = jnp.concatenate([rows[:, None] * inv, cols[:, None] * inv], axis=-1)
    cos = jnp.cos(ang)
    sin = jnp.sin(ang)
    cos_t = jnp.concatenate([cos, cos, cos, cos], axis=-1)
    sin_t = jnp.concatenate([-sin, sin, -sin, sin], axis=-1)
    return cos_t, sin_t


def kernel(x, c, ctx, c_ctx, w_ada, b_ada, norm_mix, norm_ffn, norm_final, w_in, w_out, na_rpb, pool_w, pool_scale, conv_w, swa_sink, ffn_w_gate, ffn_w_up, ffn_w_down, moe_router, moe_w_gate, moe_w_up, moe_w_down):
    b, s, d = x.shape
    lc = ctx.shape[1]
    depth = w_ada.shape[0]
    cond = jnp.concatenate([c, c_ctx[None, :]], axis=0)
    assert b == 2
    mods_all = _ada(cond, w_ada, b_ada)
    cos_t, sin_t = _rope_tables(s)
    cos_c = jnp.ones((lc, LANES), F32)
    sin_c = jnp.zeros((lc, LANES), F32)
    row_x = lambda bi: bi
    row_c = lambda bi: b

    w_proj = _prep_w_in(w_in)
    cx = ctx
    fx = fc = None
    for i in range(depth):
        last = i == depth - 1
        mods = mods_all[i].reshape(8, 6, 1, d)
        wi = (w_proj, i)
        wo = w_out[i].astype(BF16)
        if i == 0:
            proj, _ = _inproj(x, wi, norm_mix[i], mods, row_x, cos_t, sin_t)
            projc, _ = _inproj(cx, wi, norm_mix[i], mods, row_c, cos_c, sin_c)
        else:
            mods_prev = mods_all[i - 1].reshape(8, 6, 1, d)
            proj, x = _inproj_res(x, fx, mods_prev, wi, norm_mix[i], mods, row_x, cos_t, sin_t)
            projc, cx = _inproj_res(cx, fc, mods_prev, wi, norm_mix[i], mods, row_c, cos_c, sin_c)
        bias = _na_bias(na_rpb[i])
        o_a = _na_attn(proj, projc, bias)
        o_d = _swa_attn(proj, projc, swa_sink[i])
        moe = i % 2 == 1
        router = moe_router[i // 2] if moe else None
        outs = _outproj(o_a, o_d, proj, x, wo, pool_w[i], pool_scale[i], conv_w[i], norm_ffn[i], mods, row_x,
                        router)
        x, h2 = outs[0], outs[1]
        if not last:
            oc_a, oc_d = _ctx_attn(projc, swa_sink[i])
            cx, h2c = _outproj(oc_a, oc_d, projc, cx, wo, pool_w[i], pool_scale[i], conv_w[i], norm_ffn[i],
                               mods, row_c)
        j = i // 2
        if moe:
            if not last:
                raise NotImplementedError("an expert layer that is not the last layer")
            route = outs[2]
            y1, y2 = _moe(h2, route.reshape(b * s, 4 * LANES), moe_w_gate, moe_w_up, moe_w_down, j)
            return _final_moe(x, y1, y2, route, mods, norm_final)
        fx = _ffn(h2.reshape(b * s, d), ffn_w_gate, ffn_w_up, ffn_w_down, j).reshape(b, s, d)
        if not last:
            fc = _ffn(h2c.reshape(b * lc, d), ffn_w_gate, ffn_w_up, ffn_w_down, j).reshape(b, lc, d)
    return _final(x, fx, mods_all[depth - 1].reshape(8, 6, 1, d), norm_final)


def _inproj_res(x, f, mods_prev, w, nw, mods, mod_row, cos_t, sin_t):
    both = jnp.concatenate([mods, mods_prev], axis=1)
    return _inproj(x, w, nw, both, mod_row, cos_t, sin_t, res=(f, 6 + 5))
```

```python
import functools

import jax
import jax.numpy as jnp
from jax import lax
from jax.experimental import pallas as pl
from jax.experimental.pallas import tpu as pltpu
from jax.experimental.pallas import tpu_sc as plsc

F32 = jnp.float32
BF16 = jnp.bfloat16

D_MODEL = 2048
GRID_W = 64
HEAD_DIM = 64
EPS = 1e-6
NEG_INF = -1e30
GROUP_W = 512
NA_KH = 8
NA_KW = 16
POOL_WINDOWS = (2, 4, 8, 16)
POOL_CH = 128
SWA_WINDOW = 128
SWA_BLOCK = 128
ROPE_BASE = 10000.0
N_EXPERTS = 8
LANES = 128
SUBLANES = 8
HALO = 16

C_UBC = 0
C_QA = 2048
C_KA = 2560
C_VA = 3072
C_QD = 3584
C_KD = 4096
C_VD = 4352
D_PROJ = 4608
TN_PROJ = 512

VMEM_LIMIT = 56 * 1024 * 1024


def _cparams(sem):
    return pltpu.CompilerParams(dimension_semantics=sem, vmem_limit_bytes=VMEM_LIMIT)


def _dot(a, b):
    return jnp.dot(a, b, preferred_element_type=F32)


def _dot_t(a, b):
    return lax.dot_general(a, b, (((1,), (1,)), ((), ())), preferred_element_type=F32)


def _rms_mod(xv, nw, sh, sc):
    ms = jnp.mean(xv * xv, axis=-1, keepdims=True)
    y = xv * lax.rsqrt(ms + EPS) * nw
    return y * (1.0 + sc) + sh


PACK_PLANES = 4
PLANE_W = D_MODEL // 2 // PACK_PLANES
SC_WINDOW = 128


def _pack_bf16_pairs(v):
    half = v.shape[1] // 2
    hi = lax.bitcast_convert_type(v[:, :half].astype(BF16).astype(F32), jnp.uint32)
    lo = lax.bitcast_convert_type(v[:, half:].astype(BF16).astype(F32), jnp.uint32)
    return hi | (lo >> 16)


def _unpack_hi(w):
    return lax.bitcast_convert_type(w & jnp.uint32(0xFFFF0000), F32)


def _unpack_lo(w):
    return lax.bitcast_convert_type(w << 16, F32)


def _ada_kernel(c_ref, w_ref, b_ref, o_ref):
    tn = w_ref.shape[2]
    o_ref[...] = jnp.zeros(o_ref.shape, F32)
    for r in range(3):
        cv = c_ref[r]
        m = cv * (1.0 / (1.0 + jnp.exp(-cv)))
        cols = []
        for j in range(tn // LANES):
            wj = w_ref[0, :, j * LANES:(j + 1) * LANES]
            cols.append(jnp.sum(wj * m, axis=0, keepdims=True))
        o_ref[0, r:r + 1, :] = jnp.concatenate(cols, axis=1) + b_ref[0]


def _ada(cond, w_ada, b_ada):
    depth, d, n6 = w_ada.shape
    tn = 1024
    cb = jnp.broadcast_to(cond[:, :, None], (3, d, LANES))
    return pl.pallas_call(
        _ada_kernel,
        grid=(depth, n6 // tn),
        in_specs=[
            pl.BlockSpec((3, d, LANES), lambda i, j: (0, 0, 0)),
            pl.BlockSpec((1, d, tn), lambda i, j: (i, 0, j)),
            pl.BlockSpec((1, 1, tn), lambda i, j: (i, 0, j)),
        ],
        out_specs=pl.BlockSpec((1, SUBLANES, tn), lambda i, j: (i, 0, j)),
        out_shape=jax.ShapeDtypeStruct((depth, SUBLANES, n6), F32),
        compiler_params=_cparams(("arbitrary", "arbitrary")),
        name="ada",
    )(cb, w_ada, b_ada.reshape(depth, 1, n6))


def _rope(a, cosv, sinv, lo32):
    sw = jnp.where(lo32, pltpu.roll(a, 96, 1), pltpu.roll(a, 32, 1))
    return a * cosv + sw * sinv


def _inproj_kernel(*refs, has_res):
    if has_res:
        x_ref, f_ref, g_ref, nw_ref, sh_ref, sc_ref, cos_ref, sin_ref, w_ref, o_ref, x2_ref = refs
    else:
        x_ref, nw_ref, sh_ref, sc_ref, cos_ref, sin_ref, w_ref, o_ref = refs
    xv = x_ref[0]
    if has_res:
        xv = xv + g_ref[0, 0] * f_ref[0].astype(F32)
        x2_ref[0] = xv
    h = _rms_mod(xv, nw_ref[...], sh_ref[0, 0], sc_ref[0, 0]).astype(BF16)
    tm = h.shape[0]
    lane = lax.broadcasted_iota(jnp.int32, (tm, LANES), 1)
    lo32 = (lane % HEAD_DIM) < (HEAD_DIM // 2)
    rope_end = C_VD
    for n in range(D_PROJ // TN_PROJ):
        c0 = n * TN_PROJ
        acc = _dot(h, w_ref[0, :, c0:c0 + TN_PROJ])
        if c0 + TN_PROJ <= C_QD:
            o_ref[0, :, c0:c0 + TN_PROJ] = acc.astype(BF16)
            continue
        for g in range(TN_PROJ // LANES):
            a = acc[:, g * LANES:(g + 1) * LANES]
            if c0 + g * LANES < rope_end:
                a = _rope(a, cos_ref[...], sin_ref[...], lo32)
            o_ref[0, :, c0 + g * LANES:c0 + (g + 1) * LANES] = a.astype(BF16)


def _inproj(x, w, nw, mods, mod_row, cos_t, sin_t, res=None):
    b, s, d = x.shape
    w_all, layer = w
    tm = min(s, 512)
    has_res = res is not None
    row = mod_row
    xspec = pl.BlockSpec((1, tm, d), lambda bi, m: (bi, m, 0))

    def mspec(k):
        return pl.BlockSpec((1, 1, 1, d), lambda bi, m: (row(bi), k, 0, 0))

    in_specs = [xspec]
    args = [x]
    if has_res:
        f, gk = res
        in_specs += [xspec, mspec(gk)]
        args += [f, mods]
    in_specs += [
        pl.BlockSpec((1, d), lambda bi, m: (0, 0)),
        mspec(0), mspec(1),
        pl.BlockSpec((tm, LANES), lambda bi, m: (m, 0)),
        pl.BlockSpec((tm, LANES), lambda bi, m: (m, 0)),
        pl.BlockSpec((1, d, D_PROJ), lambda bi, m: (layer, 0, 0), pipeline_mode=pl.Buffered(1)),
    ]
    args += [nw.reshape(1, d), mods, mods, cos_t, sin_t, w_all]
    out_shape = [jax.ShapeDtypeStruct((b, s, D_PROJ), BF16)]
    out_specs = [pl.BlockSpec((1, tm, D_PROJ), lambda bi, m: (bi, m, 0))]
    if has_res:
        out_shape.append(jax.ShapeDtypeStruct((b, s, d), F32))
        out_specs.append(xspec)
    outs = pl.pallas_call(
        functools.partial(_inproj_kernel, has_res=has_res),
        grid=(b, s // tm),
        in_specs=in_specs,
        out_specs=out_specs,
        out_shape=out_shape,
        compiler_params=_cparams(("arbitrary", "arbitrary")),
        name="inproj",
    )(*args)
    return outs if has_res else (outs[0], None)


def _na_bias_kernel(rpb_ref, o_ref):
    h = pl.program_id(0)
    nd = 2 * NA_KH - 1
    nj = 2 * NA_KW - 1
    q = lax.broadcasted_iota(jnp.int32, (GRID_W, LANES), 0)
    lane = lax.broadcasted_iota(jnp.int32, (GRID_W, LANES), 1)
    kw = lane % GRID_W
    dc = jnp.clip(kw - q, -(NA_KW - 1), NA_KW - 1) + (NA_KW - 1)
    c0 = jnp.clip(q - NA_KW // 2, 0, GRID_W - NA_KW)
    valid = (kw >= c0) & (kw < c0 + NA_KW)
    tabs = []
    for d in range(nd):
        t = jnp.zeros((GRID_W, LANES), F32)
        for j in range(nj):
            t = jnp.where(dc == j, rpb_ref[h * (nd * nj) + d * nj + j], t)
        tabs.append(jnp.where(valid, t, NEG_INF))
    for c in range(NA_KH):
        for g in range(NA_KH // 2):
            d_lo = 2 * g - c + (NA_KH - 1)
            o_ref[c, 0, :, g * LANES:(g + 1) * LANES] = jnp.where(lane < GRID_W, tabs[d_lo], tabs[d_lo + 1])


def _na_bias(rpb):
    nh = rpb.shape[0]
    return pl.pallas_call(
        _na_bias_kernel,
        grid=(nh,),
        in_specs=[pl.BlockSpec(memory_space=pltpu.SMEM)],
        out_specs=pl.BlockSpec((NA_KH, 1, GRID_W, NA_KH * GRID_W), lambda h: (0, h, 0, 0)),
        out_shape=jax.ShapeDtypeStruct((NA_KH, nh, GRID_W, NA_KH * GRID_W), F32),
        compiler_params=_cparams(("arbitrary",)),
        name="na_bias",
    )(rpb.reshape(-1))


def _softmax_pv_staged(chains):
    add = lambda a, b: a + b
    ms = []
    for sp, _, extra in chains:
        m = functools.reduce(jnp.maximum, [jnp.max(s, axis=-1, keepdims=True) for s in sp])
        ms.append(m if extra is None else jnp.maximum(m, extra))
    ps = [[jnp.exp(s - m) for s in sp] for (sp, _, _), m in zip(chains, ms)]
    dens = []
    for pp, (_, _, extra), m in zip(ps, chains, ms):
        den = functools.reduce(add, [jnp.sum(p, axis=-1, keepdims=True) for p in pp])
        dens.append(den if extra is None else den + jnp.exp(extra - m))
    outs = [functools.reduce(add, [_dot(p.astype(BF16), v) for p, v in zip(pp, vp)])
            for pp, (_, vp, _) in zip(ps, chains)]
    return [o / d for o, d in zip(outs, dens)]


ROWS_PER_STEP = 16
NA_ROW_GROUP = 4


def _na_kernel(q_ref, k_ref, v_ref, kc_ref, vc_ref, bias_ref, o_ref, *, rows):
    rb = pl.program_id(2)
    lo = lax.broadcasted_iota(jnp.int32, (NA_ROW_GROUP * GRID_W, LANES), 1) < HEAD_DIM
    lo_row = lax.broadcasted_iota(jnp.int32, (GRID_W, LANES), 1) < HEAD_DIM
    kc = kc_ref[0]
    vc = vc_ref[0]
    kh = min(NA_KH, rows)
    win = kh * GRID_W

    g = NA_ROW_GROUP
    gq = g * GRID_W
    zero = jnp.zeros((gq, LANES), BF16)
    for i0 in range(0, ROWS_PER_STEP, g):
        q = q_ref[0, i0 * GRID_W:(i0 + g) * GRID_W, :]
        q2 = jnp.concatenate([jnp.where(lo, q, zero), jnp.where(lo, zero, q)], axis=0)
        s_ctx = _dot_t(q2, kc)
        pieces = [(hh, j, slice(hh * gq + j * GRID_W, hh * gq + (j + 1) * GRID_W))
                  for hh in range(2) for j in range(g)]
        vws, s_loc = [], {}
        for j in range(g):
            r = rb * ROWS_PER_STEP + i0 + j
            r0 = jnp.clip(r - kh // 2, 0, rows - kh)
            c = r - r0
            koff = pl.multiple_of(r0 * GRID_W, GRID_W)
            vws.append(v_ref[0, pl.ds(koff, win), :])
            qp = jnp.concatenate([q2[sl] for hh, jj, sl in pieces if jj == j], axis=0)
            sp = _dot_t(qp, k_ref[0, pl.ds(koff, win), :])
            for hh in range(2):
                s_loc[hh, j] = sp[hh * GRID_W:(hh + 1) * GRID_W] + bias_ref[c, hh]
        ms = {(hh, j): jnp.maximum(jnp.max(s_loc[hh, j], axis=-1, keepdims=True),
                                   jnp.max(s_ctx[sl], axis=-1, keepdims=True)) for hh, j, sl in pieces}
        p_loc = {(hh, j): jnp.exp(s_loc[hh, j] - ms[hh, j]) for hh, j, _ in pieces}
        p_ctx = {(hh, j): jnp.exp(s_ctx[sl] - ms[hh, j]) for hh, j, sl in pieces}
        den = {k: jnp.sum(p_loc[k], axis=-1, keepdims=True) + jnp.sum(p_ctx[k], axis=-1, keepdims=True)
               for k in p_loc}
        pv_ctx = _dot(jnp.concatenate([p_ctx[hh, j].astype(BF16) for hh, j, _ in pieces], axis=0), vc)
        for j in range(g):
            pv = _dot(jnp.concatenate([p_loc[0, j].astype(BF16), p_loc[1, j].astype(BF16)], axis=0), vws[j])
            o = [(pv[hh * GRID_W:(hh + 1) * GRID_W] + pv_ctx[hh * gq + j * GRID_W:hh * gq + (j + 1) * GRID_W])
                 / den[hh, j] for hh in range(2)]
            i = i0 + j
            o_ref[0, i * GRID_W:(i + 1) * GRID_W, :] = jnp.where(lo_row, o[0], o[1]).astype(BF16)


def _na_attn(proj, projc, bias):
    b, s, _ = proj.shape
    lc = projc.shape[1]
    rows = s // GRID_W
    assert rows >= NA_KH and rows % ROWS_PER_STEP == 0
    tq = ROWS_PER_STEP * GRID_W
    npair = GROUP_W // LANES
    return pl.pallas_call(
        functools.partial(_na_kernel, rows=rows),
        grid=(b, npair, rows // ROWS_PER_STEP),
        in_specs=[
            pl.BlockSpec((1, tq, LANES), lambda bi, p, r: (bi, r, C_QA // LANES + p)),
            pl.BlockSpec((1, s, LANES), lambda bi, p, r: (bi, 0, C_KA // LANES + p)),
            pl.BlockSpec((1, s, LANES), lambda bi, p, r: (bi, 0, C_VA // LANES + p)),
            pl.BlockSpec((1, lc, LANES), lambda bi, p, r: (bi, 0, C_KA // LANES + p)),
            pl.BlockSpec((1, lc, LANES), lambda bi, p, r: (bi, 0, C_VA // LANES + p)),
            pl.BlockSpec((NA_KH, 2, GRID_W, NA_KH * GRID_W), lambda bi, p, r: (0, p, 0, 0)),
        ],
        out_specs=pl.BlockSpec((1, tq, LANES), lambda bi, p, r: (bi, r, p)),
        out_shape=jax.ShapeDtypeStruct((b, s, GROUP_W), BF16),
        compiler_params=_cparams(("arbitrary", "arbitrary", "arbitrary")),
        name="na_attn",
    )(proj, proj, proj, projc, projc, bias)


def _gqa_chains(q, kv, sink_ref, k_parts, v_parts, mask):
    m = q.shape[0]
    lane = lax.broadcasted_iota(jnp.int32, (m, LANES), 1)
    lo = lane < HEAD_DIM
    qs = []
    for j in range(2):
        qg = q[:, (kv * 2 + j) * LANES:(kv * 2 + j + 1) * LANES]
        qs.append(jnp.where(lo, qg, jnp.zeros_like(qg)))
        qs.append(jnp.where(lo, jnp.zeros_like(qg), qg))
    qq = jnp.concatenate(qs, axis=0)
    s_all = [_dot_t(qq, k) for k in k_parts]
    chains = []
    for i in range(4):
        parts = [s[i * m:(i + 1) * m] for s in s_all]
        if mask is not None:
            parts[0] = jnp.where(mask, parts[0], NEG_INF)
        chains.append((parts, v_parts, jnp.full((m, 1), sink_ref[kv * 4 + i], F32)))
    return chains


def _gqa_store(o_ref, heads, row0=0):
    m = heads[0].shape[0]
    lane = lax.broadcasted_iota(jnp.int32, heads[0].shape, 1)
    lo = lane < HEAD_DIM
    for g in range(len(heads) // 2):
        o_ref[0, row0:row0 + m, g * LANES:(g + 1) * LANES] = (
            jnp.where(lo, heads[2 * g], heads[2 * g + 1]).astype(BF16))


SWA_BLOCKS_PER_STEP = 2


def _swa_kernel(sink_ref, q_ref, k_ref, v_ref, kc_ref, vc_ref, o_ref):
    nb = pl.num_programs(1) * SWA_BLOCKS_PER_STEP
    kc = kc_ref[0]
    vc = vc_ref[0]
    qi = lax.broadcasted_iota(jnp.int32, (SWA_BLOCK, 3 * SWA_BLOCK), 0)
    kj = lax.broadcasted_iota(jnp.int32, (SWA_BLOCK, 3 * SWA_BLOCK), 1)
    for blk in range(SWA_BLOCKS_PER_STEP):
        n = pl.program_id(1) * SWA_BLOCKS_PER_STEP + blk
        start = jnp.clip(n - 1, 0, nb - 3)
        koff = pl.multiple_of(start * SWA_BLOCK, SWA_BLOCK)
        kw = k_ref[0, pl.ds(koff, 3 * SWA_BLOCK), :]
        vw = v_ref[0, pl.ds(koff, 3 * SWA_BLOCK), :]
        valid = jnp.abs((n - start) * SWA_BLOCK + qi - kj) <= SWA_WINDOW
        q = q_ref[0, blk * SWA_BLOCK:(blk + 1) * SWA_BLOCK, :]
        chains = []
        for kv in range(2):
            sl = slice(kv * LANES, (kv + 1) * LANES)
            chains += _gqa_chains(q, kv, sink_ref, [kw[:, sl], kc[:, sl]], [vw[:, sl], vc[:, sl]], valid)
        _gqa_store(o_ref, _softmax_pv_staged(chains), blk * SWA_BLOCK)


def _swa_attn(proj, projc, sink):
    b, s, _ = proj.shape
    lc = projc.shape[1]
    nb = s // SWA_BLOCK
    assert nb >= 3 and nb % SWA_BLOCKS_PER_STEP == 0
    w2 = 2 * LANES
    tq = SWA_BLOCKS_PER_STEP * SWA_BLOCK
    return pl.pallas_call(
        _swa_kernel,
        grid=(b, nb // SWA_BLOCKS_PER_STEP),
        in_specs=[
            pl.BlockSpec(memory_space=pltpu.SMEM),
            pl.BlockSpec((1, tq, GROUP_W), lambda bi, n: (bi, n, C_QD // GROUP_W)),
            pl.BlockSpec((1, s, w2), lambda bi, n: (bi, 0, C_KD // w2)),
            pl.BlockSpec((1, s, w2), lambda bi, n: (bi, 0, C_VD // w2)),
            pl.BlockSpec((1, lc, w2), lambda bi, n: (bi, 0, C_KD // w2)),
            pl.BlockSpec((1, lc, w2), lambda bi, n: (bi, 0, C_VD // w2)),
        ],
        out_specs=pl.BlockSpec((1, tq, GROUP_W), lambda bi, n: (bi, n, 0)),
        out_shape=jax.ShapeDtypeStruct((b, s, GROUP_W), BF16),
        compiler_params=_cparams(("arbitrary", "arbitrary")),
        name="swa_attn",
    )(sink.reshape(-1), proj, proj, proj, projc, projc)


def _ctx_attn_kernel(sink_ref, qa_ref, ka_ref, va_ref, qd_ref, kd_ref, vd_ref, oa_ref, od_ref):
    m = qa_ref.shape[1]
    lane = lax.broadcasted_iota(jnp.int32, (m, LANES), 1)
    lo = lane < HEAD_DIM
    chains = []
    for p in range(GROUP_W // LANES):
        sl = slice(p * LANES, (p + 1) * LANES)
        q = qa_ref[0, :, sl]
        k = ka_ref[0, :, sl]
        v = va_ref[0, :, sl]
        for hh in range(2):
            qm = jnp.where(lo if hh == 0 else jnp.logical_not(lo), q, jnp.zeros_like(q))
            chains.append(([_dot_t(qm, k)], [v], None))
    _gqa_store(oa_ref, _softmax_pv_staged(chains))
    q = qd_ref[0]
    chains = []
    for kv in range(2):
        sl = slice(kv * LANES, (kv + 1) * LANES)
        chains += _gqa_chains(q, kv, sink_ref, [kd_ref[0, :, sl]], [vd_ref[0, :, sl]], None)
    _gqa_store(od_ref, _softmax_pv_staged(chains))


def _ctx_attn(projc, sink):
    b, lc, _ = projc.shape
    w2 = 2 * LANES

    def spec(width, col):
        return pl.BlockSpec((1, lc, width), lambda bi: (bi, 0, col // width))

    ospec = pl.BlockSpec((1, lc, GROUP_W), lambda bi: (bi, 0, 0))
    return pl.pallas_call(
        _ctx_attn_kernel,
        grid=(b,),
        in_specs=[pl.BlockSpec(memory_space=pltpu.SMEM),
                  spec(GROUP_W, C_QA), spec(GROUP_W, C_KA), spec(GROUP_W, C_VA),
                  spec(GROUP_W, C_QD), spec(w2, C_KD), spec(w2, C_VD)],
        out_specs=[ospec, ospec],
        out_shape=[jax.ShapeDtypeStruct((b, lc, GROUP_W), BF16)] * 2,
        compiler_params=_cparams(("arbitrary",)),
        name="ctx_attn",
    )(sink.reshape(-1), projc, projc, projc, projc, projc, projc)


def _outproj_kernel(*refs, tm, moe):
    (oa_ref, od_ref, u_ref, up_ref, un_ref, pw_ref, ps_ref, cw_ref, wo_ref, x_ref, g1_ref,
     nw_ref, sh_ref, sc_ref) = refs[:14]
    rest = refs[14:]
    if moe:
        rt_ref, xo_ref, h_ref, gate_ref, ext_ub, ext_u = rest
    else:
        xo_ref, h_ref, ext_ub, ext_u = rest
    m = pl.program_id(1)
    nm = pl.num_programs(1)
    g4 = GROUP_W

    def cols(ref, k):
        return ref[0, :, k * g4:(k + 1) * g4].astype(F32)

    ub = cols(u_ref, 0)
    gb = cols(u_ref, 1)
    has_prev = m > 0
    has_next = m < nm - 1
    ext_ub[0:HALO] = jnp.where(has_prev, cols(up_ref, 0), 0.0)
    ext_ub[HALO:HALO + tm] = ub
    ext_ub[HALO + tm:] = jnp.where(has_next, cols(un_ref, 0), 0.0)
    ext_u[0:HALO] = jnp.where(has_prev, cols(up_ref, 2) * cols(up_ref, 3), 0.0)
    ext_u[HALO:HALO + tm] = cols(u_ref, 2) * cols(u_ref, 3)
    ext_u[HALO + tm:] = jnp.where(has_next, cols(un_ref, 2) * cols(un_ref, 3), 0.0)

    n_tok = nm * tm
    t = m * tm + lax.broadcasted_iota(jnp.int32, (tm, LANES), 0)
    obs = []
    for g, w in enumerate(POOL_WINDOWS):
        sl = slice(g * LANES, (g + 1) * LANES)
        acc = None
        for d in range(-(w // 2), w - w // 2):
            term = ext_ub[HALO + d:HALO + d + tm, sl]
            acc = term if acc is None else acc + term
        cnt = (jnp.clip(t + (w - w // 2), 0, n_tok) - jnp.clip(t - w // 2, 0, n_tok)).astype(F32)
        pooled = acc / cnt - ub[:, sl]
        obs.append(_dot(pooled.astype(BF16), pw_ref[g]) * ps_ref[:, sl])
    o_b = jnp.concatenate(obs, axis=1).astype(BF16)

    y = (ext_u[HALO - 1:HALO - 1 + tm] * cw_ref[0:1, :] + ext_u[HALO:HALO + tm] * cw_ref[1:2, :]
         + ext_u[HALO + 1:HALO + 1 + tm] * cw_ref[2:3, :])
    o_c = (gb * y).astype(BF16)

    yy = _dot(jnp.concatenate([oa_ref[0], o_b, o_c, od_ref[0]], axis=1), wo_ref[...])
    xn = x_ref[0] + g1_ref[0, 0] * yy
    xo_ref[0] = xn
    h2 = _rms_mod(xn, nw_ref[...], sh_ref[0, 0], sc_ref[0, 0])

    if not moe:
        h_ref[0] = h2.astype(BF16)
    else:
        packed = _pack_bf16_pairs(h2)
        for c in range(PACK_PLANES):
            h_ref[c] = packed[:, c * PLANE_W:(c + 1) * PLANE_W]
        lane = lax.broadcasted_iota(jnp.int32, (tm, LANES), 1)
        logits = jnp.full((tm, LANES), -jnp.inf, F32)
        for e in range(N_EXPERTS):
            le = jnp.sum(h2 * rt_ref[e:e + 1, :], axis=-1, keepdims=True)
            logits = jnp.where(lane == e, le, logits)
        v1 = jnp.max(logits, axis=-1, keepdims=True)
        i1 = jnp.min(jnp.where(logits == v1, lane, LANES), axis=-1, keepdims=True)
        l2 = jnp.where(lane == i1, -jnp.inf, logits)
        v2 = jnp.max(l2, axis=-1, keepdims=True)
        i2 = jnp.min(jnp.where(l2 == v2, lane, LANES), axis=-1, keepdims=True)
        e2 = jnp.exp(v2 - v1)
        p1 = 1.0 / (1.0 + e2)
        p2 = e2 / (1.0 + e2)
        gate_ref[0, :, 0:LANES] = jnp.where(lane == i1, 1.0, 0.0)
        gate_ref[0, :, LANES:2 * LANES] = jnp.where(lane == i2, 1.0, 0.0)
        gate_ref[0, :, 2 * LANES:3 * LANES] = jnp.broadcast_to(p1, (tm, LANES))
        gate_ref[0, :, 3 * LANES:4 * LANES] = jnp.broadcast_to(p2, (tm, LANES))


def _outproj(o_a, o_d, proj, x, wo, pool_w, pool_scale, conv_w, nw, mods, mod_row, router=None):
    b, s, d = x.shape
    tm = min(s, 512)
    moe = router is not None
    row = mod_row
    nh = s // HALO
    th = tm // HALO

    def mspec(k):
        return pl.BlockSpec((1, 1, 1, d), lambda bi, m: (row(bi), k, 0, 0))

    gspec = pl.BlockSpec((1, tm, GROUP_W), lambda bi, m: (bi, m, 0))
    xspec = pl.BlockSpec((1, tm, d), lambda bi, m: (bi, m, 0))
    in_specs = [
        gspec, gspec,
        pl.BlockSpec((1, tm, 4 * GROUP_W), lambda bi, m: (bi, m, 0)),
        pl.BlockSpec((1, HALO, 4 * GROUP_W), lambda bi, m: (bi, jnp.maximum(m * th - 1, 0), 0)),
        pl.BlockSpec((1, HALO, 4 * GROUP_W), lambda bi, m: (bi, jnp.minimum((m + 1) * th, nh - 1), 0)),
        pl.BlockSpec((4, POOL_CH, POOL_CH), lambda bi, m: (0, 0, 0)),
        pl.BlockSpec((1, GROUP_W), lambda bi, m: (0, 0)),
        pl.BlockSpec((3, GROUP_W), lambda bi, m: (0, 0)),
        pl.BlockSpec((d, d), lambda bi, m: (0, 0), pipeline_mode=pl.Buffered(1)),
        xspec, mspec(2),
        pl.BlockSpec((1, d), lambda bi, m: (0, 0)),
        mspec(3), mspec(4),
    ]
    args = [o_a, o_d, proj, proj, proj, pool_w.astype(BF16), pool_scale.reshape(1, GROUP_W), conv_w, wo,
            x, mods, nw.reshape(1, d), mods, mods]
    if moe:
        nm = s // tm
        in_specs.append(pl.BlockSpec((N_EXPERTS, d), lambda bi, m: (0, 0)))
        args.append(router.T)
        out_shape = [jax.ShapeDtypeStruct((b, s, d), F32),
                     jax.ShapeDtypeStruct((PACK_PLANES, b * s, PLANE_W), jnp.uint32),
                     jax.ShapeDtypeStruct((b, s, 4 * LANES), F32)]
        out_specs = [xspec,
                     pl.BlockSpec((PACK_PLANES, tm, PLANE_W), lambda bi, m: (0, bi * nm + m, 0)),
                     pl.BlockSpec((1, tm, 4 * LANES), lambda bi, m: (bi, m, 0))]
    else:
        out_shape = [jax.ShapeDtypeStruct((b, s, d), F32), jax.ShapeDtypeStruct((b, s, d), BF16)]
        out_specs = [xspec, xspec]
    return pl.pallas_call(
        functools.partial(_outproj_kernel, tm=tm, moe=moe),
        grid=(b, s // tm),
        in_specs=in_specs,
        out_specs=out_specs,
        out_shape=out_shape,
        scratch_shapes=[pltpu.VMEM((tm + 2 * HALO, GROUP_W), F32), pltpu.VMEM((tm + 2 * HALO, GROUP_W), F32)],
        compiler_params=_cparams(("arbitrary", "arbitrary")),
        name="outproj",
    )(*args)


def _swiglu_accumulate(h_ref, wg_ref, wu_ref, wd_ref, acc_ref, rows=None):
    r = slice(0, h_ref.shape[0] if rows is None else rows)
    h = h_ref[r, :]
    a = _dot(h, wg_ref[0].astype(BF16))
    u = _dot(h, wu_ref[0].astype(BF16))
    act = a * (1.0 / (1.0 + jnp.exp(-a))) * u
    acc_ref[r, :] += _dot(act.astype(BF16), wd_ref[0].astype(BF16))


def _ffn_kernel(h_ref, wg_ref, wu_ref, wd_ref, o_ref, acc_ref):
    f = pl.program_id(1)

    @pl.when(f == 0)
    def _():
        acc_ref[...] = jnp.zeros(acc_ref.shape, F32)

    _swiglu_accumulate(h_ref, wg_ref, wu_ref, wd_ref, acc_ref)

    @pl.when(f == pl.num_programs(1) - 1)
    def _():
        o_ref[...] = acc_ref[...].astype(BF16)


def _ffn(h, wg, wu, wd, layer):
    mt, d = h.shape
    ff = wg.shape[2]
    tm = min(mt, 1024)
    tf = 256
    return pl.pallas_call(
        _ffn_kernel,
        grid=(mt // tm, ff // tf),
        in_specs=[
            pl.BlockSpec((tm, d), lambda m, f: (m, 0)),
            pl.BlockSpec((1, d, tf), lambda m, f: (layer, 0, f)),
            pl.BlockSpec((1, d, tf), lambda m, f: (layer, 0, f)),
            pl.BlockSpec((1, tf, d), lambda m, f: (layer, f, 0)),
        ],
        out_specs=pl.BlockSpec((tm, d), lambda m, f: (m, 0)),
        out_shape=jax.ShapeDtypeStruct((mt, d), BF16),
        scratch_shapes=[pltpu.VMEM((tm, d), F32)],
        compiler_params=_cparams(("arbitrary", "arbitrary")),
        name="ffn",
    )(h, wg, wu, wd)


def _rank_kernel(r_ref, o_ref, cnt_ref, carry):
    i = pl.program_id(0)
    tm = r_ref.shape[0]

    @pl.when(i == 0)
    def _():
        carry[...] = jnp.zeros(carry.shape, F32)

    oh1 = r_ref[:, 0:LANES]
    oh2 = r_ref[:, LANES:2 * LANES]
    sel = oh1 + oh2
    row = lax.broadcasted_iota(jnp.int32, (tm, tm), 0)
    col = lax.broadcasted_iota(jnp.int32, (tm, tm), 1)
    tri = jnp.where(col < row, 1.0, 0.0).astype(BF16)
    excl = _dot(tri, sel.astype(BF16)) + carry[0:1, :]
    lane = lax.broadcasted_iota(jnp.int32, (tm, LANES), 1)
    lane_f = lane.astype(F32)
    e1 = jnp.sum(oh1 * lane_f, axis=-1, keepdims=True)
    r1 = jnp.sum(oh1 * excl, axis=-1, keepdims=True)
    e2 = jnp.sum(oh2 * lane_f, axis=-1, keepdims=True)
    r2 = jnp.sum(oh2 * excl, axis=-1, keepdims=True)
    meta = jnp.where(lane == 0, e1, jnp.where(lane == 1, r1, jnp.where(lane == 2, e2,
                     jnp.where(lane == 3, r2, 0.0))))
    o_ref[...] = meta.T[0:SUBLANES, :]
    carry[...] = carry[...] + jnp.sum(sel, axis=0, keepdims=True)
    cnt_ref[...] = carry[...]


def _rank(route):
    t = route.shape[0]
    tm = min(t, 512)
    return pl.pallas_call(
        _rank_kernel,
        grid=(t // tm,),
        in_specs=[pl.BlockSpec((tm, 2 * LANES), lambda i: (i, 0))],
        out_specs=[pl.BlockSpec((SUBLANES, tm), lambda i: (0, i)),
                   pl.BlockSpec((SUBLANES, LANES), lambda i: (0, 0))],
        out_shape=[jax.ShapeDtypeStruct((SUBLANES, t), F32), jax.ShapeDtypeStruct((SUBLANES, LANES), F32)],
        scratch_shapes=[pltpu.VMEM((SUBLANES, LANES), F32)],
        compiler_params=_cparams(("arbitrary",)),
        name="rank",
    )(route)


def _sc_mesh():
    return plsc.VectorSubcoreMesh(core_axis_name="core", subcore_axis_name="subcore")


def _sc_scatter2(x, idx1, idx2, n_out):
    n, w = x.shape

    @functools.partial(pl.kernel, out_type=jax.ShapeDtypeStruct((n_out, w), x.dtype), mesh=_sc_mesh(),
                       scratch_types=[], name="sc_dispatch")
    def k(x_hbm, i1_hbm, i2_hbm, o_hbm):
        def body(x_vmem, i1_vmem, i2_vmem):
            pltpu.sync_copy(x_vmem, o_hbm.at[i1_vmem.at[0]])
            pltpu.sync_copy(x_vmem, o_hbm.at[i2_vmem.at[0]])

        pltpu.emit_pipeline(
            body, grid=(n // SC_WINDOW,),
            in_specs=[pl.BlockSpec((SC_WINDOW, w), lambda i: (i, 0)),
                      pl.BlockSpec((1, SC_WINDOW), lambda i: (0, i)),
                      pl.BlockSpec((1, SC_WINDOW), lambda i: (0, i))],
            out_specs=[], core_axis_name=("core", "subcore"),
            dimension_semantics=(pltpu.PARALLEL,))(x_hbm, i1_hbm, i2_hbm)

    return k(x, idx1.reshape(1, n), idx2.reshape(1, n))


def _sc_gather(y, idx):
    (n,) = idx.shape
    w = y.shape[1]

    @functools.partial(pl.kernel, out_type=jax.ShapeDtypeStruct((n, w), y.dtype), mesh=_sc_mesh(),
                       scratch_types=[], name="sc_combine")
    def k(y_hbm, i_hbm, o_hbm):
        def body(i_vmem, o_vmem):
            pltpu.sync_copy(y_hbm.at[i_vmem.at[0]], o_vmem)

        pltpu.emit_pipeline(
            body, grid=(n // SC_WINDOW,),
            in_specs=[pl.BlockSpec((1, SC_WINDOW), lambda i: (0, i))],
            out_specs=[pl.BlockSpec((SC_WINDOW, w), lambda i: (i, 0))],
            core_axis_name=("core", "subcore"),
            dimension_semantics=(pltpu.PARALLEL,))(i_hbm, o_hbm)

    return k(y, idx.reshape(1, n))


MOE_SUB = 256


def _moe_kernel(te_ref, nu_ref, tv_ref, h_ref, wg_ref, wu_ref, wd_ref, o_ref, acc_ref, hb_ref):
    m = pl.program_id(0)
    f = pl.program_id(1)
    tm = hb_ref.shape[0]
    half = hb_ref.shape[1] // 2

    @pl.when(m < nu_ref[0])
    def _():
        @pl.when(f == 0)
        def _():
            acc_ref[...] = jnp.zeros(acc_ref.shape, F32)
            routed = lax.broadcasted_iota(jnp.int32, (tm, PLANE_W), 0) < tv_ref[m]
            for c in range(PACK_PLANES):
                w = jnp.where(routed, h_ref[c], jnp.uint32(0))
                hb_ref[:, c * PLANE_W:(c + 1) * PLANE_W] = _unpack_hi(w).astype(BF16)
                hb_ref[:, half + c * PLANE_W:half + (c + 1) * PLANE_W] = _unpack_lo(w).astype(BF16)

        n_sub = (tv_ref[m] + MOE_SUB - 1) // MOE_SUB
        for k in range(1, tm // MOE_SUB + 1):
            @pl.when(n_sub == k)
            def _():
                _swiglu_accumulate(hb_ref, wg_ref, wu_ref, wd_ref, acc_ref, k * MOE_SUB)

        @pl.when(f == pl.num_programs(1) - 1)
        def _():
            packed = _pack_bf16_pairs(acc_ref[...])
            for c in range(PACK_PLANES):
                o_ref[c] = packed[:, c * PLANE_W:(c + 1) * PLANE_W]


MOE_TM = 1024


def _moe_routed(hp, tile_expert, n_used, tile_valid, wg, wu, wd, layer, tm):
    _, r, _ = hp.shape
    _, ne, d, ff = wg.shape
    wg = wg.reshape(-1, d, ff)
    wu = wu.reshape(-1, d, ff)
    wd = wd.reshape(-1, ff, d)
    e0 = layer * ne
    tf = 256
    nf = ff // tf

    def row_map(m, f, te, nu, tv):
        return (0, jnp.minimum(m, nu[0] - 1), 0)

    def fidx(m, f, nu):
        return jnp.where(m < nu[0], f, nf - 1)

    hspec = pl.BlockSpec((PACK_PLANES, tm, PLANE_W), row_map)
    return pl.pallas_call(
        _moe_kernel,
        grid_spec=pltpu.PrefetchScalarGridSpec(
            num_scalar_prefetch=3,
            grid=(r // tm, nf),
            in_specs=[
                hspec,
                pl.BlockSpec((1, d, tf), lambda m, f, te, nu, tv: (e0 + te[m], 0, fidx(m, f, nu))),
                pl.BlockSpec((1, d, tf), lambda m, f, te, nu, tv: (e0 + te[m], 0, fidx(m, f, nu))),
                pl.BlockSpec((1, tf, d), lambda m, f, te, nu, tv: (e0 + te[m], fidx(m, f, nu), 0)),
            ],
            out_specs=hspec,
            scratch_shapes=[pltpu.VMEM((tm, d), F32), pltpu.VMEM((tm, d), BF16)],
        ),
        out_shape=jax.ShapeDtypeStruct(hp.shape, jnp.uint32),
        compiler_params=_cparams(("arbitrary", "arbitrary")),
        name="moe",
    )(tile_expert, n_used, tile_valid, hp, wg, wu, wd)


def _route_plan(meta, counts, tm, n_tiles):
    e1, r1, e2, r2 = (meta[k].astype(jnp.int32) for k in range(4))
    cnt = counts[0, :N_EXPERTS].astype(jnp.int32)
    tiles_per = (cnt + tm - 1) // tm
    tile_end = jnp.cumsum(tiles_per)
    start_row = (tile_end - tiles_per) * tm
    pos1 = start_row[e1] + r1
    pos2 = start_row[e2] + r2
    n_used = tile_end[-1]
    tiles = jnp.arange(n_tiles, dtype=jnp.int32)
    tile_expert = jnp.sum((tiles[:, None] >= tile_end[None, :]).astype(jnp.int32), axis=1)
    last_expert = jnp.sum((n_used - 1 >= tile_end).astype(jnp.int32))
    tile_expert = jnp.minimum(tile_expert, last_expert)
    tile_start = (tile_end - tiles_per)[tile_expert]
    tile_valid = jnp.clip(cnt[tile_expert] - (tiles - tile_start) * tm, 0, tm)
    return pos1, pos2, tile_expert, n_used.reshape(1), tile_valid


def _plane_rows(pos, n_rows):
    return (jnp.arange(PACK_PLANES, dtype=jnp.int32)[:, None] * n_rows + pos[None, :]).reshape(-1)


def _moe(hp, route, wg, wu, wd, layer):
    _, t, _ = hp.shape
    tm = min(MOE_TM, t)
    n_tiles = 2 * t // tm + N_EXPERTS
    n_rows = n_tiles * tm
    meta, counts = _rank(route)
    pos1, pos2, tile_expert, n_used, tile_valid = _route_plan(meta, counts, tm, n_tiles)
    i1 = _plane_rows(pos1, n_rows)
    i2 = _plane_rows(pos2, n_rows)
    hs = _sc_scatter2(hp.reshape(PACK_PLANES * t, PLANE_W), i1, i2, PACK_PLANES * n_rows)
    ys = _moe_routed(hs.reshape(PACK_PLANES, n_rows, PLANE_W), tile_expert, n_used, tile_valid,
                     wg, wu, wd, layer, tm)
    ys = ys.reshape(PACK_PLANES * n_rows, PLANE_W)
    y1 = _sc_gather(ys, i1).reshape(PACK_PLANES, t, PLANE_W)
    y2 = _sc_gather(ys, i2).reshape(PACK_PLANES, t, PLANE_W)
    return y1, y2


def _final_kernel(x_ref, f_ref, g_ref, nw_ref, o_ref):
    xv = x_ref[0] + g_ref[0, 0] * f_ref[0].astype(F32)
    ms = jnp.mean(xv * xv, axis=-1, keepdims=True)
    o_ref[0] = xv * lax.rsqrt(ms + EPS) * nw_ref[...]


def _final(x, f, mods, nw):
    b, s, d = x.shape
    tm = min(s, 512)
    xspec = pl.BlockSpec((1, tm, d), lambda bi, m: (bi, m, 0))
    return pl.pallas_call(
        _final_kernel,
        grid=(b, s // tm),
        in_specs=[xspec, xspec,
                  pl.BlockSpec((1, 1, 1, d), lambda bi, m: (bi, 5, 0, 0)),
                  pl.BlockSpec((1, d), lambda bi, m: (0, 0))],
        out_specs=xspec,
        out_shape=jax.ShapeDtypeStruct((b, s, d), F32),
        compiler_params=_cparams(("arbitrary", "arbitrary")),
        name="final",
    )(x, f, mods, nw.reshape(1, d))


def _final_moe_kernel(x_ref, y1_ref, y2_ref, p_ref, g_ref, nw_ref, o_ref):
    tm, d = x_ref.shape[1], x_ref.shape[2]
    half = d // 2
    p = p_ref[0]
    p1 = jnp.concatenate([p[:, 0:LANES]] * (PLANE_W // LANES), axis=1)
    p2 = jnp.concatenate([p[:, LANES:2 * LANES]] * (PLANE_W // LANES), axis=1)
    ssq = jnp.zeros((tm, 1), F32)
    for c in range(PACK_PLANES):
        w1 = y1_ref[c]
        w2 = y2_ref[c]
        for unpack, off in ((_unpack_hi, 0), (_unpack_lo, half)):
            sl = slice(off + c * PLANE_W, off + (c + 1) * PLANE_W)
            f = p1 * unpack(w1) + p2 * unpack(w2)
            xv = x_ref[0, :, sl] + g_ref[0, 0, :, sl] * f
            o_ref[0, :, sl] = xv
            ssq = ssq + jnp.sum(xv * xv, axis=-1, keepdims=True)
    o_ref[0] = o_ref[0] * lax.rsqrt(ssq / d + EPS) * nw_ref[...]


def _final_moe(x, y1, y2, route, mods, nw):
    b, s, d = x.shape
    tm = min(s, 512)
    nm = s // tm
    xspec = pl.BlockSpec((1, tm, d), lambda bi, m: (bi, m, 0))
    yspec = pl.BlockSpec((PACK_PLANES, tm, PLANE_W), lambda bi, m: (0, bi * nm + m, 0))
    return pl.pallas_call(
        _final_moe_kernel,
        grid=(b, nm),
        in_specs=[xspec, yspec, yspec,
                  pl.BlockSpec((1, tm, 2 * LANES), lambda bi, m: (bi, m, 1)),
                  pl.BlockSpec((1, 1, 1, d), lambda bi, m: (bi, 5, 0, 0)),
                  pl.BlockSpec((1, d), lambda bi, m: (0, 0))],
        out_specs=xspec,
        out_shape=jax.ShapeDtypeStruct((b, s, d), F32),
        compiler_params=_cparams(("arbitrary", "arbitrary")),
        name="final",
    )(x, y1, y2, route, mods, nw.reshape(1, d))


def _w_in_plan():
    nb = GROUP_W // LANES
    plan = [(3 * nb + j, 0, 0) for j in range(4 * nb)]
    plan += [(j, 1, 0) for j in range(nb)]
    plan += [(nb + j, 0, 0) for j in range(2 * nb)]
    plan += [(7 * nb + j, 1, 0) for j in range(nb)]
    plan += [(8 * nb, 0, 1), (8 * nb, 0, 2)]
    plan += [(8 * nb + 1, 0, 1), (8 * nb + 1, 0, 2)]
    assert len(plan) * LANES == D_PROJ
    return jnp.asarray(plan, jnp.int32).T.reshape(-1)


def _prep_w_kernel(plan_ref, w_ref, o_ref):
    j = pl.program_id(1)
    nblk = pl.num_programs(1)
    x = w_ref[0]
    lane = lax.broadcasted_iota(jnp.int32, x.shape, 1)
    lo = lane < HEAD_DIM
    other = pltpu.roll(x, HEAD_DIM, 1)
    mode = plan_ref[2 * nblk + j]
    x = jnp.where(mode == 1, jnp.where(lo, x, other), jnp.where(mode == 2, jnp.where(lo, other, x), x))
    scale = jnp.where(plan_ref[nblk + j] == 1, HEAD_DIM ** -0.5, 1.0)
    o_ref[0] = (x * scale).astype(BF16)


def _prep_w_in(w_in):
    depth, d, _ = w_in.shape
    nblk = D_PROJ // LANES
    return pl.pallas_call(
        _prep_w_kernel,
        grid_spec=pltpu.PrefetchScalarGridSpec(
            num_scalar_prefetch=1,
            grid=(depth, nblk),
            in_specs=[pl.BlockSpec((1, d, LANES), lambda i, j, plan: (i, 0, plan[j]))],
            out_specs=pl.BlockSpec((1, d, LANES), lambda i, j, plan: (i, 0, j)),
        ),
        out_shape=jax.ShapeDtypeStruct((depth, d, D_PROJ), BF16),
        compiler_params=_cparams(("arbitrary", "arbitrary")),
        name="prep_w_in",
    )(_w_in_plan(), w_in)


def _rope_tables(n):
    t = jnp.arange(n, dtype=jnp.int32)
    rows = (t // GRID_W).astype(F32)
    cols = (t % GRID_W).astype(F32)
    nf = HEAD_DIM // 4
    inv = ROPE_BASE ** (-jnp.arange(nf, dtype=F32) / nf)
    ang = jnp.concatenate([rows[:, None] * inv, cols[:, None] * inv], axis=-1)
    cos = jnp.cos(ang)
    sin = jnp.sin(ang)
    cos_t = jnp.concatenate([cos, cos, cos, cos], axis=-1)
    sin_t = jnp.concatenate([-sin, sin, -sin, sin], axis=-1)
    return cos_t, sin_t


def kernel(x, c, ctx, c_ctx, w_ada, b_ada, norm_mix, norm_ffn, norm_final, w_in, w_out, na_rpb, pool_w, pool_scale, conv_w, swa_sink, ffn_w_gate, ffn_w_up, ffn_w_down, moe_router, moe_w_gate, moe_w_up, moe_w_down):
    b, s, d = x.shape
    lc = ctx.shape[1]
    depth = w_ada.shape[0]
    cond = jnp.concatenate([c, c_ctx[None, :]], axis=0)
    assert b == 2
    mods_all = _ada(cond, w_ada, b_ada)
    cos_t, sin_t = _rope_tables(s)
    cos_c = jnp.ones((lc, LANES), F32)
    sin_c = jnp.zeros((lc, LANES), F32)
    row_x = lambda bi: bi
    row_c = lambda bi: b

    w_proj = _prep_w_in(w_in)
    cx = ctx
    fx = fc = None
    for i in range(depth):
        last = i == depth - 1
        mods = mods_all[i].reshape(SUBLANES, 6, 1, d)
        wi = (w_proj, i)
        wo = w_out[i].astype(BF16)
        if i == 0:
            proj, _ = _inproj(x, wi, norm_mix[i], mods, row_x, cos_t, sin_t)
            projc, _ = _inproj(cx, wi, norm_mix[i], mods, row_c, cos_c, sin_c)
        else:
            mods_prev = mods_all[i - 1].reshape(SUBLANES, 6, 1, d)
            proj, x = _inproj_res(x, fx, mods_prev, wi, norm_mix[i], mods, row_x, cos_t, sin_t)
            projc, cx = _inproj_res(cx, fc, mods_prev, wi, norm_mix[i], mods, row_c, cos_c, sin_c)
        bias = _na_bias(na_rpb[i])
        o_a = _na_attn(proj, projc, bias)
        o_d = _swa_attn(proj, projc, swa_sink[i])
        moe = i % 2 == 1
        router = moe_router[i // 2] if moe else None
        outs = _outproj(o_a, o_d, proj, x, wo, pool_w[i], pool_scale[i], conv_w[i], norm_ffn[i], mods, row_x,
                        router)
        x, h2 = outs[0], outs[1]
        if not last:
            oc_a, oc_d = _ctx_attn(projc, swa_sink[i])
            cx, h2c = _outproj(oc_a, oc_d, projc, cx, wo, pool_w[i], pool_scale[i], conv_w[i], norm_ffn[i],
                               mods, row_c)
        j = i // 2
        if moe:
            if not last:
                raise NotImplementedError("an expert layer that is not the last layer")
            route = outs[2]
            y1, y2 = _moe(h2, route.reshape(b * s, 4 * LANES), moe_w_gate, moe_w_up, moe_w_down, j)
            return _final_moe(x, y1, y2, route, mods, norm_final)
        fx = _ffn(h2.reshape(b * s, d), ffn_w_gate, ffn_w_up, ffn_w_down, j).reshape(b, s, d)
        if not last:
            fc = _ffn(h2c.reshape(b * lc, d), ffn_w_gate, ffn_w_up, ffn_w_down, j).reshape(b, lc, d)
    return _final(x, fx, mods_all[depth - 1].reshape(SUBLANES, 6, 1, d), norm_final)


def _inproj_res(x, f, mods_prev, w, nw, mods, mod_row, cos_t, sin_t):
    both = jnp.concatenate([mods, mods_prev], axis=1)
    return _inproj(x, w, nw, both, mod_row, cos_t, sin_t, res=(f, 6 + 5))
```

```python
import functools

import jax
import jax.numpy as jnp
from jax import lax
from jax.experimental import pallas as pl
from jax.experimental.pallas import tpu as pltpu
from jax.experimental.pallas import tpu_sc as plsc

F32 = jnp.float32
BF16 = jnp.bfloat16

D_MODEL = 2048
GRID_W = 64
HEAD_DIM = 64
EPS = 1e-6
NEG_INF = -1e30
GROUP_W = 512
NA_KH = 8
NA_KW = 16
POOL_WINDOWS = (2, 4, 8, 16)
POOL_CH = 128
SWA_WINDOW = 128
SWA_BLOCK = 128
ROPE_BASE = 10000.0
N_EXPERTS = 8
LANES = 128
SUBLANES = 8
HALO = 16

C_UBC = 0
C_QA = 2048
C_KA = 2560
C_VA = 3072
C_QD = 3584
C_KD = 4096
C_VD = 4352
D_PROJ = 4608
TN_PROJ = 512

VMEM_LIMIT = 56 * 1024 * 1024


def _cparams(sem):
    return pltpu.CompilerParams(dimension_semantics=sem, vmem_limit_bytes=VMEM_LIMIT)


def _dot(a, b):
    return jnp.dot(a, b, preferred_element_type=F32)


def _dot_t(a, b):
    return lax.dot_general(a, b, (((1,), (1,)), ((), ())), preferred_element_type=F32)


def _rms_mod(xv, nw, sh, sc):
    ms = jnp.mean(xv * xv, axis=-1, keepdims=True)
    y = xv * lax.rsqrt(ms + EPS) * nw
    return y * (1.0 + sc) + sh


PACK_PLANES = 4
PLANE_W = D_MODEL // 2 // PACK_PLANES
SC_WINDOW = 128


def _pack_bf16_pairs(v):
    half = v.shape[1] // 2
    hi = lax.bitcast_convert_type(v[:, :half].astype(BF16).astype(F32), jnp.uint32)
    lo = lax.bitcast_convert_type(v[:, half:].astype(BF16).astype(F32), jnp.uint32)
    return hi | (lo >> 16)


def _unpack_hi(w):
    return lax.bitcast_convert_type(w & jnp.uint32(0xFFFF0000), F32)


def _unpack_lo(w):
    return lax.bitcast_convert_type(w << 16, F32)


def _ada_kernel(c_ref, w_ref, b_ref, o_ref, silu_scr):
    tn = w_ref.shape[2]

    @pl.when((pl.program_id(0) == 0) & (pl.program_id(1) == 0))
    def _():
        cv = c_ref[...]
        silu_scr[...] = cv * (1.0 / (1.0 + jnp.exp(-cv)))

    o_ref[...] = jnp.zeros(o_ref.shape, F32)
    for r in range(3):
        m = silu_scr[r]
        cols = []
        for j in range(tn // LANES):
            wj = w_ref[0, :, j * LANES:(j + 1) * LANES]
            cols.append(jnp.sum(wj * m, axis=0, keepdims=True))
        o_ref[0, r:r + 1, :] = jnp.concatenate(cols, axis=1) + b_ref[0]


def _ada(cond, w_ada, b_ada):
    depth, d, n6 = w_ada.shape
    tn = 1024
    cb = jnp.broadcast_to(cond[:, :, None], (3, d, LANES))
    return pl.pallas_call(
        _ada_kernel,
        grid=(depth, n6 // tn),
        in_specs=[
            pl.BlockSpec((3, d, LANES), lambda i, j: (0, 0, 0)),
            pl.BlockSpec((1, d, tn), lambda i, j: (i, 0, j)),
            pl.BlockSpec((1, 1, tn), lambda i, j: (i, 0, j)),
        ],
        out_specs=pl.BlockSpec((1, SUBLANES, tn), lambda i, j: (i, 0, j)),
        out_shape=jax.ShapeDtypeStruct((depth, SUBLANES, n6), F32),
        scratch_shapes=[pltpu.VMEM((3, d, LANES), F32)],
        compiler_params=_cparams(("arbitrary", "arbitrary")),
        name="ada",
    )(cb, w_ada, b_ada.reshape(depth, 1, n6))


def _rope(a, cosv, sinv, lo32):
    sw = jnp.where(lo32, pltpu.roll(a, 96, 1), pltpu.roll(a, 32, 1))
    return a * cosv + sw * sinv


def _inproj_kernel(*refs, has_res):
    if has_res:
        x_ref, f_ref, g_ref, nw_ref, sh_ref, sc_ref, cos_ref, sin_ref, w_ref, o_ref, x2_ref = refs
    else:
        x_ref, nw_ref, sh_ref, sc_ref, cos_ref, sin_ref, w_ref, o_ref = refs
    xv = x_ref[0]
    if has_res:
        xv = xv + g_ref[0, 0] * f_ref[0].astype(F32)
        x2_ref[0] = xv
    h = _rms_mod(xv, nw_ref[...], sh_ref[0, 0], sc_ref[0, 0]).astype(BF16)
    tm = h.shape[0]
    lane = lax.broadcasted_iota(jnp.int32, (tm, LANES), 1)
    lo32 = (lane % HEAD_DIM) < (HEAD_DIM // 2)
    rope_end = C_VD
    for n in range(D_PROJ // TN_PROJ):
        c0 = n * TN_PROJ
        acc = _dot(h, w_ref[0, :, c0:c0 + TN_PROJ])
        if c0 + TN_PROJ <= C_QD:
            o_ref[0, :, c0:c0 + TN_PROJ] = acc.astype(BF16)
            continue
        for g in range(TN_PROJ // LANES):
            a = acc[:, g * LANES:(g + 1) * LANES]
            if c0 + g * LANES < rope_end:
                a = _rope(a, cos_ref[...], sin_ref[...], lo32)
            o_ref[0, :, c0 + g * LANES:c0 + (g + 1) * LANES] = a.astype(BF16)


def _inproj(x, w, nw, mods, mod_row, cos_t, sin_t, res=None):
    b, s, d = x.shape
    w_all, layer = w
    tm = min(s, 512)
    has_res = res is not None
    row = mod_row
    xspec = pl.BlockSpec((1, tm, d), lambda bi, m: (bi, m, 0))

    def mspec(k):
        return pl.BlockSpec((1, 1, 1, d), lambda bi, m: (row(bi), k, 0, 0))

    in_specs = [xspec]
    args = [x]
    if has_res:
        f, gk = res
        in_specs += [xspec, mspec(gk)]
        args += [f, mods]
    in_specs += [
        pl.BlockSpec((1, d), lambda bi, m: (0, 0)),
        mspec(0), mspec(1),
        pl.BlockSpec((tm, LANES), lambda bi, m: (m, 0)),
        pl.BlockSpec((tm, LANES), lambda bi, m: (m, 0)),
        pl.BlockSpec((1, d, D_PROJ), lambda bi, m: (layer, 0, 0), pipeline_mode=pl.Buffered(1)),
    ]
    args += [nw.reshape(1, d), mods, mods, cos_t, sin_t, w_all]
    out_shape = [jax.ShapeDtypeStruct((b, s, D_PROJ), BF16)]
    out_specs = [pl.BlockSpec((1, tm, D_PROJ), lambda bi, m: (bi, m, 0))]
    if has_res:
        out_shape.append(jax.ShapeDtypeStruct((b, s, d), F32))
        out_specs.append(xspec)
    outs = pl.pallas_call(
        functools.partial(_inproj_kernel, has_res=has_res),
        grid=(b, s // tm),
        in_specs=in_specs,
        out_specs=out_specs,
        out_shape=out_shape,
        compiler_params=_cparams(("arbitrary", "arbitrary")),
        name="inproj",
    )(*args)
    return outs if has_res else (outs[0], None)


def _na_bias_kernel(rpb_ref, o_ref):
    h = pl.program_id(0)
    nd = 2 * NA_KH - 1
    nj = 2 * NA_KW - 1
    q = lax.broadcasted_iota(jnp.int32, (GRID_W, LANES), 0)
    lane = lax.broadcasted_iota(jnp.int32, (GRID_W, LANES), 1)
    kw = lane % GRID_W
    dc = jnp.clip(kw - q, -(NA_KW - 1), NA_KW - 1) + (NA_KW - 1)
    c0 = jnp.clip(q - NA_KW // 2, 0, GRID_W - NA_KW)
    valid = (kw >= c0) & (kw < c0 + NA_KW)
    tabs = []
    for d in range(nd):
        t = jnp.zeros((GRID_W, LANES), F32)
        for j in range(nj):
            t = jnp.where(dc == j, rpb_ref[h * (nd * nj) + d * nj + j], t)
        tabs.append(jnp.where(valid, t, NEG_INF))
    for c in range(NA_KH):
        for g in range(NA_KH // 2):
            d_lo = 2 * g - c + (NA_KH - 1)
            o_ref[c, 0, :, g * LANES:(g + 1) * LANES] = jnp.where(lane < GRID_W, tabs[d_lo], tabs[d_lo + 1])


def _na_bias(rpb):
    nh = rpb.shape[0]
    return pl.pallas_call(
        _na_bias_kernel,
        grid=(nh,),
        in_specs=[pl.BlockSpec(memory_space=pltpu.SMEM)],
        out_specs=pl.BlockSpec((NA_KH, 1, GRID_W, NA_KH * GRID_W), lambda h: (0, h, 0, 0)),
        out_shape=jax.ShapeDtypeStruct((NA_KH, nh, GRID_W, NA_KH * GRID_W), F32),
        compiler_params=_cparams(("arbitrary",)),
        name="na_bias",
    )(rpb.reshape(-1))


def _softmax_pv_staged(chains):
    add = lambda a, b: a + b
    ms = []
    for sp, _, extra in chains:
        m = functools.reduce(jnp.maximum, [jnp.max(s, axis=-1, keepdims=True) for s in sp])
        ms.append(m if extra is None else jnp.maximum(m, extra))
    ps = [[jnp.exp(s - m) for s in sp] for (sp, _, _), m in zip(chains, ms)]
    dens = []
    for pp, (_, _, extra), m in zip(ps, chains, ms):
        den = functools.reduce(add, [jnp.sum(p, axis=-1, keepdims=True) for p in pp])
        dens.append(den if extra is None else den + jnp.exp(extra - m))
    outs = [functools.reduce(add, [_dot(p.astype(BF16), v) for p, v in zip(pp, vp)])
            for pp, (_, vp, _) in zip(ps, chains)]
    return [o / d for o, d in zip(outs, dens)]


ROWS_PER_STEP = 16
NA_ROW_GROUP = 4


def _na_kernel(q_ref, k_ref, v_ref, kc_ref, vc_ref, bias_ref, o_ref, *, rows):
    rb = pl.program_id(2)
    lo = lax.broadcasted_iota(jnp.int32, (NA_ROW_GROUP * GRID_W, LANES), 1) < HEAD_DIM
    lo_row = lax.broadcasted_iota(jnp.int32, (GRID_W, LANES), 1) < HEAD_DIM
    kc = kc_ref[0]
    vc = vc_ref[0]
    kh = min(NA_KH, rows)
    win = kh * GRID_W

    g = NA_ROW_GROUP
    gq = g * GRID_W
    zero = jnp.zeros((gq, LANES), BF16)
    for i0 in range(0, ROWS_PER_STEP, g):
        q = q_ref[0, i0 * GRID_W:(i0 + g) * GRID_W, :]
        q2 = jnp.concatenate([jnp.where(lo, q, zero), jnp.where(lo, zero, q)], axis=0)
        s_ctx = _dot_t(q2, kc)
        pieces = [(hh, j, slice(hh * gq + j * GRID_W, hh * gq + (j + 1) * GRID_W))
                  for hh in range(2) for j in range(g)]
        vws, s_loc = [], {}
        for j in range(g):
            r = rb * ROWS_PER_STEP + i0 + j
            r0 = jnp.clip(r - kh // 2, 0, rows - kh)
            c = r - r0
            koff = pl.multiple_of(r0 * GRID_W, GRID_W)
            vws.append(v_ref[0, pl.ds(koff, win), :])
            qp = jnp.concatenate([q2[sl] for hh, jj, sl in pieces if jj == j], axis=0)
            sp = _dot_t(qp, k_ref[0, pl.ds(koff, win), :])
            for hh in range(2):
                s_loc[hh, j] = sp[hh * GRID_W:(hh + 1) * GRID_W] + bias_ref[c, hh]
        ms = {(hh, j): jnp.maximum(jnp.max(s_loc[hh, j], axis=-1, keepdims=True),
                                   jnp.max(s_ctx[sl], axis=-1, keepdims=True)) for hh, j, sl in pieces}
        p_loc = {(hh, j): jnp.exp(s_loc[hh, j] - ms[hh, j]) for hh, j, _ in pieces}
        p_ctx = {(hh, j): jnp.exp(s_ctx[sl] - ms[hh, j]) for hh, j, sl in pieces}
        den = {k: jnp.sum(p_loc[k], axis=-1, keepdims=True) + jnp.sum(p_ctx[k], axis=-1, keepdims=True)
               for k in p_loc}
        pv_ctx = _dot(jnp.concatenate([p_ctx[hh, j].astype(BF16) for hh, j, _ in pieces], axis=0), vc)
        for j in range(g):
            pv = _dot(jnp.concatenate([p_loc[0, j].astype(BF16), p_loc[1, j].astype(BF16)], axis=0), vws[j])
            o = [(pv[hh * GRID_W:(hh + 1) * GRID_W] + pv_ctx[hh * gq + j * GRID_W:hh * gq + (j + 1) * GRID_W])
                 / den[hh, j] for hh in range(2)]
            i = i0 + j
            o_ref[0, i * GRID_W:(i + 1) * GRID_W, :] = jnp.where(lo_row, o[0], o[1]).astype(BF16)


def _na_attn(proj, projc, bias):
    b, s, _ = proj.shape
    lc = projc.shape[1]
    rows = s // GRID_W
    assert rows >= NA_KH and rows % ROWS_PER_STEP == 0
    tq = ROWS_PER_STEP * GRID_W
    npair = GROUP_W // LANES
    return pl.pallas_call(
        functools.partial(_na_kernel, rows=rows),
        grid=(b, npair, rows // ROWS_PER_STEP),
        in_specs=[
            pl.BlockSpec((1, tq, LANES), lambda bi, p, r: (bi, r, C_QA // LANES + p)),
            pl.BlockSpec((1, s, LANES), lambda bi, p, r: (bi, 0, C_KA // LANES + p)),
            pl.BlockSpec((1, s, LANES), lambda bi, p, r: (bi, 0, C_VA // LANES + p)),
            pl.BlockSpec((1, lc, LANES), lambda bi, p, r: (bi, 0, C_KA // LANES + p)),
            pl.BlockSpec((1, lc, LANES), lambda bi, p, r: (bi, 0, C_VA // LANES + p)),
            pl.BlockSpec((NA_KH, 2, GRID_W, NA_KH * GRID_W), lambda bi, p, r: (0, p, 0, 0)),
        ],
        out_specs=pl.BlockSpec((1, tq, LANES), lambda bi, p, r: (bi, r, p)),
        out_shape=jax.ShapeDtypeStruct((b, s, GROUP_W), BF16),
        compiler_params=_cparams(("arbitrary", "arbitrary", "arbitrary")),
        name="na_attn",
    )(proj, proj, proj, projc, projc, bias)


def _gqa_chains(q, kv, sink_ref, k_parts, v_parts, mask):
    m = q.shape[0]
    lane = lax.broadcasted_iota(jnp.int32, (m, LANES), 1)
    lo = lane < HEAD_DIM
    qs = []
    for j in range(2):
        qg = q[:, (kv * 2 + j) * LANES:(kv * 2 + j + 1) * LANES]
        qs.append(jnp.where(lo, qg, jnp.zeros_like(qg)))
        qs.append(jnp.where(lo, jnp.zeros_like(qg), qg))
    qq = jnp.concatenate(qs, axis=0)
    s_all = [_dot_t(qq, k) for k in k_parts]
    chains = []
    for i in range(4):
        parts = [s[i * m:(i + 1) * m] for s in s_all]
        if mask is not None:
            parts[0] = jnp.where(mask, parts[0], NEG_INF)
        chains.append((parts, v_parts, jnp.full((m, 1), sink_ref[kv * 4 + i], F32)))
    return chains


def _gqa_store(o_ref, heads, row0=0):
    m = heads[0].shape[0]
    lane = lax.broadcasted_iota(jnp.int32, heads[0].shape, 1)
    lo = lane < HEAD_DIM
    for g in range(len(heads) // 2):
        o_ref[0, row0:row0 + m, g * LANES:(g + 1) * LANES] = (
            jnp.where(lo, heads[2 * g], heads[2 * g + 1]).astype(BF16))


SWA_BLOCKS_PER_STEP = 2


def _swa_kernel(sink_ref, q_ref, k_ref, v_ref, kc_ref, vc_ref, o_ref):
    nb = pl.num_programs(1) * SWA_BLOCKS_PER_STEP
    kc = kc_ref[0]
    vc = vc_ref[0]
    qi = lax.broadcasted_iota(jnp.int32, (SWA_BLOCK, 3 * SWA_BLOCK), 0)
    kj = lax.broadcasted_iota(jnp.int32, (SWA_BLOCK, 3 * SWA_BLOCK), 1)
    for blk in range(SWA_BLOCKS_PER_STEP):
        n = pl.program_id(1) * SWA_BLOCKS_PER_STEP + blk
        start = jnp.clip(n - 1, 0, nb - 3)
        koff = pl.multiple_of(start * SWA_BLOCK, SWA_BLOCK)
        kw = k_ref[0, pl.ds(koff, 3 * SWA_BLOCK), :]
        vw = v_ref[0, pl.ds(koff, 3 * SWA_BLOCK), :]
        valid = jnp.abs((n - start) * SWA_BLOCK + qi - kj) <= SWA_WINDOW
        q = q_ref[0, blk * SWA_BLOCK:(blk + 1) * SWA_BLOCK, :]
        chains = []
        for kv in range(2):
            sl = slice(kv * LANES, (kv + 1) * LANES)
            chains += _gqa_chains(q, kv, sink_ref, [kw[:, sl], kc[:, sl]], [vw[:, sl], vc[:, sl]], valid)
        _gqa_store(o_ref, _softmax_pv_staged(chains), blk * SWA_BLOCK)


def _swa_attn(proj, projc, sink):
    b, s, _ = proj.shape
    lc = projc.shape[1]
    nb = s // SWA_BLOCK
    assert nb >= 3 and nb % SWA_BLOCKS_PER_STEP == 0
    w2 = 2 * LANES
    tq = SWA_BLOCKS_PER_STEP * SWA_BLOCK
    return pl.pallas_call(
        _swa_kernel,
        grid=(b, nb // SWA_BLOCKS_PER_STEP),
        in_specs=[
            pl.BlockSpec(memory_space=pltpu.SMEM),
            pl.BlockSpec((1, tq, GROUP_W), lambda bi, n: (bi, n, C_QD // GROUP_W)),
            pl.BlockSpec((1, s, w2), lambda bi, n: (bi, 0, C_KD // w2)),
            pl.BlockSpec((1, s, w2), lambda bi, n: (bi, 0, C_VD // w2)),
            pl.BlockSpec((1, lc, w2), lambda bi, n: (bi, 0, C_KD // w2)),
            pl.BlockSpec((1, lc, w2), lambda bi, n: (bi, 0, C_VD // w2)),
        ],
        out_specs=pl.BlockSpec((1, tq, GROUP_W), lambda bi, n: (bi, n, 0)),
        out_shape=jax.ShapeDtypeStruct((b, s, GROUP_W), BF16),
        compiler_params=_cparams(("arbitrary", "arbitrary")),
        name="swa_attn",
    )(sink.reshape(-1), proj, proj, proj, projc, projc)


def _ctx_attn_kernel(sink_ref, qa_ref, ka_ref, va_ref, qd_ref, kd_ref, vd_ref, oa_ref, od_ref):
    m = qa_ref.shape[1]
    lane = lax.broadcasted_iota(jnp.int32, (m, LANES), 1)
    lo = lane < HEAD_DIM
    chains = []
    for p in range(GROUP_W // LANES):
        sl = slice(p * LANES, (p + 1) * LANES)
        q = qa_ref[0, :, sl]
        k = ka_ref[0, :, sl]
        v = va_ref[0, :, sl]
        for hh in range(2):
            qm = jnp.where(lo if hh == 0 else jnp.logical_not(lo), q, jnp.zeros_like(q))
            chains.append(([_dot_t(qm, k)], [v], None))
    _gqa_store(oa_ref, _softmax_pv_staged(chains))
    q = qd_ref[0]
    chains = []
    for kv in range(2):
        sl = slice(kv * LANES, (kv + 1) * LANES)
        chains += _gqa_chains(q, kv, sink_ref, [kd_ref[0, :, sl]], [vd_ref[0, :, sl]], None)
    _gqa_store(od_ref, _softmax_pv_staged(chains))


def _ctx_attn(projc, sink):
    b, lc, _ = projc.shape
    w2 = 2 * LANES

    def spec(width, col):
        return pl.BlockSpec((1, lc, width), lambda bi: (bi, 0, col // width))

    ospec = pl.BlockSpec((1, lc, GROUP_W), lambda bi: (bi, 0, 0))
    return pl.pallas_call(
        _ctx_attn_kernel,
        grid=(b,),
        in_specs=[pl.BlockSpec(memory_space=pltpu.SMEM),
                  spec(GROUP_W, C_QA), spec(GROUP_W, C_KA), spec(GROUP_W, C_VA),
                  spec(GROUP_W, C_QD), spec(w2, C_KD), spec(w2, C_VD)],
        out_specs=[ospec, ospec],
        out_shape=[jax.ShapeDtypeStruct((b, lc, GROUP_W), BF16)] * 2,
        compiler_params=_cparams(("arbitrary",)),
        name="ctx_attn",
    )(sink.reshape(-1), projc, projc, projc, projc, projc, projc)


def _outproj_kernel(*refs, tm, moe):
    (oa_ref, od_ref, u_ref, up_ref, un_ref, pw_ref, ps_ref, cw_ref, wo_ref, x_ref, g1_ref,
     nw_ref, sh_ref, sc_ref) = refs[:14]
    rest = refs[14:]
    if moe:
        rt_ref, xo_ref, h_ref, gate_ref, ext_ub, ext_u = rest
    else:
        xo_ref, h_ref, ext_ub, ext_u = rest
    m = pl.program_id(1)
    nm = pl.num_programs(1)
    g4 = GROUP_W

    def cols(ref, k):
        return ref[0, :, k * g4:(k + 1) * g4].astype(F32)

    ub = cols(u_ref, 0)
    gb = cols(u_ref, 1)
    has_prev = m > 0
    has_next = m < nm - 1
    ext_ub[0:HALO] = jnp.where(has_prev, cols(up_ref, 0), 0.0)
    ext_ub[HALO:HALO + tm] = ub
    ext_ub[HALO + tm:] = jnp.where(has_next, cols(un_ref, 0), 0.0)
    ext_u[0:HALO] = jnp.where(has_prev, cols(up_ref, 2) * cols(up_ref, 3), 0.0)
    ext_u[HALO:HALO + tm] = cols(u_ref, 2) * cols(u_ref, 3)
    ext_u[HALO + tm:] = jnp.where(has_next, cols(un_ref, 2) * cols(un_ref, 3), 0.0)

    n_tok = nm * tm
    t = m * tm + lax.broadcasted_iota(jnp.int32, (tm, LANES), 0)
    obs = []
    for g, w in enumerate(POOL_WINDOWS):
        sl = slice(g * LANES, (g + 1) * LANES)
        acc = None
        for d in range(-(w // 2), w - w // 2):
            term = ext_ub[HALO + d:HALO + d + tm, sl]
            acc = term if acc is None else acc + term
        cnt = (jnp.clip(t + (w - w // 2), 0, n_tok) - jnp.clip(t - w // 2, 0, n_tok)).astype(F32)
        pooled = acc / cnt - ub[:, sl]
        obs.append(_dot(pooled.astype(BF16), pw_ref[g]) * ps_ref[:, sl])
    o_b = jnp.concatenate(obs, axis=1).astype(BF16)

    y = (ext_u[HALO - 1:HALO - 1 + tm] * cw_ref[0:1, :] + ext_u[HALO:HALO + tm] * cw_ref[1:2, :]
         + ext_u[HALO + 1:HALO + 1 + tm] * cw_ref[2:3, :])
    o_c = (gb * y).astype(BF16)

    yy = _dot(jnp.concatenate([oa_ref[0], o_b, o_c, od_ref[0]], axis=1), wo_ref[...])
    xn = x_ref[0] + g1_ref[0, 0] * yy
    xo_ref[0] = xn
    h2 = _rms_mod(xn, nw_ref[...], sh_ref[0, 0], sc_ref[0, 0])

    if not moe:
        h_ref[0] = h2.astype(BF16)
    else:
        packed = _pack_bf16_pairs(h2)
        for c in range(PACK_PLANES):
            h_ref[c] = packed[:, c * PLANE_W:(c + 1) * PLANE_W]
        lane = lax.broadcasted_iota(jnp.int32, (tm, LANES), 1)
        logits = jnp.full((tm, LANES), -jnp.inf, F32)
        for e in range(N_EXPERTS):
            le = jnp.sum(h2 * rt_ref[e:e + 1, :], axis=-1, keepdims=True)
            logits = jnp.where(lane == e, le, logits)
        v1 = jnp.max(logits, axis=-1, keepdims=True)
        i1 = jnp.min(jnp.where(logits == v1, lane, LANES), axis=-1, keepdims=True)
        l2 = jnp.where(lane == i1, -jnp.inf, logits)
        v2 = jnp.max(l2, axis=-1, keepdims=True)
        i2 = jnp.min(jnp.where(l2 == v2, lane, LANES), axis=-1, keepdims=True)
        e2 = jnp.exp(v2 - v1)
        p1 = 1.0 / (1.0 + e2)
        p2 = e2 / (1.0 + e2)
        gate_ref[0, :, 0:LANES] = jnp.where(lane == i1, 1.0, 0.0)
        gate_ref[0, :, LANES:2 * LANES] = jnp.where(lane == i2, 1.0, 0.0)
        gate_ref[0, :, 2 * LANES:3 * LANES] = jnp.broadcast_to(p1, (tm, LANES))
        gate_ref[0, :, 3 * LANES:4 * LANES] = jnp.broadcast_to(p2, (tm, LANES))


def _outproj(o_a, o_d, proj, x, wo, pool_w, pool_scale, conv_w, nw, mods, mod_row, router=None):
    b, s, d = x.shape
    tm = min(s, 512)
    moe = router is not None
    row = mod_row
    nh = s // HALO
    th = tm // HALO

    def mspec(k):
        return pl.BlockSpec((1, 1, 1, d), lambda bi, m: (row(bi), k, 0, 0))

    gspec = pl.BlockSpec((1, tm, GROUP_W), lambda bi, m: (bi, m, 0))
    xspec = pl.BlockSpec((1, tm, d), lambda bi, m: (bi, m, 0))
    in_specs = [
        gspec, gspec,
        pl.BlockSpec((1, tm, 4 * GROUP_W), lambda bi, m: (bi, m, 0)),
        pl.BlockSpec((1, HALO, 4 * GROUP_W), lambda bi, m: (bi, jnp.maximum(m * th - 1, 0), 0)),
        pl.BlockSpec((1, HALO, 4 * GROUP_W), lambda bi, m: (bi, jnp.minimum((m + 1) * th, nh - 1), 0)),
        pl.BlockSpec((4, POOL_CH, POOL_CH), lambda bi, m: (0, 0, 0)),
        pl.BlockSpec((1, GROUP_W), lambda bi, m: (0, 0)),
        pl.BlockSpec((3, GROUP_W), lambda bi, m: (0, 0)),
        pl.BlockSpec((d, d), lambda bi, m: (0, 0), pipeline_mode=pl.Buffered(1)),
        xspec, mspec(2),
        pl.BlockSpec((1, d), lambda bi, m: (0, 0)),
        mspec(3), mspec(4),
    ]
    args = [o_a, o_d, proj, proj, proj, pool_w.astype(BF16), pool_scale.reshape(1, GROUP_W), conv_w, wo,
            x, mods, nw.reshape(1, d), mods, mods]
    if moe:
        nm = s // tm
        in_specs.append(pl.BlockSpec((N_EXPERTS, d), lambda bi, m: (0, 0)))
        args.append(router.T)
        out_shape = [jax.ShapeDtypeStruct((b, s, d), F32),
                     jax.ShapeDtypeStruct((PACK_PLANES, b * s, PLANE_W), jnp.uint32),
                     jax.ShapeDtypeStruct((b, s, 4 * LANES), F32)]
        out_specs = [xspec,
                     pl.BlockSpec((PACK_PLANES, tm, PLANE_W), lambda bi, m: (0, bi * nm + m, 0)),
                     pl.BlockSpec((1, tm, 4 * LANES), lambda bi, m: (bi, m, 0))]
    else:
        out_shape = [jax.ShapeDtypeStruct((b, s, d), F32), jax.ShapeDtypeStruct((b, s, d), BF16)]
        out_specs = [xspec, xspec]
    return pl.pallas_call(
        functools.partial(_outproj_kernel, tm=tm, moe=moe),
        grid=(b, s // tm),
        in_specs=in_specs,
        out_specs=out_specs,
        out_shape=out_shape,
        scratch_shapes=[pltpu.VMEM((tm + 2 * HALO, GROUP_W), F32), pltpu.VMEM((tm + 2 * HALO, GROUP_W), F32)],
        compiler_params=_cparams(("arbitrary", "arbitrary")),
        name="outproj",
    )(*args)


def _swiglu_accumulate(h_ref, wg_ref, wu_ref, wd_ref, acc_ref, rows=None):
    r = slice(0, h_ref.shape[0] if rows is None else rows)
    h = h_ref[r, :]
    a = _dot(h, wg_ref[0].astype(BF16))
    u = _dot(h, wu_ref[0].astype(BF16))
    act = a * (1.0 / (1.0 + jnp.exp(-a))) * u
    acc_ref[r, :] += _dot(act.astype(BF16), wd_ref[0].astype(BF16))


def _ffn_kernel(h_ref, wg_ref, wu_ref, wd_ref, o_ref, acc_ref):
    f = pl.program_id(1)

    @pl.when(f == 0)
    def _():
        acc_ref[...] = jnp.zeros(acc_ref.shape, F32)

    _swiglu_accumulate(h_ref, wg_ref, wu_ref, wd_ref, acc_ref)

    @pl.when(f == pl.num_programs(1) - 1)
    def _():
        o_ref[...] = acc_ref[...].astype(BF16)


def _ffn(h, wg, wu, wd, layer):
    mt, d = h.shape
    ff = wg.shape[2]
    tm = min(mt, 1024)
    tf = 256
    return pl.pallas_call(
        _ffn_kernel,
        grid=(mt // tm, ff // tf),
        in_specs=[
            pl.BlockSpec((tm, d), lambda m, f: (m, 0)),
            pl.BlockSpec((1, d, tf), lambda m, f: (layer, 0, f)),
            pl.BlockSpec((1, d, tf), lambda m, f: (layer, 0, f)),
            pl.BlockSpec((1, tf, d), lambda m, f: (layer, f, 0)),
        ],
        out_specs=pl.BlockSpec((tm, d), lambda m, f: (m, 0)),
        out_shape=jax.ShapeDtypeStruct((mt, d), BF16),
        scratch_shapes=[pltpu.VMEM((tm, d), F32)],
        compiler_params=_cparams(("arbitrary", "arbitrary")),
        name="ffn",
    )(h, wg, wu, wd)


def _rank_kernel(r_ref, o_ref, cnt_ref, carry):
    i = pl.program_id(0)
    tm = r_ref.shape[0]

    @pl.when(i == 0)
    def _():
        carry[...] = jnp.zeros(carry.shape, F32)

    oh1 = r_ref[:, 0:LANES]
    oh2 = r_ref[:, LANES:2 * LANES]
    sel = oh1 + oh2
    row = lax.broadcasted_iota(jnp.int32, (tm, tm), 0)
    col = lax.broadcasted_iota(jnp.int32, (tm, tm), 1)
    tri = jnp.where(col < row, 1.0, 0.0).astype(BF16)
    excl = _dot(tri, sel.astype(BF16)) + carry[0:1, :]
    lane = lax.broadcasted_iota(jnp.int32, (tm, LANES), 1)
    lane_f = lane.astype(F32)
    e1 = jnp.sum(oh1 * lane_f, axis=-1, keepdims=True)
    r1 = jnp.sum(oh1 * excl, axis=-1, keepdims=True)
    e2 = jnp.sum(oh2 * lane_f, axis=-1, keepdims=True)
    r2 = jnp.sum(oh2 * excl, axis=-1, keepdims=True)
    meta = jnp.where(lane == 0, e1, jnp.where(lane == 1, r1, jnp.where(lane == 2, e2,
                     jnp.where(lane == 3, r2, 0.0))))
    o_ref[...] = meta.T[0:SUBLANES, :]
    carry[...] = carry[...] + jnp.sum(sel, axis=0, keepdims=True)
    cnt_ref[...] = carry[...]


def _rank(route):
    t = route.shape[0]
    tm = min(t, 512)
    return pl.pallas_call(
        _rank_kernel,
        grid=(t // tm,),
        in_specs=[pl.BlockSpec((tm, 2 * LANES), lambda i: (i, 0))],
        out_specs=[pl.BlockSpec((SUBLANES, tm), lambda i: (0, i)),
                   pl.BlockSpec((SUBLANES, LANES), lambda i: (0, 0))],
        out_shape=[jax.ShapeDtypeStruct((SUBLANES, t), F32), jax.ShapeDtypeStruct((SUBLANES, LANES), F32)],
        scratch_shapes=[pltpu.VMEM((SUBLANES, LANES), F32)],
        compiler_params=_cparams(("arbitrary",)),
        name="rank",
    )(route)


def _sc_mesh():
    return plsc.VectorSubcoreMesh(core_axis_name="core", subcore_axis_name="subcore")


def _sc_scatter2(x, idx1, idx2, n_out):
    n, w = x.shape

    @functools.partial(pl.kernel, out_type=jax.ShapeDtypeStruct((n_out, w), x.dtype), mesh=_sc_mesh(),
                       scratch_types=[], name="sc_dispatch")
    def k(x_hbm, i1_hbm, i2_hbm, o_hbm):
        def body(x_vmem, i1_vmem, i2_vmem):
            pltpu.sync_copy(x_vmem, o_hbm.at[i1_vmem.at[0]])
            pltpu.sync_copy(x_vmem, o_hbm.at[i2_vmem.at[0]])

        pltpu.emit_pipeline(
            body, grid=(n // SC_WINDOW,),
            in_specs=[pl.BlockSpec((SC_WINDOW, w), lambda i: (i, 0)),
                      pl.BlockSpec((1, SC_WINDOW), lambda i: (0, i)),
                      pl.BlockSpec((1, SC_WINDOW), lambda i: (0, i))],
            out_specs=[], core_axis_name=("core", "subcore"),
            dimension_semantics=(pltpu.PARALLEL,))(x_hbm, i1_hbm, i2_hbm)

    return k(x, idx1.reshape(1, n), idx2.reshape(1, n))


def _sc_gather(y, idx):
    (n,) = idx.shape
    w = y.shape[1]

    @functools.partial(pl.kernel, out_type=jax.ShapeDtypeStruct((n, w), y.dtype), mesh=_sc_mesh(),
                       scratch_types=[], name="sc_combine")
    def k(y_hbm, i_hbm, o_hbm):
        def body(i_vmem, o_vmem):
            pltpu.sync_copy(y_hbm.at[i_vmem.at[0]], o_vmem)

        pltpu.emit_pipeline(
            body, grid=(n // SC_WINDOW,),
            in_specs=[pl.BlockSpec((1, SC_WINDOW), lambda i: (0, i))],
            out_specs=[pl.BlockSpec((SC_WINDOW, w), lambda i: (i, 0))],
            core_axis_name=("core", "subcore"),
            dimension_semantics=(pltpu.PARALLEL,))(i_hbm, o_hbm)

    return k(y, idx.reshape(1, n))


MOE_SUB = 256


def _moe_kernel(te_ref, nu_ref, tv_ref, h_ref, wg_ref, wu_ref, wd_ref, o_ref, acc_ref, hb_ref):
    m = pl.program_id(0)
    f = pl.program_id(1)
    tm = hb_ref.shape[0]
    half = hb_ref.shape[1] // 2

    @pl.when(m < nu_ref[0])
    def _():
        @pl.when(f == 0)
        def _():
            acc_ref[...] = jnp.zeros(acc_ref.shape, F32)
            routed = lax.broadcasted_iota(jnp.int32, (tm, PLANE_W), 0) < tv_ref[m]
            for c in range(PACK_PLANES):
                w = jnp.where(routed, h_ref[c], jnp.uint32(0))
                hb_ref[:, c * PLANE_W:(c + 1) * PLANE_W] = _unpack_hi(w).astype(BF16)
                hb_ref[:, half + c * PLANE_W:half + (c + 1) * PLANE_W] = _unpack_lo(w).astype(BF16)

        n_sub = (tv_ref[m] + MOE_SUB - 1) // MOE_SUB
        for k in range(1, tm // MOE_SUB + 1):
            @pl.when(n_sub == k)
            def _():
                _swiglu_accumulate(hb_ref, wg_ref, wu_ref, wd_ref, acc_ref, k * MOE_SUB)

        @pl.when(f == pl.num_programs(1) - 1)
        def _():
            packed = _pack_bf16_pairs(acc_ref[...])
            for c in range(PACK_PLANES):
                o_ref[c] = packed[:, c * PLANE_W:(c + 1) * PLANE_W]


MOE_TM = 1024


def _moe_routed(hp, tile_expert, n_used, tile_valid, wg, wu, wd, layer, tm):
    _, r, _ = hp.shape
    _, ne, d, ff = wg.shape
    wg = wg.reshape(-1, d, ff)
    wu = wu.reshape(-1, d, ff)
    wd = wd.reshape(-1, ff, d)
    e0 = layer * ne
    tf = 256
    nf = ff // tf

    def row_map(m, f, te, nu, tv):
        return (0, jnp.minimum(m, nu[0] - 1), 0)

    def fidx(m, f, nu):
        return jnp.where(m < nu[0], f, nf - 1)

    hspec = pl.BlockSpec((PACK_PLANES, tm, PLANE_W), row_map)
    return pl.pallas_call(
        _moe_kernel,
        grid_spec=pltpu.PrefetchScalarGridSpec(
            num_scalar_prefetch=3,
            grid=(r // tm, nf),
            in_specs=[
                hspec,
                pl.BlockSpec((1, d, tf), lambda m, f, te, nu, tv: (e0 + te[m], 0, fidx(m, f, nu))),
                pl.BlockSpec((1, d, tf), lambda m, f, te, nu, tv: (e0 + te[m], 0, fidx(m, f, nu))),
                pl.BlockSpec((1, tf, d), lambda m, f, te, nu, tv: (e0 + te[m], fidx(m, f, nu), 0)),
            ],
            out_specs=hspec,
            scratch_shapes=[pltpu.VMEM((tm, d), F32), pltpu.VMEM((tm, d), BF16)],
        ),
        out_shape=jax.ShapeDtypeStruct(hp.shape, jnp.uint32),
        compiler_params=_cparams(("arbitrary", "arbitrary")),
        name="moe",
    )(tile_expert, n_used, tile_valid, hp, wg, wu, wd)


def _route_plan(meta, counts, tm, n_tiles):
    e1, r1, e2, r2 = (meta[k].astype(jnp.int32) for k in range(4))
    cnt = counts[0, :N_EXPERTS].astype(jnp.int32)
    tiles_per = (cnt + tm - 1) // tm
    tile_end = jnp.cumsum(tiles_per)
    start_row = (tile_end - tiles_per) * tm
    pos1 = start_row[e1] + r1
    pos2 = start_row[e2] + r2
    n_used = tile_end[-1]
    tiles = jnp.arange(n_tiles, dtype=jnp.int32)
    tile_expert = jnp.sum((tiles[:, None] >= tile_end[None, :]).astype(jnp.int32), axis=1)
    last_expert = jnp.sum((n_used - 1 >= tile_end).astype(jnp.int32))
    tile_expert = jnp.minimum(tile_expert, last_expert)
    tile_start = (tile_end - tiles_per)[tile_expert]
    tile_valid = jnp.clip(cnt[tile_expert] - (tiles - tile_start) * tm, 0, tm)
    return pos1, pos2, tile_expert, n_used.reshape(1), tile_valid


def _plane_rows(pos, n_rows):
    return (jnp.arange(PACK_PLANES, dtype=jnp.int32)[:, None] * n_rows + pos[None, :]).reshape(-1)


def _moe(hp, route, wg, wu, wd, layer):
    _, t, _ = hp.shape
    tm = min(MOE_TM, t)
    n_tiles = 2 * t // tm + N_EXPERTS
    n_rows = n_tiles * tm
    meta, counts = _rank(route)
    pos1, pos2, tile_expert, n_used, tile_valid = _route_plan(meta, counts, tm, n_tiles)
    i1 = _plane_rows(pos1, n_rows)
    i2 = _plane_rows(pos2, n_rows)
    hs = _sc_scatter2(hp.reshape(PACK_PLANES * t, PLANE_W), i1, i2, PACK_PLANES * n_rows)
    ys = _moe_routed(hs.reshape(PACK_PLANES, n_rows, PLANE_W), tile_expert, n_used, tile_valid,
                     wg, wu, wd, layer, tm)
    ys = ys.reshape(PACK_PLANES * n_rows, PLANE_W)
    y1 = _sc_gather(ys, i1).reshape(PACK_PLANES, t, PLANE_W)
    y2 = _sc_gather(ys, i2).reshape(PACK_PLANES, t, PLANE_W)
    return y1, y2


def _final_kernel(x_ref, f_ref, g_ref, nw_ref, o_ref):
    xv = x_ref[0] + g_ref[0, 0] * f_ref[0].astype(F32)
    ms = jnp.mean(xv * xv, axis=-1, keepdims=True)
    o_ref[0] = xv * lax.rsqrt(ms + EPS) * nw_ref[...]


def _final(x, f, mods, nw):
    b, s, d = x.shape
    tm = min(s, 512)
    xspec = pl.BlockSpec((1, tm, d), lambda bi, m: (bi, m, 0))
    return pl.pallas_call(
        _final_kernel,
        grid=(b, s // tm),
        in_specs=[xspec, xspec,
                  pl.BlockSpec((1, 1, 1, d), lambda bi, m: (bi, 5, 0, 0)),
                  pl.BlockSpec((1, d), lambda bi, m: (0, 0))],
        out_specs=xspec,
        out_shape=jax.ShapeDtypeStruct((b, s, d), F32),
        compiler_params=_cparams(("arbitrary", "arbitrary")),
        name="final",
    )(x, f, mods, nw.reshape(1, d))


def _final_moe_kernel(x_ref, y1_ref, y2_ref, p_ref, g_ref, nw_ref, o_ref):
    tm, d = x_ref.shape[1], x_ref.shape[2]
    half = d // 2
    p = p_ref[0]
    p1 = jnp.concatenate([p[:, 0:LANES]] * (PLANE_W // LANES), axis=1)
    p2 = jnp.concatenate([p[:, LANES:2 * LANES]] * (PLANE_W // LANES), axis=1)
    ssq = jnp.zeros((tm, 1), F32)
    for c in range(PACK_PLANES):
        w1 = y1_ref[c]
        w2 = y2_ref[c]
        for unpack, off in ((_unpack_hi, 0), (_unpack_lo, half)):
            sl = slice(off + c * PLANE_W, off + (c + 1) * PLANE_W)
            f = p1 * unpack(w1) + p2 * unpack(w2)
            xv = x_ref[0, :, sl] + g_ref[0, 0, :, sl] * f
            o_ref[0, :, sl] = xv
            ssq = ssq + jnp.sum(xv * xv, axis=-1, keepdims=True)
    o_ref[0] = o_ref[0] * lax.rsqrt(ssq / d + EPS) * nw_ref[...]


def _final_moe(x, y1, y2, route, mods, nw):
    b, s, d = x.shape
    tm = min(s, 512)
    nm = s // tm
    xspec = pl.BlockSpec((1, tm, d), lambda bi, m: (bi, m, 0))
    yspec = pl.BlockSpec((PACK_PLANES, tm, PLANE_W), lambda bi, m: (0, bi * nm + m, 0))
    return pl.pallas_call(
        _final_moe_kernel,
        grid=(b, nm),
        in_specs=[xspec, yspec, yspec,
                  pl.BlockSpec((1, tm, 2 * LANES), lambda bi, m: (bi, m, 1)),
                  pl.BlockSpec((1, 1, 1, d), lambda bi, m: (bi, 5, 0, 0)),
                  pl.BlockSpec((1, d), lambda bi, m: (0, 0))],
        out_specs=xspec,
        out_shape=jax.ShapeDtypeStruct((b, s, d), F32),
        compiler_params=_cparams(("arbitrary", "arbitrary")),
        name="final",
    )(x, y1, y2, route, mods, nw.reshape(1, d))


PREP_W = 2 * LANES


def _w_in_plan():
    nb = GROUP_W // PREP_W
    plan = [(3 * nb + j, 0, 0) for j in range(4 * nb)]
    plan += [(j, 1, 0) for j in range(nb)]
    plan += [(nb + j, 0, 0) for j in range(2 * nb)]
    plan += [(7 * nb + j, 1, 0) for j in range(nb)]
    plan += [(8 * nb, 0, 1), (8 * nb, 0, 2)]
    assert len(plan) * PREP_W == D_PROJ
    return jnp.asarray(plan, jnp.int32).T.reshape(-1)


def _prep_w_kernel(plan_ref, w_ref, o_ref):
    j = pl.program_id(1)
    nblk = pl.num_programs(1)
    x = w_ref[0]
    mode = plan_ref[2 * nblk + j]
    src = jnp.where(mode == 2, x[:, LANES:], x[:, :LANES])
    lo = lax.broadcasted_iota(jnp.int32, src.shape, 1) < HEAD_DIM
    other = pltpu.roll(src, HEAD_DIM, 1)
    dup = jnp.concatenate([jnp.where(lo, src, other), jnp.where(lo, other, src)], axis=1)
    scale = jnp.where(plan_ref[nblk + j] == 1, HEAD_DIM ** -0.5, 1.0)
    o_ref[0] = (jnp.where(mode == 0, x, dup) * scale).astype(BF16)


def _prep_w_in(w_in):
    depth, d, _ = w_in.shape
    nblk = D_PROJ // PREP_W
    return pl.pallas_call(
        _prep_w_kernel,
        grid_spec=pltpu.PrefetchScalarGridSpec(
            num_scalar_prefetch=1,
            grid=(depth, nblk),
            in_specs=[pl.BlockSpec((1, d, PREP_W), lambda i, j, plan: (i, 0, plan[j]))],
            out_specs=pl.BlockSpec((1, d, PREP_W), lambda i, j, plan: (i, 0, j)),
        ),
        out_shape=jax.ShapeDtypeStruct((depth, d, D_PROJ), BF16),
        compiler_params=_cparams(("arbitrary", "arbitrary")),
        name="prep_w_in",
    )(_w_in_plan(), w_in)


def _rope_tables(n):
    t = jnp.arange(n, dtype=jnp.int32)
    rows = (t // GRID_W).astype(F32)
    cols = (t % GRID_W).astype(F32)
    nf = HEAD_DIM // 4
    inv = ROPE_BASE ** (-jnp.arange(nf, dtype=F32) / nf)
    ang = jnp.concatenate([rows[:, None] * inv, cols[:, None] * inv], axis=-1)
    cos = jnp.cos(ang)
    sin = jnp.sin(ang)
    cos_t = jnp.concatenate([cos, cos, cos, cos], axis=-1)
    sin_t = jnp.concatenate([-sin, sin, -sin, sin], axis=-1)
    return cos_t, sin_t


def kernel(x, c, ctx, c_ctx, w_ada, b_ada, norm_mix, norm_ffn, norm_final, w_in, w_out, na_rpb, pool_w, pool_scale, conv_w, swa_sink, ffn_w_gate, ffn_w_up, ffn_w_down, moe_router, moe_w_gate, moe_w_up, moe_w_down):
    b, s, d = x.shape
    lc = ctx.shape[1]
    depth = w_ada.shape[0]
    cond = jnp.concatenate([c, c_ctx[None, :]], axis=0)
    assert b == 2
    mods_all = _ada(cond, w_ada, b_ada)
    cos_t, sin_t = _rope_tables(s)
    cos_c = jnp.ones((lc, LANES), F32)
    sin_c = jnp.zeros((lc, LANES), F32)
    row_x = lambda bi: bi
    row_c = lambda bi: b

    w_proj = _prep_w_in(w_in)
    cx = ctx
    fx = fc = None
    for i in range(depth):
        last = i == depth - 1
        mods = mods_all[i].reshape(SUBLANES, 6, 1, d)
        wi = (w_proj, i)
        wo = w_out[i].astype(BF16)
        if i == 0:
            proj, _ = _inproj(x, wi, norm_mix[i], mods, row_x, cos_t, sin_t)
            projc, _ = _inproj(cx, wi, norm_mix[i], mods, row_c, cos_c, sin_c)
        else:
            mods_prev = mods_all[i - 1].reshape(SUBLANES, 6, 1, d)
            proj, x = _inproj_res(x, fx, mods_prev, wi, norm_mix[i], mods, row_x, cos_t, sin_t)
            projc, cx = _inproj_res(cx, fc, mods_prev, wi, norm_mix[i], mods, row_c, cos_c, sin_c)
        bias = _na_bias(na_rpb[i])
        o_a = _na_attn(proj, projc, bias)
        o_d = _swa_attn(proj, projc, swa_sink[i])
        moe = i % 2 == 1
        router = moe_router[i // 2] if moe else None
        outs = _outproj(o_a, o_d, proj, x, wo, pool_w[i], pool_scale[i], conv_w[i], norm_ffn[i], mods, row_x,
                        router)
        x, h2 = outs[0], outs[1]
        if not last:
            oc_a, oc_d = _ctx_attn(projc, swa_sink[i])
            cx, h2c = _outproj(oc_a, oc_d, projc, cx, wo, pool_w[i], pool_scale[i], conv_w[i], norm_ffn[i],
                               mods, row_c)
        j = i // 2
        if moe:
            if not last:
                raise NotImplementedError("an expert layer that is not the last layer")
            route = outs[2]
            y1, y2 = _moe(h2, route.reshape(b * s, 4 * LANES), moe_w_gate, moe_w_up, moe_w_down, j)
            return _final_moe(x, y1, y2, route, mods, norm_final)
        fx = _ffn(h2.reshape(b * s, d), ffn_w_gate, ffn_w_up, ffn_w_down, j).reshape(b, s, d)
        if not last:
            fc = _ffn(h2c.reshape(b * lc, d), ffn_w_gate, ffn_w_up, ffn_w_down, j).reshape(b, lc, d)
    return _final(x, fx, mods_all[depth - 1].reshape(SUBLANES, 6, 1, d), norm_final)


def _inproj_res(x, f, mods_prev, w, nw, mods, mod_row, cos_t, sin_t):
    both = jnp.concatenate([mods, mods_prev], axis=1)
    return _inproj(x, w, nw, both, mod_row, cos_t, sin_t, res=(f, 6 + 5))
```

```python
import functools

import jax
import jax.numpy as jnp
from jax import lax
from jax.experimental import pallas as pl
from jax.experimental.pallas import tpu as pltpu
from jax.experimental.pallas import tpu_sc as plsc

F32 = jnp.float32
BF16 = jnp.bfloat16

D_MODEL = 2048
GRID_W = 64
HEAD_DIM = 64
EPS = 1e-6
NEG_INF = -1e30
GROUP_W = 512
NA_KH = 8
NA_KW = 16
POOL_WINDOWS = (2, 4, 8, 16)
POOL_CH = 128
SWA_WINDOW = 128
SWA_BLOCK = 128
ROPE_BASE = 10000.0
N_EXPERTS = 8
LANES = 128
SUBLANES = 8
HALO = 16

C_UBC = 0
C_QA = 2048
C_KA = 2560
C_VA = 3072
C_QD = 3584
C_KD = 4096
C_VD = 4352
D_PROJ = 4608
TN_PROJ = 512

VMEM_LIMIT = 56 * 1024 * 1024


def _cparams(sem):
    return pltpu.CompilerParams(dimension_semantics=sem, vmem_limit_bytes=VMEM_LIMIT)


def _dot(a, b):
    return jnp.dot(a, b, preferred_element_type=F32)


def _dot_t(a, b):
    return lax.dot_general(a, b, (((1,), (1,)), ((), ())), preferred_element_type=F32)


def _rms_mod(xv, nw, sh, sc):
    ms = jnp.mean(xv * xv, axis=-1, keepdims=True)
    return (xv * lax.rsqrt(ms + EPS)) * (nw * (1.0 + sc)) + sh


PACK_PLANES = 4
PLANE_W = D_MODEL // 2 // PACK_PLANES
SC_WINDOW = 128


def _pack_bf16_pairs(v):
    half = v.shape[1] // 2
    hi = lax.bitcast_convert_type(v[:, :half].astype(BF16).astype(F32), jnp.uint32)
    lo = lax.bitcast_convert_type(v[:, half:].astype(BF16).astype(F32), jnp.uint32)
    return hi | (lo >> 16)


def _unpack_hi(w):
    return lax.bitcast_convert_type(w & jnp.uint32(0xFFFF0000), F32)


def _unpack_lo(w):
    return lax.bitcast_convert_type(w << 16, F32)


def _ada_kernel(c_ref, w_ref, b_ref, o_ref, silu_scr):
    tn = w_ref.shape[2]

    @pl.when((pl.program_id(0) == 0) & (pl.program_id(1) == 0))
    def _():
        cv = c_ref[...]
        silu_scr[...] = cv * (1.0 / (1.0 + jnp.exp(-cv)))

    o_ref[...] = jnp.zeros(o_ref.shape, F32)
    for r in range(3):
        m = silu_scr[r]
        cols = []
        for j in range(tn // LANES):
            wj = w_ref[0, :, j * LANES:(j + 1) * LANES]
            cols.append(jnp.sum(wj * m, axis=0, keepdims=True))
        o_ref[0, r:r + 1, :] = jnp.concatenate(cols, axis=1) + b_ref[0]


def _ada(cond, w_ada, b_ada):
    depth, d, n6 = w_ada.shape
    tn = 1024
    cb = jnp.broadcast_to(cond[:, :, None], (3, d, LANES))
    return pl.pallas_call(
        _ada_kernel,
        grid=(depth, n6 // tn),
        in_specs=[
            pl.BlockSpec((3, d, LANES), lambda i, j: (0, 0, 0)),
            pl.BlockSpec((1, d, tn), lambda i, j: (i, 0, j)),
            pl.BlockSpec((1, 1, tn), lambda i, j: (i, 0, j)),
        ],
        out_specs=pl.BlockSpec((1, SUBLANES, tn), lambda i, j: (i, 0, j)),
        out_shape=jax.ShapeDtypeStruct((depth, SUBLANES, n6), F32),
        scratch_shapes=[pltpu.VMEM((3, d, LANES), F32)],
        compiler_params=_cparams(("arbitrary", "arbitrary")),
        name="ada",
    )(cb, w_ada, b_ada.reshape(depth, 1, n6))


def _rope(a, cosv, sinv, lo32):
    sw = jnp.where(lo32, pltpu.roll(a, 96, 1), pltpu.roll(a, 32, 1))
    return a * cosv + sw * sinv


def _inproj_kernel(*refs, has_res):
    if has_res:
        x_ref, f_ref, g_ref, nw_ref, sh_ref, sc_ref, cos_ref, sin_ref, w_ref, o_ref, x2_ref = refs
    else:
        x_ref, nw_ref, sh_ref, sc_ref, cos_ref, sin_ref, w_ref, o_ref = refs
    xv = x_ref[0]
    if has_res:
        xv = xv + g_ref[0, 0] * f_ref[0].astype(F32)
        x2_ref[0] = xv
    h = _rms_mod(xv, nw_ref[...], sh_ref[0, 0], sc_ref[0, 0]).astype(BF16)
    tm = h.shape[0]
    lane = lax.broadcasted_iota(jnp.int32, (tm, LANES), 1)
    lo32 = (lane % HEAD_DIM) < (HEAD_DIM // 2)
    rope_end = C_VD
    for n in range(D_PROJ // TN_PROJ):
        c0 = n * TN_PROJ
        acc = _dot(h, w_ref[0, :, c0:c0 + TN_PROJ])
        if c0 + TN_PROJ <= C_QD:
            o_ref[0, :, c0:c0 + TN_PROJ] = acc.astype(BF16)
            continue
        for g in range(TN_PROJ // LANES):
            a = acc[:, g * LANES:(g + 1) * LANES]
            if c0 + g * LANES < rope_end:
                a = _rope(a, cos_ref[...], sin_ref[...], lo32)
            o_ref[0, :, c0 + g * LANES:c0 + (g + 1) * LANES] = a.astype(BF16)


def _inproj(x, w, nw, mods, mod_row, cos_t, sin_t, res=None):
    b, s, d = x.shape
    w_all, layer = w
    tm = min(s, 512)
    has_res = res is not None
    row = mod_row
    xspec = pl.BlockSpec((1, tm, d), lambda bi, m: (bi, m, 0))

    def mspec(k):
        return pl.BlockSpec((1, 1, 1, d), lambda bi, m: (row(bi), k, 0, 0))

    in_specs = [xspec]
    args = [x]
    if has_res:
        f, gk = res
        in_specs += [xspec, mspec(gk)]
        args += [f, mods]
    in_specs += [
        pl.BlockSpec((1, d), lambda bi, m: (0, 0)),
        mspec(0), mspec(1),
        pl.BlockSpec((tm, LANES), lambda bi, m: (m, 0)),
        pl.BlockSpec((tm, LANES), lambda bi, m: (m, 0)),
        pl.BlockSpec((1, d, D_PROJ), lambda bi, m: (layer, 0, 0), pipeline_mode=pl.Buffered(1)),
    ]
    args += [nw.reshape(1, d), mods, mods, cos_t, sin_t, w_all]
    out_shape = [jax.ShapeDtypeStruct((b, s, D_PROJ), BF16)]
    out_specs = [pl.BlockSpec((1, tm, D_PROJ), lambda bi, m: (bi, m, 0))]
    if has_res:
        out_shape.append(jax.ShapeDtypeStruct((b, s, d), F32))
        out_specs.append(xspec)
    outs = pl.pallas_call(
        functools.partial(_inproj_kernel, has_res=has_res),
        grid=(b, s // tm),
        in_specs=in_specs,
        out_specs=out_specs,
        out_shape=out_shape,
        compiler_params=_cparams(("arbitrary", "arbitrary")),
        name="inproj",
    )(*args)
    return outs if has_res else (outs[0], None)


def _na_bias_kernel(rpb_ref, o_ref):
    h = pl.program_id(0)
    nd = 2 * NA_KH - 1
    nj = 2 * NA_KW - 1
    q = lax.broadcasted_iota(jnp.int32, (GRID_W, LANES), 0)
    lane = lax.broadcasted_iota(jnp.int32, (GRID_W, LANES), 1)
    kw = lane % GRID_W
    dc = jnp.clip(kw - q, -(NA_KW - 1), NA_KW - 1) + (NA_KW - 1)
    c0 = jnp.clip(q - NA_KW // 2, 0, GRID_W - NA_KW)
    valid = (kw >= c0) & (kw < c0 + NA_KW)
    tabs = []
    for d in range(nd):
        t = jnp.zeros((GRID_W, LANES), F32)
        for j in range(nj):
            t = jnp.where(dc == j, rpb_ref[h * (nd * nj) + d * nj + j], t)
        tabs.append(jnp.where(valid, t, NEG_INF))
    for c in range(NA_KH):
        for g in range(NA_KH // 2):
            d_lo = 2 * g - c + (NA_KH - 1)
            o_ref[c, 0, :, g * LANES:(g + 1) * LANES] = jnp.where(lane < GRID_W, tabs[d_lo], tabs[d_lo + 1])


def _na_bias(rpb):
    nh = rpb.shape[0]
    return pl.pallas_call(
        _na_bias_kernel,
        grid=(nh,),
        in_specs=[pl.BlockSpec(memory_space=pltpu.SMEM)],
        out_specs=pl.BlockSpec((NA_KH, 1, GRID_W, NA_KH * GRID_W), lambda h: (0, h, 0, 0)),
        out_shape=jax.ShapeDtypeStruct((NA_KH, nh, GRID_W, NA_KH * GRID_W), F32),
        compiler_params=_cparams(("arbitrary",)),
        name="na_bias",
    )(rpb.reshape(-1))


def _softmax_pv_staged(chains):
    add = lambda a, b: a + b
    ms = []
    for sp, _, extra in chains:
        m = functools.reduce(jnp.maximum, [jnp.max(s, axis=-1, keepdims=True) for s in sp])
        ms.append(m if extra is None else jnp.maximum(m, extra))
    ps = [[jnp.exp(s - m) for s in sp] for (sp, _, _), m in zip(chains, ms)]
    dens = []
    for pp, (_, _, extra), m in zip(ps, chains, ms):
        den = functools.reduce(add, [jnp.sum(p, axis=-1, keepdims=True) for p in pp])
        dens.append(den if extra is None else den + jnp.exp(extra - m))
    outs = [functools.reduce(add, [_dot(p.astype(BF16), v) for p, v in zip(pp, vp)])
            for pp, (_, vp, _) in zip(ps, chains)]
    return [o / d for o, d in zip(outs, dens)]


ROWS_PER_STEP = 16
NA_ROW_GROUP = 4


def _na_kernel(q_ref, k_ref, v_ref, kc_ref, vc_ref, bias_ref, o_ref, *, rows):
    rb = pl.program_id(2)
    lo = lax.broadcasted_iota(jnp.int32, (NA_ROW_GROUP * GRID_W, LANES), 1) < HEAD_DIM
    lo_row = lax.broadcasted_iota(jnp.int32, (GRID_W, LANES), 1) < HEAD_DIM
    kc = kc_ref[0]
    vc = vc_ref[0]
    kh = min(NA_KH, rows)
    win = kh * GRID_W

    g = NA_ROW_GROUP
    gq = g * GRID_W
    zero = jnp.zeros((gq, LANES), BF16)
    for i0 in range(0, ROWS_PER_STEP, g):
        q = q_ref[0, i0 * GRID_W:(i0 + g) * GRID_W, :]
        q2 = jnp.concatenate([jnp.where(lo, q, zero), jnp.where(lo, zero, q)], axis=0)
        s_ctx = _dot_t(q2, kc)
        pieces = [(hh, j, slice(hh * gq + j * GRID_W, hh * gq + (j + 1) * GRID_W))
                  for hh in range(2) for j in range(g)]
        vws, s_loc = [], {}
        for j in range(g):
            r = rb * ROWS_PER_STEP + i0 + j
            r0 = jnp.clip(r - kh // 2, 0, rows - kh)
            c = r - r0
            koff = pl.multiple_of(r0 * GRID_W, GRID_W)
            vws.append(v_ref[0, pl.ds(koff, win), :])
            qp = jnp.concatenate([q2[sl] for hh, jj, sl in pieces if jj == j], axis=0)
            sp = _dot_t(qp, k_ref[0, pl.ds(koff, win), :])
            for hh in range(2):
                s_loc[hh, j] = sp[hh * GRID_W:(hh + 1) * GRID_W] + bias_ref[c, hh]
        ms = {(hh, j): jnp.maximum(jnp.max(s_loc[hh, j], axis=-1, keepdims=True),
                                   jnp.max(s_ctx[sl], axis=-1, keepdims=True)) for hh, j, sl in pieces}
        p_loc = {(hh, j): jnp.exp(s_loc[hh, j] - ms[hh, j]) for hh, j, _ in pieces}
        p_ctx = {(hh, j): jnp.exp(s_ctx[sl] - ms[hh, j]) for hh, j, sl in pieces}
        den = {k: jnp.sum(p_loc[k], axis=-1, keepdims=True) + jnp.sum(p_ctx[k], axis=-1, keepdims=True)
               for k in p_loc}
        pv_ctx = _dot(jnp.concatenate([p_ctx[hh, j].astype(BF16) for hh, j, _ in pieces], axis=0), vc)
        for j in range(g):
            pv = _dot(jnp.concatenate([p_loc[0, j].astype(BF16), p_loc[1, j].astype(BF16)], axis=0), vws[j])
            o = [(pv[hh * GRID_W:(hh + 1) * GRID_W] + pv_ctx[hh * gq + j * GRID_W:hh * gq + (j + 1) * GRID_W])
                 / den[hh, j] for hh in range(2)]
            i = i0 + j
            o_ref[0, i * GRID_W:(i + 1) * GRID_W, :] = jnp.where(lo_row, o[0], o[1]).astype(BF16)


def _na_attn(proj, projc, bias):
    b, s, _ = proj.shape
    lc = projc.shape[1]
    rows = s // GRID_W
    assert rows >= NA_KH and rows % ROWS_PER_STEP == 0
    tq = ROWS_PER_STEP * GRID_W
    npair = GROUP_W // LANES
    return pl.pallas_call(
        functools.partial(_na_kernel, rows=rows),
        grid=(b, npair, rows // ROWS_PER_STEP),
        in_specs=[
            pl.BlockSpec((1, tq, LANES), lambda bi, p, r: (bi, r, C_QA // LANES + p)),
            pl.BlockSpec((1, s, LANES), lambda bi, p, r: (bi, 0, C_KA // LANES + p)),
            pl.BlockSpec((1, s, LANES), lambda bi, p, r: (bi, 0, C_VA // LANES + p)),
            pl.BlockSpec((1, lc, LANES), lambda bi, p, r: (bi, 0, C_KA // LANES + p)),
            pl.BlockSpec((1, lc, LANES), lambda bi, p, r: (bi, 0, C_VA // LANES + p)),
            pl.BlockSpec((NA_KH, 2, GRID_W, NA_KH * GRID_W), lambda bi, p, r: (0, p, 0, 0)),
        ],
        out_specs=pl.BlockSpec((1, tq, LANES), lambda bi, p, r: (bi, r, p)),
        out_shape=jax.ShapeDtypeStruct((b, s, GROUP_W), BF16),
        compiler_params=_cparams(("arbitrary", "arbitrary", "arbitrary")),
        name="na_attn",
    )(proj, proj, proj, projc, projc, bias)


def _gqa_chains(q, kv, sink_ref, k_parts, v_parts, mask):
    m = q.shape[0]
    lane = lax.broadcasted_iota(jnp.int32, (m, LANES), 1)
    lo = lane < HEAD_DIM
    qs = []
    for j in range(2):
        qg = q[:, (kv * 2 + j) * LANES:(kv * 2 + j + 1) * LANES]
        qs.append(jnp.where(lo, qg, jnp.zeros_like(qg)))
        qs.append(jnp.where(lo, jnp.zeros_like(qg), qg))
    qq = jnp.concatenate(qs, axis=0)
    s_all = [_dot_t(qq, k) for k in k_parts]
    chains = []
    for i in range(4):
        parts = [s[i * m:(i + 1) * m] for s in s_all]
        if mask is not None:
            parts[0] = jnp.where(mask, parts[0], NEG_INF)
        chains.append((parts, v_parts, jnp.full((m, 1), sink_ref[kv * 4 + i], F32)))
    return chains


def _gqa_store(o_ref, heads, row0=0):
    m = heads[0].shape[0]
    lane = lax.broadcasted_iota(jnp.int32, heads[0].shape, 1)
    lo = lane < HEAD_DIM
    for g in range(len(heads) // 2):
        o_ref[0, row0:row0 + m, g * LANES:(g + 1) * LANES] = (
            jnp.where(lo, heads[2 * g], heads[2 * g + 1]).astype(BF16))


SWA_BLOCKS_PER_STEP = 2


def _swa_kernel(sink_ref, q_ref, k_ref, v_ref, kc_ref, vc_ref, o_ref):
    nb = pl.num_programs(1) * SWA_BLOCKS_PER_STEP
    kc = kc_ref[0]
    vc = vc_ref[0]
    qi = lax.broadcasted_iota(jnp.int32, (SWA_BLOCK, 3 * SWA_BLOCK), 0)
    kj = lax.broadcasted_iota(jnp.int32, (SWA_BLOCK, 3 * SWA_BLOCK), 1)
    for blk in range(SWA_BLOCKS_PER_STEP):
        n = pl.program_id(1) * SWA_BLOCKS_PER_STEP + blk
        start = jnp.clip(n - 1, 0, nb - 3)
        koff = pl.multiple_of(start * SWA_BLOCK, SWA_BLOCK)
        kw = k_ref[0, pl.ds(koff, 3 * SWA_BLOCK), :]
        vw = v_ref[0, pl.ds(koff, 3 * SWA_BLOCK), :]
        valid = jnp.abs((n - start) * SWA_BLOCK + qi - kj) <= SWA_WINDOW
        q = q_ref[0, blk * SWA_BLOCK:(blk + 1) * SWA_BLOCK, :]
        chains = []
        for kv in range(2):
            sl = slice(kv * LANES, (kv + 1) * LANES)
            chains += _gqa_chains(q, kv, sink_ref, [kw[:, sl], kc[:, sl]], [vw[:, sl], vc[:, sl]], valid)
        _gqa_store(o_ref, _softmax_pv_staged(chains), blk * SWA_BLOCK)


def _swa_attn(proj, projc, sink):
    b, s, _ = proj.shape
    lc = projc.shape[1]
    nb = s // SWA_BLOCK
    assert nb >= 3 and nb % SWA_BLOCKS_PER_STEP == 0
    w2 = 2 * LANES
    tq = SWA_BLOCKS_PER_STEP * SWA_BLOCK
    return pl.pallas_call(
        _swa_kernel,
        grid=(b, nb // SWA_BLOCKS_PER_STEP),
        in_specs=[
            pl.BlockSpec(memory_space=pltpu.SMEM),
            pl.BlockSpec((1, tq, GROUP_W), lambda bi, n: (bi, n, C_QD // GROUP_W)),
            pl.BlockSpec((1, s, w2), lambda bi, n: (bi, 0, C_KD // w2)),
            pl.BlockSpec((1, s, w2), lambda bi, n: (bi, 0, C_VD // w2)),
            pl.BlockSpec((1, lc, w2), lambda bi, n: (bi, 0, C_KD // w2)),
            pl.BlockSpec((1, lc, w2), lambda bi, n: (bi, 0, C_VD // w2)),
        ],
        out_specs=pl.BlockSpec((1, tq, GROUP_W), lambda bi, n: (bi, n, 0)),
        out_shape=jax.ShapeDtypeStruct((b, s, GROUP_W), BF16),
        compiler_params=_cparams(("arbitrary", "arbitrary")),
        name="swa_attn",
    )(sink.reshape(-1), proj, proj, proj, projc, projc)


def _ctx_attn_kernel(sink_ref, qa_ref, ka_ref, va_ref, qd_ref, kd_ref, vd_ref, oa_ref, od_ref):
    m = qa_ref.shape[1]
    lane = lax.broadcasted_iota(jnp.int32, (m, LANES), 1)
    lo = lane < HEAD_DIM
    chains = []
    for p in range(GROUP_W // LANES):
        sl = slice(p * LANES, (p + 1) * LANES)
        q = qa_ref[0, :, sl]
        k = ka_ref[0, :, sl]
        v = va_ref[0, :, sl]
        for hh in range(2):
            qm = jnp.where(lo if hh == 0 else jnp.logical_not(lo), q, jnp.zeros_like(q))
            chains.append(([_dot_t(qm, k)], [v], None))
    _gqa_store(oa_ref, _softmax_pv_staged(chains))
    q = qd_ref[0]
    chains = []
    for kv in range(2):
        sl = slice(kv * LANES, (kv + 1) * LANES)
        chains += _gqa_chains(q, kv, sink_ref, [kd_ref[0, :, sl]], [vd_ref[0, :, sl]], None)
    _gqa_store(od_ref, _softmax_pv_staged(chains))


def _ctx_attn(projc, sink):
    b, lc, _ = projc.shape
    w2 = 2 * LANES

    def spec(width, col):
        return pl.BlockSpec((1, lc, width), lambda bi: (bi, 0, col // width))

    ospec = pl.BlockSpec((1, lc, GROUP_W), lambda bi: (bi, 0, 0))
    return pl.pallas_call(
        _ctx_attn_kernel,
        grid=(b,),
        in_specs=[pl.BlockSpec(memory_space=pltpu.SMEM),
                  spec(GROUP_W, C_QA), spec(GROUP_W, C_KA), spec(GROUP_W, C_VA),
                  spec(GROUP_W, C_QD), spec(w2, C_KD), spec(w2, C_VD)],
        out_specs=[ospec, ospec],
        out_shape=[jax.ShapeDtypeStruct((b, lc, GROUP_W), BF16)] * 2,
        compiler_params=_cparams(("arbitrary",)),
        name="ctx_attn",
    )(sink.reshape(-1), projc, projc, projc, projc, projc, projc)


OUTPROJ_ROW_GROUPS = 2


def _outproj_kernel(*refs, tm, moe):
    (oa_ref, od_ref, u_ref, up_ref, un_ref, pw_ref, ps_ref, cw_ref, wo_ref, x_ref, g1_ref,
     nw_ref, sh_ref, sc_ref) = refs[:14]
    rest = refs[14:]
    if moe:
        rt_ref, xo_ref, h_ref, gate_ref, ext_ub, ext_u = rest
    else:
        xo_ref, h_ref, ext_ub, ext_u = rest
    m = pl.program_id(1)
    nm = pl.num_programs(1)
    g4 = GROUP_W

    def cols(ref, k):
        return ref[0, :, k * g4:(k + 1) * g4].astype(F32)

    ub = cols(u_ref, 0)
    has_prev = m > 0
    has_next = m < nm - 1
    ext_ub[0:HALO] = jnp.where(has_prev, cols(up_ref, 0), 0.0)
    ext_ub[HALO:HALO + tm] = ub
    ext_ub[HALO + tm:] = jnp.where(has_next, cols(un_ref, 0), 0.0)
    ext_u[0:HALO] = jnp.where(has_prev, cols(up_ref, 2) * cols(up_ref, 3), 0.0)
    ext_u[HALO:HALO + tm] = cols(u_ref, 2) * cols(u_ref, 3)
    ext_u[HALO + tm:] = jnp.where(has_next, cols(un_ref, 2) * cols(un_ref, 3), 0.0)

    n_tok = nm * tm
    rg = tm // OUTPROJ_ROW_GROUPS
    groups = [(i * rg, (i + 1) * rg) for i in range(OUTPROJ_ROW_GROUPS)]

    def mixers(r0, r1):
        t = m * tm + r0 + lax.broadcasted_iota(jnp.int32, (rg, LANES), 0)
        obs = []
        for g, w in enumerate(POOL_WINDOWS):
            sl = slice(g * LANES, (g + 1) * LANES)
            acc = None
            for d in range(-(w // 2), w - w // 2):
                term = ext_ub[HALO + r0 + d:HALO + r1 + d, sl]
                acc = term if acc is None else acc + term
            cnt = (jnp.clip(t + (w - w // 2), 0, n_tok) - jnp.clip(t - w // 2, 0, n_tok)).astype(F32)
            pooled = acc / cnt - ext_ub[HALO + r0:HALO + r1, sl]
            obs.append(_dot(pooled.astype(BF16), pw_ref[g]) * ps_ref[:, sl])
        y = (ext_u[HALO + r0 - 1:HALO + r1 - 1] * cw_ref[0:1, :] + ext_u[HALO + r0:HALO + r1] * cw_ref[1:2, :]
             + ext_u[HALO + r0 + 1:HALO + r1 + 1] * cw_ref[2:3, :])
        o_c = u_ref[0, r0:r1, g4:2 * g4].astype(F32) * y
        return jnp.concatenate([oa_ref[0, r0:r1, :]] + [o.astype(BF16) for o in obs]
                               + [o_c.astype(BF16), od_ref[0, r0:r1, :]], axis=1)

    def norm(r0, r1, yy):
        xn = x_ref[0, r0:r1, :] + g1_ref[0, 0] * yy
        xo_ref[0, r0:r1, :] = xn
        return _rms_mod(xn, nw_ref[...], sh_ref[0, 0], sc_ref[0, 0])

    def route(r0, r1, h2):
        packed = _pack_bf16_pairs(h2)
        for c in range(PACK_PLANES):
            h_ref[c, r0:r1, :] = packed[:, c * PLANE_W:(c + 1) * PLANE_W]
        lane = lax.broadcasted_iota(jnp.int32, (rg, LANES), 1)
        h_hi = h2.astype(BF16)
        h_lo = (h2 - h_hi.astype(F32)).astype(BF16)
        logits = _dot(h_hi, rt_ref[0]) + (_dot(h_lo, rt_ref[0]) + _dot(h_hi, rt_ref[1]))
        logits = jnp.where(lane < N_EXPERTS, logits, -jnp.inf)
        v1 = jnp.max(logits, axis=-1, keepdims=True)
        i1 = jnp.min(jnp.where(logits == v1, lane, LANES), axis=-1, keepdims=True)
        l2 = jnp.where(lane == i1, -jnp.inf, logits)
        v2 = jnp.max(l2, axis=-1, keepdims=True)
        i2 = jnp.min(jnp.where(l2 == v2, lane, LANES), axis=-1, keepdims=True)
        e2 = jnp.exp(v2 - v1)
        p1 = 1.0 / (1.0 + e2)
        p2 = e2 / (1.0 + e2)
        gate_ref[0, r0:r1, 0:LANES] = jnp.where(lane == i1, 1.0, 0.0)
        gate_ref[0, r0:r1, LANES:2 * LANES] = jnp.where(lane == i2, 1.0, 0.0)
        gate_ref[0, r0:r1, 2 * LANES:3 * LANES] = jnp.broadcast_to(p1, (rg, LANES))
        gate_ref[0, r0:r1, 3 * LANES:4 * LANES] = jnp.broadcast_to(p2, (rg, LANES))

    lhs = [mixers(r0, r1) for r0, r1 in groups]
    yys = [_dot(a, wo_ref[...]) for a in lhs]
    h2s = [norm(r0, r1, yy) for (r0, r1), yy in zip(groups, yys)]
    for (r0, r1), h2 in zip(groups, h2s):
        if moe:
            route(r0, r1, h2)
        else:
            h_ref[0, r0:r1, :] = h2.astype(BF16)


def _outproj(o_a, o_d, proj, x, wo, pool_w, pool_scale, conv_w, nw, mods, mod_row, router=None):
    b, s, d = x.shape
    tm = min(s, 512)
    moe = router is not None
    row = mod_row
    nh = s // HALO
    th = tm // HALO

    def mspec(k):
        return pl.BlockSpec((1, 1, 1, d), lambda bi, m: (row(bi), k, 0, 0))

    gspec = pl.BlockSpec((1, tm, GROUP_W), lambda bi, m: (bi, m, 0))
    xspec = pl.BlockSpec((1, tm, d), lambda bi, m: (bi, m, 0))
    in_specs = [
        gspec, gspec,
        pl.BlockSpec((1, tm, 4 * GROUP_W), lambda bi, m: (bi, m, 0)),
        pl.BlockSpec((1, HALO, 4 * GROUP_W), lambda bi, m: (bi, jnp.maximum(m * th - 1, 0), 0)),
        pl.BlockSpec((1, HALO, 4 * GROUP_W), lambda bi, m: (bi, jnp.minimum((m + 1) * th, nh - 1), 0)),
        pl.BlockSpec((4, POOL_CH, POOL_CH), lambda bi, m: (0, 0, 0)),
        pl.BlockSpec((1, GROUP_W), lambda bi, m: (0, 0)),
        pl.BlockSpec((3, GROUP_W), lambda bi, m: (0, 0)),
        pl.BlockSpec((d, d), lambda bi, m: (0, 0), pipeline_mode=pl.Buffered(1)),
        xspec, mspec(2),
        pl.BlockSpec((1, d), lambda bi, m: (0, 0)),
        mspec(3), mspec(4),
    ]
    args = [o_a, o_d, proj, proj, proj, pool_w.astype(BF16), pool_scale.reshape(1, GROUP_W), conv_w, wo,
            x, mods, nw.reshape(1, d), mods, mods]
    if moe:
        nm = s // tm
        in_specs.append(pl.BlockSpec((2, d, LANES), lambda bi, m: (0, 0, 0)))
        r_pad = jnp.pad(router, ((0, 0), (0, LANES - router.shape[1])))
        r_hi = r_pad.astype(BF16)
        args.append(jnp.stack([r_hi, (r_pad - r_hi.astype(F32)).astype(BF16)]))
        out_shape = [jax.ShapeDtypeStruct((b, s, d), F32),
                     jax.ShapeDtypeStruct((PACK_PLANES, b * s, PLANE_W), jnp.uint32),
                     jax.ShapeDtypeStruct((b, s, 4 * LANES), F32)]
        out_specs = [xspec,
                     pl.BlockSpec((PACK_PLANES, tm, PLANE_W), lambda bi, m: (0, bi * nm + m, 0)),
                     pl.BlockSpec((1, tm, 4 * LANES), lambda bi, m: (bi, m, 0))]
    else:
        out_shape = [jax.ShapeDtypeStruct((b, s, d), F32), jax.ShapeDtypeStruct((b, s, d), BF16)]
        out_specs = [xspec, xspec]
    return pl.pallas_call(
        functools.partial(_outproj_kernel, tm=tm, moe=moe),
        grid=(b, s // tm),
        in_specs=in_specs,
        out_specs=out_specs,
        out_shape=out_shape,
        scratch_shapes=[pltpu.VMEM((tm + 2 * HALO, GROUP_W), F32), pltpu.VMEM((tm + 2 * HALO, GROUP_W), F32)],
        compiler_params=_cparams(("arbitrary", "arbitrary")),
        name="outproj",
    )(*args)


def _swiglu_accumulate(h_ref, wg_ref, wu_ref, wd_ref, acc_ref, rows=None):
    r = slice(0, h_ref.shape[0] if rows is None else rows)
    h = h_ref[r, :]
    a = _dot(h, wg_ref[0].astype(BF16))
    u = _dot(h, wu_ref[0].astype(BF16))
    act = a * (1.0 / (1.0 + jnp.exp(-a))) * u
    acc_ref[r, :] += _dot(act.astype(BF16), wd_ref[0].astype(BF16))


def _ffn_kernel(h_ref, wg_ref, wu_ref, wd_ref, o_ref, acc_ref):
    f = pl.program_id(1)

    @pl.when(f == 0)
    def _():
        acc_ref[...] = jnp.zeros(acc_ref.shape, F32)

    _swiglu_accumulate(h_ref, wg_ref, wu_ref, wd_ref, acc_ref)

    @pl.when(f == pl.num_programs(1) - 1)
    def _():
        o_ref[...] = acc_ref[...].astype(BF16)


def _ffn(h, wg, wu, wd, layer):
    mt, d = h.shape
    ff = wg.shape[2]
    tm = min(mt, 1024)
    tf = 256
    return pl.pallas_call(
        _ffn_kernel,
        grid=(mt // tm, ff // tf),
        in_specs=[
            pl.BlockSpec((tm, d), lambda m, f: (m, 0)),
            pl.BlockSpec((1, d, tf), lambda m, f: (layer, 0, f)),
            pl.BlockSpec((1, d, tf), lambda m, f: (layer, 0, f)),
            pl.BlockSpec((1, tf, d), lambda m, f: (layer, f, 0)),
        ],
        out_specs=pl.BlockSpec((tm, d), lambda m, f: (m, 0)),
        out_shape=jax.ShapeDtypeStruct((mt, d), BF16),
        scratch_shapes=[pltpu.VMEM((tm, d), F32)],
        compiler_params=_cparams(("arbitrary", "arbitrary")),
        name="ffn",
    )(h, wg, wu, wd)


def _rank_kernel(r_ref, o_ref, cnt_ref, carry):
    i = pl.program_id(0)
    tm = r_ref.shape[0]

    @pl.when(i == 0)
    def _():
        carry[...] = jnp.zeros(carry.shape, F32)

    oh1 = r_ref[:, 0:LANES]
    oh2 = r_ref[:, LANES:2 * LANES]
    sel = oh1 + oh2
    row = lax.broadcasted_iota(jnp.int32, (tm, tm), 0)
    col = lax.broadcasted_iota(jnp.int32, (tm, tm), 1)
    tri = jnp.where(col < row, 1.0, 0.0).astype(BF16)
    excl = _dot(tri, sel.astype(BF16)) + carry[0:1, :]
    lane = lax.broadcasted_iota(jnp.int32, (tm, LANES), 1)
    lane_f = lane.astype(F32)
    e1 = jnp.sum(oh1 * lane_f, axis=-1, keepdims=True)
    r1 = jnp.sum(oh1 * excl, axis=-1, keepdims=True)
    e2 = jnp.sum(oh2 * lane_f, axis=-1, keepdims=True)
    r2 = jnp.sum(oh2 * excl, axis=-1, keepdims=True)
    meta = jnp.where(lane == 0, e1, jnp.where(lane == 1, r1, jnp.where(lane == 2, e2,
                     jnp.where(lane == 3, r2, 0.0))))
    o_ref[...] = meta.T[0:SUBLANES, :]
    carry[...] = carry[...] + jnp.sum(sel, axis=0, keepdims=True)
    cnt_ref[...] = carry[...]


def _rank(route):
    t = route.shape[0]
    tm = min(t, 512)
    return pl.pallas_call(
        _rank_kernel,
        grid=(t // tm,),
        in_specs=[pl.BlockSpec((tm, 2 * LANES), lambda i: (i, 0))],
        out_specs=[pl.BlockSpec((SUBLANES, tm), lambda i: (0, i)),
                   pl.BlockSpec((SUBLANES, LANES), lambda i: (0, 0))],
        out_shape=[jax.ShapeDtypeStruct((SUBLANES, t), F32), jax.ShapeDtypeStruct((SUBLANES, LANES), F32)],
        scratch_shapes=[pltpu.VMEM((SUBLANES, LANES), F32)],
        compiler_params=_cparams(("arbitrary",)),
        name="rank",
    )(route)


def _sc_mesh():
    return plsc.VectorSubcoreMesh(core_axis_name="core", subcore_axis_name="subcore")


def _sc_scatter2(x, idx1, idx2, n_out):
    n, w = x.shape

    @functools.partial(pl.kernel, out_type=jax.ShapeDtypeStruct((n_out, w), x.dtype), mesh=_sc_mesh(),
                       scratch_types=[], name="sc_dispatch")
    def k(x_hbm, i1_hbm, i2_hbm, o_hbm):
        def body(x_vmem, i1_vmem, i2_vmem):
            pltpu.sync_copy(x_vmem, o_hbm.at[i1_vmem.at[0]])
            pltpu.sync_copy(x_vmem, o_hbm.at[i2_vmem.at[0]])

        pltpu.emit_pipeline(
            body, grid=(n // SC_WINDOW,),
            in_specs=[pl.BlockSpec((SC_WINDOW, w), lambda i: (i, 0)),
                      pl.BlockSpec((1, SC_WINDOW), lambda i: (0, i)),
                      pl.BlockSpec((1, SC_WINDOW), lambda i: (0, i))],
            out_specs=[], core_axis_name=("core", "subcore"),
            dimension_semantics=(pltpu.PARALLEL,))(x_hbm, i1_hbm, i2_hbm)

    return k(x, idx1.reshape(1, n), idx2.reshape(1, n))


def _sc_gather(y, idx):
    (n,) = idx.shape
    w = y.shape[1]

    @functools.partial(pl.kernel, out_type=jax.ShapeDtypeStruct((n, w), y.dtype), mesh=_sc_mesh(),
                       scratch_types=[], name="sc_combine")
    def k(y_hbm, i_hbm, o_hbm):
        def body(i_vmem, o_vmem):
            pltpu.sync_copy(y_hbm.at[i_vmem.at[0]], o_vmem)

        pltpu.emit_pipeline(
            body, grid=(n // SC_WINDOW,),
            in_specs=[pl.BlockSpec((1, SC_WINDOW), lambda i: (0, i))],
            out_specs=[pl.BlockSpec((SC_WINDOW, w), lambda i: (i, 0))],
            core_axis_name=("core", "subcore"),
            dimension_semantics=(pltpu.PARALLEL,))(i_hbm, o_hbm)

    return k(y, idx.reshape(1, n))


MOE_SUB = 256


def _moe_kernel(te_ref, nu_ref, tv_ref, h_ref, wg_ref, wu_ref, wd_ref, o_ref, acc_ref, hb_ref):
    m = pl.program_id(0)
    f = pl.program_id(1)
    tm = hb_ref.shape[0]
    half = hb_ref.shape[1] // 2

    @pl.when(m < nu_ref[0])
    def _():
        @pl.when(f == 0)
        def _():
            acc_ref[...] = jnp.zeros(acc_ref.shape, F32)
            routed = lax.broadcasted_iota(jnp.int32, (tm, PLANE_W), 0) < tv_ref[m]
            for c in range(PACK_PLANES):
                w = jnp.where(routed, h_ref[c], jnp.uint32(0))
                hb_ref[:, c * PLANE_W:(c + 1) * PLANE_W] = _unpack_hi(w).astype(BF16)
                hb_ref[:, half + c * PLANE_W:half + (c + 1) * PLANE_W] = _unpack_lo(w).astype(BF16)

        n_sub = (tv_ref[m] + MOE_SUB - 1) // MOE_SUB
        for k in range(1, tm // MOE_SUB + 1):
            @pl.when(n_sub == k)
            def _():
                _swiglu_accumulate(hb_ref, wg_ref, wu_ref, wd_ref, acc_ref, k * MOE_SUB)

        @pl.when(f == pl.num_programs(1) - 1)
        def _():
            packed = _pack_bf16_pairs(acc_ref[...])
            for c in range(PACK_PLANES):
                o_ref[c] = packed[:, c * PLANE_W:(c + 1) * PLANE_W]


MOE_TM = 1024


def _moe_routed(hp, tile_expert, n_used, tile_valid, wg, wu, wd, layer, tm):
    _, r, _ = hp.shape
    _, ne, d, ff = wg.shape
    wg = wg.reshape(-1, d, ff)
    wu = wu.reshape(-1, d, ff)
    wd = wd.reshape(-1, ff, d)
    e0 = layer * ne
    tf = 256
    nf = ff // tf

    def row_map(m, f, te, nu, tv):
        return (0, jnp.minimum(m, nu[0] - 1), 0)

    def fidx(m, f, nu):
        return jnp.where(m < nu[0], f, nf - 1)

    hspec = pl.BlockSpec((PACK_PLANES, tm, PLANE_W), row_map)
    return pl.pallas_call(
        _moe_kernel,
        grid_spec=pltpu.PrefetchScalarGridSpec(
            num_scalar_prefetch=3,
            grid=(r // tm, nf),
            in_specs=[
                hspec,
                pl.BlockSpec((1, d, tf), lambda m, f, te, nu, tv: (e0 + te[m], 0, fidx(m, f, nu))),
                pl.BlockSpec((1, d, tf), lambda m, f, te, nu, tv: (e0 + te[m], 0, fidx(m, f, nu))),
                pl.BlockSpec((1, tf, d), lambda m, f, te, nu, tv: (e0 + te[m], fidx(m, f, nu), 0)),
            ],
            out_specs=hspec,
            scratch_shapes=[pltpu.VMEM((tm, d), F32), pltpu.VMEM((tm, d), BF16)],
        ),
        out_shape=jax.ShapeDtypeStruct(hp.shape, jnp.uint32),
        compiler_params=_cparams(("arbitrary", "arbitrary")),
        name="moe",
    )(tile_expert, n_used, tile_valid, hp, wg, wu, wd)


def _route_plan(meta, counts, tm, n_tiles):
    e1, r1, e2, r2 = (meta[k].astype(jnp.int32) for k in range(4))
    cnt = counts[0, :N_EXPERTS].astype(jnp.int32)
    tiles_per = (cnt + tm - 1) // tm
    tile_end = jnp.cumsum(tiles_per)
    start_row = (tile_end - tiles_per) * tm
    pos1 = start_row[e1] + r1
    pos2 = start_row[e2] + r2
    n_used = tile_end[-1]
    tiles = jnp.arange(n_tiles, dtype=jnp.int32)
    tile_expert = jnp.sum((tiles[:, None] >= tile_end[None, :]).astype(jnp.int32), axis=1)
    last_expert = jnp.sum((n_used - 1 >= tile_end).astype(jnp.int32))
    tile_expert = jnp.minimum(tile_expert, last_expert)
    tile_start = (tile_end - tiles_per)[tile_expert]
    tile_valid = jnp.clip(cnt[tile_expert] - (tiles - tile_start) * tm, 0, tm)
    return pos1, pos2, tile_expert, n_used.reshape(1), tile_valid


def _plane_rows(pos, n_rows):
    return (jnp.arange(PACK_PLANES, dtype=jnp.int32)[:, None] * n_rows + pos[None, :]).reshape(-1)


def _moe(hp, route, wg, wu, wd, layer):
    _, t, _ = hp.shape
    tm = min(MOE_TM, t)
    n_tiles = 2 * t // tm + N_EXPERTS
    n_rows = n_tiles * tm
    meta, counts = _rank(route)
    pos1, pos2, tile_expert, n_used, tile_valid = _route_plan(meta, counts, tm, n_tiles)
    i1 = _plane_rows(pos1, n_rows)
    i2 = _plane_rows(pos2, n_rows)
    hs = _sc_scatter2(hp.reshape(PACK_PLANES * t, PLANE_W), i1, i2, PACK_PLANES * n_rows)
    ys = _moe_routed(hs.reshape(PACK_PLANES, n_rows, PLANE_W), tile_expert, n_used, tile_valid,
                     wg, wu, wd, layer, tm)
    ys = ys.reshape(PACK_PLANES * n_rows, PLANE_W)
    y1 = _sc_gather(ys, i1).reshape(PACK_PLANES, t, PLANE_W)
    y2 = _sc_gather(ys, i2).reshape(PACK_PLANES, t, PLANE_W)
    return y1, y2


def _final_kernel(x_ref, f_ref, g_ref, nw_ref, o_ref):
    xv = x_ref[0] + g_ref[0, 0] * f_ref[0].astype(F32)
    ms = jnp.mean(xv * xv, axis=-1, keepdims=True)
    o_ref[0] = xv * lax.rsqrt(ms + EPS) * nw_ref[...]


def _final(x, f, mods, nw):
    b, s, d = x.shape
    tm = min(s, 512)
    xspec = pl.BlockSpec((1, tm, d), lambda bi, m: (bi, m, 0))
    return pl.pallas_call(
        _final_kernel,
        grid=(b, s // tm),
        in_specs=[xspec, xspec,
                  pl.BlockSpec((1, 1, 1, d), lambda bi, m: (bi, 5, 0, 0)),
                  pl.BlockSpec((1, d), lambda bi, m: (0, 0))],
        out_specs=xspec,
        out_shape=jax.ShapeDtypeStruct((b, s, d), F32),
        compiler_params=_cparams(("arbitrary", "arbitrary")),
        name="final",
    )(x, f, mods, nw.reshape(1, d))


def _final_moe_kernel(x_ref, y1_ref, y2_ref, p_ref, g_ref, nw_ref, o_ref):
    tm, d = x_ref.shape[1], x_ref.shape[2]
    half = d // 2
    p = p_ref[0]
    p1 = jnp.concatenate([p[:, 0:LANES]] * (PLANE_W // LANES), axis=1)
    p2 = jnp.concatenate([p[:, LANES:2 * LANES]] * (PLANE_W // LANES), axis=1)
    ssq = jnp.zeros((tm, 1), F32)
    for c in range(PACK_PLANES):
        w1 = y1_ref[c]
        w2 = y2_ref[c]
        for unpack, off in ((_unpack_hi, 0), (_unpack_lo, half)):
            sl = slice(off + c * PLANE_W, off + (c + 1) * PLANE_W)
            f = p1 * unpack(w1) + p2 * unpack(w2)
            xv = x_ref[0, :, sl] + g_ref[0, 0, :, sl] * f
            o_ref[0, :, sl] = xv
            ssq = ssq + jnp.sum(xv * xv, axis=-1, keepdims=True)
    o_ref[0] = o_ref[0] * lax.rsqrt(ssq / d + EPS) * nw_ref[...]


def _final_moe(x, y1, y2, route, mods, nw):
    b, s, d = x.shape
    tm = min(s, 512)
    nm = s // tm
    xspec = pl.BlockSpec((1, tm, d), lambda bi, m: (bi, m, 0))
    yspec = pl.BlockSpec((PACK_PLANES, tm, PLANE_W), lambda bi, m: (0, bi * nm + m, 0))
    return pl.pallas_call(
        _final_moe_kernel,
        grid=(b, nm),
        in_specs=[xspec, yspec, yspec,
                  pl.BlockSpec((1, tm, 2 * LANES), lambda bi, m: (bi, m, 1)),
                  pl.BlockSpec((1, 1, 1, d), lambda bi, m: (bi, 5, 0, 0)),
                  pl.BlockSpec((1, d), lambda bi, m: (0, 0))],
        out_specs=xspec,
        out_shape=jax.ShapeDtypeStruct((b, s, d), F32),
        compiler_params=_cparams(("arbitrary", "arbitrary")),
        name="final",
    )(x, y1, y2, route, mods, nw.reshape(1, d))


PREP_W = 2 * LANES


def _w_in_plan():
    nb = GROUP_W // PREP_W
    plan = [(3 * nb + j, 0, 0) for j in range(4 * nb)]
    plan += [(j, 1, 0) for j in range(nb)]
    plan += [(nb + j, 0, 0) for j in range(2 * nb)]
    plan += [(7 * nb + j, 1, 0) for j in range(nb)]
    plan += [(8 * nb, 0, 1), (8 * nb, 0, 2)]
    assert len(plan) * PREP_W == D_PROJ
    return jnp.asarray(plan, jnp.int32).T.reshape(-1)


def _prep_w_kernel(plan_ref, w_ref, o_ref):
    j = pl.program_id(1)
    nblk = pl.num_programs(1)
    x = w_ref[0]
    mode = plan_ref[2 * nblk + j]
    src = jnp.where(mode == 2, x[:, LANES:], x[:, :LANES])
    lo = lax.broadcasted_iota(jnp.int32, src.shape, 1) < HEAD_DIM
    other = pltpu.roll(src, HEAD_DIM, 1)
    dup = jnp.concatenate([jnp.where(lo, src, other), jnp.where(lo, other, src)], axis=1)
    scale = jnp.where(plan_ref[nblk + j] == 1, HEAD_DIM ** -0.5, 1.0)
    o_ref[0] = (jnp.where(mode == 0, x, dup) * scale).astype(BF16)


def _prep_w_in(w_in):
    depth, d, _ = w_in.shape
    nblk = D_PROJ // PREP_W
    return pl.pallas_call(
        _prep_w_kernel,
        grid_spec=pltpu.PrefetchScalarGridSpec(
            num_scalar_prefetch=1,
            grid=(depth, nblk),
            in_specs=[pl.BlockSpec((1, d, PREP_W), lambda i, j, plan: (i, 0, plan[j]))],
            out_specs=pl.BlockSpec((1, d, PREP_W), lambda i, j, plan: (i, 0, j)),
        ),
        out_shape=jax.ShapeDtypeStruct((depth, d, D_PROJ), BF16),
        compiler_params=_cparams(("arbitrary", "arbitrary")),
        name="prep_w_in",
    )(_w_in_plan(), w_in)


def _rope_tables(n):
    t = jnp.arange(n, dtype=jnp.int32)
    rows = (t // GRID_W).astype(F32)
    cols = (t % GRID_W).astype(F32)
    nf = HEAD_DIM // 4
    inv = ROPE_BASE ** (-jnp.arange(nf, dtype=F32) / nf)
    ang = jnp.concatenate([rows[:, None] * inv, cols[:, None] * inv], axis=-1)
    cos = jnp.cos(ang)
    sin = jnp.sin(ang)
    cos_t = jnp.concatenate([cos, cos, cos, cos], axis=-1)
    sin_t = jnp.concatenate([-sin, sin, -sin, sin], axis=-1)
    return cos_t, sin_t


def kernel(x, c, ctx, c_ctx, w_ada, b_ada, norm_mix, norm_ffn, norm_final, w_in, w_out, na_rpb, pool_w, pool_scale, conv_w, swa_sink, ffn_w_gate, ffn_w_up, ffn_w_down, moe_router, moe_w_gate, moe_w_up, moe_w_down):
    b, s, d = x.shape
    lc = ctx.shape[1]
    depth = w_ada.shape[0]
    cond = jnp.concatenate([c, c_ctx[None, :]], axis=0)
    assert b == 2
    mods_all = _ada(cond, w_ada, b_ada)
    cos_t, sin_t = _rope_tables(s)
    cos_c = jnp.ones((lc, LANES), F32)
    sin_c = jnp.zeros((lc, LANES), F32)
    row_x = lambda bi: bi
    row_c = lambda bi: b

    w_proj = _prep_w_in(w_in)
    cx = ctx
    fx = fc = None
    for i in range(depth):
        last = i == depth - 1
        mods = mods_all[i].reshape(SUBLANES, 6, 1, d)
        wi = (w_proj, i)
        wo = w_out[i].astype(BF16)
        if i == 0:
            proj, _ = _inproj(x, wi, norm_mix[i], mods, row_x, cos_t, sin_t)
            projc, _ = _inproj(cx, wi, norm_mix[i], mods, row_c, cos_c, sin_c)
        else:
            mods_prev = mods_all[i - 1].reshape(SUBLANES, 6, 1, d)
            proj, x = _inproj_res(x, fx, mods_prev, wi, norm_mix[i], mods, row_x, cos_t, sin_t)
            projc, cx = _inproj_res(cx, fc, mods_prev, wi, norm_mix[i], mods, row_c, cos_c, sin_c)
        bias = _na_bias(na_rpb[i])
        o_a = _na_attn(proj, projc, bias)
        o_d = _swa_attn(proj, projc, swa_sink[i])
        moe = i % 2 == 1
        router = moe_router[i // 2] if moe else None
        outs = _outproj(o_a, o_d, proj, x, wo, pool_w[i], pool_scale[i], conv_w[i], norm_ffn[i], mods, row_x,
                        router)
        x, h2 = outs[0], outs[1]
        if not last:
            oc_a, oc_d = _ctx_attn(projc, swa_sink[i])
            cx, h2c = _outproj(oc_a, oc_d, projc, cx, wo, pool_w[i], pool_scale[i], conv_w[i], norm_ffn[i],
                               mods, row_c)
        j = i // 2
        if moe:
            if not last:
                raise NotImplementedError("an expert layer that is not the last layer")
            route = outs[2]
            y1, y2 = _moe(h2, route.reshape(b * s, 4 * LANES), moe_w_gate, moe_w_up, moe_w_down, j)
            return _final_moe(x, y1, y2, route, mods, norm_final)
        fx = _ffn(h2.reshape(b * s, d), ffn_w_gate, ffn_w_up, ffn_w_down, j).reshape(b, s, d)
        if not last:
            fc = _ffn(h2c.reshape(b * lc, d), ffn_w_gate, ffn_w_up, ffn_w_down, j).reshape(b, lc, d)
    return _final(x, fx, mods_all[depth - 1].reshape(SUBLANES, 6, 1, d), norm_final)


def _inproj_res(x, f, mods_prev, w, nw, mods, mod_row, cos_t, sin_t):
    both = jnp.concatenate([mods, mods_prev], axis=1)
    return _inproj(x, w, nw, both, mod_row, cos_t, sin_t, res=(f, 6 + 5))
```

```python
import functools

import jax
import jax.numpy as jnp
from jax import lax
from jax.experimental import pallas as pl
from jax.experimental.pallas import tpu as pltpu
from jax.experimental.pallas import tpu_sc as plsc

F32 = jnp.float32
BF16 = jnp.bfloat16

D_MODEL = 2048
GRID_W = 64
HEAD_DIM = 64
EPS = 1e-6
NEG_INF = -1e30
GROUP_W = 512
NA_KH = 8
NA_KW = 16
POOL_WINDOWS = (2, 4, 8, 16)
POOL_CH = 128
SWA_WINDOW = 128
SWA_BLOCK = 128
ROPE_BASE = 10000.0
N_EXPERTS = 8
LANES = 128
SUBLANES = 8
HALO = 16

C_UBC = 0
C_QA = 2048
C_KA = 2560
C_VA = 3072
C_QD = 3584
C_KD = 4096
C_VD = 4352
D_PROJ = 4608
TN_PROJ = 512

VMEM_LIMIT = 56 * 1024 * 1024


def _cparams(sem):
    return pltpu.CompilerParams(dimension_semantics=sem, vmem_limit_bytes=VMEM_LIMIT)


def _dot(a, b):
    return jnp.dot(a, b, preferred_element_type=F32)


def _dot_t(a, b):
    return lax.dot_general(a, b, (((1,), (1,)), ((), ())), preferred_element_type=F32)


def _rms_mod(xv, nw, sh, sc):
    ms = jnp.mean(xv * xv, axis=-1, keepdims=True)
    return (xv * lax.rsqrt(ms + EPS)) * (nw * (1.0 + sc)) + sh


PACK_PLANES = 4
PLANE_W = D_MODEL // 2 // PACK_PLANES
SC_WINDOW = 128


def _pack_bf16_pairs(v):
    half = v.shape[1] // 2
    hi = lax.bitcast_convert_type(v[:, :half].astype(BF16).astype(F32), jnp.uint32)
    lo = lax.bitcast_convert_type(v[:, half:].astype(BF16).astype(F32), jnp.uint32)
    return hi | (lo >> 16)


def _unpack_hi(w):
    return lax.bitcast_convert_type(w & jnp.uint32(0xFFFF0000), F32)


def _unpack_lo(w):
    return lax.bitcast_convert_type(w << 16, F32)


def _ada_kernel(c_ref, w_ref, b_ref, o_ref, silu_scr):
    tn = w_ref.shape[2]

    @pl.when((pl.program_id(0) == 0) & (pl.program_id(1) == 0))
    def _():
        cv = c_ref[...]
        silu_scr[...] = cv * (1.0 / (1.0 + jnp.exp(-cv)))

    o_ref[...] = jnp.zeros(o_ref.shape, F32)
    for r in range(3):
        m = silu_scr[r]
        cols = []
        for j in range(tn // LANES):
            wj = w_ref[0, :, j * LANES:(j + 1) * LANES]
            cols.append(jnp.sum(wj * m, axis=0, keepdims=True))
        o_ref[0, r:r + 1, :] = jnp.concatenate(cols, axis=1) + b_ref[0]


def _ada(cond, w_ada, b_ada):
    depth, d, n6 = w_ada.shape
    tn = 2048
    cb = jnp.broadcast_to(cond[:, :, None], (3, d, LANES))
    return pl.pallas_call(
        _ada_kernel,
        grid=(depth, n6 // tn),
        in_specs=[
            pl.BlockSpec((3, d, LANES), lambda i, j: (0, 0, 0)),
            pl.BlockSpec((1, d, tn), lambda i, j: (i, 0, j)),
            pl.BlockSpec((1, 1, tn), lambda i, j: (i, 0, j)),
        ],
        out_specs=pl.BlockSpec((1, SUBLANES, tn), lambda i, j: (i, 0, j)),
        out_shape=jax.ShapeDtypeStruct((depth, SUBLANES, n6), F32),
        scratch_shapes=[pltpu.VMEM((3, d, LANES), F32)],
        compiler_params=_cparams(("arbitrary", "arbitrary")),
        name="ada",
    )(cb, w_ada, b_ada.reshape(depth, 1, n6))


def _rope(a, cosv, sinv, lo32):
    sw = jnp.where(lo32, pltpu.roll(a, 96, 1), pltpu.roll(a, 32, 1))
    return a * cosv + sw * sinv


def _inproj_kernel(*refs, has_res):
    if has_res:
        x_ref, f_ref, g_ref, nw_ref, sh_ref, sc_ref, cos_ref, sin_ref, w_ref, o_ref, x2_ref = refs
    else:
        x_ref, nw_ref, sh_ref, sc_ref, cos_ref, sin_ref, w_ref, o_ref = refs
    xv = x_ref[0]
    if has_res:
        xv = xv + g_ref[0, 0] * f_ref[0].astype(F32)
        x2_ref[0] = xv
    h = _rms_mod(xv, nw_ref[...], sh_ref[0, 0], sc_ref[0, 0]).astype(BF16)
    tm = h.shape[0]
    lane = lax.broadcasted_iota(jnp.int32, (tm, LANES), 1)
    lo32 = (lane % HEAD_DIM) < (HEAD_DIM // 2)
    rope_end = C_VD
    for n in range(D_PROJ // TN_PROJ):
        c0 = n * TN_PROJ
        acc = _dot(h, w_ref[0, :, c0:c0 + TN_PROJ])
        if c0 + TN_PROJ <= C_QD:
            o_ref[0, :, c0:c0 + TN_PROJ] = acc.astype(BF16)
            continue
        for g in range(TN_PROJ // LANES):
            a = acc[:, g * LANES:(g + 1) * LANES]
            if c0 + g * LANES < rope_end:
                a = _rope(a, cos_ref[...], sin_ref[...], lo32)
            o_ref[0, :, c0 + g * LANES:c0 + (g + 1) * LANES] = a.astype(BF16)


def _inproj(x, w, nw, mods, mod_row, cos_t, sin_t, res=None):
    b, s, d = x.shape
    w_all, layer = w
    tm = min(s, 512)
    has_res = res is not None
    row = mod_row
    xspec = pl.BlockSpec((1, tm, d), lambda bi, m: (bi, m, 0))

    def mspec(k):
        return pl.BlockSpec((1, 1, 1, d), lambda bi, m: (row(bi), k, 0, 0))

    in_specs = [xspec]
    args = [x]
    if has_res:
        f, gk = res
        in_specs += [xspec, mspec(gk)]
        args += [f, mods]
    in_specs += [
        pl.BlockSpec((1, d), lambda bi, m: (0, 0)),
        mspec(0), mspec(1),
        pl.BlockSpec((tm, LANES), lambda bi, m: (m, 0)),
        pl.BlockSpec((tm, LANES), lambda bi, m: (m, 0)),
        pl.BlockSpec((1, d, D_PROJ), lambda bi, m: (layer, 0, 0), pipeline_mode=pl.Buffered(1)),
    ]
    args += [nw.reshape(1, d), mods, mods, cos_t, sin_t, w_all]
    out_shape = [jax.ShapeDtypeStruct((b, s, D_PROJ), BF16)]
    out_specs = [pl.BlockSpec((1, tm, D_PROJ), lambda bi, m: (bi, m, 0))]
    if has_res:
        out_shape.append(jax.ShapeDtypeStruct((b, s, d), F32))
        out_specs.append(xspec)
    outs = pl.pallas_call(
        functools.partial(_inproj_kernel, has_res=has_res),
        grid=(b, s // tm),
        in_specs=in_specs,
        out_specs=out_specs,
        out_shape=out_shape,
        compiler_params=_cparams(("arbitrary", "arbitrary")),
        name="inproj",
    )(*args)
    return outs if has_res else (outs[0], None)


def _na_bias_kernel(rpb_ref, o_ref):
    h = pl.program_id(0)
    nd = 2 * NA_KH - 1
    nj = 2 * NA_KW - 1
    q = lax.broadcasted_iota(jnp.int32, (GRID_W, LANES), 0)
    lane = lax.broadcasted_iota(jnp.int32, (GRID_W, LANES), 1)
    kw = lane % GRID_W
    dc = jnp.clip(kw - q, -(NA_KW - 1), NA_KW - 1) + (NA_KW - 1)
    c0 = jnp.clip(q - NA_KW // 2, 0, GRID_W - NA_KW)
    valid = (kw >= c0) & (kw < c0 + NA_KW)
    tabs = []
    for d in range(nd):
        t = jnp.zeros((GRID_W, LANES), F32)
        for j in range(nj):
            t = jnp.where(dc == j, rpb_ref[h * (nd * nj) + d * nj + j], t)
        tabs.append(jnp.where(valid, t, NEG_INF))
    for c in range(NA_KH):
        for g in range(NA_KH // 2):
            d_lo = 2 * g - c + (NA_KH - 1)
            o_ref[c, 0, :, g * LANES:(g + 1) * LANES] = jnp.where(lane < GRID_W, tabs[d_lo], tabs[d_lo + 1])


def _na_bias(rpb):
    nh = rpb.shape[0]
    return pl.pallas_call(
        _na_bias_kernel,
        grid=(nh,),
        in_specs=[pl.BlockSpec(memory_space=pltpu.SMEM)],
        out_specs=pl.BlockSpec((NA_KH, 1, GRID_W, NA_KH * GRID_W), lambda h: (0, h, 0, 0)),
        out_shape=jax.ShapeDtypeStruct((NA_KH, nh, GRID_W, NA_KH * GRID_W), F32),
        compiler_params=_cparams(("arbitrary",)),
        name="na_bias",
    )(rpb.reshape(-1))


def _softmax_pv_staged(chains):
    add = lambda a, b: a + b
    ms = []
    for sp, _, extra in chains:
        m = functools.reduce(jnp.maximum, [jnp.max(s, axis=-1, keepdims=True) for s in sp])
        ms.append(m if extra is None else jnp.maximum(m, extra))
    ps = [[jnp.exp(s - m) for s in sp] for (sp, _, _), m in zip(chains, ms)]
    dens = []
    for pp, (_, _, extra), m in zip(ps, chains, ms):
        den = functools.reduce(add, [jnp.sum(p, axis=-1, keepdims=True) for p in pp])
        dens.append(den if extra is None else den + jnp.exp(extra - m))
    outs = [functools.reduce(add, [_dot(p.astype(BF16), v) for p, v in zip(pp, vp)])
            for pp, (_, vp, _) in zip(ps, chains)]
    return [o / d for o, d in zip(outs, dens)]


NA_MAX_ROWS_PER_STEP = 32
NA_ROW_GROUP = 4


def _na_kernel(q_ref, k_ref, v_ref, kc_ref, vc_ref, bias_ref, o_ref, *, rows):
    rb = pl.program_id(2)
    rows_per_step = q_ref.shape[1] // GRID_W
    lo = lax.broadcasted_iota(jnp.int32, (NA_ROW_GROUP * GRID_W, LANES), 1) < HEAD_DIM
    lo_row = lax.broadcasted_iota(jnp.int32, (GRID_W, LANES), 1) < HEAD_DIM
    kc = kc_ref[0]
    vc = vc_ref[0]
    kh = min(NA_KH, rows)
    win = kh * GRID_W

    g = NA_ROW_GROUP
    gq = g * GRID_W
    zero = jnp.zeros((gq, LANES), BF16)
    for i0 in range(0, rows_per_step, g):
        q = q_ref[0, i0 * GRID_W:(i0 + g) * GRID_W, :]
        q2 = jnp.concatenate([jnp.where(lo, q, zero), jnp.where(lo, zero, q)], axis=0)
        s_ctx = _dot_t(q2, kc)
        pieces = [(hh, j, slice(hh * gq + j * GRID_W, hh * gq + (j + 1) * GRID_W))
                  for hh in range(2) for j in range(g)]
        vws, s_loc = [], {}
        for j in range(g):
            r = rb * rows_per_step + i0 + j
            r0 = jnp.clip(r - kh // 2, 0, rows - kh)
            c = r - r0
            koff = pl.multiple_of(r0 * GRID_W, GRID_W)
            vws.append(v_ref[0, pl.ds(koff, win), :])
            qp = jnp.concatenate([q2[sl] for hh, jj, sl in pieces if jj == j], axis=0)
            sp = _dot_t(qp, k_ref[0, pl.ds(koff, win), :])
            for hh in range(2):
                s_loc[hh, j] = sp[hh * GRID_W:(hh + 1) * GRID_W] + bias_ref[c, hh]
        ms = {(hh, j): jnp.maximum(jnp.max(s_loc[hh, j], axis=-1, keepdims=True),
                                   jnp.max(s_ctx[sl], axis=-1, keepdims=True)) for hh, j, sl in pieces}
        p_loc = {(hh, j): jnp.exp(s_loc[hh, j] - ms[hh, j]) for hh, j, _ in pieces}
        p_ctx = {(hh, j): jnp.exp(s_ctx[sl] - ms[hh, j]) for hh, j, sl in pieces}
        den = {k: jnp.sum(p_loc[k], axis=-1, keepdims=True) + jnp.sum(p_ctx[k], axis=-1, keepdims=True)
               for k in p_loc}
        pv_ctx = _dot(jnp.concatenate([p_ctx[hh, j].astype(BF16) for hh, j, _ in pieces], axis=0), vc)
        for j in range(g):
            pv = _dot(jnp.concatenate([p_loc[0, j].astype(BF16), p_loc[1, j].astype(BF16)], axis=0), vws[j])
            o = [(pv[hh * GRID_W:(hh + 1) * GRID_W] + pv_ctx[hh * gq + j * GRID_W:hh * gq + (j + 1) * GRID_W])
                 / den[hh, j] for hh in range(2)]
            i = i0 + j
            o_ref[0, i * GRID_W:(i + 1) * GRID_W, :] = jnp.where(lo_row, o[0], o[1]).astype(BF16)


def _na_attn(proj, projc, bias):
    b, s, _ = proj.shape
    lc = projc.shape[1]
    rows = s // GRID_W
    rows_per_step = min(NA_MAX_ROWS_PER_STEP, rows)
    assert rows >= NA_KH and rows % rows_per_step == 0 and rows_per_step % NA_ROW_GROUP == 0
    tq = rows_per_step * GRID_W
    npair = GROUP_W // LANES
    return pl.pallas_call(
        functools.partial(_na_kernel, rows=rows),
        grid=(b, npair, rows // rows_per_step),
        in_specs=[
            pl.BlockSpec((1, tq, LANES), lambda bi, p, r: (bi, r, C_QA // LANES + p)),
            pl.BlockSpec((1, s, LANES), lambda bi, p, r: (bi, 0, C_KA // LANES + p)),
            pl.BlockSpec((1, s, LANES), lambda bi, p, r: (bi, 0, C_VA // LANES + p)),
            pl.BlockSpec((1, lc, LANES), lambda bi, p, r: (bi, 0, C_KA // LANES + p)),
            pl.BlockSpec((1, lc, LANES), lambda bi, p, r: (bi, 0, C_VA // LANES + p)),
            pl.BlockSpec((NA_KH, 2, GRID_W, NA_KH * GRID_W), lambda bi, p, r: (0, p, 0, 0)),
        ],
        out_specs=pl.BlockSpec((1, tq, LANES), lambda bi, p, r: (bi, r, p)),
        out_shape=jax.ShapeDtypeStruct((b, s, GROUP_W), BF16),
        compiler_params=_cparams(("arbitrary", "arbitrary", "arbitrary")),
        name="na_attn",
    )(proj, proj, proj, projc, projc, bias)


def _gqa_chains(q, kv, sink_ref, k_parts, v_parts, mask):
    m = q.shape[0]
    lane = lax.broadcasted_iota(jnp.int32, (m, LANES), 1)
    lo = lane < HEAD_DIM
    qs = []
    for j in range(2):
        qg = q[:, (kv * 2 + j) * LANES:(kv * 2 + j + 1) * LANES]
        qs.append(jnp.where(lo, qg, jnp.zeros_like(qg)))
        qs.append(jnp.where(lo, jnp.zeros_like(qg), qg))
    qq = jnp.concatenate(qs, axis=0)
    s_all = [_dot_t(qq, k) for k in k_parts]
    chains = []
    for i in range(4):
        parts = [s[i * m:(i + 1) * m] for s in s_all]
        if mask is not None:
            parts[0] = jnp.where(mask, parts[0], NEG_INF)
        chains.append((parts, v_parts, jnp.full((m, 1), sink_ref[kv * 4 + i], F32)))
    return chains


def _gqa_store(o_ref, heads, row0=0):
    m = heads[0].shape[0]
    lane = lax.broadcasted_iota(jnp.int32, heads[0].shape, 1)
    lo = lane < HEAD_DIM
    for g in range(len(heads) // 2):
        o_ref[0, row0:row0 + m, g * LANES:(g + 1) * LANES] = (
            jnp.where(lo, heads[2 * g], heads[2 * g + 1]).astype(BF16))


SWA_BLOCKS_PER_STEP = 4


def _swa_kernel(sink_ref, q_ref, k_ref, v_ref, kc_ref, vc_ref, o_ref):
    nb = pl.num_programs(1) * SWA_BLOCKS_PER_STEP
    kc = kc_ref[0]
    vc = vc_ref[0]
    qi = lax.broadcasted_iota(jnp.int32, (SWA_BLOCK, 3 * SWA_BLOCK), 0)
    kj = lax.broadcasted_iota(jnp.int32, (SWA_BLOCK, 3 * SWA_BLOCK), 1)
    for blk in range(SWA_BLOCKS_PER_STEP):
        n = pl.program_id(1) * SWA_BLOCKS_PER_STEP + blk
        start = jnp.clip(n - 1, 0, nb - 3)
        koff = pl.multiple_of(start * SWA_BLOCK, SWA_BLOCK)
        kw = k_ref[0, pl.ds(koff, 3 * SWA_BLOCK), :]
        vw = v_ref[0, pl.ds(koff, 3 * SWA_BLOCK), :]
        valid = jnp.abs((n - start) * SWA_BLOCK + qi - kj) <= SWA_WINDOW
        q = q_ref[0, blk * SWA_BLOCK:(blk + 1) * SWA_BLOCK, :]
        chains = []
        for kv in range(2):
            sl = slice(kv * LANES, (kv + 1) * LANES)
            chains += _gqa_chains(q, kv, sink_ref, [kw[:, sl], kc[:, sl]], [vw[:, sl], vc[:, sl]], valid)
        _gqa_store(o_ref, _softmax_pv_staged(chains), blk * SWA_BLOCK)


def _swa_attn(proj, projc, sink):
    b, s, _ = proj.shape
    lc = projc.shape[1]
    nb = s // SWA_BLOCK
    assert nb >= 3 and nb % SWA_BLOCKS_PER_STEP == 0
    w2 = 2 * LANES
    tq = SWA_BLOCKS_PER_STEP * SWA_BLOCK
    return pl.pallas_call(
        _swa_kernel,
        grid=(b, nb // SWA_BLOCKS_PER_STEP),
        in_specs=[
            pl.BlockSpec(memory_space=pltpu.SMEM),
            pl.BlockSpec((1, tq, GROUP_W), lambda bi, n: (bi, n, C_QD // GROUP_W)),
            pl.BlockSpec((1, s, w2), lambda bi, n: (bi, 0, C_KD // w2)),
            pl.BlockSpec((1, s, w2), lambda bi, n: (bi, 0, C_VD // w2)),
            pl.BlockSpec((1, lc, w2), lambda bi, n: (bi, 0, C_KD // w2)),
            pl.BlockSpec((1, lc, w2), lambda bi, n: (bi, 0, C_VD // w2)),
        ],
        out_specs=pl.BlockSpec((1, tq, GROUP_W), lambda bi, n: (bi, n, 0)),
        out_shape=jax.ShapeDtypeStruct((b, s, GROUP_W), BF16),
        compiler_params=_cparams(("arbitrary", "arbitrary")),
        name="swa_attn",
    )(sink.reshape(-1), proj, proj, proj, projc, projc)


def _ctx_attn_kernel(sink_ref, qa_ref, ka_ref, va_ref, qd_ref, kd_ref, vd_ref, oa_ref, od_ref):
    m = qa_ref.shape[1]
    lane = lax.broadcasted_iota(jnp.int32, (m, LANES), 1)
    lo = lane < HEAD_DIM
    chains = []
    for p in range(GROUP_W // LANES):
        sl = slice(p * LANES, (p + 1) * LANES)
        q = qa_ref[0, :, sl]
        k = ka_ref[0, :, sl]
        v = va_ref[0, :, sl]
        for hh in range(2):
            qm = jnp.where(lo if hh == 0 else jnp.logical_not(lo), q, jnp.zeros_like(q))
            chains.append(([_dot_t(qm, k)], [v], None))
    _gqa_store(oa_ref, _softmax_pv_staged(chains))
    q = qd_ref[0]
    chains = []
    for kv in range(2):
        sl = slice(kv * LANES, (kv + 1) * LANES)
        chains += _gqa_chains(q, kv, sink_ref, [kd_ref[0, :, sl]], [vd_ref[0, :, sl]], None)
    _gqa_store(od_ref, _softmax_pv_staged(chains))


def _ctx_attn(projc, sink):
    b, lc, _ = projc.shape
    w2 = 2 * LANES

    def spec(width, col):
        return pl.BlockSpec((1, lc, width), lambda bi: (bi, 0, col // width))

    ospec = pl.BlockSpec((1, lc, GROUP_W), lambda bi: (bi, 0, 0))
    return pl.pallas_call(
        _ctx_attn_kernel,
        grid=(b,),
        in_specs=[pl.BlockSpec(memory_space=pltpu.SMEM),
                  spec(GROUP_W, C_QA), spec(GROUP_W, C_KA), spec(GROUP_W, C_VA),
                  spec(GROUP_W, C_QD), spec(w2, C_KD), spec(w2, C_VD)],
        out_specs=[ospec, ospec],
        out_shape=[jax.ShapeDtypeStruct((b, lc, GROUP_W), BF16)] * 2,
        compiler_params=_cparams(("arbitrary",)),
        name="ctx_attn",
    )(sink.reshape(-1), projc, projc, projc, projc, projc, projc)


OUTPROJ_ROW_GROUPS = 2


def _outproj_kernel(*refs, tm, moe):
    (oa_ref, od_ref, u_ref, up_ref, un_ref, pw_ref, ps_ref, cw_ref, wo_ref, x_ref, g1_ref,
     nw_ref, sh_ref, sc_ref) = refs[:14]
    rest = refs[14:]
    if moe:
        rt_ref, xo_ref, h_ref, gate_ref, ext_ub, ext_u = rest
    else:
        xo_ref, h_ref, ext_ub, ext_u = rest
    m = pl.program_id(1)
    nm = pl.num_programs(1)
    g4 = GROUP_W

    def cols(ref, k):
        return ref[0, :, k * g4:(k + 1) * g4].astype(F32)

    ub = cols(u_ref, 0)
    has_prev = m > 0
    has_next = m < nm - 1
    ext_ub[0:HALO] = jnp.where(has_prev, cols(up_ref, 0), 0.0)
    ext_ub[HALO:HALO + tm] = ub
    ext_ub[HALO + tm:] = jnp.where(has_next, cols(un_ref, 0), 0.0)
    ext_u[0:HALO] = jnp.where(has_prev, cols(up_ref, 2) * cols(up_ref, 3), 0.0)
    ext_u[HALO:HALO + tm] = cols(u_ref, 2) * cols(u_ref, 3)
    ext_u[HALO + tm:] = jnp.where(has_next, cols(un_ref, 2) * cols(un_ref, 3), 0.0)

    n_tok = nm * tm
    rg = tm // OUTPROJ_ROW_GROUPS
    groups = [(i * rg, (i + 1) * rg) for i in range(OUTPROJ_ROW_GROUPS)]

    def mixers(r0, r1):
        t = m * tm + r0 + lax.broadcasted_iota(jnp.int32, (rg, LANES), 0)
        obs = []
        for g, w in enumerate(POOL_WINDOWS):
            sl = slice(g * LANES, (g + 1) * LANES)
            acc = None
            for d in range(-(w // 2), w - w // 2):
                term = ext_ub[HALO + r0 + d:HALO + r1 + d, sl]
                acc = term if acc is None else acc + term
            cnt = (jnp.clip(t + (w - w // 2), 0, n_tok) - jnp.clip(t - w // 2, 0, n_tok)).astype(F32)
            pooled = acc / cnt - ext_ub[HALO + r0:HALO + r1, sl]
            obs.append(_dot(pooled.astype(BF16), pw_ref[g]) * ps_ref[:, sl])
        y = (ext_u[HALO + r0 - 1:HALO + r1 - 1] * cw_ref[0:1, :] + ext_u[HALO + r0:HALO + r1] * cw_ref[1:2, :]
             + ext_u[HALO + r0 + 1:HALO + r1 + 1] * cw_ref[2:3, :])
        o_c = u_ref[0, r0:r1, g4:2 * g4].astype(F32) * y
        return jnp.concatenate([oa_ref[0, r0:r1, :]] + [o.astype(BF16) for o in obs]
                               + [o_c.astype(BF16), od_ref[0, r0:r1, :]], axis=1)

    def norm(r0, r1, yy):
        xn = x_ref[0, r0:r1, :] + g1_ref[0, 0] * yy
        xo_ref[0, r0:r1, :] = xn
        return _rms_mod(xn, nw_ref[...], sh_ref[0, 0], sc_ref[0, 0])

    def route(r0, r1, h2):
        packed = _pack_bf16_pairs(h2)
        for c in range(PACK_PLANES):
            h_ref[c, r0:r1, :] = packed[:, c * PLANE_W:(c + 1) * PLANE_W]
        lane = lax.broadcasted_iota(jnp.int32, (rg, LANES), 1)
        h_hi = h2.astype(BF16)
        h_lo = (h2 - h_hi.astype(F32)).astype(BF16)
        logits = _dot(h_hi, rt_ref[0]) + (_dot(h_lo, rt_ref[0]) + _dot(h_hi, rt_ref[1]))
        logits = jnp.where(lane < N_EXPERTS, logits, -jnp.inf)
        v1 = jnp.max(logits, axis=-1, keepdims=True)
        i1 = jnp.min(jnp.where(logits == v1, lane, LANES), axis=-1, keepdims=True)
        l2 = jnp.where(lane == i1, -jnp.inf, logits)
        v2 = jnp.max(l2, axis=-1, keepdims=True)
        i2 = jnp.min(jnp.where(l2 == v2, lane, LANES), axis=-1, keepdims=True)
        e2 = jnp.exp(v2 - v1)
        p1 = 1.0 / (1.0 + e2)
        p2 = e2 / (1.0 + e2)
        gate_ref[0, r0:r1, 0:LANES] = jnp.where(lane == i1, 1.0, 0.0)
        gate_ref[0, r0:r1, LANES:2 * LANES] = jnp.where(lane == i2, 1.0, 0.0)
        gate_ref[0, r0:r1, 2 * LANES:3 * LANES] = jnp.broadcast_to(p1, (rg, LANES))
        gate_ref[0, r0:r1, 3 * LANES:4 * LANES] = jnp.broadcast_to(p2, (rg, LANES))

    lhs = [mixers(r0, r1) for r0, r1 in groups]
    yys = [_dot(a, wo_ref[...]) for a in lhs]
    h2s = [norm(r0, r1, yy) for (r0, r1), yy in zip(groups, yys)]
    for (r0, r1), h2 in zip(groups, h2s):
        if moe:
            route(r0, r1, h2)
        else:
            h_ref[0, r0:r1, :] = h2.astype(BF16)


def _outproj(o_a, o_d, proj, x, wo, pool_w, pool_scale, conv_w, nw, mods, mod_row, router=None):
    b, s, d = x.shape
    tm = min(s, 512)
    moe = router is not None
    row = mod_row
    nh = s // HALO
    th = tm // HALO

    def mspec(k):
        return pl.BlockSpec((1, 1, 1, d), lambda bi, m: (row(bi), k, 0, 0))

    gspec = pl.BlockSpec((1, tm, GROUP_W), lambda bi, m: (bi, m, 0))
    xspec = pl.BlockSpec((1, tm, d), lambda bi, m: (bi, m, 0))
    in_specs = [
        gspec, gspec,
        pl.BlockSpec((1, tm, 4 * GROUP_W), lambda bi, m: (bi, m, 0)),
        pl.BlockSpec((1, HALO, 4 * GROUP_W), lambda bi, m: (bi, jnp.maximum(m * th - 1, 0), 0)),
        pl.BlockSpec((1, HALO, 4 * GROUP_W), lambda bi, m: (bi, jnp.minimum((m + 1) * th, nh - 1), 0)),
        pl.BlockSpec((4, POOL_CH, POOL_CH), lambda bi, m: (0, 0, 0)),
        pl.BlockSpec((1, GROUP_W), lambda bi, m: (0, 0)),
        pl.BlockSpec((3, GROUP_W), lambda bi, m: (0, 0)),
        pl.BlockSpec((d, d), lambda bi, m: (0, 0), pipeline_mode=pl.Buffered(1)),
        xspec, mspec(2),
        pl.BlockSpec((1, d), lambda bi, m: (0, 0)),
        mspec(3), mspec(4),
    ]
    args = [o_a, o_d, proj, proj, proj, pool_w.astype(BF16), pool_scale.reshape(1, GROUP_W), conv_w, wo,
            x, mods, nw.reshape(1, d), mods, mods]
    if moe:
        nm = s // tm
        in_specs.append(pl.BlockSpec((2, d, LANES), lambda bi, m: (0, 0, 0)))
        r_pad = jnp.pad(router, ((0, 0), (0, LANES - router.shape[1])))
        r_hi = r_pad.astype(BF16)
        args.append(jnp.stack([r_hi, (r_pad - r_hi.astype(F32)).astype(BF16)]))
        out_shape = [jax.ShapeDtypeStruct((b, s, d), F32),
                     jax.ShapeDtypeStruct((PACK_PLANES, b * s, PLANE_W), jnp.uint32),
                     jax.ShapeDtypeStruct((b, s, 4 * LANES), F32)]
        out_specs = [xspec,
                     pl.BlockSpec((PACK_PLANES, tm, PLANE_W), lambda bi, m: (0, bi * nm + m, 0)),
                     pl.BlockSpec((1, tm, 4 * LANES), lambda bi, m: (bi, m, 0))]
    else:
        out_shape = [jax.ShapeDtypeStruct((b, s, d), F32), jax.ShapeDtypeStruct((b, s, d), BF16)]
        out_specs = [xspec, xspec]
    return pl.pallas_call(
        functools.partial(_outproj_kernel, tm=tm, moe=moe),
        grid=(b, s // tm),
        in_specs=in_specs,
        out_specs=out_specs,
        out_shape=out_shape,
        scratch_shapes=[pltpu.VMEM((tm + 2 * HALO, GROUP_W), F32), pltpu.VMEM((tm + 2 * HALO, GROUP_W), F32)],
        compiler_params=_cparams(("arbitrary", "arbitrary")),
        name="outproj",
    )(*args)


def _swiglu_accumulate(h_ref, wg_ref, wu_ref, wd_ref, acc_ref, rows=None):
    r = slice(0, h_ref.shape[0] if rows is None else rows)
    h = h_ref[r, :]
    a = _dot(h, wg_ref[0].astype(BF16))
    u = _dot(h, wu_ref[0].astype(BF16))
    act = a * (1.0 / (1.0 + jnp.exp(-a))) * u
    acc_ref[r, :] += _dot(act.astype(BF16), wd_ref[0].astype(BF16))


def _ffn_kernel(h_ref, wg_ref, wu_ref, wd_ref, o_ref, acc_ref):
    f = pl.program_id(1)

    @pl.when(f == 0)
    def _():
        acc_ref[...] = jnp.zeros(acc_ref.shape, F32)

    _swiglu_accumulate(h_ref, wg_ref, wu_ref, wd_ref, acc_ref)

    @pl.when(f == pl.num_programs(1) - 1)
    def _():
        o_ref[...] = acc_ref[...].astype(BF16)


def _ffn(h, wg, wu, wd, layer):
    mt, d = h.shape
    ff = wg.shape[2]
    tm = min(mt, 1024)
    tf = 256
    return pl.pallas_call(
        _ffn_kernel,
        grid=(mt // tm, ff // tf),
        in_specs=[
            pl.BlockSpec((tm, d), lambda m, f: (m, 0)),
            pl.BlockSpec((1, d, tf), lambda m, f: (layer, 0, f)),
            pl.BlockSpec((1, d, tf), lambda m, f: (layer, 0, f)),
            pl.BlockSpec((1, tf, d), lambda m, f: (layer, f, 0)),
        ],
        out_specs=pl.BlockSpec((tm, d), lambda m, f: (m, 0)),
        out_shape=jax.ShapeDtypeStruct((mt, d), BF16),
        scratch_shapes=[pltpu.VMEM((tm, d), F32)],
        compiler_params=_cparams(("arbitrary", "arbitrary")),
        name="ffn",
    )(h, wg, wu, wd)


def _rank_kernel(r_ref, o_ref, cnt_ref, carry):
    i = pl.program_id(0)
    tm = r_ref.shape[0]

    @pl.when(i == 0)
    def _():
        carry[...] = jnp.zeros(carry.shape, F32)

    oh1 = r_ref[:, 0:LANES]
    oh2 = r_ref[:, LANES:2 * LANES]
    sel = oh1 + oh2
    row = lax.broadcasted_iota(jnp.int32, (tm, tm), 0)
    col = lax.broadcasted_iota(jnp.int32, (tm, tm), 1)
    tri = jnp.where(col < row, 1.0, 0.0).astype(BF16)
    excl = _dot(tri, sel.astype(BF16)) + carry[0:1, :]
    lane = lax.broadcasted_iota(jnp.int32, (tm, LANES), 1)
    lane_f = lane.astype(F32)
    e1 = jnp.sum(oh1 * lane_f, axis=-1, keepdims=True)
    r1 = jnp.sum(oh1 * excl, axis=-1, keepdims=True)
    e2 = jnp.sum(oh2 * lane_f, axis=-1, keepdims=True)
    r2 = jnp.sum(oh2 * excl, axis=-1, keepdims=True)
    meta = jnp.where(lane == 0, e1, jnp.where(lane == 1, r1, jnp.where(lane == 2, e2,
                     jnp.where(lane == 3, r2, 0.0))))
    o_ref[...] = meta.T[0:SUBLANES, :]
    carry[...] = carry[...] + jnp.sum(sel, axis=0, keepdims=True)
    cnt_ref[...] = carry[...]


def _rank(route):
    t = route.shape[0]
    tm = min(t, 512)
    return pl.pallas_call(
        _rank_kernel,
        grid=(t // tm,),
        in_specs=[pl.BlockSpec((tm, 2 * LANES), lambda i: (i, 0))],
        out_specs=[pl.BlockSpec((SUBLANES, tm), lambda i: (0, i)),
                   pl.BlockSpec((SUBLANES, LANES), lambda i: (0, 0))],
        out_shape=[jax.ShapeDtypeStruct((SUBLANES, t), F32), jax.ShapeDtypeStruct((SUBLANES, LANES), F32)],
        scratch_shapes=[pltpu.VMEM((SUBLANES, LANES), F32)],
        compiler_params=_cparams(("arbitrary",)),
        name="rank",
    )(route)


def _sc_mesh():
    return plsc.VectorSubcoreMesh(core_axis_name="core", subcore_axis_name="subcore")


def _sc_scatter2(x, idx1, idx2, n_out):
    n, w = x.shape

    @functools.partial(pl.kernel, out_type=jax.ShapeDtypeStruct((n_out, w), x.dtype), mesh=_sc_mesh(),
                       scratch_types=[], name="sc_dispatch")
    def k(x_hbm, i1_hbm, i2_hbm, o_hbm):
        def body(x_vmem, i1_vmem, i2_vmem):
            pltpu.sync_copy(x_vmem, o_hbm.at[i1_vmem.at[0]])
            pltpu.sync_copy(x_vmem, o_hbm.at[i2_vmem.at[0]])

        pltpu.emit_pipeline(
            body, grid=(n // SC_WINDOW,),
            in_specs=[pl.BlockSpec((SC_WINDOW, w), lambda i: (i, 0)),
                      pl.BlockSpec((1, SC_WINDOW), lambda i: (0, i)),
                      pl.BlockSpec((1, SC_WINDOW), lambda i: (0, i))],
            out_specs=[], core_axis_name=("core", "subcore"),
            dimension_semantics=(pltpu.PARALLEL,))(x_hbm, i1_hbm, i2_hbm)

    return k(x, idx1.reshape(1, n), idx2.reshape(1, n))


def _sc_gather(y, idx):
    (n,) = idx.shape
    w = y.shape[1]

    @functools.partial(pl.kernel, out_type=jax.ShapeDtypeStruct((n, w), y.dtype), mesh=_sc_mesh(),
                       scratch_types=[], name="sc_combine")
    def k(y_hbm, i_hbm, o_hbm):
        def body(i_vmem, o_vmem):
            pltpu.sync_copy(y_hbm.at[i_vmem.at[0]], o_vmem)

        pltpu.emit_pipeline(
            body, grid=(n // SC_WINDOW,),
            in_specs=[pl.BlockSpec((1, SC_WINDOW), lambda i: (0, i))],
            out_specs=[pl.BlockSpec((SC_WINDOW, w), lambda i: (i, 0))],
            core_axis_name=("core", "subcore"),
            dimension_semantics=(pltpu.PARALLEL,))(i_hbm, o_hbm)

    return k(y, idx.reshape(1, n))


MOE_SUB = 256


def _moe_kernel(te_ref, nu_ref, tv_ref, h_ref, wg_ref, wu_ref, wd_ref, o_ref, acc_ref, hb_ref):
    m = pl.program_id(0)
    f = pl.program_id(1)
    tm = hb_ref.shape[0]
    half = hb_ref.shape[1] // 2

    @pl.when(m < nu_ref[0])
    def _():
        @pl.when(f == 0)
        def _():
            acc_ref[...] = jnp.zeros(acc_ref.shape, F32)
            routed = lax.broadcasted_iota(jnp.int32, (tm, PLANE_W), 0) < tv_ref[m]
            for c in range(PACK_PLANES):
                w = jnp.where(routed, h_ref[c], jnp.uint32(0))
                hb_ref[:, c * PLANE_W:(c + 1) * PLANE_W] = _unpack_hi(w).astype(BF16)
                hb_ref[:, half + c * PLANE_W:half + (c + 1) * PLANE_W] = _unpack_lo(w).astype(BF16)

        n_sub = (tv_ref[m] + MOE_SUB - 1) // MOE_SUB
        for k in range(1, tm // MOE_SUB + 1):
            @pl.when(n_sub == k)
            def _():
                _swiglu_accumulate(hb_ref, wg_ref, wu_ref, wd_ref, acc_ref, k * MOE_SUB)

        @pl.when(f == pl.num_programs(1) - 1)
        def _():
            packed = _pack_bf16_pairs(acc_ref[...])
            for c in range(PACK_PLANES):
                o_ref[c] = packed[:, c * PLANE_W:(c + 1) * PLANE_W]


MOE_TM = 1024


def _moe_routed(hp, tile_expert, n_used, tile_valid, wg, wu, wd, layer, tm):
    _, r, _ = hp.shape
    _, ne, d, ff = wg.shape
    wg = wg.reshape(-1, d, ff)
    wu = wu.reshape(-1, d, ff)
    wd = wd.reshape(-1, ff, d)
    e0 = layer * ne
    tf = 256
    nf = ff // tf

    def row_map(m, f, te, nu, tv):
        return (0, jnp.minimum(m, nu[0] - 1), 0)

    def fidx(m, f, nu):
        return jnp.where(m < nu[0], f, nf - 1)

    hspec = pl.BlockSpec((PACK_PLANES, tm, PLANE_W), row_map)
    return pl.pallas_call(
        _moe_kernel,
        grid_spec=pltpu.PrefetchScalarGridSpec(
            num_scalar_prefetch=3,
            grid=(r // tm, nf),
            in_specs=[
                hspec,
                pl.BlockSpec((1, d, tf), lambda m, f, te, nu, tv: (e0 + te[m], 0, fidx(m, f, nu))),
                pl.BlockSpec((1, d, tf), lambda m, f, te, nu, tv: (e0 + te[m], 0, fidx(m, f, nu))),
                pl.BlockSpec((1, tf, d), lambda m, f, te, nu, tv: (e0 + te[m], fidx(m, f, nu), 0)),
            ],
            out_specs=hspec,
            scratch_shapes=[pltpu.VMEM((tm, d), F32), pltpu.VMEM((tm, d), BF16)],
        ),
        out_shape=jax.ShapeDtypeStruct(hp.shape, jnp.uint32),
        compiler_params=_cparams(("arbitrary", "arbitrary")),
        name="moe",
    )(tile_expert, n_used, tile_valid, hp, wg, wu, wd)


def _route_plan(meta, counts, tm, n_tiles):
    e1, r1, e2, r2 = (meta[k].astype(jnp.int32) for k in range(4))
    cnt = counts[0, :N_EXPERTS].astype(jnp.int32)
    tiles_per = (cnt + tm - 1) // tm
    tile_end = jnp.cumsum(tiles_per)
    start_row = (tile_end - tiles_per) * tm
    pos1 = start_row[e1] + r1
    pos2 = start_row[e2] + r2
    n_used = tile_end[-1]
    tiles = jnp.arange(n_tiles, dtype=jnp.int32)
    tile_expert = jnp.sum((tiles[:, None] >= tile_end[None, :]).astype(jnp.int32), axis=1)
    last_expert = jnp.sum((n_used - 1 >= tile_end).astype(jnp.int32))
    tile_expert = jnp.minimum(tile_expert, last_expert)
    tile_start = (tile_end - tiles_per)[tile_expert]
    tile_valid = jnp.clip(cnt[tile_expert] - (tiles - tile_start) * tm, 0, tm)
    return pos1, pos2, tile_expert, n_used.reshape(1), tile_valid


def _plane_rows(pos, n_rows):
    return (jnp.arange(PACK_PLANES, dtype=jnp.int32)[:, None] * n_rows + pos[None, :]).reshape(-1)


def _moe(hp, route, wg, wu, wd, layer):
    _, t, _ = hp.shape
    tm = min(MOE_TM, t)
    n_tiles = 2 * t // tm + N_EXPERTS
    n_rows = n_tiles * tm
    meta, counts = _rank(route)
    pos1, pos2, tile_expert, n_used, tile_valid = _route_plan(meta, counts, tm, n_tiles)
    i1 = _plane_rows(pos1, n_rows)
    i2 = _plane_rows(pos2, n_rows)
    hs = _sc_scatter2(hp.reshape(PACK_PLANES * t, PLANE_W), i1, i2, PACK_PLANES * n_rows)
    ys = _moe_routed(hs.reshape(PACK_PLANES, n_rows, PLANE_W), tile_expert, n_used, tile_valid,
                     wg, wu, wd, layer, tm)
    ys = ys.reshape(PACK_PLANES * n_rows, PLANE_W)
    y1 = _sc_gather(ys, i1).reshape(PACK_PLANES, t, PLANE_W)
    y2 = _sc_gather(ys, i2).reshape(PACK_PLANES, t, PLANE_W)
    return y1, y2


def _final_kernel(x_ref, f_ref, g_ref, nw_ref, o_ref):
    xv = x_ref[0] + g_ref[0, 0] * f_ref[0].astype(F32)
    ms = jnp.mean(xv * xv, axis=-1, keepdims=True)
    o_ref[0] = xv * lax.rsqrt(ms + EPS) * nw_ref[...]


def _final(x, f, mods, nw):
    b, s, d = x.shape
    tm = min(s, 512)
    xspec = pl.BlockSpec((1, tm, d), lambda bi, m: (bi, m, 0))
    return pl.pallas_call(
        _final_kernel,
        grid=(b, s // tm),
        in_specs=[xspec, xspec,
                  pl.BlockSpec((1, 1, 1, d), lambda bi, m: (bi, 5, 0, 0)),
                  pl.BlockSpec((1, d), lambda bi, m: (0, 0))],
        out_specs=xspec,
        out_shape=jax.ShapeDtypeStruct((b, s, d), F32),
        compiler_params=_cparams(("arbitrary", "arbitrary")),
        name="final",
    )(x, f, mods, nw.reshape(1, d))


def _final_moe_kernel(x_ref, y1_ref, y2_ref, p_ref, g_ref, nw_ref, o_ref):
    tm, d = x_ref.shape[1], x_ref.shape[2]
    half = d // 2
    p = p_ref[0]
    p1 = jnp.concatenate([p[:, 0:LANES]] * (PLANE_W // LANES), axis=1)
    p2 = jnp.concatenate([p[:, LANES:2 * LANES]] * (PLANE_W // LANES), axis=1)
    ssq = jnp.zeros((tm, 1), F32)
    for c in range(PACK_PLANES):
        w1 = y1_ref[c]
        w2 = y2_ref[c]
        for unpack, off in ((_unpack_hi, 0), (_unpack_lo, half)):
            sl = slice(off + c * PLANE_W, off + (c + 1) * PLANE_W)
            f = p1 * unpack(w1) + p2 * unpack(w2)
            xv = x_ref[0, :, sl] + g_ref[0, 0, :, sl] * f
            o_ref[0, :, sl] = xv
            ssq = ssq + jnp.sum(xv * xv, axis=-1, keepdims=True)
    o_ref[0] = o_ref[0] * lax.rsqrt(ssq / d + EPS) * nw_ref[...]


def _final_moe(x, y1, y2, route, mods, nw):
    b, s, d = x.shape
    tm = min(s, 512)
    nm = s // tm
    xspec = pl.BlockSpec((1, tm, d), lambda bi, m: (bi, m, 0))
    yspec = pl.BlockSpec((PACK_PLANES, tm, PLANE_W), lambda bi, m: (0, bi * nm + m, 0))
    return pl.pallas_call(
        _final_moe_kernel,
        grid=(b, nm),
        in_specs=[xspec, yspec, yspec,
                  pl.BlockSpec((1, tm, 2 * LANES), lambda bi, m: (bi, m, 1)),
                  pl.BlockSpec((1, 1, 1, d), lambda bi, m: (bi, 5, 0, 0)),
                  pl.BlockSpec((1, d), lambda bi, m: (0, 0))],
        out_specs=xspec,
        out_shape=jax.ShapeDtypeStruct((b, s, d), F32),
        compiler_params=_cparams(("arbitrary", "arbitrary")),
        name="final",
    )(x, y1, y2, route, mods, nw.reshape(1, d))


PREP_W = 2 * LANES


def _w_in_plan():
    nb = GROUP_W // PREP_W
    plan = [(3 * nb + j, 0, 0) for j in range(4 * nb)]
    plan += [(j, 1, 0) for j in range(nb)]
    plan += [(nb + j, 0, 0) for j in range(2 * nb)]
    plan += [(7 * nb + j, 1, 0) for j in range(nb)]
    plan += [(8 * nb, 0, 1), (8 * nb, 0, 2)]
    assert len(plan) * PREP_W == D_PROJ
    return jnp.asarray(plan, jnp.int32).T.reshape(-1)


def _prep_w_kernel(plan_ref, w_ref, o_ref):
    j = pl.program_id(1)
    nblk = pl.num_programs(1)
    x = w_ref[0]
    mode = plan_ref[2 * nblk + j]
    src = jnp.where(mode == 2, x[:, LANES:], x[:, :LANES])
    lo = lax.broadcasted_iota(jnp.int32, src.shape, 1) < HEAD_DIM
    other = pltpu.roll(src, HEAD_DIM, 1)
    dup = jnp.concatenate([jnp.where(lo, src, other), jnp.where(lo, other, src)], axis=1)
    scale = jnp.where(plan_ref[nblk + j] == 1, HEAD_DIM ** -0.5, 1.0)
    o_ref[0] = (jnp.where(mode == 0, x, dup) * scale).astype(BF16)


def _prep_w_in(w_in):
    depth, d, _ = w_in.shape
    nblk = D_PROJ // PREP_W
    return pl.pallas_call(
        _prep_w_kernel,
        grid_spec=pltpu.PrefetchScalarGridSpec(
            num_scalar_prefetch=1,
            grid=(depth, nblk),
            in_specs=[pl.BlockSpec((1, d, PREP_W), lambda i, j, plan: (i, 0, plan[j]))],
            out_specs=pl.BlockSpec((1, d, PREP_W), lambda i, j, plan: (i, 0, j)),
        ),
        out_shape=jax.ShapeDtypeStruct((depth, d, D_PROJ), BF16),
        compiler_params=_cparams(("arbitrary", "arbitrary")),
        name="prep_w_in",
    )(_w_in_plan(), w_in)


def _rope_tables(n):
    t = jnp.arange(n, dtype=jnp.int32)
    rows = (t // GRID_W).astype(F32)
    cols = (t % GRID_W).astype(F32)
    nf = HEAD_DIM // 4
    inv = ROPE_BASE ** (-jnp.arange(nf, dtype=F32) / nf)
    ang = jnp.concatenate([rows[:, None] * inv, cols[:, None] * inv], axis=-1)
    cos = jnp.cos(ang)
    sin = jnp.sin(ang)
    cos_t = jnp.concatenate([cos, cos, cos, cos], axis=-1)
    sin_t = jnp.concatenate([-sin, sin, -sin, sin], axis=-1)
    return cos_t, sin_t


def kernel(x, c, ctx, c_ctx, w_ada, b_ada, norm_mix, norm_ffn, norm_final, w_in, w_out, na_rpb, pool_w, pool_scale, conv_w, swa_sink, ffn_w_gate, ffn_w_up, ffn_w_down, moe_router, moe_w_gate, moe_w_up, moe_w_down):
    b, s, d = x.shape
    lc = ctx.shape[1]
    depth = w_ada.shape[0]
    cond = jnp.concatenate([c, c_ctx[None, :]], axis=0)
    assert b == 2
    mods_all = _ada(cond, w_ada, b_ada)
    cos_t, sin_t = _rope_tables(s)
    cos_c = jnp.ones((lc, LANES), F32)
    sin_c = jnp.zeros((lc, LANES), F32)
    row_x = lambda bi: bi
    row_c = lambda bi: b

    w_proj = _prep_w_in(w_in)
    cx = ctx
    fx = fc = None
    for i in range(depth):
        last = i == depth - 1
        mods = mods_all[i].reshape(SUBLANES, 6, 1, d)
        wi = (w_proj, i)
        wo = w_out[i].astype(BF16)
        if i == 0:
            proj, _ = _inproj(x, wi, norm_mix[i], mods, row_x, cos_t, sin_t)
            projc, _ = _inproj(cx, wi, norm_mix[i], mods, row_c, cos_c, sin_c)
        else:
            mods_prev = mods_all[i - 1].reshape(SUBLANES, 6, 1, d)
            proj, x = _inproj_res(x, fx, mods_prev, wi, norm_mix[i], mods, row_x, cos_t, sin_t)
            projc, cx = _inproj_res(cx, fc, mods_prev, wi, norm_mix[i], mods, row_c, cos_c, sin_c)
        bias = _na_bias(na_rpb[i])
        o_a = _na_attn(proj, projc, bias)
        o_d = _swa_attn(proj, projc, swa_sink[i])
        moe = i % 2 == 1
        router = moe_router[i // 2] if moe else None
        outs = _outproj(o_a, o_d, proj, x, wo, pool_w[i], pool_scale[i], conv_w[i], norm_ffn[i], mods, row_x,
                        router)
        x, h2 = outs[0], outs[1]
        if not last:
            oc_a, oc_d = _ctx_attn(projc, swa_sink[i])
            cx, h2c = _outproj(oc_a, oc_d, projc, cx, wo, pool_w[i], pool_scale[i], conv_w[i], norm_ffn[i],
                               mods, row_c)
        j = i // 2
        if moe:
            if not last:
                raise NotImplementedError("an expert layer that is not the last layer")
            route = outs[2]
            y1, y2 = _moe(h2, route.reshape(b * s, 4 * LANES), moe_w_gate, moe_w_up, moe_w_down, j)
            return _final_moe(x, y1, y2, route, mods, norm_final)
        fx = _ffn(h2.reshape(b * s, d), ffn_w_gate, ffn_w_up, ffn_w_down, j).reshape(b, s, d)
        if not last:
            fc = _ffn(h2c.reshape(b * lc, d), ffn_w_gate, ffn_w_up, ffn_w_down, j).reshape(b, lc, d)
    return _final(x, fx, mods_all[depth - 1].reshape(SUBLANES, 6, 1, d), norm_final)


def _inproj_res(x, f, mods_prev, w, nw, mods, mod_row, cos_t, sin_t):
    both = jnp.concatenate([mods, mods_prev], axis=1)
    return _inproj(x, w, nw, both, mod_row, cos_t, sin_t, res=(f, 6 + 5))
```

```python
import functools

import jax
import jax.numpy as jnp
from jax import lax
from jax.experimental import pallas as pl
from jax.experimental.pallas import tpu as pltpu
from jax.experimental.pallas import tpu_sc as plsc

F32 = jnp.float32
BF16 = jnp.bfloat16

D_MODEL = 2048
GRID_W = 64
HEAD_DIM = 64
EPS = 1e-6
NEG_INF = -1e30
GROUP_W = 512
NA_KH = 8
NA_KW = 16
POOL_WINDOWS = (2, 4, 8, 16)
POOL_CH = 128
SWA_WINDOW = 128
SWA_BLOCK = 128
ROPE_BASE = 10000.0
N_EXPERTS = 8
LANES = 128
SUBLANES = 8
HALO = 16

C_UBC = 0
C_QA = 2048
C_KA = 2560
C_VA = 3072
C_QD = 3584
C_KD = 4096
C_VD = 4352
D_PROJ = 4608
TN_PROJ = 512

VMEM_LIMIT = 56 * 1024 * 1024


def _cparams(sem):
    return pltpu.CompilerParams(dimension_semantics=sem, vmem_limit_bytes=VMEM_LIMIT)


def _dot(a, b):
    return jnp.dot(a, b, preferred_element_type=F32)


def _dot_t(a, b):
    return lax.dot_general(a, b, (((1,), (1,)), ((), ())), preferred_element_type=F32)


def _rms_mod(xv, nw, sh, sc):
    ms = jnp.mean(xv * xv, axis=-1, keepdims=True)
    return (xv * lax.rsqrt(ms + EPS)) * (nw * (1.0 + sc)) + sh


PACK_PLANES = 4
PLANE_W = D_MODEL // 2 // PACK_PLANES
SC_WINDOW = 128


def _pack_bf16_pairs(v):
    half = v.shape[1] // 2
    hi = lax.bitcast_convert_type(v[:, :half].astype(BF16).astype(F32), jnp.uint32)
    lo = lax.bitcast_convert_type(v[:, half:].astype(BF16).astype(F32), jnp.uint32)
    return hi | (lo >> 16)


def _unpack_hi(w):
    return lax.bitcast_convert_type(w & jnp.uint32(0xFFFF0000), F32)


def _unpack_lo(w):
    return lax.bitcast_convert_type(w << 16, F32)


def _ada_kernel(c_ref, w_ref, b_ref, o_ref, silu_scr):
    tn = w_ref.shape[2]

    @pl.when((pl.program_id(0) == 0) & (pl.program_id(1) == 0))
    def _():
        cv = c_ref[...]
        silu_scr[...] = cv * (1.0 / (1.0 + jnp.exp(-cv)))

    o_ref[...] = jnp.zeros(o_ref.shape, F32)
    for r in range(3):
        m = silu_scr[r]
        cols = []
        for j in range(tn // LANES):
            wj = w_ref[0, :, j * LANES:(j + 1) * LANES]
            cols.append(jnp.sum(wj * m, axis=0, keepdims=True))
        o_ref[0, r:r + 1, :] = jnp.concatenate(cols, axis=1) + b_ref[0]


def _ada(cond, w_ada, b_ada):
    depth, d, n6 = w_ada.shape
    tn = 2048
    cb = jnp.broadcast_to(cond[:, :, None], (3, d, LANES))
    return pl.pallas_call(
        _ada_kernel,
        grid=(depth, n6 // tn),
        in_specs=[
            pl.BlockSpec((3, d, LANES), lambda i, j: (0, 0, 0)),
            pl.BlockSpec((1, d, tn), lambda i, j: (i, 0, j)),
            pl.BlockSpec((1, 1, tn), lambda i, j: (i, 0, j)),
        ],
        out_specs=pl.BlockSpec((1, SUBLANES, tn), lambda i, j: (i, 0, j)),
        out_shape=jax.ShapeDtypeStruct((depth, SUBLANES, n6), F32),
        scratch_shapes=[pltpu.VMEM((3, d, LANES), F32)],
        compiler_params=_cparams(("arbitrary", "arbitrary")),
        name="ada",
    )(cb, w_ada, b_ada.reshape(depth, 1, n6))


def _rope(a, cosv, sinv, lo32):
    sw = jnp.where(lo32, pltpu.roll(a, 96, 1), pltpu.roll(a, 32, 1))
    return a * cosv + sw * sinv


def _inproj_kernel(*refs, has_res):
    if has_res:
        x_ref, f_ref, g_ref, nw_ref, sh_ref, sc_ref, cos_ref, sin_ref, w_ref, o_ref, x2_ref = refs
    else:
        x_ref, nw_ref, sh_ref, sc_ref, cos_ref, sin_ref, w_ref, o_ref = refs
    xv = x_ref[0]
    if has_res:
        xv = xv + g_ref[0, 0] * f_ref[0].astype(F32)
        x2_ref[0] = xv
    h = _rms_mod(xv, nw_ref[...], sh_ref[0, 0], sc_ref[0, 0]).astype(BF16)
    tm = h.shape[0]
    lane = lax.broadcasted_iota(jnp.int32, (tm, LANES), 1)
    lo32 = (lane % HEAD_DIM) < (HEAD_DIM // 2)
    rope_end = C_VD
    for n in range(D_PROJ // TN_PROJ):
        c0 = n * TN_PROJ
        acc = _dot(h, w_ref[0, :, c0:c0 + TN_PROJ])
        if c0 + TN_PROJ <= C_QD:
            o_ref[0, :, c0:c0 + TN_PROJ] = acc.astype(BF16)
            continue
        for g in range(TN_PROJ // LANES):
            a = acc[:, g * LANES:(g + 1) * LANES]
            if c0 + g * LANES < rope_end:
                a = _rope(a, cos_ref[...], sin_ref[...], lo32)
            o_ref[0, :, c0 + g * LANES:c0 + (g + 1) * LANES] = a.astype(BF16)


def _inproj(x, w, nw, mods, mod_row, cos_t, sin_t, res=None):
    b, s, d = x.shape
    w_all, layer = w
    tm = min(s, 512)
    has_res = res is not None
    row = mod_row
    xspec = pl.BlockSpec((1, tm, d), lambda bi, m: (bi, m, 0))

    def mspec(k):
        return pl.BlockSpec((1, 1, 1, d), lambda bi, m: (row(bi), k, 0, 0))

    in_specs = [xspec]
    args = [x]
    if has_res:
        f, gk = res
        in_specs += [xspec, mspec(gk)]
        args += [f, mods]
    in_specs += [
        pl.BlockSpec((1, d), lambda bi, m: (0, 0)),
        mspec(0), mspec(1),
        pl.BlockSpec((tm, LANES), lambda bi, m: (m, 0)),
        pl.BlockSpec((tm, LANES), lambda bi, m: (m, 0)),
        pl.BlockSpec((1, d, D_PROJ), lambda bi, m: (layer, 0, 0), pipeline_mode=pl.Buffered(1)),
    ]
    args += [nw.reshape(1, d), mods, mods, cos_t, sin_t, w_all]
    out_shape = [jax.ShapeDtypeStruct((b, s, D_PROJ), BF16)]
    out_specs = [pl.BlockSpec((1, tm, D_PROJ), lambda bi, m: (bi, m, 0))]
    if has_res:
        out_shape.append(jax.ShapeDtypeStruct((b, s, d), F32))
        out_specs.append(xspec)
    outs = pl.pallas_call(
        functools.partial(_inproj_kernel, has_res=has_res),
        grid=(b, s // tm),
        in_specs=in_specs,
        out_specs=out_specs,
        out_shape=out_shape,
        compiler_params=_cparams(("arbitrary", "arbitrary")),
        name="inproj",
    )(*args)
    return outs if has_res else (outs[0], None)


def _na_bias_kernel(rpb_ref, o_ref):
    h = pl.program_id(0)
    nd = 2 * NA_KH - 1
    nj = 2 * NA_KW - 1
    q = lax.broadcasted_iota(jnp.int32, (GRID_W, LANES), 0)
    lane = lax.broadcasted_iota(jnp.int32, (GRID_W, LANES), 1)
    kw = lane % GRID_W
    dc = jnp.clip(kw - q, -(NA_KW - 1), NA_KW - 1) + (NA_KW - 1)
    c0 = jnp.clip(q - NA_KW // 2, 0, GRID_W - NA_KW)
    valid = (kw >= c0) & (kw < c0 + NA_KW)
    tabs = []
    for d in range(nd):
        t = jnp.zeros((GRID_W, LANES), F32)
        for j in range(nj):
            t = jnp.where(dc == j, rpb_ref[h * (nd * nj) + d * nj + j], t)
        tabs.append(jnp.where(valid, t, NEG_INF))
    for c in range(NA_KH):
        for g in range(NA_KH // 2):
            d_lo = 2 * g - c + (NA_KH - 1)
            o_ref[c, 0, :, g * LANES:(g + 1) * LANES] = jnp.where(lane < GRID_W, tabs[d_lo], tabs[d_lo + 1])


def _na_bias(rpb):
    nh = rpb.shape[0]
    return pl.pallas_call(
        _na_bias_kernel,
        grid=(nh,),
        in_specs=[pl.BlockSpec(memory_space=pltpu.SMEM)],
        out_specs=pl.BlockSpec((NA_KH, 1, GRID_W, NA_KH * GRID_W), lambda h: (0, h, 0, 0)),
        out_shape=jax.ShapeDtypeStruct((NA_KH, nh, GRID_W, NA_KH * GRID_W), F32),
        compiler_params=_cparams(("arbitrary",)),
        name="na_bias",
    )(rpb.reshape(-1))


def _softmax_pv_staged(chains):
    add = lambda a, b: a + b
    ms = []
    for sp, _, extra in chains:
        m = functools.reduce(jnp.maximum, [jnp.max(s, axis=-1, keepdims=True) for s in sp])
        ms.append(m if extra is None else jnp.maximum(m, extra))
    ps = [[jnp.exp(s - m) for s in sp] for (sp, _, _), m in zip(chains, ms)]
    dens = []
    for pp, (_, _, extra), m in zip(ps, chains, ms):
        den = functools.reduce(add, [jnp.sum(p, axis=-1, keepdims=True) for p in pp])
        dens.append(den if extra is None else den + jnp.exp(extra - m))
    outs = [functools.reduce(add, [_dot(p.astype(BF16), v) for p, v in zip(pp, vp)])
            for pp, (_, vp, _) in zip(ps, chains)]
    return [o / d for o, d in zip(outs, dens)]


NA_MAX_ROWS_PER_STEP = 32
NA_ROW_GROUP = 4


def _na_kernel(q_ref, k_ref, v_ref, kc_ref, vc_ref, bias_ref, o_ref, *, rows):
    rb = pl.program_id(2)
    rows_per_step = q_ref.shape[1] // GRID_W
    lo = lax.broadcasted_iota(jnp.int32, (NA_ROW_GROUP * GRID_W, LANES), 1) < HEAD_DIM
    lo_row = lax.broadcasted_iota(jnp.int32, (GRID_W, LANES), 1) < HEAD_DIM
    kc = kc_ref[0]
    vc = vc_ref[0]
    kh = min(NA_KH, rows)
    win = kh * GRID_W

    g = NA_ROW_GROUP
    gq = g * GRID_W
    zero = jnp.zeros((gq, LANES), BF16)
    for i0 in range(0, rows_per_step, g):
        q = q_ref[0, i0 * GRID_W:(i0 + g) * GRID_W, :]
        q2 = jnp.concatenate([jnp.where(lo, q, zero), jnp.where(lo, zero, q)], axis=0)
        s_ctx = _dot_t(q2, kc)
        pieces = [(hh, j, slice(hh * gq + j * GRID_W, hh * gq + (j + 1) * GRID_W))
                  for hh in range(2) for j in range(g)]
        vws, s_loc = [], {}
        for j in range(g):
            r = rb * rows_per_step + i0 + j
            r0 = jnp.clip(r - kh // 2, 0, rows - kh)
            c = r - r0
            koff = pl.multiple_of(r0 * GRID_W, GRID_W)
            vws.append(v_ref[0, pl.ds(koff, win), :])
            qp = jnp.concatenate([q2[sl] for hh, jj, sl in pieces if jj == j], axis=0)
            sp = _dot_t(qp, k_ref[0, pl.ds(koff, win), :])
            for hh in range(2):
                s_loc[hh, j] = sp[hh * GRID_W:(hh + 1) * GRID_W] + bias_ref[c, hh]
        ms = {(hh, j): jnp.maximum(jnp.max(s_loc[hh, j], axis=-1, keepdims=True),
                                   jnp.max(s_ctx[sl], axis=-1, keepdims=True)) for hh, j, sl in pieces}
        p_loc = {(hh, j): jnp.exp(s_loc[hh, j] - ms[hh, j]) for hh, j, _ in pieces}
        p_ctx = {(hh, j): jnp.exp(s_ctx[sl] - ms[hh, j]) for hh, j, sl in pieces}
        den = {k: jnp.sum(p_loc[k], axis=-1, keepdims=True) + jnp.sum(p_ctx[k], axis=-1, keepdims=True)
               for k in p_loc}
        pv_ctx = _dot(jnp.concatenate([p_ctx[hh, j].astype(BF16) for hh, j, _ in pieces], axis=0), vc)
        for j in range(g):
            pv = _dot(jnp.concatenate([p_loc[0, j].astype(BF16), p_loc[1, j].astype(BF16)], axis=0), vws[j])
            o = [(pv[hh * GRID_W:(hh + 1) * GRID_W] + pv_ctx[hh * gq + j * GRID_W:hh * gq + (j + 1) * GRID_W])
                 / den[hh, j] for hh in range(2)]
            i = i0 + j
            o_ref[0, i * GRID_W:(i + 1) * GRID_W, :] = jnp.where(lo_row, o[0], o[1]).astype(BF16)


def _na_attn(proj, projc, bias):
    b, s, _ = proj.shape
    lc = projc.shape[1]
    rows = s // GRID_W
    rows_per_step = min(NA_MAX_ROWS_PER_STEP, rows)
    assert rows >= NA_KH and rows % rows_per_step == 0 and rows_per_step % NA_ROW_GROUP == 0
    tq = rows_per_step * GRID_W
    npair = GROUP_W // LANES
    return pl.pallas_call(
        functools.partial(_na_kernel, rows=rows),
        grid=(b, npair, rows // rows_per_step),
        in_specs=[
            pl.BlockSpec((1, tq, LANES), lambda bi, p, r: (bi, r, C_QA // LANES + p)),
            pl.BlockSpec((1, s, LANES), lambda bi, p, r: (bi, 0, C_KA // LANES + p)),
            pl.BlockSpec((1, s, LANES), lambda bi, p, r: (bi, 0, C_VA // LANES + p)),
            pl.BlockSpec((1, lc, LANES), lambda bi, p, r: (bi, 0, C_KA // LANES + p)),
            pl.BlockSpec((1, lc, LANES), lambda bi, p, r: (bi, 0, C_VA // LANES + p)),
            pl.BlockSpec((NA_KH, 2, GRID_W, NA_KH * GRID_W), lambda bi, p, r: (0, p, 0, 0)),
        ],
        out_specs=pl.BlockSpec((1, tq, LANES), lambda bi, p, r: (bi, r, p)),
        out_shape=jax.ShapeDtypeStruct((b, s, GROUP_W), BF16),
        compiler_params=_cparams(("arbitrary", "arbitrary", "arbitrary")),
        name="na_attn",
    )(proj, proj, proj, projc, projc, bias)


def _gqa_chains(q, kv, sink_ref, k_parts, v_parts, mask):
    m = q.shape[0]
    lane = lax.broadcasted_iota(jnp.int32, (m, LANES), 1)
    lo = lane < HEAD_DIM
    qs = []
    for j in range(2):
        qg = q[:, (kv * 2 + j) * LANES:(kv * 2 + j + 1) * LANES]
        qs.append(jnp.where(lo, qg, jnp.zeros_like(qg)))
        qs.append(jnp.where(lo, jnp.zeros_like(qg), qg))
    qq = jnp.concatenate(qs, axis=0)
    s_all = [_dot_t(qq, k) for k in k_parts]
    chains = []
    for i in range(4):
        parts = [s[i * m:(i + 1) * m] for s in s_all]
        if mask is not None:
            parts[0] = jnp.where(mask, parts[0], NEG_INF)
        chains.append((parts, v_parts, jnp.full((m, 1), sink_ref[kv * 4 + i], F32)))
    return chains


def _gqa_store(o_ref, heads, row0=0):
    m = heads[0].shape[0]
    lane = lax.broadcasted_iota(jnp.int32, heads[0].shape, 1)
    lo = lane < HEAD_DIM
    for g in range(len(heads) // 2):
        o_ref[0, row0:row0 + m, g * LANES:(g + 1) * LANES] = (
            jnp.where(lo, heads[2 * g], heads[2 * g + 1]).astype(BF16))


SWA_BLOCKS_PER_STEP = 4


def _swa_kernel(sink_ref, q_ref, k_ref, v_ref, kc_ref, vc_ref, o_ref):
    nb = pl.num_programs(1) * SWA_BLOCKS_PER_STEP
    kc = kc_ref[0]
    vc = vc_ref[0]
    qi = lax.broadcasted_iota(jnp.int32, (SWA_BLOCK, 3 * SWA_BLOCK), 0)
    kj = lax.broadcasted_iota(jnp.int32, (SWA_BLOCK, 3 * SWA_BLOCK), 1)
    for blk in range(SWA_BLOCKS_PER_STEP):
        n = pl.program_id(1) * SWA_BLOCKS_PER_STEP + blk
        start = jnp.clip(n - 1, 0, nb - 3)
        koff = pl.multiple_of(start * SWA_BLOCK, SWA_BLOCK)
        kw = k_ref[0, pl.ds(koff, 3 * SWA_BLOCK), :]
        vw = v_ref[0, pl.ds(koff, 3 * SWA_BLOCK), :]
        valid = jnp.abs((n - start) * SWA_BLOCK + qi - kj) <= SWA_WINDOW
        q = q_ref[0, blk * SWA_BLOCK:(blk + 1) * SWA_BLOCK, :]
        chains = []
        for kv in range(2):
            sl = slice(kv * LANES, (kv + 1) * LANES)
            chains += _gqa_chains(q, kv, sink_ref, [kw[:, sl], kc[:, sl]], [vw[:, sl], vc[:, sl]], valid)
        _gqa_store(o_ref, _softmax_pv_staged(chains), blk * SWA_BLOCK)


def _swa_attn(proj, projc, sink):
    b, s, _ = proj.shape
    lc = projc.shape[1]
    nb = s // SWA_BLOCK
    assert nb >= 3 and nb % SWA_BLOCKS_PER_STEP == 0
    w2 = 2 * LANES
    tq = SWA_BLOCKS_PER_STEP * SWA_BLOCK
    return pl.pallas_call(
        _swa_kernel,
        grid=(b, nb // SWA_BLOCKS_PER_STEP),
        in_specs=[
            pl.BlockSpec(memory_space=pltpu.SMEM),
            pl.BlockSpec((1, tq, GROUP_W), lambda bi, n: (bi, n, C_QD // GROUP_W)),
            pl.BlockSpec((1, s, w2), lambda bi, n: (bi, 0, C_KD // w2)),
            pl.BlockSpec((1, s, w2), lambda bi, n: (bi, 0, C_VD // w2)),
            pl.BlockSpec((1, lc, w2), lambda bi, n: (bi, 0, C_KD // w2)),
            pl.BlockSpec((1, lc, w2), lambda bi, n: (bi, 0, C_VD // w2)),
        ],
        out_specs=pl.BlockSpec((1, tq, GROUP_W), lambda bi, n: (bi, n, 0)),
        out_shape=jax.ShapeDtypeStruct((b, s, GROUP_W), BF16),
        compiler_params=_cparams(("arbitrary", "arbitrary")),
        name="swa_attn",
    )(sink.reshape(-1), proj, proj, proj, projc, projc)


def _ctx_attn_kernel(sink_ref, qa_ref, ka_ref, va_ref, qd_ref, kd_ref, vd_ref, oa_ref, od_ref):
    m = qa_ref.shape[1]
    lane = lax.broadcasted_iota(jnp.int32, (m, LANES), 1)
    lo = lane < HEAD_DIM
    chains = []
    for p in range(GROUP_W // LANES):
        sl = slice(p * LANES, (p + 1) * LANES)
        q = qa_ref[0, :, sl]
        k = ka_ref[0, :, sl]
        v = va_ref[0, :, sl]
        for hh in range(2):
            qm = jnp.where(lo if hh == 0 else jnp.logical_not(lo), q, jnp.zeros_like(q))
            chains.append(([_dot_t(qm, k)], [v], None))
    _gqa_store(oa_ref, _softmax_pv_staged(chains))
    q = qd_ref[0]
    chains = []
    for kv in range(2):
        sl = slice(kv * LANES, (kv + 1) * LANES)
        chains += _gqa_chains(q, kv, sink_ref, [kd_ref[0, :, sl]], [vd_ref[0, :, sl]], None)
    _gqa_store(od_ref, _softmax_pv_staged(chains))


def _ctx_attn(projc, sink):
    b, lc, _ = projc.shape
    w2 = 2 * LANES

    def spec(width, col):
        return pl.BlockSpec((1, lc, width), lambda bi: (bi, 0, col // width))

    ospec = pl.BlockSpec((1, lc, GROUP_W), lambda bi: (bi, 0, 0))
    return pl.pallas_call(
        _ctx_attn_kernel,
        grid=(b,),
        in_specs=[pl.BlockSpec(memory_space=pltpu.SMEM),
                  spec(GROUP_W, C_QA), spec(GROUP_W, C_KA), spec(GROUP_W, C_VA),
                  spec(GROUP_W, C_QD), spec(w2, C_KD), spec(w2, C_VD)],
        out_specs=[ospec, ospec],
        out_shape=[jax.ShapeDtypeStruct((b, lc, GROUP_W), BF16)] * 2,
        compiler_params=_cparams(("arbitrary",)),
        name="ctx_attn",
    )(sink.reshape(-1), projc, projc, projc, projc, projc, projc)


OUTPROJ_ROW_GROUPS = 2


def _outproj_kernel(*refs, tm, moe):
    (oa_ref, od_ref, u_ref, up_ref, un_ref, pw_ref, ps_ref, cw_ref, wo_ref, x_ref, g1_ref,
     nw_ref, sh_ref, sc_ref) = refs[:14]
    rest = refs[14:]
    if moe:
        rt_ref, xo_ref, h_ref, gate_ref, ext_ub, ext_u = rest
    else:
        xo_ref, h_ref, ext_ub, ext_u = rest
    m = pl.program_id(1)
    nm = pl.num_programs(1)
    g4 = GROUP_W

    def cols(ref, k):
        return ref[0, :, k * g4:(k + 1) * g4].astype(F32)

    ub = cols(u_ref, 0)
    has_prev = m > 0
    has_next = m < nm - 1
    ext_ub[0:HALO] = jnp.where(has_prev, cols(up_ref, 0), 0.0)
    ext_ub[HALO:HALO + tm] = ub
    ext_ub[HALO + tm:] = jnp.where(has_next, cols(un_ref, 0), 0.0)
    ext_u[0:HALO] = jnp.where(has_prev, cols(up_ref, 2) * cols(up_ref, 3), 0.0)
    ext_u[HALO:HALO + tm] = cols(u_ref, 2) * cols(u_ref, 3)
    ext_u[HALO + tm:] = jnp.where(has_next, cols(un_ref, 2) * cols(un_ref, 3), 0.0)

    n_tok = nm * tm
    rg = tm // OUTPROJ_ROW_GROUPS
    groups = [(i * rg, (i + 1) * rg) for i in range(OUTPROJ_ROW_GROUPS)]

    def mixers(r0, r1):
        t = m * tm + r0 + lax.broadcasted_iota(jnp.int32, (rg, LANES), 0)
        obs = []
        for g, w in enumerate(POOL_WINDOWS):
            sl = slice(g * LANES, (g + 1) * LANES)
            acc = None
            for d in range(-(w // 2), w - w // 2):
                term = ext_ub[HALO + r0 + d:HALO + r1 + d, sl]
                acc = term if acc is None else acc + term
            cnt = (jnp.clip(t + (w - w // 2), 0, n_tok) - jnp.clip(t - w // 2, 0, n_tok)).astype(F32)
            pooled = acc / cnt - ext_ub[HALO + r0:HALO + r1, sl]
            obs.append(_dot(pooled.astype(BF16), pw_ref[g]) * ps_ref[:, sl])
        y = (ext_u[HALO + r0 - 1:HALO + r1 - 1] * cw_ref[0:1, :] + ext_u[HALO + r0:HALO + r1] * cw_ref[1:2, :]
             + ext_u[HALO + r0 + 1:HALO + r1 + 1] * cw_ref[2:3, :])
        o_c = u_ref[0, r0:r1, g4:2 * g4].astype(F32) * y
        return jnp.concatenate([oa_ref[0, r0:r1, :]] + [o.astype(BF16) for o in obs]
                               + [o_c.astype(BF16), od_ref[0, r0:r1, :]], axis=1)

    def norm(r0, r1, yy):
        xn = x_ref[0, r0:r1, :] + g1_ref[0, 0] * yy
        xo_ref[0, r0:r1, :] = xn
        return _rms_mod(xn, nw_ref[...], sh_ref[0, 0], sc_ref[0, 0])

    def route(r0, r1, h2):
        packed = _pack_bf16_pairs(h2)
        for c in range(PACK_PLANES):
            h_ref[c, r0:r1, :] = packed[:, c * PLANE_W:(c + 1) * PLANE_W]
        lane = lax.broadcasted_iota(jnp.int32, (rg, LANES), 1)
        h_hi = _unpack_hi(lax.bitcast_convert_type(h2, jnp.uint32))
        h_lo = (h2 - h_hi).astype(BF16)
        h_hi = h_hi.astype(BF16)
        logits = _dot(h_hi, rt_ref[0]) + (_dot(h_lo, rt_ref[0]) + _dot(h_hi, rt_ref[1]))
        logits = jnp.where(lane < N_EXPERTS, logits, -jnp.inf)
        v1 = jnp.max(logits, axis=-1, keepdims=True)
        i1 = jnp.min(jnp.where(logits == v1, lane, LANES), axis=-1, keepdims=True)
        l2 = jnp.where(lane == i1, -jnp.inf, logits)
        v2 = jnp.max(l2, axis=-1, keepdims=True)
        i2 = jnp.min(jnp.where(l2 == v2, lane, LANES), axis=-1, keepdims=True)
        e2 = jnp.exp(v2 - v1)
        p1 = 1.0 / (1.0 + e2)
        p2 = e2 / (1.0 + e2)
        gate_ref[0, r0:r1, 0:LANES] = jnp.where(lane == i1, 1.0, 0.0)
        gate_ref[0, r0:r1, LANES:2 * LANES] = jnp.where(lane == i2, 1.0, 0.0)
        gate_ref[0, r0:r1, 2 * LANES:3 * LANES] = jnp.broadcast_to(p1, (rg, LANES))
        gate_ref[0, r0:r1, 3 * LANES:4 * LANES] = jnp.broadcast_to(p2, (rg, LANES))

    lhs = [mixers(r0, r1) for r0, r1 in groups]
    yys = [_dot(a, wo_ref[...]) for a in lhs]
    h2s = [norm(r0, r1, yy) for (r0, r1), yy in zip(groups, yys)]
    for (r0, r1), h2 in zip(groups, h2s):
        if moe:
            route(r0, r1, h2)
        else:
            h_ref[0, r0:r1, :] = h2.astype(BF16)


def _outproj(o_a, o_d, proj, x, wo, pool_w, pool_scale, conv_w, nw, mods, mod_row, router=None):
    b, s, d = x.shape
    tm = min(s, 512)
    moe = router is not None
    row = mod_row
    nh = s // HALO
    th = tm // HALO

    def mspec(k):
        return pl.BlockSpec((1, 1, 1, d), lambda bi, m: (row(bi), k, 0, 0))

    gspec = pl.BlockSpec((1, tm, GROUP_W), lambda bi, m: (bi, m, 0))
    xspec = pl.BlockSpec((1, tm, d), lambda bi, m: (bi, m, 0))
    in_specs = [
        gspec, gspec,
        pl.BlockSpec((1, tm, 4 * GROUP_W), lambda bi, m: (bi, m, 0)),
        pl.BlockSpec((1, HALO, 4 * GROUP_W), lambda bi, m: (bi, jnp.maximum(m * th - 1, 0), 0)),
        pl.BlockSpec((1, HALO, 4 * GROUP_W), lambda bi, m: (bi, jnp.minimum((m + 1) * th, nh - 1), 0)),
        pl.BlockSpec((4, POOL_CH, POOL_CH), lambda bi, m: (0, 0, 0)),
        pl.BlockSpec((1, GROUP_W), lambda bi, m: (0, 0)),
        pl.BlockSpec((3, GROUP_W), lambda bi, m: (0, 0)),
        pl.BlockSpec((d, d), lambda bi, m: (0, 0), pipeline_mode=pl.Buffered(1)),
        xspec, mspec(2),
        pl.BlockSpec((1, d), lambda bi, m: (0, 0)),
        mspec(3), mspec(4),
    ]
    args = [o_a, o_d, proj, proj, proj, pool_w.astype(BF16), pool_scale.reshape(1, GROUP_W), conv_w, wo,
            x, mods, nw.reshape(1, d), mods, mods]
    if moe:
        nm = s // tm
        in_specs.append(pl.BlockSpec((2, d, LANES), lambda bi, m: (0, 0, 0)))
        r_pad = jnp.pad(router, ((0, 0), (0, LANES - router.shape[1])))
        r_hi = _unpack_hi(lax.bitcast_convert_type(r_pad, jnp.uint32))
        args.append(jnp.stack([r_hi, r_pad - r_hi]).astype(BF16))
        out_shape = [jax.ShapeDtypeStruct((b, s, d), F32),
                     jax.ShapeDtypeStruct((PACK_PLANES, b * s, PLANE_W), jnp.uint32),
                     jax.ShapeDtypeStruct((b, s, 4 * LANES), F32)]
        out_specs = [xspec,
                     pl.BlockSpec((PACK_PLANES, tm, PLANE_W), lambda bi, m: (0, bi * nm + m, 0)),
                     pl.BlockSpec((1, tm, 4 * LANES), lambda bi, m: (bi, m, 0))]
    else:
        out_shape = [jax.ShapeDtypeStruct((b, s, d), F32), jax.ShapeDtypeStruct((b, s, d), BF16)]
        out_specs = [xspec, xspec]
    return pl.pallas_call(
        functools.partial(_outproj_kernel, tm=tm, moe=moe),
        grid=(b, s // tm),
        in_specs=in_specs,
        out_specs=out_specs,
        out_shape=out_shape,
        scratch_shapes=[pltpu.VMEM((tm + 2 * HALO, GROUP_W), F32), pltpu.VMEM((tm + 2 * HALO, GROUP_W), F32)],
        compiler_params=_cparams(("arbitrary", "arbitrary")),
        name="outproj",
    )(*args)


def _swiglu_accumulate(h_ref, wg_ref, wu_ref, wd_ref, acc_ref, rows=None):
    r = slice(0, h_ref.shape[0] if rows is None else rows)
    h = h_ref[r, :]
    a = _dot(h, wg_ref[0].astype(BF16))
    u = _dot(h, wu_ref[0].astype(BF16))
    act = a * (1.0 / (1.0 + jnp.exp(-a))) * u
    acc_ref[r, :] += _dot(act.astype(BF16), wd_ref[0].astype(BF16))


def _ffn_kernel(h_ref, wg_ref, wu_ref, wd_ref, o_ref, acc_ref):
    f = pl.program_id(1)

    @pl.when(f == 0)
    def _():
        acc_ref[...] = jnp.zeros(acc_ref.shape, F32)

    _swiglu_accumulate(h_ref, wg_ref, wu_ref, wd_ref, acc_ref)

    @pl.when(f == pl.num_programs(1) - 1)
    def _():
        o_ref[...] = acc_ref[...].astype(BF16)


def _ffn(h, wg, wu, wd, layer):
    mt, d = h.shape
    ff = wg.shape[2]
    tm = min(mt, 1024)
    tf = 256
    return pl.pallas_call(
        _ffn_kernel,
        grid=(mt // tm, ff // tf),
        in_specs=[
            pl.BlockSpec((tm, d), lambda m, f: (m, 0)),
            pl.BlockSpec((1, d, tf), lambda m, f: (layer, 0, f)),
            pl.BlockSpec((1, d, tf), lambda m, f: (layer, 0, f)),
            pl.BlockSpec((1, tf, d), lambda m, f: (layer, f, 0)),
        ],
        out_specs=pl.BlockSpec((tm, d), lambda m, f: (m, 0)),
        out_shape=jax.ShapeDtypeStruct((mt, d), BF16),
        scratch_shapes=[pltpu.VMEM((tm, d), F32)],
        compiler_params=_cparams(("arbitrary", "arbitrary")),
        name="ffn",
    )(h, wg, wu, wd)


def _rank_kernel(r_ref, o_ref, cnt_ref, carry):
    i = pl.program_id(0)
    tm = r_ref.shape[0]

    @pl.when(i == 0)
    def _():
        carry[...] = jnp.zeros(carry.shape, F32)

    oh1 = r_ref[:, 0:LANES]
    oh2 = r_ref[:, LANES:2 * LANES]
    sel = oh1 + oh2
    row = lax.broadcasted_iota(jnp.int32, (tm, tm), 0)
    col = lax.broadcasted_iota(jnp.int32, (tm, tm), 1)
    tri = jnp.where(col < row, 1.0, 0.0).astype(BF16)
    excl = _dot(tri, sel.astype(BF16)) + carry[0:1, :]
    lane = lax.broadcasted_iota(jnp.int32, (tm, LANES), 1)
    lane_f = lane.astype(F32)
    e1 = jnp.sum(oh1 * lane_f, axis=-1, keepdims=True)
    r1 = jnp.sum(oh1 * excl, axis=-1, keepdims=True)
    e2 = jnp.sum(oh2 * lane_f, axis=-1, keepdims=True)
    r2 = jnp.sum(oh2 * excl, axis=-1, keepdims=True)
    meta = jnp.where(lane == 0, e1, jnp.where(lane == 1, r1, jnp.where(lane == 2, e2,
                     jnp.where(lane == 3, r2, 0.0))))
    o_ref[...] = meta.T[0:SUBLANES, :]
    carry[...] = carry[...] + jnp.sum(sel, axis=0, keepdims=True)
    cnt_ref[...] = carry[...]


def _rank(route):
    t = route.shape[0]
    tm = min(t, 512)
    return pl.pallas_call(
        _rank_kernel,
        grid=(t // tm,),
        in_specs=[pl.BlockSpec((tm, 2 * LANES), lambda i: (i, 0))],
        out_specs=[pl.BlockSpec((SUBLANES, tm), lambda i: (0, i)),
                   pl.BlockSpec((SUBLANES, LANES), lambda i: (0, 0))],
        out_shape=[jax.ShapeDtypeStruct((SUBLANES, t), F32), jax.ShapeDtypeStruct((SUBLANES, LANES), F32)],
        scratch_shapes=[pltpu.VMEM((SUBLANES, LANES), F32)],
        compiler_params=_cparams(("arbitrary",)),
        name="rank",
    )(route)


def _sc_mesh():
    return plsc.VectorSubcoreMesh(core_axis_name="core", subcore_axis_name="subcore")


def _sc_scatter2(x, idx1, idx2, n_out):
    n, w = x.shape

    @functools.partial(pl.kernel, out_type=jax.ShapeDtypeStruct((n_out, w), x.dtype), mesh=_sc_mesh(),
                       scratch_types=[], name="sc_dispatch")
    def k(x_hbm, i1_hbm, i2_hbm, o_hbm):
        def body(x_vmem, i1_vmem, i2_vmem):
            pltpu.sync_copy(x_vmem, o_hbm.at[i1_vmem.at[0]])
            pltpu.sync_copy(x_vmem, o_hbm.at[i2_vmem.at[0]])

        pltpu.emit_pipeline(
            body, grid=(n // SC_WINDOW,),
            in_specs=[pl.BlockSpec((SC_WINDOW, w), lambda i: (i, 0)),
                      pl.BlockSpec((1, SC_WINDOW), lambda i: (0, i)),
                      pl.BlockSpec((1, SC_WINDOW), lambda i: (0, i))],
            out_specs=[], core_axis_name=("core", "subcore"),
            dimension_semantics=(pltpu.PARALLEL,))(x_hbm, i1_hbm, i2_hbm)

    return k(x, idx1.reshape(1, n), idx2.reshape(1, n))


def _sc_gather(y, idx):
    (n,) = idx.shape
    w = y.shape[1]

    @functools.partial(pl.kernel, out_type=jax.ShapeDtypeStruct((n, w), y.dtype), mesh=_sc_mesh(),
                       scratch_types=[], name="sc_combine")
    def k(y_hbm, i_hbm, o_hbm):
        def body(i_vmem, o_vmem):
            pltpu.sync_copy(y_hbm.at[i_vmem.at[0]], o_vmem)

        pltpu.emit_pipeline(
            body, grid=(n // SC_WINDOW,),
            in_specs=[pl.BlockSpec((1, SC_WINDOW), lambda i: (0, i))],
            out_specs=[pl.BlockSpec((SC_WINDOW, w), lambda i: (i, 0))],
            core_axis_name=("core", "subcore"),
            dimension_semantics=(pltpu.PARALLEL,))(i_hbm, o_hbm)

    return k(y, idx.reshape(1, n))


MOE_SUB = 256


def _moe_kernel(te_ref, nu_ref, tv_ref, h_ref, wg_ref, wu_ref, wd_ref, o_ref, acc_ref, hb_ref):
    m = pl.program_id(0)
    f = pl.program_id(1)
    tm = hb_ref.shape[0]
    half = hb_ref.shape[1] // 2

    @pl.when(m < nu_ref[0])
    def _():
        @pl.when(f == 0)
        def _():
            acc_ref[...] = jnp.zeros(acc_ref.shape, F32)
            routed = lax.broadcasted_iota(jnp.int32, (tm, PLANE_W), 0) < tv_ref[m]
            for c in range(PACK_PLANES):
                w = jnp.where(routed, h_ref[c], jnp.uint32(0))
                hb_ref[:, c * PLANE_W:(c + 1) * PLANE_W] = _unpack_hi(w).astype(BF16)
                hb_ref[:, half + c * PLANE_W:half + (c + 1) * PLANE_W] = _unpack_lo(w).astype(BF16)

        n_sub = (tv_ref[m] + MOE_SUB - 1) // MOE_SUB
        for k in range(1, tm // MOE_SUB + 1):
            @pl.when(n_sub == k)
            def _():
                _swiglu_accumulate(hb_ref, wg_ref, wu_ref, wd_ref, acc_ref, k * MOE_SUB)

        @pl.when(f == pl.num_programs(1) - 1)
        def _():
            packed = _pack_bf16_pairs(acc_ref[...])
            for c in range(PACK_PLANES):
                o_ref[c] = packed[:, c * PLANE_W:(c + 1) * PLANE_W]


MOE_TM = 1024


def _moe_routed(hp, tile_expert, n_used, tile_valid, wg, wu, wd, layer, tm):
    _, r, _ = hp.shape
    _, ne, d, ff = wg.shape
    wg = wg.reshape(-1, d, ff)
    wu = wu.reshape(-1, d, ff)
    wd = wd.reshape(-1, ff, d)
    e0 = layer * ne
    tf = 256
    nf = ff // tf

    def row_map(m, f, te, nu, tv):
        return (0, jnp.minimum(m, nu[0] - 1), 0)

    def fidx(m, f, nu):
        return jnp.where(m < nu[0], f, nf - 1)

    hspec = pl.BlockSpec((PACK_PLANES, tm, PLANE_W), row_map)
    return pl.pallas_call(
        _moe_kernel,
        grid_spec=pltpu.PrefetchScalarGridSpec(
            num_scalar_prefetch=3,
            grid=(r // tm, nf),
            in_specs=[
                hspec,
                pl.BlockSpec((1, d, tf), lambda m, f, te, nu, tv: (e0 + te[m], 0, fidx(m, f, nu))),
                pl.BlockSpec((1, d, tf), lambda m, f, te, nu, tv: (e0 + te[m], 0, fidx(m, f, nu))),
                pl.BlockSpec((1, tf, d), lambda m, f, te, nu, tv: (e0 + te[m], fidx(m, f, nu), 0)),
            ],
            out_specs=hspec,
            scratch_shapes=[pltpu.VMEM((tm, d), F32), pltpu.VMEM((tm, d), BF16)],
        ),
        out_shape=jax.ShapeDtypeStruct(hp.shape, jnp.uint32),
        compiler_params=_cparams(("arbitrary", "arbitrary")),
        name="moe",
    )(tile_expert, n_used, tile_valid, hp, wg, wu, wd)


def _route_plan(meta, counts, tm, n_tiles):
    e1, r1, e2, r2 = (meta[k].astype(jnp.int32) for k in range(4))
    cnt = counts[0, :N_EXPERTS].astype(jnp.int32)
    tiles_per = (cnt + tm - 1) // tm
    tile_end = jnp.cumsum(tiles_per)
    start_row = (tile_end - tiles_per) * tm
    pos1 = start_row[e1] + r1
    pos2 = start_row[e2] + r2
    n_used = tile_end[-1]
    tiles = jnp.arange(n_tiles, dtype=jnp.int32)
    tile_expert = jnp.sum((tiles[:, None] >= tile_end[None, :]).astype(jnp.int32), axis=1)
    last_expert = jnp.sum((n_used - 1 >= tile_end).astype(jnp.int32))
    tile_expert = jnp.minimum(tile_expert, last_expert)
    tile_start = (tile_end - tiles_per)[tile_expert]
    tile_valid = jnp.clip(cnt[tile_expert] - (tiles - tile_start) * tm, 0, tm)
    return pos1, pos2, tile_expert, n_used.reshape(1), tile_valid


def _plane_rows(pos, n_rows):
    return (jnp.arange(PACK_PLANES, dtype=jnp.int32)[:, None] * n_rows + pos[None, :]).reshape(-1)


def _moe(hp, route, wg, wu, wd, layer):
    _, t, _ = hp.shape
    tm = min(MOE_TM, t)
    n_tiles = 2 * t // tm + N_EXPERTS
    n_rows = n_tiles * tm
    meta, counts = _rank(route)
    pos1, pos2, tile_expert, n_used, tile_valid = _route_plan(meta, counts, tm, n_tiles)
    i1 = _plane_rows(pos1, n_rows)
    i2 = _plane_rows(pos2, n_rows)
    hs = _sc_scatter2(hp.reshape(PACK_PLANES * t, PLANE_W), i1, i2, PACK_PLANES * n_rows)
    ys = _moe_routed(hs.reshape(PACK_PLANES, n_rows, PLANE_W), tile_expert, n_used, tile_valid,
                     wg, wu, wd, layer, tm)
    ys = ys.reshape(PACK_PLANES * n_rows, PLANE_W)
    y1 = _sc_gather(ys, i1).reshape(PACK_PLANES, t, PLANE_W)
    y2 = _sc_gather(ys, i2).reshape(PACK_PLANES, t, PLANE_W)
    return y1, y2


def _final_kernel(x_ref, f_ref, g_ref, nw_ref, o_ref):
    xv = x_ref[0] + g_ref[0, 0] * f_ref[0].astype(F32)
    ms = jnp.mean(xv * xv, axis=-1, keepdims=True)
    o_ref[0] = xv * lax.rsqrt(ms + EPS) * nw_ref[...]


def _final(x, f, mods, nw):
    b, s, d = x.shape
    tm = min(s, 512)
    xspec = pl.BlockSpec((1, tm, d), lambda bi, m: (bi, m, 0))
    return pl.pallas_call(
        _final_kernel,
        grid=(b, s // tm),
        in_specs=[xspec, xspec,
                  pl.BlockSpec((1, 1, 1, d), lambda bi, m: (bi, 5, 0, 0)),
                  pl.BlockSpec((1, d), lambda bi, m: (0, 0))],
        out_specs=xspec,
        out_shape=jax.ShapeDtypeStruct((b, s, d), F32),
        compiler_params=_cparams(("arbitrary", "arbitrary")),
        name="final",
    )(x, f, mods, nw.reshape(1, d))


def _final_moe_kernel(x_ref, y1_ref, y2_ref, p_ref, g_ref, nw_ref, o_ref):
    tm, d = x_ref.shape[1], x_ref.shape[2]
    half = d // 2
    p = p_ref[0]
    p1 = jnp.concatenate([p[:, 0:LANES]] * (PLANE_W // LANES), axis=1)
    p2 = jnp.concatenate([p[:, LANES:2 * LANES]] * (PLANE_W // LANES), axis=1)
    ssq = jnp.zeros((tm, 1), F32)
    for c in range(PACK_PLANES):
        w1 = y1_ref[c]
        w2 = y2_ref[c]
        for unpack, off in ((_unpack_hi, 0), (_unpack_lo, half)):
            sl = slice(off + c * PLANE_W, off + (c + 1) * PLANE_W)
            f = p1 * unpack(w1) + p2 * unpack(w2)
            xv = x_ref[0, :, sl] + g_ref[0, 0, :, sl] * f
            o_ref[0, :, sl] = xv
            ssq = ssq + jnp.sum(xv * xv, axis=-1, keepdims=True)
    o_ref[0] = o_ref[0] * lax.rsqrt(ssq / d + EPS) * nw_ref[...]


def _final_moe(x, y1, y2, route, mods, nw):
    b, s, d = x.shape
    tm = min(s, 512)
    nm = s // tm
    xspec = pl.BlockSpec((1, tm, d), lambda bi, m: (bi, m, 0))
    yspec = pl.BlockSpec((PACK_PLANES, tm, PLANE_W), lambda bi, m: (0, bi * nm + m, 0))
    return pl.pallas_call(
        _final_moe_kernel,
        grid=(b, nm),
        in_specs=[xspec, yspec, yspec,
                  pl.BlockSpec((1, tm, 2 * LANES), lambda bi, m: (bi, m, 1)),
                  pl.BlockSpec((1, 1, 1, d), lambda bi, m: (bi, 5, 0, 0)),
                  pl.BlockSpec((1, d), lambda bi, m: (0, 0))],
        out_specs=xspec,
        out_shape=jax.ShapeDtypeStruct((b, s, d), F32),
        compiler_params=_cparams(("arbitrary", "arbitrary")),
        name="final",
    )(x, y1, y2, route, mods, nw.reshape(1, d))


PREP_W = 2 * LANES


def _w_in_plan():
    nb = GROUP_W // PREP_W
    plan = [(3 * nb + j, 0, 0) for j in range(4 * nb)]
    plan += [(j, 1, 0) for j in range(nb)]
    plan += [(nb + j, 0, 0) for j in range(2 * nb)]
    plan += [(7 * nb + j, 1, 0) for j in range(nb)]
    plan += [(8 * nb, 0, 1), (8 * nb, 0, 2)]
    assert len(plan) * PREP_W == D_PROJ
    return jnp.asarray(plan, jnp.int32).T.reshape(-1)


def _prep_w_kernel(plan_ref, w_ref, o_ref):
    j = pl.program_id(1)
    nblk = pl.num_programs(1)
    x = w_ref[0]
    mode = plan_ref[2 * nblk + j]
    src = jnp.where(mode == 2, x[:, LANES:], x[:, :LANES])
    lo = lax.broadcasted_iota(jnp.int32, src.shape, 1) < HEAD_DIM
    other = pltpu.roll(src, HEAD_DIM, 1)
    dup = jnp.concatenate([jnp.where(lo, src, other), jnp.where(lo, other, src)], axis=1)
    scale = jnp.where(plan_ref[nblk + j] == 1, HEAD_DIM ** -0.5, 1.0)
    o_ref[0] = (jnp.where(mode == 0, x, dup) * scale).astype(BF16)


def _prep_w_in(w_in):
    depth, d, _ = w_in.shape
    nblk = D_PROJ // PREP_W
    return pl.pallas_call(
        _prep_w_kernel,
        grid_spec=pltpu.PrefetchScalarGridSpec(
            num_scalar_prefetch=1,
            grid=(depth, nblk),
            in_specs=[pl.BlockSpec((1, d, PREP_W), lambda i, j, plan: (i, 0, plan[j]))],
            out_specs=pl.BlockSpec((1, d, PREP_W), lambda i, j, plan: (i, 0, j)),
        ),
        out_shape=jax.ShapeDtypeStruct((depth, d, D_PROJ), BF16),
        compiler_params=_cparams(("arbitrary", "arbitrary")),
        name="prep_w_in",
    )(_w_in_plan(), w_in)


def _rope_tables(n):
    t = jnp.arange(n, dtype=jnp.int32)
    rows = (t // GRID_W).astype(F32)
    cols = (t % GRID_W).astype(F32)
    nf = HEAD_DIM // 4
    inv = ROPE_BASE ** (-jnp.arange(nf, dtype=F32) / nf)
    ang = jnp.concatenate([rows[:, None] * inv, cols[:, None] * inv], axis=-1)
    cos = jnp.cos(ang)
    sin = jnp.sin(ang)
    cos_t = jnp.concatenate([cos, cos, cos, cos], axis=-1)
    sin_t = jnp.concatenate([-sin, sin, -sin, sin], axis=-1)
    return cos_t, sin_t


def kernel(x, c, ctx, c_ctx, w_ada, b_ada, norm_mix, norm_ffn, norm_final, w_in, w_out, na_rpb, pool_w, pool_scale, conv_w, swa_sink, ffn_w_gate, ffn_w_up, ffn_w_down, moe_router, moe_w_gate, moe_w_up, moe_w_down):
    b, s, d = x.shape
    lc = ctx.shape[1]
    depth = w_ada.shape[0]
    cond = jnp.concatenate([c, c_ctx[None, :]], axis=0)
    assert b == 2
    mods_all = _ada(cond, w_ada, b_ada)
    cos_t, sin_t = _rope_tables(s)
    cos_c = jnp.ones((lc, LANES), F32)
    sin_c = jnp.zeros((lc, LANES), F32)
    row_x = lambda bi: bi
    row_c = lambda bi: b

    w_proj = _prep_w_in(w_in)
    cx = ctx
    fx = fc = None
    for i in range(depth):
        last = i == depth - 1
        mods = mods_all[i].reshape(SUBLANES, 6, 1, d)
        wi = (w_proj, i)
        wo = w_out[i].astype(BF16)
        if i == 0:
            proj, _ = _inproj(x, wi, norm_mix[i], mods, row_x, cos_t, sin_t)
            projc, _ = _inproj(cx, wi, norm_mix[i], mods, row_c, cos_c, sin_c)
        else:
            mods_prev = mods_all[i - 1].reshape(SUBLANES, 6, 1, d)
            proj, x = _inproj_res(x, fx, mods_prev, wi, norm_mix[i], mods, row_x, cos_t, sin_t)
            projc, cx = _inproj_res(cx, fc, mods_prev, wi, norm_mix[i], mods, row_c, cos_c, sin_c)
        bias = _na_bias(na_rpb[i])
        o_a = _na_attn(proj, projc, bias)
        o_d = _swa_attn(proj, projc, swa_sink[i])
        moe = i % 2 == 1
        router = moe_router[i // 2] if moe else None
        outs = _outproj(o_a, o_d, proj, x, wo, pool_w[i], pool_scale[i], conv_w[i], norm_ffn[i], mods, row_x,
                        router)
        x, h2 = outs[0], outs[1]
        if not last:
            oc_a, oc_d = _ctx_attn(projc, swa_sink[i])
            cx, h2c = _outproj(oc_a, oc_d, projc, cx, wo, pool_w[i], pool_scale[i], conv_w[i], norm_ffn[i],
                               mods, row_c)
        j = i // 2
        if moe:
            if not last:
                raise NotImplementedError("an expert layer that is not the last layer")
            route = outs[2]
            y1, y2 = _moe(h2, route.reshape(b * s, 4 * LANES), moe_w_gate, moe_w_up, moe_w_down, j)
            return _final_moe(x, y1, y2, route, mods, norm_final)
        fx = _ffn(h2.reshape(b * s, d), ffn_w_gate, ffn_w_up, ffn_w_down, j).reshape(b, s, d)
        if not last:
            fc = _ffn(h2c.reshape(b * lc, d), ffn_w_gate, ffn_w_up, ffn_w_down, j).reshape(b, lc, d)
    return _final(x, fx, mods_all[depth - 1].reshape(SUBLANES, 6, 1, d), norm_final)


def _inproj_res(x, f, mods_prev, w, nw, mods, mod_row, cos_t, sin_t):
    both = jnp.concatenate([mods, mods_prev], axis=1)
    return _inproj(x, w, nw, both, mod_row, cos_t, sin_t, res=(f, 6 + 5))
```

```python
import functools

import jax
import jax.numpy as jnp
from jax import lax
from jax.experimental import pallas as pl
from jax.experimental.pallas import tpu as pltpu
from jax.experimental.pallas import tpu_sc as plsc

F32 = jnp.float32
BF16 = jnp.bfloat16

D_MODEL = 2048
GRID_W = 64
HEAD_DIM = 64
EPS = 1e-6
NEG_INF = -1e30
GROUP_W = 512
NA_KH = 8
NA_KW = 16
POOL_WINDOWS = (2, 4, 8, 16)
POOL_CH = 128
SWA_WINDOW = 128
SWA_BLOCK = 128
ROPE_BASE = 10000.0
N_EXPERTS = 8
LANES = 128
SUBLANES = 8
HALO = 16

C_UBC = 0
C_QA = 2048
C_KA = 2560
C_VA = 3072
C_QD = 3584
C_KD = 4096
C_VD = 4352
D_PROJ = 4608
TN_PROJ = 512

VMEM_LIMIT = 56 * 1024 * 1024
VMEM_LIMIT_MOE = 58 * 1024 * 1024


def _cparams(sem, vmem_limit=VMEM_LIMIT):
    return pltpu.CompilerParams(dimension_semantics=sem, vmem_limit_bytes=vmem_limit)


def _dot(a, b):
    return jnp.dot(a, b, preferred_element_type=F32)


def _dot_t(a, b):
    return lax.dot_general(a, b, (((1,), (1,)), ((), ())), preferred_element_type=F32)


def _rms_mod(xv, nw, sh, sc):
    ms = jnp.mean(xv * xv, axis=-1, keepdims=True)
    return (xv * lax.rsqrt(ms + EPS)) * (nw * (1.0 + sc)) + sh


PACK_PLANES = 4
PLANE_W = D_MODEL // 2 // PACK_PLANES
SC_WINDOW = 128


def _pack_bf16_pairs(v):
    half = v.shape[1] // 2
    hi = lax.bitcast_convert_type(v[:, :half].astype(BF16).astype(F32), jnp.uint32)
    lo = lax.bitcast_convert_type(v[:, half:].astype(BF16).astype(F32), jnp.uint32)
    return hi | (lo >> 16)


def _unpack_hi(w):
    return lax.bitcast_convert_type(w & jnp.uint32(0xFFFF0000), F32)


def _unpack_lo(w):
    return lax.bitcast_convert_type(w << 16, F32)


def _ada_kernel(c_ref, w_ref, b_ref, o_ref, silu_scr):
    tn = w_ref.shape[2]

    @pl.when((pl.program_id(0) == 0) & (pl.program_id(1) == 0))
    def _():
        cv = c_ref[...]
        silu_scr[...] = cv * (1.0 / (1.0 + jnp.exp(-cv)))

    o_ref[...] = jnp.zeros(o_ref.shape, F32)
    for r in range(3):
        m = silu_scr[r]
        cols = []
        for j in range(tn // LANES):
            wj = w_ref[0, :, j * LANES:(j + 1) * LANES]
            cols.append(jnp.sum(wj * m, axis=0, keepdims=True))
        o_ref[0, r:r + 1, :] = jnp.concatenate(cols, axis=1) + b_ref[0]


def _ada(cond, w_ada, b_ada):
    depth, d, n6 = w_ada.shape
    tn = 2048
    cb = jnp.broadcast_to(cond[:, :, None], (3, d, LANES))
    return pl.pallas_call(
        _ada_kernel,
        grid=(depth, n6 // tn),
        in_specs=[
            pl.BlockSpec((3, d, LANES), lambda i, j: (0, 0, 0)),
            pl.BlockSpec((1, d, tn), lambda i, j: (i, 0, j)),
            pl.BlockSpec((1, 1, tn), lambda i, j: (i, 0, j)),
        ],
        out_specs=pl.BlockSpec((1, SUBLANES, tn), lambda i, j: (i, 0, j)),
        out_shape=jax.ShapeDtypeStruct((depth, SUBLANES, n6), F32),
        scratch_shapes=[pltpu.VMEM((3, d, LANES), F32)],
        compiler_params=_cparams(("arbitrary", "arbitrary")),
        name="ada",
    )(cb, w_ada, b_ada.reshape(depth, 1, n6))


def _rope(a, cosv, sinv, lo32):
    sw = jnp.where(lo32, pltpu.roll(a, 96, 1), pltpu.roll(a, 32, 1))
    return a * cosv + sw * sinv


def _inproj_kernel(*refs, has_res):
    if has_res:
        x_ref, f_ref, g_ref, nw_ref, sh_ref, sc_ref, cos_ref, sin_ref, w_ref, o_ref, x2_ref = refs
    else:
        x_ref, nw_ref, sh_ref, sc_ref, cos_ref, sin_ref, w_ref, o_ref = refs
    xv = x_ref[0]
    if has_res:
        xv = xv + g_ref[0, 0] * f_ref[0].astype(F32)
        x2_ref[0] = xv
    h = _rms_mod(xv, nw_ref[...], sh_ref[0, 0], sc_ref[0, 0]).astype(BF16)
    tm = h.shape[0]
    lane = lax.broadcasted_iota(jnp.int32, (tm, LANES), 1)
    lo32 = (lane % HEAD_DIM) < (HEAD_DIM // 2)
    rope_end = C_VD
    for n in range(D_PROJ // TN_PROJ):
        c0 = n * TN_PROJ
        acc = _dot(h, w_ref[0, :, c0:c0 + TN_PROJ])
        if c0 + TN_PROJ <= C_QD:
            o_ref[0, :, c0:c0 + TN_PROJ] = acc.astype(BF16)
            continue
        for g in range(TN_PROJ // LANES):
            a = acc[:, g * LANES:(g + 1) * LANES]
            if c0 + g * LANES < rope_end:
                a = _rope(a, cos_ref[...], sin_ref[...], lo32)
            o_ref[0, :, c0 + g * LANES:c0 + (g + 1) * LANES] = a.astype(BF16)


def _inproj(x, w, nw, mods, mod_row, cos_t, sin_t, res=None):
    b, s, d = x.shape
    w_all, layer = w
    tm = min(s, 512)
    has_res = res is not None
    row = mod_row
    xspec = pl.BlockSpec((1, tm, d), lambda bi, m: (bi, m, 0))

    def mspec(k):
        return pl.BlockSpec((1, 1, 1, d), lambda bi, m: (row(bi), k, 0, 0))

    in_specs = [xspec]
    args = [x]
    if has_res:
        f, gk = res
        in_specs += [xspec, mspec(gk)]
        args += [f, mods]
    in_specs += [
        pl.BlockSpec((1, d), lambda bi, m: (0, 0)),
        mspec(0), mspec(1),
        pl.BlockSpec((tm, LANES), lambda bi, m: (m, 0)),
        pl.BlockSpec((tm, LANES), lambda bi, m: (m, 0)),
        pl.BlockSpec((1, d, D_PROJ), lambda bi, m: (layer, 0, 0), pipeline_mode=pl.Buffered(1)),
    ]
    args += [nw.reshape(1, d), mods, mods, cos_t, sin_t, w_all]
    out_shape = [jax.ShapeDtypeStruct((b, s, D_PROJ), BF16)]
    out_specs = [pl.BlockSpec((1, tm, D_PROJ), lambda bi, m: (bi, m, 0))]
    if has_res:
        out_shape.append(jax.ShapeDtypeStruct((b, s, d), F32))
        out_specs.append(xspec)
    outs = pl.pallas_call(
        functools.partial(_inproj_kernel, has_res=has_res),
        grid=(b, s // tm),
        in_specs=in_specs,
        out_specs=out_specs,
        out_shape=out_shape,
        compiler_params=_cparams(("arbitrary", "arbitrary")),
        name="inproj",
    )(*args)
    return outs if has_res else (outs[0], None)


def _na_bias_kernel(rpb_ref, o_ref):
    h = pl.program_id(0)
    nd = 2 * NA_KH - 1
    nj = 2 * NA_KW - 1
    q = lax.broadcasted_iota(jnp.int32, (GRID_W, LANES), 0)
    lane = lax.broadcasted_iota(jnp.int32, (GRID_W, LANES), 1)
    kw = lane % GRID_W
    dc = jnp.clip(kw - q, -(NA_KW - 1), NA_KW - 1) + (NA_KW - 1)
    c0 = jnp.clip(q - NA_KW // 2, 0, GRID_W - NA_KW)
    valid = (kw >= c0) & (kw < c0 + NA_KW)
    tabs = []
    for d in range(nd):
        t = jnp.zeros((GRID_W, LANES), F32)
        for j in range(nj):
            t = jnp.where(dc == j, rpb_ref[h * (nd * nj) + d * nj + j], t)
        tabs.append(jnp.where(valid, t, NEG_INF))
    for c in range(NA_KH):
        for g in range(NA_KH // 2):
            d_lo = 2 * g - c + (NA_KH - 1)
            o_ref[c, 0, :, g * LANES:(g + 1) * LANES] = jnp.where(lane < GRID_W, tabs[d_lo], tabs[d_lo + 1])


def _na_bias(rpb):
    nh = rpb.shape[0]
    return pl.pallas_call(
        _na_bias_kernel,
        grid=(nh,),
        in_specs=[pl.BlockSpec(memory_space=pltpu.SMEM)],
        out_specs=pl.BlockSpec((NA_KH, 1, GRID_W, NA_KH * GRID_W), lambda h: (0, h, 0, 0)),
        out_shape=jax.ShapeDtypeStruct((NA_KH, nh, GRID_W, NA_KH * GRID_W), F32),
        compiler_params=_cparams(("arbitrary",)),
        name="na_bias",
    )(rpb.reshape(-1))


def _softmax_pv_staged(chains):
    add = lambda a, b: a + b
    ms = []
    for sp, _, extra in chains:
        m = functools.reduce(jnp.maximum, [jnp.max(s, axis=-1, keepdims=True) for s in sp])
        ms.append(m if extra is None else jnp.maximum(m, extra))
    ps = [[jnp.exp(s - m) for s in sp] for (sp, _, _), m in zip(chains, ms)]
    dens = []
    for pp, (_, _, extra), m in zip(ps, chains, ms):
        den = functools.reduce(add, [jnp.sum(p, axis=-1, keepdims=True) for p in pp])
        dens.append(den if extra is None else den + jnp.exp(extra - m))
    outs = [functools.reduce(add, [_dot(p.astype(BF16), v) for p, v in zip(pp, vp)])
            for pp, (_, vp, _) in zip(ps, chains)]
    return [o / d for o, d in zip(outs, dens)]


NA_MAX_ROWS_PER_STEP = 32
NA_ROW_GROUP = 4


def _na_kernel(q_ref, k_ref, v_ref, kc_ref, vc_ref, bias_ref, o_ref, *, rows):
    rb = pl.program_id(2)
    rows_per_step = q_ref.shape[1] // GRID_W
    lo = lax.broadcasted_iota(jnp.int32, (NA_ROW_GROUP * GRID_W, LANES), 1) < HEAD_DIM
    lo_row = lax.broadcasted_iota(jnp.int32, (GRID_W, LANES), 1) < HEAD_DIM
    kc = kc_ref[0]
    vc = vc_ref[0]
    kh = min(NA_KH, rows)
    win = kh * GRID_W

    g = NA_ROW_GROUP
    gq = g * GRID_W
    zero = jnp.zeros((gq, LANES), BF16)
    for i0 in range(0, rows_per_step, g):
        q = q_ref[0, i0 * GRID_W:(i0 + g) * GRID_W, :]
        q2 = jnp.concatenate([jnp.where(lo, q, zero), jnp.where(lo, zero, q)], axis=0)
        s_ctx = _dot_t(q2, kc)
        pieces = [(hh, j, slice(hh * gq + j * GRID_W, hh * gq + (j + 1) * GRID_W))
                  for hh in range(2) for j in range(g)]
        vws, s_loc = [], {}
        for j in range(g):
            r = rb * rows_per_step + i0 + j
            r0 = jnp.clip(r - kh // 2, 0, rows - kh)
            c = r - r0
            koff = pl.multiple_of(r0 * GRID_W, GRID_W)
            vws.append(v_ref[0, pl.ds(koff, win), :])
            qp = jnp.concatenate([q2[sl] for hh, jj, sl in pieces if jj == j], axis=0)
            sp = _dot_t(qp, k_ref[0, pl.ds(koff, win), :])
            for hh in range(2):
                s_loc[hh, j] = sp[hh * GRID_W:(hh + 1) * GRID_W] + bias_ref[c, hh]
        ms = {(hh, j): jnp.maximum(jnp.max(s_loc[hh, j], axis=-1, keepdims=True),
                                   jnp.max(s_ctx[sl], axis=-1, keepdims=True)) for hh, j, sl in pieces}
        p_loc = {(hh, j): jnp.exp(s_loc[hh, j] - ms[hh, j]) for hh, j, _ in pieces}
        p_ctx = {(hh, j): jnp.exp(s_ctx[sl] - ms[hh, j]) for hh, j, sl in pieces}
        den = {k: jnp.sum(p_loc[k], axis=-1, keepdims=True) + jnp.sum(p_ctx[k], axis=-1, keepdims=True)
               for k in p_loc}
        pv_ctx = _dot(jnp.concatenate([p_ctx[hh, j].astype(BF16) for hh, j, _ in pieces], axis=0), vc)
        for j in range(g):
            pv = _dot(jnp.concatenate([p_loc[0, j].astype(BF16), p_loc[1, j].astype(BF16)], axis=0), vws[j])
            o = [(pv[hh * GRID_W:(hh + 1) * GRID_W] + pv_ctx[hh * gq + j * GRID_W:hh * gq + (j + 1) * GRID_W])
                 / den[hh, j] for hh in range(2)]
            i = i0 + j
            o_ref[0, i * GRID_W:(i + 1) * GRID_W, :] = jnp.where(lo_row, o[0], o[1]).astype(BF16)


def _na_attn(proj, projc, bias):
    b, s, _ = proj.shape
    lc = projc.shape[1]
    rows = s // GRID_W
    rows_per_step = min(NA_MAX_ROWS_PER_STEP, rows)
    assert rows >= NA_KH and rows % rows_per_step == 0 and rows_per_step % NA_ROW_GROUP == 0
    tq = rows_per_step * GRID_W
    npair = GROUP_W // LANES
    return pl.pallas_call(
        functools.partial(_na_kernel, rows=rows),
        grid=(b, npair, rows // rows_per_step),
        in_specs=[
            pl.BlockSpec((1, tq, LANES), lambda bi, p, r: (bi, r, C_QA // LANES + p)),
            pl.BlockSpec((1, s, LANES), lambda bi, p, r: (bi, 0, C_KA // LANES + p)),
            pl.BlockSpec((1, s, LANES), lambda bi, p, r: (bi, 0, C_VA // LANES + p)),
            pl.BlockSpec((1, lc, LANES), lambda bi, p, r: (bi, 0, C_KA // LANES + p)),
            pl.BlockSpec((1, lc, LANES), lambda bi, p, r: (bi, 0, C_VA // LANES + p)),
            pl.BlockSpec((NA_KH, 2, GRID_W, NA_KH * GRID_W), lambda bi, p, r: (0, p, 0, 0)),
        ],
        out_specs=pl.BlockSpec((1, tq, LANES), lambda bi, p, r: (bi, r, p)),
        out_shape=jax.ShapeDtypeStruct((b, s, GROUP_W), BF16),
        compiler_params=_cparams(("arbitrary", "arbitrary", "arbitrary")),
        name="na_attn",
    )(proj, proj, proj, projc, projc, bias)


def _gqa_chains(q, kv, sink_ref, k_parts, v_parts, mask):
    m = q.shape[0]
    lane = lax.broadcasted_iota(jnp.int32, (m, LANES), 1)
    lo = lane < HEAD_DIM
    qs = []
    for j in range(2):
        qg = q[:, (kv * 2 + j) * LANES:(kv * 2 + j + 1) * LANES]
        qs.append(jnp.where(lo, qg, jnp.zeros_like(qg)))
        qs.append(jnp.where(lo, jnp.zeros_like(qg), qg))
    qq = jnp.concatenate(qs, axis=0)
    s_all = [_dot_t(qq, k) for k in k_parts]
    chains = []
    for i in range(4):
        parts = [s[i * m:(i + 1) * m] for s in s_all]
        if mask is not None:
            parts[0] = jnp.where(mask, parts[0], NEG_INF)
        chains.append((parts, v_parts, jnp.full((m, 1), sink_ref[kv * 4 + i], F32)))
    return chains


def _gqa_store(o_ref, heads, row0=0):
    m = heads[0].shape[0]
    lane = lax.broadcasted_iota(jnp.int32, heads[0].shape, 1)
    lo = lane < HEAD_DIM
    for g in range(len(heads) // 2):
        o_ref[0, row0:row0 + m, g * LANES:(g + 1) * LANES] = (
            jnp.where(lo, heads[2 * g], heads[2 * g + 1]).astype(BF16))


SWA_BLOCKS_PER_STEP = 4


def _swa_kernel(sink_ref, q_ref, k_ref, v_ref, kc_ref, vc_ref, o_ref):
    nb = pl.num_programs(1) * SWA_BLOCKS_PER_STEP
    kc = kc_ref[0]
    vc = vc_ref[0]
    qi = lax.broadcasted_iota(jnp.int32, (SWA_BLOCK, 3 * SWA_BLOCK), 0)
    kj = lax.broadcasted_iota(jnp.int32, (SWA_BLOCK, 3 * SWA_BLOCK), 1)
    for blk in range(SWA_BLOCKS_PER_STEP):
        n = pl.program_id(1) * SWA_BLOCKS_PER_STEP + blk
        start = jnp.clip(n - 1, 0, nb - 3)
        koff = pl.multiple_of(start * SWA_BLOCK, SWA_BLOCK)
        kw = k_ref[0, pl.ds(koff, 3 * SWA_BLOCK), :]
        vw = v_ref[0, pl.ds(koff, 3 * SWA_BLOCK), :]
        valid = jnp.abs((n - start) * SWA_BLOCK + qi - kj) <= SWA_WINDOW
        q = q_ref[0, blk * SWA_BLOCK:(blk + 1) * SWA_BLOCK, :]
        chains = []
        for kv in range(2):
            sl = slice(kv * LANES, (kv + 1) * LANES)
            chains += _gqa_chains(q, kv, sink_ref, [kw[:, sl], kc[:, sl]], [vw[:, sl], vc[:, sl]], valid)
        _gqa_store(o_ref, _softmax_pv_staged(chains), blk * SWA_BLOCK)


def _swa_attn(proj, projc, sink):
    b, s, _ = proj.shape
    lc = projc.shape[1]
    nb = s // SWA_BLOCK
    assert nb >= 3 and nb % SWA_BLOCKS_PER_STEP == 0
    w2 = 2 * LANES
    tq = SWA_BLOCKS_PER_STEP * SWA_BLOCK
    return pl.pallas_call(
        _swa_kernel,
        grid=(b, nb // SWA_BLOCKS_PER_STEP),
        in_specs=[
            pl.BlockSpec(memory_space=pltpu.SMEM),
            pl.BlockSpec((1, tq, GROUP_W), lambda bi, n: (bi, n, C_QD // GROUP_W)),
            pl.BlockSpec((1, s, w2), lambda bi, n: (bi, 0, C_KD // w2)),
            pl.BlockSpec((1, s, w2), lambda bi, n: (bi, 0, C_VD // w2)),
            pl.BlockSpec((1, lc, w2), lambda bi, n: (bi, 0, C_KD // w2)),
            pl.BlockSpec((1, lc, w2), lambda bi, n: (bi, 0, C_VD // w2)),
        ],
        out_specs=pl.BlockSpec((1, tq, GROUP_W), lambda bi, n: (bi, n, 0)),
        out_shape=jax.ShapeDtypeStruct((b, s, GROUP_W), BF16),
        compiler_params=_cparams(("arbitrary", "arbitrary")),
        name="swa_attn",
    )(sink.reshape(-1), proj, proj, proj, projc, projc)


def _ctx_attn_kernel(sink_ref, qa_ref, ka_ref, va_ref, qd_ref, kd_ref, vd_ref, oa_ref, od_ref):
    m = qa_ref.shape[1]
    lane = lax.broadcasted_iota(jnp.int32, (m, LANES), 1)
    lo = lane < HEAD_DIM
    chains = []
    for p in range(GROUP_W // LANES):
        sl = slice(p * LANES, (p + 1) * LANES)
        q = qa_ref[0, :, sl]
        k = ka_ref[0, :, sl]
        v = va_ref[0, :, sl]
        for hh in range(2):
            qm = jnp.where(lo if hh == 0 else jnp.logical_not(lo), q, jnp.zeros_like(q))
            chains.append(([_dot_t(qm, k)], [v], None))
    _gqa_store(oa_ref, _softmax_pv_staged(chains))
    q = qd_ref[0]
    chains = []
    for kv in range(2):
        sl = slice(kv * LANES, (kv + 1) * LANES)
        chains += _gqa_chains(q, kv, sink_ref, [kd_ref[0, :, sl]], [vd_ref[0, :, sl]], None)
    _gqa_store(od_ref, _softmax_pv_staged(chains))


def _ctx_attn(projc, sink):
    b, lc, _ = projc.shape
    w2 = 2 * LANES

    def spec(width, col):
        return pl.BlockSpec((1, lc, width), lambda bi: (bi, 0, col // width))

    ospec = pl.BlockSpec((1, lc, GROUP_W), lambda bi: (bi, 0, 0))
    return pl.pallas_call(
        _ctx_attn_kernel,
        grid=(b,),
        in_specs=[pl.BlockSpec(memory_space=pltpu.SMEM),
                  spec(GROUP_W, C_QA), spec(GROUP_W, C_KA), spec(GROUP_W, C_VA),
                  spec(GROUP_W, C_QD), spec(w2, C_KD), spec(w2, C_VD)],
        out_specs=[ospec, ospec],
        out_shape=[jax.ShapeDtypeStruct((b, lc, GROUP_W), BF16)] * 2,
        compiler_params=_cparams(("arbitrary",)),
        name="ctx_attn",
    )(sink.reshape(-1), projc, projc, projc, projc, projc, projc)


OUTPROJ_ROW_GROUPS = 2


def _outproj_kernel(*refs, tm, moe):
    (oa_ref, od_ref, u_ref, up_ref, un_ref, pw_ref, ps_ref, cw_ref, wo_ref, x_ref, g1_ref,
     nw_ref, sh_ref, sc_ref) = refs[:14]
    rest = refs[14:]
    if moe:
        rt_ref, xo_ref, h_ref, gate_ref, ext_ub, ext_u = rest
    else:
        xo_ref, h_ref, ext_ub, ext_u = rest
    m = pl.program_id(1)
    nm = pl.num_programs(1)
    g4 = GROUP_W

    def cols(ref, k):
        return ref[0, :, k * g4:(k + 1) * g4].astype(F32)

    ub = cols(u_ref, 0)
    has_prev = m > 0
    has_next = m < nm - 1
    ext_ub[0:HALO] = jnp.where(has_prev, cols(up_ref, 0), 0.0)
    ext_ub[HALO:HALO + tm] = ub
    ext_ub[HALO + tm:] = jnp.where(has_next, cols(un_ref, 0), 0.0)
    ext_u[0:HALO] = jnp.where(has_prev, cols(up_ref, 2) * cols(up_ref, 3), 0.0)
    ext_u[HALO:HALO + tm] = cols(u_ref, 2) * cols(u_ref, 3)
    ext_u[HALO + tm:] = jnp.where(has_next, cols(un_ref, 2) * cols(un_ref, 3), 0.0)

    n_tok = nm * tm
    rg = tm // OUTPROJ_ROW_GROUPS
    groups = [(i * rg, (i + 1) * rg) for i in range(OUTPROJ_ROW_GROUPS)]

    def mixers(r0, r1):
        t = m * tm + r0 + lax.broadcasted_iota(jnp.int32, (rg, LANES), 0)
        obs = []
        for g, w in enumerate(POOL_WINDOWS):
            sl = slice(g * LANES, (g + 1) * LANES)
            acc = None
            for d in range(-(w // 2), w - w // 2):
                term = ext_ub[HALO + r0 + d:HALO + r1 + d, sl]
                acc = term if acc is None else acc + term
            cnt = (jnp.clip(t + (w - w // 2), 0, n_tok) - jnp.clip(t - w // 2, 0, n_tok)).astype(F32)
            pooled = acc / cnt - ext_ub[HALO + r0:HALO + r1, sl]
            obs.append(_dot(pooled.astype(BF16), pw_ref[g]) * ps_ref[:, sl])
        y = (ext_u[HALO + r0 - 1:HALO + r1 - 1] * cw_ref[0:1, :] + ext_u[HALO + r0:HALO + r1] * cw_ref[1:2, :]
             + ext_u[HALO + r0 + 1:HALO + r1 + 1] * cw_ref[2:3, :])
        o_c = u_ref[0, r0:r1, g4:2 * g4].astype(F32) * y
        return jnp.concatenate([oa_ref[0, r0:r1, :]] + [o.astype(BF16) for o in obs]
                               + [o_c.astype(BF16), od_ref[0, r0:r1, :]], axis=1)

    def norm(r0, r1, yy):
        xn = x_ref[0, r0:r1, :] + g1_ref[0, 0] * yy
        xo_ref[0, r0:r1, :] = xn
        return _rms_mod(xn, nw_ref[...], sh_ref[0, 0], sc_ref[0, 0])

    def route(r0, r1, h2):
        packed = _pack_bf16_pairs(h2)
        for c in range(PACK_PLANES):
            h_ref[c, r0:r1, :] = packed[:, c * PLANE_W:(c + 1) * PLANE_W]
        lane = lax.broadcasted_iota(jnp.int32, (rg, LANES), 1)
        h_hi = _unpack_hi(lax.bitcast_convert_type(h2, jnp.uint32))
        h_lo = (h2 - h_hi).astype(BF16)
        h_hi = h_hi.astype(BF16)
        logits = _dot(h_hi, rt_ref[0]) + (_dot(h_lo, rt_ref[0]) + _dot(h_hi, rt_ref[1]))
        logits = jnp.where(lane < N_EXPERTS, logits, -jnp.inf)
        v1 = jnp.max(logits, axis=-1, keepdims=True)
        i1 = jnp.min(jnp.where(logits == v1, lane, LANES), axis=-1, keepdims=True)
        l2 = jnp.where(lane == i1, -jnp.inf, logits)
        v2 = jnp.max(l2, axis=-1, keepdims=True)
        i2 = jnp.min(jnp.where(l2 == v2, lane, LANES), axis=-1, keepdims=True)
        e2 = jnp.exp(v2 - v1)
        p1 = 1.0 / (1.0 + e2)
        p2 = e2 / (1.0 + e2)
        gate_ref[0, r0:r1, 0:LANES] = jnp.where(lane == i1, 1.0, 0.0)
        gate_ref[0, r0:r1, LANES:2 * LANES] = jnp.where(lane == i2, 1.0, 0.0)
        gate_ref[0, r0:r1, 2 * LANES:3 * LANES] = jnp.broadcast_to(p1, (rg, LANES))
        gate_ref[0, r0:r1, 3 * LANES:4 * LANES] = jnp.broadcast_to(p2, (rg, LANES))

    lhs = [mixers(r0, r1) for r0, r1 in groups]
    yys = [_dot(a, wo_ref[...]) for a in lhs]
    h2s = [norm(r0, r1, yy) for (r0, r1), yy in zip(groups, yys)]
    for (r0, r1), h2 in zip(groups, h2s):
        if moe:
            route(r0, r1, h2)
        else:
            h_ref[0, r0:r1, :] = h2.astype(BF16)


def _outproj(o_a, o_d, proj, x, wo, pool_w, pool_scale, conv_w, nw, mods, mod_row, router=None):
    b, s, d = x.shape
    tm = min(s, 512)
    moe = router is not None
    row = mod_row
    nh = s // HALO
    th = tm // HALO

    def mspec(k):
        return pl.BlockSpec((1, 1, 1, d), lambda bi, m: (row(bi), k, 0, 0))

    gspec = pl.BlockSpec((1, tm, GROUP_W), lambda bi, m: (bi, m, 0))
    xspec = pl.BlockSpec((1, tm, d), lambda bi, m: (bi, m, 0))
    in_specs = [
        gspec, gspec,
        pl.BlockSpec((1, tm, 4 * GROUP_W), lambda bi, m: (bi, m, 0)),
        pl.BlockSpec((1, HALO, 4 * GROUP_W), lambda bi, m: (bi, jnp.maximum(m * th - 1, 0), 0)),
        pl.BlockSpec((1, HALO, 4 * GROUP_W), lambda bi, m: (bi, jnp.minimum((m + 1) * th, nh - 1), 0)),
        pl.BlockSpec((4, POOL_CH, POOL_CH), lambda bi, m: (0, 0, 0)),
        pl.BlockSpec((1, GROUP_W), lambda bi, m: (0, 0)),
        pl.BlockSpec((3, GROUP_W), lambda bi, m: (0, 0)),
        pl.BlockSpec((d, d), lambda bi, m: (0, 0), pipeline_mode=pl.Buffered(1)),
        xspec, mspec(2),
        pl.BlockSpec((1, d), lambda bi, m: (0, 0)),
        mspec(3), mspec(4),
    ]
    args = [o_a, o_d, proj, proj, proj, pool_w.astype(BF16), pool_scale.reshape(1, GROUP_W), conv_w, wo,
            x, mods, nw.reshape(1, d), mods, mods]
    if moe:
        nm = s // tm
        in_specs.append(pl.BlockSpec((2, d, LANES), lambda bi, m: (0, 0, 0)))
        r_pad = jnp.pad(router, ((0, 0), (0, LANES - router.shape[1])))
        r_hi = _unpack_hi(lax.bitcast_convert_type(r_pad, jnp.uint32))
        args.append(jnp.stack([r_hi, r_pad - r_hi]).astype(BF16))
        out_shape = [jax.ShapeDtypeStruct((b, s, d), F32),
                     jax.ShapeDtypeStruct((PACK_PLANES, b * s, PLANE_W), jnp.uint32),
                     jax.ShapeDtypeStruct((b, s, 4 * LANES), F32)]
        out_specs = [xspec,
                     pl.BlockSpec((PACK_PLANES, tm, PLANE_W), lambda bi, m: (0, bi * nm + m, 0)),
                     pl.BlockSpec((1, tm, 4 * LANES), lambda bi, m: (bi, m, 0))]
    else:
        out_shape = [jax.ShapeDtypeStruct((b, s, d), F32), jax.ShapeDtypeStruct((b, s, d), BF16)]
        out_specs = [xspec, xspec]
    return pl.pallas_call(
        functools.partial(_outproj_kernel, tm=tm, moe=moe),
        grid=(b, s // tm),
        in_specs=in_specs,
        out_specs=out_specs,
        out_shape=out_shape,
        scratch_shapes=[pltpu.VMEM((tm + 2 * HALO, GROUP_W), F32), pltpu.VMEM((tm + 2 * HALO, GROUP_W), F32)],
        compiler_params=_cparams(("arbitrary", "arbitrary")),
        name="outproj",
    )(*args)


def _swiglu_accumulate(h_ref, wg_ref, wu_ref, wd_ref, acc_ref, rows=None):
    r = slice(0, h_ref.shape[0] if rows is None else rows)
    h = h_ref[r, :]
    a = _dot(h, wg_ref[0].astype(BF16))
    u = _dot(h, wu_ref[0].astype(BF16))
    act = a * (1.0 / (1.0 + jnp.exp(-a))) * u
    acc_ref[r, :] += _dot(act.astype(BF16), wd_ref[0].astype(BF16))


def _ffn_kernel(h_ref, wg_ref, wu_ref, wd_ref, o_ref, acc_ref):
    f = pl.program_id(1)

    @pl.when(f == 0)
    def _():
        acc_ref[...] = jnp.zeros(acc_ref.shape, F32)

    _swiglu_accumulate(h_ref, wg_ref, wu_ref, wd_ref, acc_ref)

    @pl.when(f == pl.num_programs(1) - 1)
    def _():
        o_ref[...] = acc_ref[...].astype(BF16)


def _ffn(h, wg, wu, wd, layer):
    mt, d = h.shape
    ff = wg.shape[2]
    tm = min(mt, 1024)
    tf = 256
    return pl.pallas_call(
        _ffn_kernel,
        grid=(mt // tm, ff // tf),
        in_specs=[
            pl.BlockSpec((tm, d), lambda m, f: (m, 0)),
            pl.BlockSpec((1, d, tf), lambda m, f: (layer, 0, f)),
            pl.BlockSpec((1, d, tf), lambda m, f: (layer, 0, f)),
            pl.BlockSpec((1, tf, d), lambda m, f: (layer, f, 0)),
        ],
        out_specs=pl.BlockSpec((tm, d), lambda m, f: (m, 0)),
        out_shape=jax.ShapeDtypeStruct((mt, d), BF16),
        scratch_shapes=[pltpu.VMEM((tm, d), F32)],
        compiler_params=_cparams(("arbitrary", "arbitrary")),
        name="ffn",
    )(h, wg, wu, wd)


def _rank_kernel(r_ref, o_ref, cnt_ref, carry):
    i = pl.program_id(0)
    tm = r_ref.shape[0]

    @pl.when(i == 0)
    def _():
        carry[...] = jnp.zeros(carry.shape, F32)

    oh1 = r_ref[:, 0:LANES]
    oh2 = r_ref[:, LANES:2 * LANES]
    sel = oh1 + oh2
    row = lax.broadcasted_iota(jnp.int32, (tm, tm), 0)
    col = lax.broadcasted_iota(jnp.int32, (tm, tm), 1)
    tri = jnp.where(col < row, 1.0, 0.0).astype(BF16)
    excl = _dot(tri, sel.astype(BF16)) + carry[0:1, :]
    lane = lax.broadcasted_iota(jnp.int32, (tm, LANES), 1)
    lane_f = lane.astype(F32)
    e1 = jnp.sum(oh1 * lane_f, axis=-1, keepdims=True)
    r1 = jnp.sum(oh1 * excl, axis=-1, keepdims=True)
    e2 = jnp.sum(oh2 * lane_f, axis=-1, keepdims=True)
    r2 = jnp.sum(oh2 * excl, axis=-1, keepdims=True)
    meta = jnp.where(lane == 0, e1, jnp.where(lane == 1, r1, jnp.where(lane == 2, e2,
                     jnp.where(lane == 3, r2, 0.0))))
    o_ref[...] = meta.T[0:SUBLANES, :]
    carry[...] = carry[...] + jnp.sum(sel, axis=0, keepdims=True)
    cnt_ref[...] = carry[...]


def _rank(route):
    t = route.shape[0]
    tm = min(t, 512)
    return pl.pallas_call(
        _rank_kernel,
        grid=(t // tm,),
        in_specs=[pl.BlockSpec((tm, 2 * LANES), lambda i: (i, 0))],
        out_specs=[pl.BlockSpec((SUBLANES, tm), lambda i: (0, i)),
                   pl.BlockSpec((SUBLANES, LANES), lambda i: (0, 0))],
        out_shape=[jax.ShapeDtypeStruct((SUBLANES, t), F32), jax.ShapeDtypeStruct((SUBLANES, LANES), F32)],
        scratch_shapes=[pltpu.VMEM((SUBLANES, LANES), F32)],
        compiler_params=_cparams(("arbitrary",)),
        name="rank",
    )(route)


def _sc_mesh():
    return plsc.VectorSubcoreMesh(core_axis_name="core", subcore_axis_name="subcore")


def _sc_scatter2(x, idx1, idx2, n_out):
    n, w = x.shape

    @functools.partial(pl.kernel, out_type=jax.ShapeDtypeStruct((n_out, w), x.dtype), mesh=_sc_mesh(),
                       scratch_types=[], name="sc_dispatch")
    def k(x_hbm, i1_hbm, i2_hbm, o_hbm):
        def body(x_vmem, i1_vmem, i2_vmem):
            pltpu.sync_copy(x_vmem, o_hbm.at[i1_vmem.at[0]])
            pltpu.sync_copy(x_vmem, o_hbm.at[i2_vmem.at[0]])

        pltpu.emit_pipeline(
            body, grid=(n // SC_WINDOW,),
            in_specs=[pl.BlockSpec((SC_WINDOW, w), lambda i: (i, 0)),
                      pl.BlockSpec((1, SC_WINDOW), lambda i: (0, i)),
                      pl.BlockSpec((1, SC_WINDOW), lambda i: (0, i))],
            out_specs=[], core_axis_name=("core", "subcore"),
            dimension_semantics=(pltpu.PARALLEL,))(x_hbm, i1_hbm, i2_hbm)

    return k(x, idx1.reshape(1, n), idx2.reshape(1, n))


def _sc_gather(y, idx):
    (n,) = idx.shape
    w = y.shape[1]

    @functools.partial(pl.kernel, out_type=jax.ShapeDtypeStruct((n, w), y.dtype), mesh=_sc_mesh(),
                       scratch_types=[], name="sc_combine")
    def k(y_hbm, i_hbm, o_hbm):
        def body(i_vmem, o_vmem):
            pltpu.sync_copy(y_hbm.at[i_vmem.at[0]], o_vmem)

        pltpu.emit_pipeline(
            body, grid=(n // SC_WINDOW,),
            in_specs=[pl.BlockSpec((1, SC_WINDOW), lambda i: (0, i))],
            out_specs=[pl.BlockSpec((SC_WINDOW, w), lambda i: (i, 0))],
            core_axis_name=("core", "subcore"),
            dimension_semantics=(pltpu.PARALLEL,))(i_hbm, o_hbm)

    return k(y, idx.reshape(1, n))


MOE_SUB = 256


def _moe_kernel(te_ref, nu_ref, tv_ref, h_ref, wg_ref, wu_ref, wd_ref, o_ref, acc_ref, hb_ref):
    m = pl.program_id(0)
    f = pl.program_id(1)
    tm = hb_ref.shape[0]
    half = hb_ref.shape[1] // 2

    @pl.when(m < nu_ref[0])
    def _():
        @pl.when(f == 0)
        def _():
            acc_ref[...] = jnp.zeros(acc_ref.shape, F32)
            routed = lax.broadcasted_iota(jnp.int32, (tm, PLANE_W), 0) < tv_ref[m]
            for c in range(PACK_PLANES):
                w = jnp.where(routed, h_ref[c], jnp.uint32(0))
                hb_ref[:, c * PLANE_W:(c + 1) * PLANE_W] = _unpack_hi(w).astype(BF16)
                hb_ref[:, half + c * PLANE_W:half + (c + 1) * PLANE_W] = _unpack_lo(w).astype(BF16)

        n_sub = (tv_ref[m] + MOE_SUB - 1) // MOE_SUB
        for k in range(1, tm // MOE_SUB + 1):
            @pl.when(n_sub == k)
            def _():
                _swiglu_accumulate(hb_ref, wg_ref, wu_ref, wd_ref, acc_ref, k * MOE_SUB)

        @pl.when(f == pl.num_programs(1) - 1)
        def _():
            packed = _pack_bf16_pairs(acc_ref[...])
            for c in range(PACK_PLANES):
                o_ref[c] = packed[:, c * PLANE_W:(c + 1) * PLANE_W]


MOE_TM = 1536


def _moe_routed(hp, tile_expert, n_used, tile_valid, wg, wu, wd, layer, tm):
    _, r, _ = hp.shape
    _, ne, d, ff = wg.shape
    wg = wg.reshape(-1, d, ff)
    wu = wu.reshape(-1, d, ff)
    wd = wd.reshape(-1, ff, d)
    e0 = layer * ne
    tf = 256
    nf = ff // tf

    def row_map(m, f, te, nu, tv):
        return (0, jnp.minimum(m, nu[0] - 1), 0)

    def fidx(m, f, nu):
        return jnp.where(m < nu[0], f, nf - 1)

    hspec = pl.BlockSpec((PACK_PLANES, tm, PLANE_W), row_map)
    return pl.pallas_call(
        _moe_kernel,
        grid_spec=pltpu.PrefetchScalarGridSpec(
            num_scalar_prefetch=3,
            grid=(r // tm, nf),
            in_specs=[
                hspec,
                pl.BlockSpec((1, d, tf), lambda m, f, te, nu, tv: (e0 + te[m], 0, fidx(m, f, nu))),
                pl.BlockSpec((1, d, tf), lambda m, f, te, nu, tv: (e0 + te[m], 0, fidx(m, f, nu))),
                pl.BlockSpec((1, tf, d), lambda m, f, te, nu, tv: (e0 + te[m], fidx(m, f, nu), 0)),
            ],
            out_specs=hspec,
            scratch_shapes=[pltpu.VMEM((tm, d), F32), pltpu.VMEM((tm, d), BF16)],
        ),
        out_shape=jax.ShapeDtypeStruct(hp.shape, jnp.uint32),
        compiler_params=_cparams(("arbitrary", "arbitrary"), VMEM_LIMIT_MOE),
        name="moe",
    )(tile_expert, n_used, tile_valid, hp, wg, wu, wd)


def _route_plan(meta, counts, tm, n_tiles):
    e1, r1, e2, r2 = (meta[k].astype(jnp.int32) for k in range(4))
    cnt = counts[0, :N_EXPERTS].astype(jnp.int32)
    tiles_per = (cnt + tm - 1) // tm
    tile_end = jnp.cumsum(tiles_per)
    start_row = (tile_end - tiles_per) * tm
    pos1 = start_row[e1] + r1
    pos2 = start_row[e2] + r2
    n_used = tile_end[-1]
    tiles = jnp.arange(n_tiles, dtype=jnp.int32)
    tile_expert = jnp.sum((tiles[:, None] >= tile_end[None, :]).astype(jnp.int32), axis=1)
    last_expert = jnp.sum((n_used - 1 >= tile_end).astype(jnp.int32))
    tile_expert = jnp.minimum(tile_expert, last_expert)
    tile_start = (tile_end - tiles_per)[tile_expert]
    tile_valid = jnp.clip(cnt[tile_expert] - (tiles - tile_start) * tm, 0, tm)
    return pos1, pos2, tile_expert, n_used.reshape(1), tile_valid


def _plane_rows(pos, n_rows):
    return (jnp.arange(PACK_PLANES, dtype=jnp.int32)[:, None] * n_rows + pos[None, :]).reshape(-1)


def _moe(hp, route, wg, wu, wd, layer):
    _, t, _ = hp.shape
    tm = min(MOE_TM, t)
    n_tiles = 2 * t // tm + N_EXPERTS
    n_rows = n_tiles * tm
    meta, counts = _rank(route)
    pos1, pos2, tile_expert, n_used, tile_valid = _route_plan(meta, counts, tm, n_tiles)
    i1 = _plane_rows(pos1, n_rows)
    i2 = _plane_rows(pos2, n_rows)
    hs = _sc_scatter2(hp.reshape(PACK_PLANES * t, PLANE_W), i1, i2, PACK_PLANES * n_rows)
    ys = _moe_routed(hs.reshape(PACK_PLANES, n_rows, PLANE_W), tile_expert, n_used, tile_valid,
                     wg, wu, wd, layer, tm)
    ys = ys.reshape(PACK_PLANES * n_rows, PLANE_W)
    y1 = _sc_gather(ys, i1).reshape(PACK_PLANES, t, PLANE_W)
    y2 = _sc_gather(ys, i2).reshape(PACK_PLANES, t, PLANE_W)
    return y1, y2


def _final_kernel(x_ref, f_ref, g_ref, nw_ref, o_ref):
    xv = x_ref[0] + g_ref[0, 0] * f_ref[0].astype(F32)
    ms = jnp.mean(xv * xv, axis=-1, keepdims=True)
    o_ref[0] = xv * lax.rsqrt(ms + EPS) * nw_ref[...]


def _final(x, f, mods, nw):
    b, s, d = x.shape
    tm = min(s, 512)
    xspec = pl.BlockSpec((1, tm, d), lambda bi, m: (bi, m, 0))
    return pl.pallas_call(
        _final_kernel,
        grid=(b, s // tm),
        in_specs=[xspec, xspec,
                  pl.BlockSpec((1, 1, 1, d), lambda bi, m: (bi, 5, 0, 0)),
                  pl.BlockSpec((1, d), lambda bi, m: (0, 0))],
        out_specs=xspec,
        out_shape=jax.ShapeDtypeStruct((b, s, d), F32),
        compiler_params=_cparams(("arbitrary", "arbitrary")),
        name="final",
    )(x, f, mods, nw.reshape(1, d))


def _final_moe_kernel(x_ref, y1_ref, y2_ref, p_ref, g_ref, nw_ref, o_ref):
    tm, d = x_ref.shape[1], x_ref.shape[2]
    half = d // 2
    p = p_ref[0]
    p1 = jnp.concatenate([p[:, 0:LANES]] * (PLANE_W // LANES), axis=1)
    p2 = jnp.concatenate([p[:, LANES:2 * LANES]] * (PLANE_W // LANES), axis=1)
    ssq = jnp.zeros((tm, 1), F32)
    for c in range(PACK_PLANES):
        w1 = y1_ref[c]
        w2 = y2_ref[c]
        for unpack, off in ((_unpack_hi, 0), (_unpack_lo, half)):
            sl = slice(off + c * PLANE_W, off + (c + 1) * PLANE_W)
            f = p1 * unpack(w1) + p2 * unpack(w2)
            xv = x_ref[0, :, sl] + g_ref[0, 0, :, sl] * f
            o_ref[0, :, sl] = xv
            ssq = ssq + jnp.sum(xv * xv, axis=-1, keepdims=True)
    o_ref[0] = o_ref[0] * lax.rsqrt(ssq / d + EPS) * nw_ref[...]


def _final_moe(x, y1, y2, route, mods, nw):
    b, s, d = x.shape
    tm = min(s, 512)
    nm = s // tm
    xspec = pl.BlockSpec((1, tm, d), lambda bi, m: (bi, m, 0))
    yspec = pl.BlockSpec((PACK_PLANES, tm, PLANE_W), lambda bi, m: (0, bi * nm + m, 0))
    return pl.pallas_call(
        _final_moe_kernel,
        grid=(b, nm),
        in_specs=[xspec, yspec, yspec,
                  pl.BlockSpec((1, tm, 2 * LANES), lambda bi, m: (bi, m, 1)),
                  pl.BlockSpec((1, 1, 1, d), lambda bi, m: (bi, 5, 0, 0)),
                  pl.BlockSpec((1, d), lambda bi, m: (0, 0))],
        out_specs=xspec,
        out_shape=jax.ShapeDtypeStruct((b, s, d), F32),
        compiler_params=_cparams(("arbitrary", "arbitrary")),
        name="final",
    )(x, y1, y2, route, mods, nw.reshape(1, d))


PREP_W = 2 * LANES


def _w_in_plan():
    nb = GROUP_W // PREP_W
    plan = [(3 * nb + j, 0, 0) for j in range(4 * nb)]
    plan += [(j, 1, 0) for j in range(nb)]
    plan += [(nb + j, 0, 0) for j in range(2 * nb)]
    plan += [(7 * nb + j, 1, 0) for j in range(nb)]
    plan += [(8 * nb, 0, 1), (8 * nb, 0, 2)]
    assert len(plan) * PREP_W == D_PROJ
    return jnp.asarray(plan, jnp.int32).T.reshape(-1)


def _prep_w_kernel(plan_ref, w_ref, o_ref):
    j = pl.program_id(1)
    nblk = pl.num_programs(1)
    x = w_ref[0]
    mode = plan_ref[2 * nblk + j]
    src = jnp.where(mode == 2, x[:, LANES:], x[:, :LANES])
    lo = lax.broadcasted_iota(jnp.int32, src.shape, 1) < HEAD_DIM
    other = pltpu.roll(src, HEAD_DIM, 1)
    dup = jnp.concatenate([jnp.where(lo, src, other), jnp.where(lo, other, src)], axis=1)
    scale = jnp.where(plan_ref[nblk + j] == 1, HEAD_DIM ** -0.5, 1.0)
    o_ref[0] = (jnp.where(mode == 0, x, dup) * scale).astype(BF16)


def _prep_w_in(w_in):
    depth, d, _ = w_in.shape
    nblk = D_PROJ // PREP_W
    return pl.pallas_call(
        _prep_w_kernel,
        grid_spec=pltpu.PrefetchScalarGridSpec(
            num_scalar_prefetch=1,
            grid=(depth, nblk),
            in_specs=[pl.BlockSpec((1, d, PREP_W), lambda i, j, plan: (i, 0, plan[j]))],
            out_specs=pl.BlockSpec((1, d, PREP_W), lambda i, j, plan: (i, 0, j)),
        ),
        out_shape=jax.ShapeDtypeStruct((depth, d, D_PROJ), BF16),
        compiler_params=_cparams(("arbitrary", "arbitrary")),
        name="prep_w_in",
    )(_w_in_plan(), w_in)


def _rope_tables(n):
    t = jnp.arange(n, dtype=jnp.int32)
    rows = (t // GRID_W).astype(F32)
    cols = (t % GRID_W).astype(F32)
    nf = HEAD_DIM // 4
    inv = ROPE_BASE ** (-jnp.arange(nf, dtype=F32) / nf)
    ang = jnp.concatenate([rows[:, None] * inv, cols[:, None] * inv], axis=-1)
    cos = jnp.cos(ang)
    sin = jnp.sin(ang)
    cos_t = jnp.concatenate([cos, cos, cos, cos], axis=-1)
    sin_t = jnp.concatenate([-sin, sin, -sin, sin], axis=-1)
    return cos_t, sin_t


def kernel(x, c, ctx, c_ctx, w_ada, b_ada, norm_mix, norm_ffn, norm_final, w_in, w_out, na_rpb, pool_w, pool_scale, conv_w, swa_sink, ffn_w_gate, ffn_w_up, ffn_w_down, moe_router, moe_w_gate, moe_w_up, moe_w_down):
    b, s, d = x.shape
    lc = ctx.shape[1]
    depth = w_ada.shape[0]
    cond = jnp.concatenate([c, c_ctx[None, :]], axis=0)
    assert b == 2
    mods_all = _ada(cond, w_ada, b_ada)
    cos_t, sin_t = _rope_tables(s)
    cos_c = jnp.ones((lc, LANES), F32)
    sin_c = jnp.zeros((lc, LANES), F32)
    row_x = lambda bi: bi
    row_c = lambda bi: b

    w_proj = _prep_w_in(w_in)
    cx = ctx
    fx = fc = None
    for i in range(depth):
        last = i == depth - 1
        mods = mods_all[i].reshape(SUBLANES, 6, 1, d)
        wi = (w_proj, i)
        wo = w_out[i].astype(BF16)
        if i == 0:
            proj, _ = _inproj(x, wi, norm_mix[i], mods, row_x, cos_t, sin_t)
            projc, _ = _inproj(cx, wi, norm_mix[i], mods, row_c, cos_c, sin_c)
        else:
            mods_prev = mods_all[i - 1].reshape(SUBLANES, 6, 1, d)
            proj, x = _inproj_res(x, fx, mods_prev, wi, norm_mix[i], mods, row_x, cos_t, sin_t)
            projc, cx = _inproj_res(cx, fc, mods_prev, wi, norm_mix[i], mods, row_c, cos_c, sin_c)
        bias = _na_bias(na_rpb[i])
        o_a = _na_attn(proj, projc, bias)
        o_d = _swa_attn(proj, projc, swa_sink[i])
        moe = i % 2 == 1
        router = moe_router[i // 2] if moe else None
        outs = _outproj(o_a, o_d, proj, x, wo, pool_w[i], pool_scale[i], conv_w[i], norm_ffn[i], mods, row_x,
                        router)
        x, h2 = outs[0], outs[1]
        if not last:
            oc_a, oc_d = _ctx_attn(projc, swa_sink[i])
            cx, h2c = _outproj(oc_a, oc_d, projc, cx, wo, pool_w[i], pool_scale[i], conv_w[i], norm_ffn[i],
                               mods, row_c)
        j = i // 2
        if moe:
            if not last:
                raise NotImplementedError("an expert layer that is not the last layer")
            route = outs[2]
            y1, y2 = _moe(h2, route.reshape(b * s, 4 * LANES), moe_w_gate, moe_w_up, moe_w_down, j)
            return _final_moe(x, y1, y2, route, mods, norm_final)
        fx = _ffn(h2.reshape(b * s, d), ffn_w_gate, ffn_w_up, ffn_w_down, j).reshape(b, s, d)
        if not last:
            fc = _ffn(h2c.reshape(b * lc, d), ffn_w_gate, ffn_w_up, ffn_w_down, j).reshape(b, lc, d)
    return _final(x, fx, mods_all[depth - 1].reshape(SUBLANES, 6, 1, d), norm_final)


def _inproj_res(x, f, mods_prev, w, nw, mods, mod_row, cos_t, sin_t):
    both = jnp.concatenate([mods, mods_prev], axis=1)
    return _inproj(x, w, nw, both, mod_row, cos_t, sin_t, res=(f, 6 + 5))
```

```python
import functools

import jax
import jax.numpy as jnp
from jax import lax
from jax.experimental import pallas as pl
from jax.experimental.pallas import tpu as pltpu
from jax.experimental.pallas import tpu_sc as plsc

F32 = jnp.float32
BF16 = jnp.bfloat16

D_MODEL = 2048
GRID_W = 64
HEAD_DIM = 64
EPS = 1e-6
NEG_INF = -1e30
GROUP_W = 512
NA_KH = 8
NA_KW = 16
POOL_WINDOWS = (2, 4, 8, 16)
POOL_CH = 128
SWA_WINDOW = 128
SWA_BLOCK = 128
ROPE_BASE = 10000.0
N_EXPERTS = 8
LANES = 128
SUBLANES = 8
HALO = 16

C_UBC = 0
C_QA = 2048
C_KA = 2560
C_VA = 3072
C_QD = 3584
C_KD = 4096
C_VD = 4352
D_PROJ = 4608
TN_PROJ = 512

VMEM_LIMIT = 56 * 1024 * 1024
VMEM_LIMIT_MOE = 58 * 1024 * 1024


def _cparams(sem, vmem_limit=VMEM_LIMIT):
    return pltpu.CompilerParams(dimension_semantics=sem, vmem_limit_bytes=vmem_limit)


def _dot(a, b):
    return jnp.dot(a, b, preferred_element_type=F32)


def _dot_t(a, b):
    return lax.dot_general(a, b, (((1,), (1,)), ((), ())), preferred_element_type=F32)


def _rms_mod(xv, nw, sh, sc):
    ms = jnp.mean(xv * xv, axis=-1, keepdims=True)
    return (xv * lax.rsqrt(ms + EPS)) * (nw * (1.0 + sc)) + sh


PACK_PLANES = 4
PLANE_W = D_MODEL // 2 // PACK_PLANES
SC_WINDOW = 128


def _pack_bf16_pairs(v):
    half = v.shape[1] // 2
    hi = lax.bitcast_convert_type(v[:, :half].astype(BF16).astype(F32), jnp.uint32)
    lo = lax.bitcast_convert_type(v[:, half:].astype(BF16).astype(F32), jnp.uint32)
    return hi | (lo >> 16)


def _unpack_hi(w):
    return lax.bitcast_convert_type(w & jnp.uint32(0xFFFF0000), F32)


def _unpack_lo(w):
    return lax.bitcast_convert_type(w << 16, F32)


def _ada_kernel(c_ref, w_ref, b_ref, o_ref, silu_scr):
    tn = w_ref.shape[2]

    @pl.when((pl.program_id(0) == 0) & (pl.program_id(1) == 0))
    def _():
        cv = c_ref[...]
        silu_scr[...] = cv * (1.0 / (1.0 + jnp.exp(-cv)))

    o_ref[...] = jnp.zeros(o_ref.shape, F32)
    for r in range(3):
        m = silu_scr[r]
        cols = []
        for j in range(tn // LANES):
            wj = w_ref[0, :, j * LANES:(j + 1) * LANES]
            cols.append(jnp.sum(wj * m, axis=0, keepdims=True))
        o_ref[0, r:r + 1, :] = jnp.concatenate(cols, axis=1) + b_ref[0]


def _ada(cond, w_ada, b_ada):
    depth, d, n6 = w_ada.shape
    tn = 2048
    cb = jnp.broadcast_to(cond[:, :, None], (3, d, LANES))
    return pl.pallas_call(
        _ada_kernel,
        grid=(depth, n6 // tn),
        in_specs=[
            pl.BlockSpec((3, d, LANES), lambda i, j: (0, 0, 0)),
            pl.BlockSpec((1, d, tn), lambda i, j: (i, 0, j)),
            pl.BlockSpec((1, 1, tn), lambda i, j: (i, 0, j)),
        ],
        out_specs=pl.BlockSpec((1, SUBLANES, tn), lambda i, j: (i, 0, j)),
        out_shape=jax.ShapeDtypeStruct((depth, SUBLANES, n6), F32),
        scratch_shapes=[pltpu.VMEM((3, d, LANES), F32)],
        compiler_params=_cparams(("arbitrary", "arbitrary")),
        name="ada",
    )(cb, w_ada, b_ada.reshape(depth, 1, n6))


def _rope(a, cosv, sinv, lo32):
    sw = jnp.where(lo32, pltpu.roll(a, 96, 1), pltpu.roll(a, 32, 1))
    return a * cosv + sw * sinv


def _inproj_kernel(*refs, has_res):
    if has_res:
        x_ref, f_ref, g_ref, nw_ref, sh_ref, sc_ref, cos_ref, sin_ref, w_ref, o_ref, x2_ref = refs
    else:
        x_ref, nw_ref, sh_ref, sc_ref, cos_ref, sin_ref, w_ref, o_ref = refs
    xv = x_ref[0]
    if has_res:
        xv = xv + g_ref[0, 0] * f_ref[0].astype(F32)
        x2_ref[0] = xv
    h = _rms_mod(xv, nw_ref[...], sh_ref[0, 0], sc_ref[0, 0]).astype(BF16)
    tm = h.shape[0]
    lane = lax.broadcasted_iota(jnp.int32, (tm, LANES), 1)
    lo32 = (lane % HEAD_DIM) < (HEAD_DIM // 2)
    rope_end = C_VD
    for n in range(D_PROJ // TN_PROJ):
        c0 = n * TN_PROJ
        acc = _dot(h, w_ref[0, :, c0:c0 + TN_PROJ])
        if c0 + TN_PROJ <= C_QD:
            o_ref[0, :, c0:c0 + TN_PROJ] = acc.astype(BF16)
            continue
        for g in range(TN_PROJ // LANES):
            a = acc[:, g * LANES:(g + 1) * LANES]
            if c0 + g * LANES < rope_end:
                a = _rope(a, cos_ref[...], sin_ref[...], lo32)
            o_ref[0, :, c0 + g * LANES:c0 + (g + 1) * LANES] = a.astype(BF16)


def _inproj(x, w, nw, mods, mod_row, cos_t, sin_t, res=None):
    b, s, d = x.shape
    w_all, layer = w
    tm = min(s, 512)
    has_res = res is not None
    row = mod_row
    xspec = pl.BlockSpec((1, tm, d), lambda bi, m: (bi, m, 0))

    def mspec(k):
        return pl.BlockSpec((1, 1, 1, d), lambda bi, m: (row(bi), k, 0, 0))

    in_specs = [xspec]
    args = [x]
    if has_res:
        f, gk = res
        in_specs += [xspec, mspec(gk)]
        args += [f, mods]
    in_specs += [
        pl.BlockSpec((1, d), lambda bi, m: (0, 0)),
        mspec(0), mspec(1),
        pl.BlockSpec((tm, LANES), lambda bi, m: (m, 0)),
        pl.BlockSpec((tm, LANES), lambda bi, m: (m, 0)),
        pl.BlockSpec((1, d, D_PROJ), lambda bi, m: (layer, 0, 0), pipeline_mode=pl.Buffered(1)),
    ]
    args += [nw.reshape(1, d), mods, mods, cos_t, sin_t, w_all]
    out_shape = [jax.ShapeDtypeStruct((b, s, D_PROJ), BF16)]
    out_specs = [pl.BlockSpec((1, tm, D_PROJ), lambda bi, m: (bi, m, 0))]
    if has_res:
        out_shape.append(jax.ShapeDtypeStruct((b, s, d), F32))
        out_specs.append(xspec)
    outs = pl.pallas_call(
        functools.partial(_inproj_kernel, has_res=has_res),
        grid=(b, s // tm),
        in_specs=in_specs,
        out_specs=out_specs,
        out_shape=out_shape,
        compiler_params=_cparams(("arbitrary", "arbitrary")),
        name="inproj",
    )(*args)
    return outs if has_res else (outs[0], None)


def _na_bias_kernel(rpb_ref, o_ref):
    h = pl.program_id(0)
    nd = 2 * NA_KH - 1
    nj = 2 * NA_KW - 1
    q = lax.broadcasted_iota(jnp.int32, (GRID_W, LANES), 0)
    lane = lax.broadcasted_iota(jnp.int32, (GRID_W, LANES), 1)
    kw = lane % GRID_W
    dc = jnp.clip(kw - q, -(NA_KW - 1), NA_KW - 1) + (NA_KW - 1)
    c0 = jnp.clip(q - NA_KW // 2, 0, GRID_W - NA_KW)
    valid = (kw >= c0) & (kw < c0 + NA_KW)
    tabs = []
    for d in range(nd):
        t = jnp.zeros((GRID_W, LANES), F32)
        for j in range(nj):
            t = jnp.where(dc == j, rpb_ref[h * (nd * nj) + d * nj + j], t)
        tabs.append(jnp.where(valid, t, NEG_INF))
    for c in range(NA_KH):
        for g in range(NA_KH // 2):
            d_lo = 2 * g - c + (NA_KH - 1)
            o_ref[c, 0, :, g * LANES:(g + 1) * LANES] = jnp.where(lane < GRID_W, tabs[d_lo], tabs[d_lo + 1])


def _na_bias(rpb):
    nh = rpb.shape[0]
    return pl.pallas_call(
        _na_bias_kernel,
        grid=(nh,),
        in_specs=[pl.BlockSpec(memory_space=pltpu.SMEM)],
        out_specs=pl.BlockSpec((NA_KH, 1, GRID_W, NA_KH * GRID_W), lambda h: (0, h, 0, 0)),
        out_shape=jax.ShapeDtypeStruct((NA_KH, nh, GRID_W, NA_KH * GRID_W), F32),
        compiler_params=_cparams(("arbitrary",)),
        name="na_bias",
    )(rpb.reshape(-1))


def _softmax_pv_staged(chains):
    add = lambda a, b: a + b
    ms = []
    for sp, _, extra in chains:
        m = functools.reduce(jnp.maximum, [jnp.max(s, axis=-1, keepdims=True) for s in sp])
        ms.append(m if extra is None else jnp.maximum(m, extra))
    ps = [[jnp.exp(s - m) for s in sp] for (sp, _, _), m in zip(chains, ms)]
    dens = []
    for pp, (_, _, extra), m in zip(ps, chains, ms):
        den = functools.reduce(add, [jnp.sum(p, axis=-1, keepdims=True) for p in pp])
        dens.append(den if extra is None else den + jnp.exp(extra - m))
    outs = [functools.reduce(add, [_dot(p.astype(BF16), v) for p, v in zip(pp, vp)])
            for pp, (_, vp, _) in zip(ps, chains)]
    return [o / d for o, d in zip(outs, dens)]


NA_MAX_ROWS_PER_STEP = 32
NA_ROW_GROUP = 4


def _na_kernel(q_ref, k_ref, v_ref, kc_ref, vc_ref, bias_ref, o_ref, *, rows):
    rb = pl.program_id(2)
    rows_per_step = q_ref.shape[1] // GRID_W
    lo = lax.broadcasted_iota(jnp.int32, (NA_ROW_GROUP * GRID_W, LANES), 1) < HEAD_DIM
    lo_row = lax.broadcasted_iota(jnp.int32, (GRID_W, LANES), 1) < HEAD_DIM
    kc = kc_ref[0]
    vc = vc_ref[0]
    kh = min(NA_KH, rows)
    win = kh * GRID_W

    g = NA_ROW_GROUP
    gq = g * GRID_W
    zero = jnp.zeros((gq, LANES), BF16)
    for i0 in range(0, rows_per_step, g):
        q = q_ref[0, i0 * GRID_W:(i0 + g) * GRID_W, :]
        q2 = jnp.concatenate([jnp.where(lo, q, zero), jnp.where(lo, zero, q)], axis=0)
        s_ctx = _dot_t(q2, kc)
        pieces = [(hh, j, slice(hh * gq + j * GRID_W, hh * gq + (j + 1) * GRID_W))
                  for hh in range(2) for j in range(g)]
        vws, s_loc = [], {}
        for j in range(g):
            r = rb * rows_per_step + i0 + j
            r0 = jnp.clip(r - kh // 2, 0, rows - kh)
            c = r - r0
            koff = pl.multiple_of(r0 * GRID_W, GRID_W)
            vws.append(v_ref[0, pl.ds(koff, win), :])
            qp = jnp.concatenate([q2[sl] for hh, jj, sl in pieces if jj == j], axis=0)
            sp = _dot_t(qp, k_ref[0, pl.ds(koff, win), :])
            for hh in range(2):
                s_loc[hh, j] = sp[hh * GRID_W:(hh + 1) * GRID_W] + bias_ref[c, hh]
        ms = {(hh, j): jnp.maximum(jnp.max(s_loc[hh, j], axis=-1, keepdims=True),
                                   jnp.max(s_ctx[sl], axis=-1, keepdims=True)) for hh, j, sl in pieces}
        p_loc = {(hh, j): jnp.exp(s_loc[hh, j] - ms[hh, j]) for hh, j, _ in pieces}
        p_ctx = {(hh, j): jnp.exp(s_ctx[sl] - ms[hh, j]) for hh, j, sl in pieces}
        den = {k: jnp.sum(p_loc[k], axis=-1, keepdims=True) + jnp.sum(p_ctx[k], axis=-1, keepdims=True)
               for k in p_loc}
        pv_ctx = _dot(jnp.concatenate([p_ctx[hh, j].astype(BF16) for hh, j, _ in pieces], axis=0), vc)
        for j in range(g):
            pv = _dot(jnp.concatenate([p_loc[0, j].astype(BF16), p_loc[1, j].astype(BF16)], axis=0), vws[j])
            o = [(pv[hh * GRID_W:(hh + 1) * GRID_W] + pv_ctx[hh * gq + j * GRID_W:hh * gq + (j + 1) * GRID_W])
                 / den[hh, j] for hh in range(2)]
            i = i0 + j
            o_ref[0, i * GRID_W:(i + 1) * GRID_W, :] = jnp.where(lo_row, o[0], o[1]).astype(BF16)


def _na_attn(proj, projc, bias):
    b, s, _ = proj.shape
    lc = projc.shape[1]
    rows = s // GRID_W
    rows_per_step = min(NA_MAX_ROWS_PER_STEP, rows)
    assert rows >= NA_KH and rows % rows_per_step == 0 and rows_per_step % NA_ROW_GROUP == 0
    tq = rows_per_step * GRID_W
    npair = GROUP_W // LANES
    return pl.pallas_call(
        functools.partial(_na_kernel, rows=rows),
        grid=(b, npair, rows // rows_per_step),
        in_specs=[
            pl.BlockSpec((1, tq, LANES), lambda bi, p, r: (bi, r, C_QA // LANES + p)),
            pl.BlockSpec((1, s, LANES), lambda bi, p, r: (bi, 0, C_KA // LANES + p)),
            pl.BlockSpec((1, s, LANES), lambda bi, p, r: (bi, 0, C_VA // LANES + p)),
            pl.BlockSpec((1, lc, LANES), lambda bi, p, r: (bi, 0, C_KA // LANES + p)),
            pl.BlockSpec((1, lc, LANES), lambda bi, p, r: (bi, 0, C_VA // LANES + p)),
            pl.BlockSpec((NA_KH, 2, GRID_W, NA_KH * GRID_W), lambda bi, p, r: (0, p, 0, 0)),
        ],
        out_specs=pl.BlockSpec((1, tq, LANES), lambda bi, p, r: (bi, r, p)),
        out_shape=jax.ShapeDtypeStruct((b, s, GROUP_W), BF16),
        compiler_params=_cparams(("arbitrary", "arbitrary", "arbitrary")),
        name="na_attn",
    )(proj, proj, proj, projc, projc, bias)


def _gqa_chains(q, kv, sink_ref, k_parts, v_parts, mask):
    m = q.shape[0]
    lane = lax.broadcasted_iota(jnp.int32, (m, LANES), 1)
    lo = lane < HEAD_DIM
    qs = []
    for j in range(2):
        qg = q[:, (kv * 2 + j) * LANES:(kv * 2 + j + 1) * LANES]
        qs.append(jnp.where(lo, qg, jnp.zeros_like(qg)))
        qs.append(jnp.where(lo, jnp.zeros_like(qg), qg))
    qq = jnp.concatenate(qs, axis=0)
    s_all = [_dot_t(qq, k) for k in k_parts]
    chains = []
    for i in range(4):
        parts = [s[i * m:(i + 1) * m] for s in s_all]
        if mask is not None:
            parts[0] = jnp.where(mask, parts[0], NEG_INF)
        chains.append((parts, v_parts, jnp.full((m, 1), sink_ref[kv * 4 + i], F32)))
    return chains


def _gqa_store(o_ref, heads, row0=0):
    m = heads[0].shape[0]
    lane = lax.broadcasted_iota(jnp.int32, heads[0].shape, 1)
    lo = lane < HEAD_DIM
    for g in range(len(heads) // 2):
        o_ref[0, row0:row0 + m, g * LANES:(g + 1) * LANES] = (
            jnp.where(lo, heads[2 * g], heads[2 * g + 1]).astype(BF16))


SWA_BLOCKS_PER_STEP = 4


def _swa_kernel(sink_ref, q_ref, k_ref, v_ref, kc_ref, vc_ref, o_ref):
    nb = pl.num_programs(1) * SWA_BLOCKS_PER_STEP
    kc = kc_ref[0]
    vc = vc_ref[0]
    qi = lax.broadcasted_iota(jnp.int32, (SWA_BLOCK, 3 * SWA_BLOCK), 0)
    kj = lax.broadcasted_iota(jnp.int32, (SWA_BLOCK, 3 * SWA_BLOCK), 1)
    for blk in range(SWA_BLOCKS_PER_STEP):
        n = pl.program_id(1) * SWA_BLOCKS_PER_STEP + blk
        start = jnp.clip(n - 1, 0, nb - 3)
        koff = pl.multiple_of(start * SWA_BLOCK, SWA_BLOCK)
        kw = k_ref[0, pl.ds(koff, 3 * SWA_BLOCK), :]
        vw = v_ref[0, pl.ds(koff, 3 * SWA_BLOCK), :]
        valid = jnp.abs((n - start) * SWA_BLOCK + qi - kj) <= SWA_WINDOW
        q = q_ref[0, blk * SWA_BLOCK:(blk + 1) * SWA_BLOCK, :]
        chains = []
        for kv in range(2):
            sl = slice(kv * LANES, (kv + 1) * LANES)
            chains += _gqa_chains(q, kv, sink_ref, [kw[:, sl], kc[:, sl]], [vw[:, sl], vc[:, sl]], valid)
        _gqa_store(o_ref, _softmax_pv_staged(chains), blk * SWA_BLOCK)


def _swa_attn(proj, projc, sink):
    b, s, _ = proj.shape
    lc = projc.shape[1]
    nb = s // SWA_BLOCK
    assert nb >= 3 and nb % SWA_BLOCKS_PER_STEP == 0
    w2 = 2 * LANES
    tq = SWA_BLOCKS_PER_STEP * SWA_BLOCK
    return pl.pallas_call(
        _swa_kernel,
        grid=(b, nb // SWA_BLOCKS_PER_STEP),
        in_specs=[
            pl.BlockSpec(memory_space=pltpu.SMEM),
            pl.BlockSpec((1, tq, GROUP_W), lambda bi, n: (bi, n, C_QD // GROUP_W)),
            pl.BlockSpec((1, s, w2), lambda bi, n: (bi, 0, C_KD // w2)),
            pl.BlockSpec((1, s, w2), lambda bi, n: (bi, 0, C_VD // w2)),
            pl.BlockSpec((1, lc, w2), lambda bi, n: (bi, 0, C_KD // w2)),
            pl.BlockSpec((1, lc, w2), lambda bi, n: (bi, 0, C_VD // w2)),
        ],
        out_specs=pl.BlockSpec((1, tq, GROUP_W), lambda bi, n: (bi, n, 0)),
        out_shape=jax.ShapeDtypeStruct((b, s, GROUP_W), BF16),
        compiler_params=_cparams(("arbitrary", "arbitrary")),
        name="swa_attn",
    )(sink.reshape(-1), proj, proj, proj, projc, projc)


def _ctx_attn_kernel(sink_ref, qa_ref, ka_ref, va_ref, qd_ref, kd_ref, vd_ref, oa_ref, od_ref):
    m = qa_ref.shape[1]
    lane = lax.broadcasted_iota(jnp.int32, (m, LANES), 1)
    lo = lane < HEAD_DIM
    chains = []
    for p in range(GROUP_W // LANES):
        sl = slice(p * LANES, (p + 1) * LANES)
        q = qa_ref[0, :, sl]
        k = ka_ref[0, :, sl]
        v = va_ref[0, :, sl]
        for hh in range(2):
            qm = jnp.where(lo if hh == 0 else jnp.logical_not(lo), q, jnp.zeros_like(q))
            chains.append(([_dot_t(qm, k)], [v], None))
    _gqa_store(oa_ref, _softmax_pv_staged(chains))
    q = qd_ref[0]
    chains = []
    for kv in range(2):
        sl = slice(kv * LANES, (kv + 1) * LANES)
        chains += _gqa_chains(q, kv, sink_ref, [kd_ref[0, :, sl]], [vd_ref[0, :, sl]], None)
    _gqa_store(od_ref, _softmax_pv_staged(chains))


def _ctx_attn(projc, sink):
    b, lc, _ = projc.shape
    w2 = 2 * LANES

    def spec(width, col):
        return pl.BlockSpec((1, lc, width), lambda bi: (bi, 0, col // width))

    ospec = pl.BlockSpec((1, lc, GROUP_W), lambda bi: (bi, 0, 0))
    return pl.pallas_call(
        _ctx_attn_kernel,
        grid=(b,),
        in_specs=[pl.BlockSpec(memory_space=pltpu.SMEM),
                  spec(GROUP_W, C_QA), spec(GROUP_W, C_KA), spec(GROUP_W, C_VA),
                  spec(GROUP_W, C_QD), spec(w2, C_KD), spec(w2, C_VD)],
        out_specs=[ospec, ospec],
        out_shape=[jax.ShapeDtypeStruct((b, lc, GROUP_W), BF16)] * 2,
        compiler_params=_cparams(("arbitrary",)),
        name="ctx_attn",
    )(sink.reshape(-1), projc, projc, projc, projc, projc, projc)


OUTPROJ_ROW_GROUPS = 2


def _outproj_kernel(*refs, tm, moe):
    (oa_ref, od_ref, u_ref, up_ref, un_ref, pw_ref, ps_ref, cw_ref, wo_ref, x_ref, g1_ref,
     nw_ref, sh_ref, sc_ref) = refs[:14]
    rest = refs[14:]
    if moe:
        rt_ref, xo_ref, h_ref, gate_ref, ext_ub, ext_u = rest
    else:
        xo_ref, h_ref, ext_ub, ext_u = rest
    m = pl.program_id(1)
    nm = pl.num_programs(1)
    g4 = GROUP_W

    def cols(ref, k):
        return ref[0, :, k * g4:(k + 1) * g4].astype(F32)

    ub = cols(u_ref, 0)
    has_prev = m > 0
    has_next = m < nm - 1
    ext_ub[0:HALO] = jnp.where(has_prev, cols(up_ref, 0), 0.0)
    ext_ub[HALO:HALO + tm] = ub
    ext_ub[HALO + tm:] = jnp.where(has_next, cols(un_ref, 0), 0.0)
    ext_u[0:HALO] = jnp.where(has_prev, cols(up_ref, 2) * cols(up_ref, 3), 0.0)
    ext_u[HALO:HALO + tm] = cols(u_ref, 2) * cols(u_ref, 3)
    ext_u[HALO + tm:] = jnp.where(has_next, cols(un_ref, 2) * cols(un_ref, 3), 0.0)

    n_tok = nm * tm
    rg = tm // OUTPROJ_ROW_GROUPS
    groups = [(i * rg, (i + 1) * rg) for i in range(OUTPROJ_ROW_GROUPS)]

    def mixers(r0, r1):
        t = m * tm + r0 + lax.broadcasted_iota(jnp.int32, (rg, LANES), 0)
        obs = []
        for g, w in enumerate(POOL_WINDOWS):
            sl = slice(g * LANES, (g + 1) * LANES)
            acc = None
            for d in range(-(w // 2), w - w // 2):
                term = ext_ub[HALO + r0 + d:HALO + r1 + d, sl]
                acc = term if acc is None else acc + term
            cnt = (jnp.clip(t + (w - w // 2), 0, n_tok) - jnp.clip(t - w // 2, 0, n_tok)).astype(F32)
            pooled = acc / cnt - ext_ub[HALO + r0:HALO + r1, sl]
            obs.append(_dot(pooled.astype(BF16), pw_ref[g]) * ps_ref[:, sl])
        y = (ext_u[HALO + r0 - 1:HALO + r1 - 1] * cw_ref[0:1, :] + ext_u[HALO + r0:HALO + r1] * cw_ref[1:2, :]
             + ext_u[HALO + r0 + 1:HALO + r1 + 1] * cw_ref[2:3, :])
        o_c = u_ref[0, r0:r1, g4:2 * g4].astype(F32) * y
        return jnp.concatenate([oa_ref[0, r0:r1, :]] + [o.astype(BF16) for o in obs]
                               + [o_c.astype(BF16), od_ref[0, r0:r1, :]], axis=1)

    def norm(r0, r1, yy):
        xn = x_ref[0, r0:r1, :] + g1_ref[0, 0] * yy
        xo_ref[0, r0:r1, :] = xn
        return _rms_mod(xn, nw_ref[...], sh_ref[0, 0], sc_ref[0, 0])

    def route(r0, r1, h2):
        packed = _pack_bf16_pairs(h2)
        for c in range(PACK_PLANES):
            h_ref[c, r0:r1, :] = packed[:, c * PLANE_W:(c + 1) * PLANE_W]
        lane = lax.broadcasted_iota(jnp.int32, (rg, LANES), 1)
        h_hi = _unpack_hi(lax.bitcast_convert_type(h2, jnp.uint32))
        h_lo = (h2 - h_hi).astype(BF16)
        h_hi = h_hi.astype(BF16)
        logits = _dot(h_hi, rt_ref[0]) + (_dot(h_lo, rt_ref[0]) + _dot(h_hi, rt_ref[1]))
        logits = jnp.where(lane < N_EXPERTS, logits, -jnp.inf)
        v1 = jnp.max(logits, axis=-1, keepdims=True)
        i1 = jnp.min(jnp.where(logits == v1, lane, LANES), axis=-1, keepdims=True)
        l2 = jnp.where(lane == i1, -jnp.inf, logits)
        v2 = jnp.max(l2, axis=-1, keepdims=True)
        i2 = jnp.min(jnp.where(l2 == v2, lane, LANES), axis=-1, keepdims=True)
        e2 = jnp.exp(v2 - v1)
        p1 = 1.0 / (1.0 + e2)
        p2 = e2 / (1.0 + e2)
        gate_ref[0, r0:r1, 0:LANES] = jnp.where(lane == i1, 1.0, 0.0)
        gate_ref[0, r0:r1, LANES:2 * LANES] = jnp.where(lane == i2, 1.0, 0.0)
        gate_ref[0, r0:r1, 2 * LANES:3 * LANES] = jnp.broadcast_to(p1, (rg, LANES))
        gate_ref[0, r0:r1, 3 * LANES:4 * LANES] = jnp.broadcast_to(p2, (rg, LANES))

    lhs = [mixers(r0, r1) for r0, r1 in groups]
    yys = [_dot(a, wo_ref[...]) for a in lhs]
    h2s = [norm(r0, r1, yy) for (r0, r1), yy in zip(groups, yys)]
    for (r0, r1), h2 in zip(groups, h2s):
        if moe:
            route(r0, r1, h2)
        else:
            h_ref[0, r0:r1, :] = h2.astype(BF16)


def _outproj(o_a, o_d, proj, x, wo, pool_w, pool_scale, conv_w, nw, mods, mod_row, router=None):
    b, s, d = x.shape
    tm = min(s, 512)
    moe = router is not None
    row = mod_row
    nh = s // HALO
    th = tm // HALO

    def mspec(k):
        return pl.BlockSpec((1, 1, 1, d), lambda bi, m: (row(bi), k, 0, 0))

    gspec = pl.BlockSpec((1, tm, GROUP_W), lambda bi, m: (bi, m, 0))
    xspec = pl.BlockSpec((1, tm, d), lambda bi, m: (bi, m, 0))
    in_specs = [
        gspec, gspec,
        pl.BlockSpec((1, tm, 4 * GROUP_W), lambda bi, m: (bi, m, 0)),
        pl.BlockSpec((1, HALO, 4 * GROUP_W), lambda bi, m: (bi, jnp.maximum(m * th - 1, 0), 0)),
        pl.BlockSpec((1, HALO, 4 * GROUP_W), lambda bi, m: (bi, jnp.minimum((m + 1) * th, nh - 1), 0)),
        pl.BlockSpec((4, POOL_CH, POOL_CH), lambda bi, m: (0, 0, 0)),
        pl.BlockSpec((1, GROUP_W), lambda bi, m: (0, 0)),
        pl.BlockSpec((3, GROUP_W), lambda bi, m: (0, 0)),
        pl.BlockSpec((d, d), lambda bi, m: (0, 0), pipeline_mode=pl.Buffered(1)),
        xspec, mspec(2),
        pl.BlockSpec((1, d), lambda bi, m: (0, 0)),
        mspec(3), mspec(4),
    ]
    args = [o_a, o_d, proj, proj, proj, pool_w.astype(BF16), pool_scale.reshape(1, GROUP_W), conv_w, wo,
            x, mods, nw.reshape(1, d), mods, mods]
    if moe:
        nm = s // tm
        in_specs.append(pl.BlockSpec((2, d, LANES), lambda bi, m: (0, 0, 0)))
        r_pad = jnp.pad(router, ((0, 0), (0, LANES - router.shape[1])))
        r_hi = _unpack_hi(lax.bitcast_convert_type(r_pad, jnp.uint32))
        args.append(jnp.stack([r_hi, r_pad - r_hi]).astype(BF16))
        out_shape = [jax.ShapeDtypeStruct((b, s, d), F32),
                     jax.ShapeDtypeStruct((PACK_PLANES, b * s, PLANE_W), jnp.uint32),
                     jax.ShapeDtypeStruct((b, s, 4 * LANES), F32)]
        out_specs = [xspec,
                     pl.BlockSpec((PACK_PLANES, tm, PLANE_W), lambda bi, m: (0, bi * nm + m, 0)),
                     pl.BlockSpec((1, tm, 4 * LANES), lambda bi, m: (bi, m, 0))]
    else:
        out_shape = [jax.ShapeDtypeStruct((b, s, d), F32), jax.ShapeDtypeStruct((b, s, d), BF16)]
        out_specs = [xspec, xspec]
    return pl.pallas_call(
        functools.partial(_outproj_kernel, tm=tm, moe=moe),
        grid=(b, s // tm),
        in_specs=in_specs,
        out_specs=out_specs,
        out_shape=out_shape,
        scratch_shapes=[pltpu.VMEM((tm + 2 * HALO, GROUP_W), F32), pltpu.VMEM((tm + 2 * HALO, GROUP_W), F32)],
        compiler_params=_cparams(("arbitrary", "arbitrary")),
        name="outproj",
    )(*args)


def _swiglu_accumulate(h_ref, wg_ref, wu_ref, wd_ref, acc_ref, rows=None):
    r = slice(0, h_ref.shape[0] if rows is None else rows)
    h = h_ref[r, :]
    a = _dot(h, wg_ref[0].astype(BF16))
    u = _dot(h, wu_ref[0].astype(BF16))
    act = a * (1.0 / (1.0 + jnp.exp(-a))) * u
    acc_ref[r, :] += _dot(act.astype(BF16), wd_ref[0].astype(BF16))


def _ffn_kernel(h_ref, wg_ref, wu_ref, wd_ref, o_ref, acc_ref):
    f = pl.program_id(1)

    @pl.when(f == 0)
    def _():
        acc_ref[...] = jnp.zeros(acc_ref.shape, F32)

    _swiglu_accumulate(h_ref, wg_ref, wu_ref, wd_ref, acc_ref)

    @pl.when(f == pl.num_programs(1) - 1)
    def _():
        o_ref[...] = acc_ref[...].astype(BF16)


def _ffn(h, wg, wu, wd, layer):
    mt, d = h.shape
    ff = wg.shape[2]
    tm = min(mt, 1024)
    tf = 256
    return pl.pallas_call(
        _ffn_kernel,
        grid=(mt // tm, ff // tf),
        in_specs=[
            pl.BlockSpec((tm, d), lambda m, f: (m, 0)),
            pl.BlockSpec((1, d, tf), lambda m, f: (layer, 0, f)),
            pl.BlockSpec((1, d, tf), lambda m, f: (layer, 0, f)),
            pl.BlockSpec((1, tf, d), lambda m, f: (layer, f, 0)),
        ],
        out_specs=pl.BlockSpec((tm, d), lambda m, f: (m, 0)),
        out_shape=jax.ShapeDtypeStruct((mt, d), BF16),
        scratch_shapes=[pltpu.VMEM((tm, d), F32)],
        compiler_params=_cparams(("arbitrary", "arbitrary")),
        name="ffn",
    )(h, wg, wu, wd)


def _ffn_ctx_kernel(h_ref, hc_ref, wg_ref, wu_ref, wd_ref, o_ref, oc_ref, acc_ref):
    m = pl.program_id(0)
    f = pl.program_id(1)
    last = pl.num_programs(0) - 1
    tm = acc_ref.shape[0]
    nc = hc_ref.shape[0]

    @pl.when(f == 0)
    def _():
        acc_ref[...] = jnp.zeros(acc_ref.shape, F32)

    @pl.when(m < last)
    def _():
        _swiglu_accumulate(h_ref, wg_ref, wu_ref, wd_ref, acc_ref)

    @pl.when(m == last)
    def _():
        wg = wg_ref[0].astype(BF16)
        wu = wu_ref[0].astype(BF16)
        wd = wd_ref[0].astype(BF16)
        for h, r in ((h_ref[0:tm - nc, :], slice(0, tm - nc)), (hc_ref[...], slice(tm - nc, tm))):
            a = _dot(h, wg)
            u = _dot(h, wu)
            act = a * (1.0 / (1.0 + jnp.exp(-a))) * u
            acc_ref[r, :] += _dot(act.astype(BF16), wd)

    @pl.when(f == pl.num_programs(1) - 1)
    def _():
        o_ref[...] = acc_ref[...].astype(BF16)

        @pl.when(m == last)
        def _():
            oc_ref[...] = acc_ref[tm - nc:, :].astype(BF16)


FFN_CTX_MAX_TM = 1536


def _ffn_with_ctx(h, hc, wg, wu, wd, layer):
    mt, d = h.shape
    mc = hc.shape[0]
    ff = wg.shape[2]
    tm = max(t for t in range(MOE_SUB, FFN_CTX_MAX_TM + 1, MOE_SUB) if (mt + mc) % t == 0)
    nt = (mt + mc) // tm
    assert mc <= tm and mt - (nt - 1) * tm + mc == tm
    tf = 256
    return pl.pallas_call(
        _ffn_ctx_kernel,
        grid=(nt, ff // tf),
        in_specs=[
            pl.BlockSpec((tm, d), lambda m, f: (m, 0)),
            pl.BlockSpec((mc, d), lambda m, f: (0, 0)),
            pl.BlockSpec((1, d, tf), lambda m, f: (layer, 0, f)),
            pl.BlockSpec((1, d, tf), lambda m, f: (layer, 0, f)),
            pl.BlockSpec((1, tf, d), lambda m, f: (layer, f, 0)),
        ],
        out_specs=[pl.BlockSpec((tm, d), lambda m, f: (m, 0)), pl.BlockSpec((mc, d), lambda m, f: (0, 0))],
        out_shape=[jax.ShapeDtypeStruct((mt, d), BF16), jax.ShapeDtypeStruct((mc, d), BF16)],
        scratch_shapes=[pltpu.VMEM((tm, d), F32)],
        compiler_params=_cparams(("arbitrary", "arbitrary"), VMEM_LIMIT_MOE),
        name="ffn",
    )(h, hc, wg, wu, wd)


def _rank_kernel(r_ref, o_ref, cnt_ref, carry):
    i = pl.program_id(0)
    tm = r_ref.shape[0]

    @pl.when(i == 0)
    def _():
        carry[...] = jnp.zeros(carry.shape, F32)

    oh1 = r_ref[:, 0:LANES]
    oh2 = r_ref[:, LANES:2 * LANES]
    sel = oh1 + oh2
    row = lax.broadcasted_iota(jnp.int32, (tm, tm), 0)
    col = lax.broadcasted_iota(jnp.int32, (tm, tm), 1)
    tri = jnp.where(col < row, 1.0, 0.0).astype(BF16)
    excl = _dot(tri, sel.astype(BF16)) + carry[0:1, :]
    lane = lax.broadcasted_iota(jnp.int32, (tm, LANES), 1)
    lane_f = lane.astype(F32)
    e1 = jnp.sum(oh1 * lane_f, axis=-1, keepdims=True)
    r1 = jnp.sum(oh1 * excl, axis=-1, keepdims=True)
    e2 = jnp.sum(oh2 * lane_f, axis=-1, keepdims=True)
    r2 = jnp.sum(oh2 * excl, axis=-1, keepdims=True)
    meta = jnp.where(lane == 0, e1, jnp.where(lane == 1, r1, jnp.where(lane == 2, e2,
                     jnp.where(lane == 3, r2, 0.0))))
    o_ref[...] = meta.T[0:SUBLANES, :]
    carry[...] = carry[...] + jnp.sum(sel, axis=0, keepdims=True)
    cnt_ref[...] = carry[...]


def _rank(route):
    t = route.shape[0]
    tm = min(t, 512)
    return pl.pallas_call(
        _rank_kernel,
        grid=(t // tm,),
        in_specs=[pl.BlockSpec((tm, 2 * LANES), lambda i: (i, 0))],
        out_specs=[pl.BlockSpec((SUBLANES, tm), lambda i: (0, i)),
                   pl.BlockSpec((SUBLANES, LANES), lambda i: (0, 0))],
        out_shape=[jax.ShapeDtypeStruct((SUBLANES, t), F32), jax.ShapeDtypeStruct((SUBLANES, LANES), F32)],
        scratch_shapes=[pltpu.VMEM((SUBLANES, LANES), F32)],
        compiler_params=_cparams(("arbitrary",)),
        name="rank",
    )(route)


def _sc_mesh():
    return plsc.VectorSubcoreMesh(core_axis_name="core", subcore_axis_name="subcore")


def _sc_scatter2(x, idx1, idx2, n_out):
    n, w = x.shape

    @functools.partial(pl.kernel, out_type=jax.ShapeDtypeStruct((n_out, w), x.dtype), mesh=_sc_mesh(),
                       scratch_types=[], name="sc_dispatch")
    def k(x_hbm, i1_hbm, i2_hbm, o_hbm):
        def body(x_vmem, i1_vmem, i2_vmem):
            pltpu.sync_copy(x_vmem, o_hbm.at[i1_vmem.at[0]])
            pltpu.sync_copy(x_vmem, o_hbm.at[i2_vmem.at[0]])

        pltpu.emit_pipeline(
            body, grid=(n // SC_WINDOW,),
            in_specs=[pl.BlockSpec((SC_WINDOW, w), lambda i: (i, 0)),
                      pl.BlockSpec((1, SC_WINDOW), lambda i: (0, i)),
                      pl.BlockSpec((1, SC_WINDOW), lambda i: (0, i))],
            out_specs=[], core_axis_name=("core", "subcore"),
            dimension_semantics=(pltpu.PARALLEL,))(x_hbm, i1_hbm, i2_hbm)

    return k(x, idx1.reshape(1, n), idx2.reshape(1, n))


def _sc_gather(y, idx):
    (n,) = idx.shape
    w = y.shape[1]

    @functools.partial(pl.kernel, out_type=jax.ShapeDtypeStruct((n, w), y.dtype), mesh=_sc_mesh(),
                       scratch_types=[], name="sc_combine")
    def k(y_hbm, i_hbm, o_hbm):
        def body(i_vmem, o_vmem):
            pltpu.sync_copy(y_hbm.at[i_vmem.at[0]], o_vmem)

        pltpu.emit_pipeline(
            body, grid=(n // SC_WINDOW,),
            in_specs=[pl.BlockSpec((1, SC_WINDOW), lambda i: (0, i))],
            out_specs=[pl.BlockSpec((SC_WINDOW, w), lambda i: (i, 0))],
            core_axis_name=("core", "subcore"),
            dimension_semantics=(pltpu.PARALLEL,))(i_hbm, o_hbm)

    return k(y, idx.reshape(1, n))


MOE_SUB = 256


def _moe_kernel(te_ref, nu_ref, tv_ref, h_ref, wg_ref, wu_ref, wd_ref, o_ref, acc_ref, hb_ref):
    m = pl.program_id(0)
    f = pl.program_id(1)
    tm = hb_ref.shape[0]
    half = hb_ref.shape[1] // 2

    @pl.when(m < nu_ref[0])
    def _():
        @pl.when(f == 0)
        def _():
            acc_ref[...] = jnp.zeros(acc_ref.shape, F32)
            routed = lax.broadcasted_iota(jnp.int32, (tm, PLANE_W), 0) < tv_ref[m]
            for c in range(PACK_PLANES):
                w = jnp.where(routed, h_ref[c], jnp.uint32(0))
                hb_ref[:, c * PLANE_W:(c + 1) * PLANE_W] = _unpack_hi(w).astype(BF16)
                hb_ref[:, half + c * PLANE_W:half + (c + 1) * PLANE_W] = _unpack_lo(w).astype(BF16)

        n_sub = (tv_ref[m] + MOE_SUB - 1) // MOE_SUB
        for k in range(1, tm // MOE_SUB + 1):
            @pl.when(n_sub == k)
            def _():
                _swiglu_accumulate(hb_ref, wg_ref, wu_ref, wd_ref, acc_ref, k * MOE_SUB)

        @pl.when(f == pl.num_programs(1) - 1)
        def _():
            packed = _pack_bf16_pairs(acc_ref[...])
            for c in range(PACK_PLANES):
                o_ref[c] = packed[:, c * PLANE_W:(c + 1) * PLANE_W]


MOE_TM = 1536


def _moe_routed(hp, tile_expert, n_used, tile_valid, wg, wu, wd, layer, tm):
    _, r, _ = hp.shape
    _, ne, d, ff = wg.shape
    wg = wg.reshape(-1, d, ff)
    wu = wu.reshape(-1, d, ff)
    wd = wd.reshape(-1, ff, d)
    e0 = layer * ne
    tf = 256
    nf = ff // tf

    def row_map(m, f, te, nu, tv):
        return (0, jnp.minimum(m, nu[0] - 1), 0)

    def fidx(m, f, nu):
        return jnp.where(m < nu[0], f, nf - 1)

    hspec = pl.BlockSpec((PACK_PLANES, tm, PLANE_W), row_map)
    return pl.pallas_call(
        _moe_kernel,
        grid_spec=pltpu.PrefetchScalarGridSpec(
            num_scalar_prefetch=3,
            grid=(r // tm, nf),
            in_specs=[
                hspec,
                pl.BlockSpec((1, d, tf), lambda m, f, te, nu, tv: (e0 + te[m], 0, fidx(m, f, nu))),
                pl.BlockSpec((1, d, tf), lambda m, f, te, nu, tv: (e0 + te[m], 0, fidx(m, f, nu))),
                pl.BlockSpec((1, tf, d), lambda m, f, te, nu, tv: (e0 + te[m], fidx(m, f, nu), 0)),
            ],
            out_specs=hspec,
            scratch_shapes=[pltpu.VMEM((tm, d), F32), pltpu.VMEM((tm, d), BF16)],
        ),
        out_shape=jax.ShapeDtypeStruct(hp.shape, jnp.uint32),
        compiler_params=_cparams(("arbitrary", "arbitrary"), VMEM_LIMIT_MOE),
        name="moe",
    )(tile_expert, n_used, tile_valid, hp, wg, wu, wd)


def _route_plan(meta, counts, tm, n_tiles):
    e1, r1, e2, r2 = (meta[k].astype(jnp.int32) for k in range(4))
    cnt = counts[0, :N_EXPERTS].astype(jnp.int32)
    tiles_per = (cnt + tm - 1) // tm
    tile_end = jnp.cumsum(tiles_per)
    start_row = (tile_end - tiles_per) * tm
    pos1 = start_row[e1] + r1
    pos2 = start_row[e2] + r2
    n_used = tile_end[-1]
    tiles = jnp.arange(n_tiles, dtype=jnp.int32)
    tile_expert = jnp.sum((tiles[:, None] >= tile_end[None, :]).astype(jnp.int32), axis=1)
    last_expert = jnp.sum((n_used - 1 >= tile_end).astype(jnp.int32))
    tile_expert = jnp.minimum(tile_expert, last_expert)
    tile_start = (tile_end - tiles_per)[tile_expert]
    tile_valid = jnp.clip(cnt[tile_expert] - (tiles - tile_start) * tm, 0, tm)
    return pos1, pos2, tile_expert, n_used.reshape(1), tile_valid


def _plane_rows(pos, n_rows):
    return (jnp.arange(PACK_PLANES, dtype=jnp.int32)[:, None] * n_rows + pos[None, :]).reshape(-1)


def _moe(hp, route, wg, wu, wd, layer):
    _, t, _ = hp.shape
    tm = min(MOE_TM, t)
    n_tiles = 2 * t // tm + N_EXPERTS
    n_rows = n_tiles * tm
    meta, counts = _rank(route)
    pos1, pos2, tile_expert, n_used, tile_valid = _route_plan(meta, counts, tm, n_tiles)
    i1 = _plane_rows(pos1, n_rows)
    i2 = _plane_rows(pos2, n_rows)
    hs = _sc_scatter2(hp.reshape(PACK_PLANES * t, PLANE_W), i1, i2, PACK_PLANES * n_rows)
    ys = _moe_routed(hs.reshape(PACK_PLANES, n_rows, PLANE_W), tile_expert, n_used, tile_valid,
                     wg, wu, wd, layer, tm)
    ys = ys.reshape(PACK_PLANES * n_rows, PLANE_W)
    y1 = _sc_gather(ys, i1).reshape(PACK_PLANES, t, PLANE_W)
    y2 = _sc_gather(ys, i2).reshape(PACK_PLANES, t, PLANE_W)
    return y1, y2


def _final_kernel(x_ref, f_ref, g_ref, nw_ref, o_ref):
    xv = x_ref[0] + g_ref[0, 0] * f_ref[0].astype(F32)
    ms = jnp.mean(xv * xv, axis=-1, keepdims=True)
    o_ref[0] = xv * lax.rsqrt(ms + EPS) * nw_ref[...]


def _final(x, f, mods, nw):
    b, s, d = x.shape
    tm = min(s, 512)
    xspec = pl.BlockSpec((1, tm, d), lambda bi, m: (bi, m, 0))
    return pl.pallas_call(
        _final_kernel,
        grid=(b, s // tm),
        in_specs=[xspec, xspec,
                  pl.BlockSpec((1, 1, 1, d), lambda bi, m: (bi, 5, 0, 0)),
                  pl.BlockSpec((1, d), lambda bi, m: (0, 0))],
        out_specs=xspec,
        out_shape=jax.ShapeDtypeStruct((b, s, d), F32),
        compiler_params=_cparams(("arbitrary", "arbitrary")),
        name="final",
    )(x, f, mods, nw.reshape(1, d))


def _final_moe_kernel(x_ref, y1_ref, y2_ref, p_ref, g_ref, nw_ref, o_ref):
    tm, d = x_ref.shape[1], x_ref.shape[2]
    half = d // 2
    p = p_ref[0]
    p1 = jnp.concatenate([p[:, 0:LANES]] * (PLANE_W // LANES), axis=1)
    p2 = jnp.concatenate([p[:, LANES:2 * LANES]] * (PLANE_W // LANES), axis=1)
    ssq = jnp.zeros((tm, 1), F32)
    for c in range(PACK_PLANES):
        w1 = y1_ref[c]
        w2 = y2_ref[c]
        for unpack, off in ((_unpack_hi, 0), (_unpack_lo, half)):
            sl = slice(off + c * PLANE_W, off + (c + 1) * PLANE_W)
            f = p1 * unpack(w1) + p2 * unpack(w2)
            xv = x_ref[0, :, sl] + g_ref[0, 0, :, sl] * f
            o_ref[0, :, sl] = xv
            ssq = ssq + jnp.sum(xv * xv, axis=-1, keepdims=True)
    o_ref[0] = o_ref[0] * lax.rsqrt(ssq / d + EPS) * nw_ref[...]


def _final_moe(x, y1, y2, route, mods, nw):
    b, s, d = x.shape
    tm = min(s, 512)
    nm = s // tm
    xspec = pl.BlockSpec((1, tm, d), lambda bi, m: (bi, m, 0))
    yspec = pl.BlockSpec((PACK_PLANES, tm, PLANE_W), lambda bi, m: (0, bi * nm + m, 0))
    return pl.pallas_call(
        _final_moe_kernel,
        grid=(b, nm),
        in_specs=[xspec, yspec, yspec,
                  pl.BlockSpec((1, tm, 2 * LANES), lambda bi, m: (bi, m, 1)),
                  pl.BlockSpec((1, 1, 1, d), lambda bi, m: (bi, 5, 0, 0)),
                  pl.BlockSpec((1, d), lambda bi, m: (0, 0))],
        out_specs=xspec,
        out_shape=jax.ShapeDtypeStruct((b, s, d), F32),
        compiler_params=_cparams(("arbitrary", "arbitrary")),
        name="final",
    )(x, y1, y2, route, mods, nw.reshape(1, d))


PREP_W = 2 * LANES


def _w_in_plan():
    nb = GROUP_W // PREP_W
    plan = [(3 * nb + j, 0, 0) for j in range(4 * nb)]
    plan += [(j, 1, 0) for j in range(nb)]
    plan += [(nb + j, 0, 0) for j in range(2 * nb)]
    plan += [(7 * nb + j, 1, 0) for j in range(nb)]
    plan += [(8 * nb, 0, 1), (8 * nb, 0, 2)]
    assert len(plan) * PREP_W == D_PROJ
    return jnp.asarray(plan, jnp.int32).T.reshape(-1)


def _prep_w_kernel(plan_ref, w_ref, o_ref):
    j = pl.program_id(1)
    nblk = pl.num_programs(1)
    x = w_ref[0]
    mode = plan_ref[2 * nblk + j]
    src = jnp.where(mode == 2, x[:, LANES:], x[:, :LANES])
    lo = lax.broadcasted_iota(jnp.int32, src.shape, 1) < HEAD_DIM
    other = pltpu.roll(src, HEAD_DIM, 1)
    dup = jnp.concatenate([jnp.where(lo, src, other), jnp.where(lo, other, src)], axis=1)
    scale = jnp.where(plan_ref[nblk + j] == 1, HEAD_DIM ** -0.5, 1.0)
    o_ref[0] = (jnp.where(mode == 0, x, dup) * scale).astype(BF16)


def _prep_w_in(w_in):
    depth, d, _ = w_in.shape
    nblk = D_PROJ // PREP_W
    return pl.pallas_call(
        _prep_w_kernel,
        grid_spec=pltpu.PrefetchScalarGridSpec(
            num_scalar_prefetch=1,
            grid=(depth, nblk),
            in_specs=[pl.BlockSpec((1, d, PREP_W), lambda i, j, plan: (i, 0, plan[j]))],
            out_specs=pl.BlockSpec((1, d, PREP_W), lambda i, j, plan: (i, 0, j)),
        ),
        out_shape=jax.ShapeDtypeStruct((depth, d, D_PROJ), BF16),
        compiler_params=_cparams(("arbitrary", "arbitrary")),
        name="prep_w_in",
    )(_w_in_plan(), w_in)


def _rope_tables(n):
    t = jnp.arange(n, dtype=jnp.int32)
    rows = (t // GRID_W).astype(F32)
    cols = (t % GRID_W).astype(F32)
    nf = HEAD_DIM // 4
    inv = ROPE_BASE ** (-jnp.arange(nf, dtype=F32) / nf)
    ang = jnp.concatenate([rows[:, None] * inv, cols[:, None] * inv], axis=-1)
    cos = jnp.cos(ang)
    sin = jnp.sin(ang)
    cos_t = jnp.concatenate([cos, cos, cos, cos], axis=-1)
    sin_t = jnp.concatenate([-sin, sin, -sin, sin], axis=-1)
    return cos_t, sin_t


def kernel(x, c, ctx, c_ctx, w_ada, b_ada, norm_mix, norm_ffn, norm_final, w_in, w_out, na_rpb, pool_w, pool_scale, conv_w, swa_sink, ffn_w_gate, ffn_w_up, ffn_w_down, moe_router, moe_w_gate, moe_w_up, moe_w_down):
    b, s, d = x.shape
    lc = ctx.shape[1]
    depth = w_ada.shape[0]
    cond = jnp.concatenate([c, c_ctx[None, :]], axis=0)
    assert b == 2
    mods_all = _ada(cond, w_ada, b_ada)
    cos_t, sin_t = _rope_tables(s)
    cos_c = jnp.ones((lc, LANES), F32)
    sin_c = jnp.zeros((lc, LANES), F32)
    row_x = lambda bi: bi
    row_c = lambda bi: b

    w_proj = _prep_w_in(w_in)
    cx = ctx
    fx = fc = None
    for i in range(depth):
        last = i == depth - 1
        mods = mods_all[i].reshape(SUBLANES, 6, 1, d)
        wi = (w_proj, i)
        wo = w_out[i].astype(BF16)
        if i == 0:
            proj, _ = _inproj(x, wi, norm_mix[i], mods, row_x, cos_t, sin_t)
            projc, _ = _inproj(cx, wi, norm_mix[i], mods, row_c, cos_c, sin_c)
        else:
            mods_prev = mods_all[i - 1].reshape(SUBLANES, 6, 1, d)
            proj, x = _inproj_res(x, fx, mods_prev, wi, norm_mix[i], mods, row_x, cos_t, sin_t)
            projc, cx = _inproj_res(cx, fc, mods_prev, wi, norm_mix[i], mods, row_c, cos_c, sin_c)
        bias = _na_bias(na_rpb[i])
        o_a = _na_attn(proj, projc, bias)
        o_d = _swa_attn(proj, projc, swa_sink[i])
        moe = i % 2 == 1
        router = moe_router[i // 2] if moe else None
        outs = _outproj(o_a, o_d, proj, x, wo, pool_w[i], pool_scale[i], conv_w[i], norm_ffn[i], mods, row_x,
                        router)
        x, h2 = outs[0], outs[1]
        if not last:
            oc_a, oc_d = _ctx_attn(projc, swa_sink[i])
            cx, h2c = _outproj(oc_a, oc_d, projc, cx, wo, pool_w[i], pool_scale[i], conv_w[i], norm_ffn[i],
                               mods, row_c)
        j = i // 2
        if moe:
            if not last:
                raise NotImplementedError("an expert layer that is not the last layer")
            route = outs[2]
            y1, y2 = _moe(h2, route.reshape(b * s, 4 * LANES), moe_w_gate, moe_w_up, moe_w_down, j)
            return _final_moe(x, y1, y2, route, mods, norm_final)
        if last:
            fx = _ffn(h2.reshape(b * s, d), ffn_w_gate, ffn_w_up, ffn_w_down, j).reshape(b, s, d)
        else:
            fx, fc = _ffn_with_ctx(h2.reshape(b * s, d), h2c.reshape(b * lc, d),
                                   ffn_w_gate, ffn_w_up, ffn_w_down, j)
            fx, fc = fx.reshape(b, s, d), fc.reshape(b, lc, d)
    return _final(x, fx, mods_all[depth - 1].reshape(SUBLANES, 6, 1, d), norm_final)


def _inproj_res(x, f, mods_prev, w, nw, mods, mod_row, cos_t, sin_t):
    both = jnp.concatenate([mods, mods_prev], axis=1)
    return _inproj(x, w, nw, both, mod_row, cos_t, sin_t, res=(f, 6 + 5))
```
